```python
import math
import jax
import jax.numpy as jnp
from jax import lax
import numpy as np

D_MODEL = 2048
BATCH = 4
SEQ = 4096
DEPTH = 2

GRID_W = 64
CTX_LEN = 256
Q_BLOCK = 128
ROPE_BASE = 10000.0
NORM_EPS = 1e-6

DA_HEADS = 6
DA_DQK = 64
DA_DV = 2 * DA_DQK
DA_W = DA_HEADS * DA_DV

MLA_HEADS = 6
MLA_Q_RANK = 512
MLA_KV_RANK = 256
MLA_NOPE = 128
MLA_ROPE = 64
MLA_DV = 128
MLA_W = MLA_HEADS * MLA_DV

HY_CH = 512
HY_ORDER = 2
HY_EMB = 33
HY_BANDS = (HY_EMB - 1) // 2
HY_FFN = 64
HY_MIN_DECAY = math.log(1e-2) / 1.5
HY_MAX_DECAY = math.log(1e-2) / 0.3

D_FF = 5632
N_BRANCH = 3

IN_WIDTHS = (DA_HEADS * 2 * DA_DQK, DA_HEADS * 2 * DA_DQK, DA_W, MLA_Q_RANK, MLA_KV_RANK, MLA_ROPE, 3 * HY_CH, N_BRANCH * D_MODEL)
IN_OFFSETS = tuple(sum(IN_WIDTHS[: j + 1]) for j in range(len(IN_WIDTHS) - 1))
D_IN = sum(IN_WIDTHS)

DEEPNORM_ALPHA = (2 * DEPTH) ** 0.25
DEEPNORM_BETA = (8 * DEPTH) ** -0.25

kernel_name = "hybrid_diffusion_block"


def layer_norm(u, g=None, b=None):
    uf = u.astype(jnp.float32)
    mu = jnp.mean(uf, axis=-1, keepdims=True)
    var = jnp.mean(jnp.square(uf - mu), axis=-1, keepdims=True)
    y = ((uf - mu) * lax.rsqrt(var + NORM_EPS)).astype(u.dtype)
    return y if g is None else y * g + b


def rms_norm(u, g):
    uf = u.astype(jnp.float32)
    return (uf * lax.rsqrt(jnp.mean(uf * uf, axis=-1, keepdims=True) + NORM_EPS)).astype(u.dtype) * g


def modulate(u, shift, scale):
    return u * (1 + scale) + shift


def dwconv3(u, w, b):
    up = jnp.pad(u, ((0, 0), (1, 1), (0, 0)))
    return up[:, :-2] * w[0] + up[:, 1:-1] * w[1] + up[:, 2:] * w[2] + b


def rope_1d(u, pos):
    d = u.shape[-1]
    inv = ROPE_BASE ** (-jnp.arange(0, d, 2, dtype=jnp.float32) / d)
    ang = pos.astype(jnp.float32)[:, None] * inv[None, :]
    ang = jnp.concatenate([ang, ang], axis=-1).reshape((pos.shape[0],) + (1,) * (u.ndim - 3) + (d,))
    u1, u2 = jnp.split(u, 2, axis=-1)
    rot = jnp.concatenate([-u2, u1], axis=-1)
    return u * jnp.cos(ang).astype(u.dtype) + rot * jnp.sin(ang).astype(u.dtype)


def rope_2d(u, rows, cols):
    ur, uc = jnp.split(u, 2, axis=-1)
    return jnp.concatenate([rope_1d(ur, rows), rope_1d(uc, cols)], axis=-1)


def map_query_blocks(fn, *qs):
    b, n = qs[0].shape[:2]
    nb = n // Q_BLOCK
    blocks = tuple(jnp.moveaxis(q.reshape((b, nb, Q_BLOCK) + q.shape[2:]), 1, 0) for q in qs)
    out = lax.map(lambda qb: fn(*qb), blocks)
    return jnp.moveaxis(out, 0, 1).reshape((b, n) + out.shape[3:])


def diff_attend(q, k, v, lam):
    s = jnp.einsum('bqshd,bkshd->bshqk', q, k).astype(jnp.float32) * (DA_DQK ** -0.5)
    a = jax.nn.softmax(s, axis=-1)
    w = a[:, 0] - lam * a[:, 1]
    return jnp.einsum('bhqk,bkhe->bqhe', w.astype(v.dtype), v)


def mla_attend(q_nope, q_rope, k_nope, k_rope, v):
    s = jnp.einsum('bqhd,bkhd->bhqk', q_nope, k_nope) + jnp.einsum('bqhr,bkr->bhqk', q_rope, k_rope)
    a = jax.nn.softmax(s.astype(jnp.float32) * ((MLA_NOPE + MLA_ROPE) ** -0.5), axis=-1)
    return jnp.einsum('bhqk,bkhd->bqhd', a.astype(v.dtype), v)


def attn_queries(aq, bcq, q_g, w_uq, pos):
    b, n = aq.shape[:2]
    qa = aq.reshape(b, n, 2, DA_HEADS, DA_DQK)
    q = (rms_norm(bcq, q_g) @ w_uq).reshape(b, n, MLA_HEADS, MLA_NOPE + MLA_ROPE)
    qn, qr = q[..., :MLA_NOPE], q[..., MLA_NOPE:]
    if pos is not None:
        qa, qr = rope_2d(qa, *pos), rope_2d(qr, *pos)
    return qa, qn, qr


def attn_keys(ak, av, bckv, bkr, kv_g, w_ukv, pos):
    b, n = ak.shape[:2]
    ka = ak.reshape(b, n, 2, DA_HEADS, DA_DQK)
    va = av.reshape(b, n, DA_HEADS, DA_DV)
    kv = (rms_norm(bckv, kv_g) @ w_ukv).reshape(b, n, MLA_HEADS, MLA_NOPE + MLA_DV)
    kn, vb = kv[..., :MLA_NOPE], kv[..., MLA_NOPE:]
    kr = bkr
    if pos is not None:
        ka, kr = rope_2d(ka, *pos), rope_2d(kr, *pos)
    return ka, va, kn, kr, vb


def attend(queries, keys, lam, lam_init, subln_g):
    qa, qn, qr = queries
    ka, va, kn, kr, vb = keys
    b, n = qa.shape[:2]
    oa = map_query_blocks(lambda q: diff_attend(q, ka, va, lam), qa)
    oa = rms_norm(oa, subln_g) * (1 - lam_init)
    ob = map_query_blocks(lambda q1, q2: mla_attend(q1, q2, kn, kr, vb), qn, qr)
    return oa.reshape(b, n, DA_W), ob.reshape(b, n, MLA_W)


def hyena_filters(n, w1, b1, w2, b2, w3, b3, w4, freq):
    f32 = jnp.float32
    t = jnp.linspace(0.0, 1.0, n, dtype=f32)[:, None]
    phase = (2.0 * math.pi / n) * jnp.arange(n, dtype=f32)[:, None] * jnp.linspace(1e-4, HY_BANDS - 1, HY_BANDS, dtype=f32)[None, :]
    feat = jnp.concatenate([t, jnp.cos(phase), -jnp.sin(phase)], axis=-1)
    w = freq.astype(f32)
    h = jnp.sin(w * (feat @ w1.astype(f32) + b1.astype(f32)))
    h = jnp.sin(w * (h @ w2.astype(f32) + b2.astype(f32)))
    h = jnp.sin(w * (h @ w3.astype(f32) + b3.astype(f32)))
    h = (h @ w4.astype(f32)).reshape(n, HY_ORDER, 2, HY_CH)
    deltas = jnp.abs(jnp.linspace(HY_MIN_DECAY, HY_MAX_DECAY, HY_CH, dtype=f32))
    h = h * jnp.exp(-t.reshape(n, 1, 1, 1) * deltas)
    k = jnp.concatenate([h[:, :, 0], jnp.zeros((1, HY_ORDER, HY_CH), f32), h[: n - 1, :, 1][::-1]], axis=0)
    k = k / jnp.sum(jnp.abs(k), axis=0, keepdims=True)
    return jnp.fft.rfft(k, axis=0)


def hyena_branch(u, conv_w, conv_b, w1, b1, w2, b2, w3, b3, w4, freq, skip):
    n = u.shape[1]
    v, x1, x2 = jnp.split(dwconv3(u, conv_w, conv_b), 3, axis=-1)
    kf = hyena_filters(n, w1, b1, w2, b2, w3, b3, w4, freq)
    z = v.astype(jnp.float32)
    for o, gate in enumerate((x1, x2)):
        zf = jnp.fft.rfft(z, n=2 * n, axis=1)
        conv = jnp.fft.irfft(zf * kf[None, :, o], n=2 * n, axis=1)[:, :n]
        z = gate.astype(jnp.float32) * (conv + skip[o].astype(jnp.float32) * z)
    return z.astype(u.dtype)


def merge_branches(oa, ob, oc, gate_logits, w_ba, w_bb, w_bc, w_o):
    b, n = oa.shape[:2]
    g = jax.nn.sigmoid(gate_logits.reshape(b, n, N_BRANCH, D_MODEL))
    merged = g[..., 0, :] * (oa @ w_ba) + g[..., 1, :] * (ob @ w_bb) + g[..., 2, :] * (oc @ w_bc)
    return merged @ w_o


def conv_ffn(h, w_up, conv_w, conv_b, w_down):
    a, v = jnp.split(h @ w_up, 2, axis=-1)
    return (jax.nn.silu(dwconv3(a, conv_w, conv_b)) * v) @ w_down


def setup_inputs(seed: int = 0) -> dict:
    key = jax.random.key(seed)
    ks = iter(jax.random.split(key, 48))

    def nrm(shape, scale):
        return jax.random.normal(next(ks), shape, jnp.float32) * scale

    def gain(shape):
        return 1.0 + nrm(shape, 0.02)

    L, D = DEPTH, D_MODEL
    return {
        "x": nrm((BATCH, SEQ, D), 1.0),
        "c": nrm((BATCH, D), 1.0),
        "ctx": nrm((BATCH, CTX_LEN, D), 1.0),
        "c_ctx": nrm((D,), 1.0),
        "ada_w": nrm((L, D, 6 * D), D ** -0.5),
        "ada_b": nrm((L, 6 * D), 0.02),
        "w_in": nrm((L, D, D_IN), D ** -0.5),
        "da_lambda": nrm((L, 4, DA_DQK), 0.1),
        "da_subln_g": gain((L, DA_DV)),
        "mla_q_g": gain((L, MLA_Q_RANK)),
        "mla_w_uq": nrm((L, MLA_Q_RANK, MLA_HEADS * (MLA_NOPE + MLA_ROPE)), MLA_Q_RANK ** -0.5),
        "mla_kv_g": gain((L, MLA_KV_RANK)),
        "mla_w_ukv": nrm((L, MLA_KV_RANK, MLA_HEADS * (MLA_NOPE + MLA_DV)), MLA_KV_RANK ** -0.5),
        "hy_conv_w": nrm((L, 3, 3 * HY_CH), 3 ** -0.5),
        "hy_conv_b": nrm((L, 3 * HY_CH), 0.02),
        "hy_ffn_w1": nrm((L, HY_EMB, HY_FFN), HY_EMB ** -0.5),
        "hy_ffn_b1": nrm((L, HY_FFN), 0.02),
        "hy_ffn_w2": nrm((L, HY_FFN, HY_FFN), HY_FFN ** -0.5),
        "hy_ffn_b2": nrm((L, HY_FFN), 0.02),
        "hy_ffn_w3": nrm((L, HY_FFN, HY_FFN), HY_FFN ** -0.5),
        "hy_ffn_b3": nrm((L, HY_FFN), 0.02),
        "hy_ffn_w4": nrm((L, HY_FFN, HY_ORDER * 2 * HY_CH), HY_FFN ** -0.5),
        "hy_freq": gain((L, HY_FFN)),
        "hy_skip": nrm((L, HY_ORDER, HY_CH), 1.0),
        "w_branch_a": nrm((L, DA_W, D), DA_W ** -0.5),
        "w_branch_b": nrm((L, MLA_W, D), MLA_W ** -0.5),
        "w_branch_c": nrm((L, HY_CH, D), HY_CH ** -0.5),
        "w_out": nrm((L, D, D), D ** -0.5 * DEEPNORM_BETA),
        "ln1_g": gain((L, D)),
        "ln1_b": nrm((L, D), 0.02),
        "ffn_w_up": nrm((L, D, 2 * D_FF), D ** -0.5),
        "ffn_conv_w": nrm((L, 3, D_FF), 3 ** -0.5),
        "ffn_conv_b": nrm((L, D_FF), 0.02),
        "ffn_w_down": nrm((L, D_FF, D), D_FF ** -0.5 * DEEPNORM_BETA),
        "ln2_g": gain((L, D)),
        "ln2_b": nrm((L, D), 0.02),
    }


def reference(x, c, ctx, c_ctx, ada_w, ada_b, w_in, da_lambda, da_subln_g, mla_q_g, mla_w_uq, mla_kv_g, mla_w_ukv,
              hy_conv_w, hy_conv_b, hy_ffn_w1, hy_ffn_b1, hy_ffn_w2, hy_ffn_b2, hy_ffn_w3, hy_ffn_b3, hy_ffn_w4,
              hy_freq, hy_skip, w_branch_a, w_branch_b, w_branch_c, w_out, ln1_g, ln1_b,
              ffn_w_up, ffn_conv_w, ffn_conv_b, ffn_w_down, ln2_g, ln2_b):
    n_tok = x.shape[1]
    ROWS = n_tok // GRID_W
    rows = jnp.repeat(jnp.arange(ROWS, dtype=jnp.int32), GRID_W)
    cols = jnp.tile(jnp.arange(GRID_W, dtype=jnp.int32), ROWS)
    pos_lat = (rows, cols)

    def update_stream(xs, parts, pos, keys, mod, i, lam, lam_init):
        aq, bcq, cu, gate_logits = parts[0], parts[3], parts[6], parts[7]
        g1, sh2, sc2, g2 = mod[2], mod[3], mod[4], mod[5]
        oa, ob = attend(attn_queries(aq, bcq, mla_q_g[i], mla_w_uq[i], pos), keys, lam, lam_init, da_subln_g[i])
        oc = hyena_branch(cu, hy_conv_w[i], hy_conv_b[i], hy_ffn_w1[i], hy_ffn_b1[i], hy_ffn_w2[i], hy_ffn_b2[i],
                          hy_ffn_w3[i], hy_ffn_b3[i], hy_ffn_w4[i], hy_freq[i], hy_skip[i])
        y = merge_branches(oa, ob, oc, gate_logits, w_branch_a[i], w_branch_b[i], w_branch_c[i], w_out[i])
        xs = layer_norm(DEEPNORM_ALPHA * xs + g1 * y, ln1_g[i], ln1_b[i])
        y = conv_ffn(modulate(layer_norm(xs), sh2, sc2), ffn_w_up[i], ffn_conv_w[i], ffn_conv_b[i], ffn_w_down[i])
        return layer_norm(DEEPNORM_ALPHA * xs + g2 * y, ln2_g[i], ln2_b[i])

    xl, xc = x, ctx
    for i in range(DEPTH):
        lam_init = 0.8 - 0.6 * math.exp(-0.3 * i)
        lq1, lk1, lq2, lk2 = da_lambda[i].astype(jnp.float32)
        lam = jnp.exp(jnp.sum(lq1 * lk1)) - jnp.exp(jnp.sum(lq2 * lk2)) + lam_init
        mod_l = jnp.split((jax.nn.silu(c) @ ada_w[i] + ada_b[i])[:, None, :], 6, axis=-1)
        mod_c = jnp.split((jax.nn.silu(c_ctx) @ ada_w[i] + ada_b[i])[None, None, :], 6, axis=-1)
        parts_l = jnp.split(modulate(layer_norm(xl), mod_l[0], mod_l[1]) @ w_in[i], IN_OFFSETS, axis=-1)
        parts_c = jnp.split(modulate(layer_norm(xc), mod_c[0], mod_c[1]) @ w_in[i], IN_OFFSETS, axis=-1)
        keys_c = attn_keys(parts_c[1], parts_c[2], parts_c[4], parts_c[5], mla_kv_g[i], mla_w_ukv[i], None)
        keys_l = attn_keys(parts_l[1], parts_l[2], parts_l[4], parts_l[5], mla_kv_g[i], mla_w_ukv[i], pos_lat)
        keys_all = tuple(jnp.concatenate([kc, kl], axis=1) for kc, kl in zip(keys_c, keys_l))
        new_xl = update_stream(xl, parts_l, pos_lat, keys_all, mod_l, i, lam, lam_init)
        if i < DEPTH - 1:
            xc = update_stream(xc, parts_c, None, keys_c, mod_c, i, lam, lam_init)
        xl = new_xl
    return xl
```

```python
import functools
import math

import numpy as np
import jax
import jax.numpy as jnp
from jax import lax
from jax.experimental import pallas as pl
from jax.experimental.pallas import tpu as pltpu

F32 = jnp.float32
BF16 = jnp.bfloat16

D_MODEL = 2048
DEPTH = 2
GRID_W = 64
ROPE_BASE = 10000.0
NORM_EPS = 1e-6
DA_HEADS = 6
DA_DQK = 64
DA_DV = 128
DA_W = DA_HEADS * DA_DV
MLA_HEADS = 6
MLA_Q_RANK = 512
MLA_KV_RANK = 256
MLA_NOPE = 128
MLA_ROPE = 64
MLA_DV = 128
MLA_W = MLA_HEADS * MLA_DV
MLA_QK_PAD = 256
HY_CH = 512
HY_ORDER = 2
HY_EMB = 33
HY_BANDS = (HY_EMB - 1) // 2
HY_FFN = 64
HY_MIN_DECAY = math.log(1e-2) / 1.5
HY_MAX_DECAY = math.log(1e-2) / 0.3
D_FF = 5632
N_BRANCH = 3
DEEPNORM_ALPHA = (2 * DEPTH) ** 0.25
LOG2E = 1.4426950408889634

ROW_GROUP = 256
LANE = 128
FFT_S = 128
VMEM_LIMIT = 52 * 1024 * 1024


def _cparams(*sem):
    return pltpu.CompilerParams(dimension_semantics=sem, vmem_limit_bytes=VMEM_LIMIT)


def _pick(total, prefs):
    for p in prefs:
        if total % p == 0:
            return p
    raise ValueError(f"no tile for {total} in {prefs}")


def _ln(x):
    mu = jnp.mean(x, axis=-1, keepdims=True)
    xc = x - mu
    var = jnp.mean(xc * xc, axis=-1, keepdims=True)
    return xc * lax.rsqrt(var + NORM_EPS)


def _rms(x):
    return x * lax.rsqrt(jnp.mean(x * x, axis=-1, keepdims=True) + NORM_EPS)


def _rope128(u, cos, sa, sb):
    return u * cos + pltpu.roll(u, 16, 1) * sa + pltpu.roll(u, LANE - 16, 1) * sb


def _ada_kernel(c_ref, w_ref, b_ref, o_ref):
    a = c_ref[...]
    a = a * jax.nn.sigmoid(a)
    o_ref[...] = jnp.dot(a.astype(BF16), w_ref[...].astype(BF16), preferred_element_type=F32) + b_ref[...]


def _ada(cc, ada_w, ada_b):
    L, D, N = ada_w.shape
    tn = 1024
    return pl.pallas_call(
        _ada_kernel,
        out_shape=jax.ShapeDtypeStruct((L, 8, N), F32),
        grid=(L, N // tn),
        in_specs=[pl.BlockSpec((8, D), lambda l, j: (0, 0)),
                  pl.BlockSpec((None, D, tn), lambda l, j: (l, 0, j)),
                  pl.BlockSpec((None, 1, tn), lambda l, j: (l, 0, j))],
        out_specs=pl.BlockSpec((None, 8, tn), lambda l, j: (l, 0, j)),
        compiler_params=_cparams("arbitrary", "arbitrary"),
    )(cc, ada_w, ada_b)


def _lnmod_kernel(x_ref, m_ref, o_ref):
    y = _ln(x_ref[...])
    o_ref[...] = (y * (1.0 + m_ref[1:2, :]) + m_ref[0:1, :]).astype(BF16)


def _mod_row(g, gpb, lat_groups, n_batch):
    return jnp.where(g % gpb < lat_groups, g // gpb, n_batch)


def _lnmod(xs, mod, geom):
    rows, D = xs.shape
    gpb, lat_groups, n_batch = geom
    return pl.pallas_call(
        _lnmod_kernel,
        out_shape=jax.ShapeDtypeStruct((rows, D), BF16),
        grid=(rows // ROW_GROUP,),
        in_specs=[pl.BlockSpec((ROW_GROUP, D), lambda i: (i, 0)),
                  pl.BlockSpec((None, 6, D), lambda i: (_mod_row(i, gpb, lat_groups, n_batch), 0, 0))],
        out_specs=pl.BlockSpec((ROW_GROUP, D), lambda i: (i, 0)),
        compiler_params=_cparams("arbitrary"),
    )(xs, mod)


def _mm_kernel(a_ref, w_ref, o_ref, *, act):
    acc = jnp.dot(a_ref[...], w_ref[...], preferred_element_type=F32)
    if act == "sigmoid":
        acc = jax.nn.sigmoid(acc)
    o_ref[...] = acc.astype(o_ref.dtype)


def _matmul(a, w, out_dtype, tn, act=None):
    M, K = a.shape
    N = w.shape[1]
    tm = _pick(M, (1024, 512, 256))
    return pl.pallas_call(
        functools.partial(_mm_kernel, act=act),
        out_shape=jax.ShapeDtypeStruct((M, N), out_dtype),
        grid=(M // tm, N // tn),
        in_specs=[pl.BlockSpec((tm, K), lambda i, j: (i, 0)),
                  pl.BlockSpec((K, tn), lambda i, j: (0, j))],
        out_specs=pl.BlockSpec((tm, tn), lambda i, j: (i, j)),
        compiler_params=_cparams("arbitrary", "arbitrary"),
    )(a, w)


def _qkv_kernel(h_ref, w_ref, cos_ref, sa_ref, sb_ref, o_ref, *, qscale):
    j = pl.program_id(1)
    acc = jnp.dot(h_ref[...], w_ref[...], preferred_element_type=F32)

    @pl.when(j < 2)
    def _():
        cos, sa, sb = cos_ref[...], sa_ref[...], sb_ref[...]
        scale = jnp.where(j == 0, qscale, 1.0).astype(F32)
        for c in range(DA_HEADS):
            u = acc[:, c * LANE:(c + 1) * LANE]
            o_ref[:, c * LANE:(c + 1) * LANE] = (_rope128(u, cos, sa, sb) * scale).astype(BF16)

    @pl.when(j == 2)
    def _():
        o_ref[...] = acc.astype(BF16)


def _qkv_proj(h, w_qkv, tabs):
    M, K = h.shape
    tm = _pick(M, (1024, 512, 256))
    tn = DA_W
    tab_spec = pl.BlockSpec((tm, LANE), lambda i, j: (i, 0))
    return pl.pallas_call(
        functools.partial(_qkv_kernel, qscale=DA_DQK ** -0.5 * LOG2E),
        out_shape=jax.ShapeDtypeStruct((M, 3 * DA_W), BF16),
        grid=(M // tm, 3),
        in_specs=[pl.BlockSpec((tm, K), lambda i, j: (i, 0)),
                  pl.BlockSpec((K, tn), lambda i, j: (0, j)),
                  tab_spec, tab_spec, tab_spec],
        out_specs=pl.BlockSpec((tm, tn), lambda i, j: (i, j)),
        compiler_params=_cparams("arbitrary", "arbitrary"),
    )(h, w_qkv, *tabs)


def _mla_prep_kernel(p_ref, qg_ref, kvg_ref, wuq_ref, wukv_ref, cos_ref, sa_ref, sb_ref,
                     q_ref, k_ref, v_ref, *, qscale):
    p = p_ref[...]
    cos, sa, sb = cos_ref[...], sa_ref[...], sb_ref[...]
    cq = p[:, :MLA_Q_RANK]
    ckv = p[:, MLA_Q_RANK:MLA_Q_RANK + MLA_KV_RANK]
    kr = p[:, MLA_Q_RANK + MLA_KV_RANK:]
    qn = (_rms(cq) * qg_ref[...]).astype(BF16)
    q = jnp.dot(qn, wuq_ref[...], preferred_element_type=F32)
    kvn = (_rms(ckv) * kvg_ref[...]).astype(BF16)
    kv = jnp.dot(kvn, wukv_ref[...], preferred_element_type=F32)
    krr = _rope128(kr, cos, sa, sb).astype(BF16)
    for h in range(MLA_HEADS):
        o = h * MLA_QK_PAD
        q_ref[:, o:o + LANE] = (q[:, o:o + LANE] * qscale).astype(BF16)
        q_ref[:, o + LANE:o + 2 * LANE] = (_rope128(q[:, o + LANE:o + 2 * LANE], cos, sa, sb) * qscale).astype(BF16)
        k_ref[:, o:o + LANE] = kv[:, h * LANE:(h + 1) * LANE].astype(BF16)
        k_ref[:, o + LANE:o + 2 * LANE] = krr
    v_ref[...] = kv[:, MLA_W:].astype(BF16)


def _mla_prep(p, q_g, kv_g, w_uq, w_ukv, tabs):
    M, W = p.shape
    tm = _pick(M, (512, 256))
    row = lambda i: (i, 0)
    full = lambda i: (0, 0)
    qk_w = MLA_HEADS * MLA_QK_PAD
    return pl.pallas_call(
        functools.partial(_mla_prep_kernel, qscale=(MLA_NOPE + MLA_ROPE) ** -0.5 * LOG2E),
        out_shape=(jax.ShapeDtypeStruct((M, qk_w), BF16),
                   jax.ShapeDtypeStruct((M, qk_w), BF16),
                   jax.ShapeDtypeStruct((M, MLA_W), BF16)),
        grid=(M // tm,),
        in_specs=[pl.BlockSpec((tm, W), row),
                  pl.BlockSpec((1, MLA_Q_RANK), full),
                  pl.BlockSpec((1, MLA_KV_RANK), full),
                  pl.BlockSpec(w_uq.shape, full),
                  pl.BlockSpec(w_ukv.shape, full),
                  pl.BlockSpec((tm, LANE), row), pl.BlockSpec((tm, LANE), row), pl.BlockSpec((tm, LANE), row)],
        out_specs=(pl.BlockSpec((tm, qk_w), row), pl.BlockSpec((tm, qk_w), row), pl.BlockSpec((tm, MLA_W), row)),
        compiler_params=_cparams("arbitrary"),
    )(p, q_g, kv_g, w_uq, w_ukv, *tabs)


_NT = (((1,), (1,)), ((), ()))


def _softmax_parts(s):
    e = jnp.exp2(s - jnp.max(s, axis=-1, keepdims=True))
    return e, 1.0 / jnp.sum(e, axis=-1, keepdims=True)


def _diff_attn_kernel(lam_ref, q_ref, k_ref, v_ref, g_ref, o_ref, *, n_lat, lat_blocks, ctx_queries, out_scale):
    qi = pl.program_id(2)
    lam = lam_ref[0]

    def attend(k, v):
        q = q_ref[...].astype(F32)
        lane = lax.broadcasted_iota(jnp.int32, (1, LANE), 1)
        lo = (lane < DA_DQK).astype(F32)
        q_lo = (q * lo).astype(BF16)
        q_hi = (q * (1.0 - lo)).astype(BF16)
        e1, r1 = _softmax_parts(lax.dot_general(q_lo, k, _NT, preferred_element_type=F32))
        e2, r2 = _softmax_parts(lax.dot_general(q_hi, k, _NT, preferred_element_type=F32))
        w = e1 * r1 - e2 * (lam * r2)
        o = jnp.dot(w.astype(BF16), v, preferred_element_type=F32)
        o_ref[...] = (_rms(o) * g_ref[...] * out_scale).astype(BF16)

    @pl.when(qi < lat_blocks)
    def _():
        attend(k_ref[...], v_ref[...])

    @pl.when(qi >= lat_blocks)
    def _():
        if ctx_queries:
            attend(k_ref[n_lat:, :], v_ref[n_lat:, :])
        else:
            o_ref[...] = jnp.zeros(o_ref.shape, o_ref.dtype)


def _diff_attn(qkv, lam, subln_g, n_lat, ctx_queries, lam_init):
    B, T, _ = qkv.shape
    tq = ROW_GROUP
    H = DA_HEADS
    kern = functools.partial(_diff_attn_kernel, n_lat=n_lat, lat_blocks=n_lat // tq,
                             ctx_queries=ctx_queries, out_scale=1.0 - lam_init)
    return pl.pallas_call(
        kern,
        out_shape=jax.ShapeDtypeStruct((B, T, DA_W), BF16),
        grid=(B, H, T // tq),
        in_specs=[pl.BlockSpec(memory_space=pltpu.SMEM),
                  pl.BlockSpec((None, tq, LANE), lambda b, h, i: (b, i, h)),
                  pl.BlockSpec((None, T, LANE), lambda b, h, i: (b, 0, H + h)),
                  pl.BlockSpec((None, T, LANE), lambda b, h, i: (b, 0, 2 * H + h)),
                  pl.BlockSpec((1, DA_DV), lambda b, h, i: (0, 0))],
        out_specs=pl.BlockSpec((None, tq, LANE), lambda b, h, i: (b, i, h)),
        compiler_params=_cparams("arbitrary", "arbitrary", "arbitrary"),
    )(lam, qkv, qkv, qkv, subln_g)


def _mla_attn_kernel(q_ref, k_ref, v_ref, o_ref, *, n_lat, lat_blocks, ctx_queries):
    qi = pl.program_id(2)

    def attend(k, v):
        e, r = _softmax_parts(lax.dot_general(q_ref[...], k, _NT, preferred_element_type=F32))
        o_ref[...] = jnp.dot((e * r).astype(BF16), v, preferred_element_type=F32).astype(BF16)

    @pl.when(qi < lat_blocks)
    def _():
        attend(k_ref[...], v_ref[...])

    @pl.when(qi >= lat_blocks)
    def _():
        if ctx_queries:
            attend(k_ref[n_lat:, :], v_ref[n_lat:, :])
        else:
            o_ref[...] = jnp.zeros(o_ref.shape, o_ref.dtype)


def _mla_attn(q, k, v, n_lat, ctx_queries):
    B, T, _ = q.shape
    tq = ROW_GROUP
    kern = functools.partial(_mla_attn_kernel, n_lat=n_lat, lat_blocks=n_lat // tq, ctx_queries=ctx_queries)
    return pl.pallas_call(
        kern,
        out_shape=jax.ShapeDtypeStruct((B, T, MLA_W), BF16),
        grid=(B, MLA_HEADS, T // tq),
        in_specs=[pl.BlockSpec((None, tq, MLA_QK_PAD), lambda b, h, i: (b, i, h)),
                  pl.BlockSpec((None, T, MLA_QK_PAD), lambda b, h, i: (b, 0, h)),
                  pl.BlockSpec((None, T, MLA_DV), lambda b, h, i: (b, 0, h))],
        out_specs=pl.BlockSpec((None, tq, MLA_DV), lambda b, h, i: (b, i, h)),
        compiler_params=_cparams("arbitrary", "arbitrary", "arbitrary"),
    )(q, k, v)


def _hy_filter_kernel(feat_ref, dec_ref, w1_ref, b1_ref, w2_ref, b2_ref, w3_ref, b3_ref, w4_ref, fr_ref,
                      h_ref, s_ref):
    hp = lax.Precision.HIGHEST
    fr = fr_ref[...]
    h = jnp.sin(fr * (jnp.dot(feat_ref[...], w1_ref[...], precision=hp, preferred_element_type=F32) + b1_ref[...]))
    h = jnp.sin(fr * (jnp.dot(h, w2_ref[...], precision=hp, preferred_element_type=F32) + b2_ref[...]))
    h = jnp.sin(fr * (jnp.dot(h, w3_ref[...], precision=hp, preferred_element_type=F32) + b3_ref[...]))
    h = jnp.dot(h, w4_ref[...], precision=hp, preferred_element_type=F32) * dec_ref[...]
    h_ref[...] = h
    n = h.shape[0]
    row = lax.broadcasted_iota(jnp.int32, (n, 1), 0)
    backward = (pl.program_id(0) % 2).astype(F32)
    keep = 1.0 - jnp.where(row == n - 1, 1.0, 0.0) * backward
    s_ref[...] = jnp.broadcast_to(jnp.sum(jnp.abs(h) * keep, axis=0, keepdims=True), s_ref.shape)


def _hy_filter(feat, dec, w1, b1, w2, b2, w3, b3, w4, fr):
    n = feat.shape[0]
    C = HY_CH
    full = lambda g: (0, 0)
    return pl.pallas_call(
        _hy_filter_kernel,
        out_shape=(jax.ShapeDtypeStruct((2 * HY_ORDER, n, C), F32),
                   jax.ShapeDtypeStruct((2 * HY_ORDER, 8, C), F32)),
        grid=(2 * HY_ORDER,),
        in_specs=[pl.BlockSpec(feat.shape, full), pl.BlockSpec(dec.shape, full),
                  pl.BlockSpec(w1.shape, full), pl.BlockSpec(b1.shape, full),
                  pl.BlockSpec(w2.shape, full), pl.BlockSpec(b2.shape, full),
                  pl.BlockSpec(w3.shape, full), pl.BlockSpec(b3.shape, full),
                  pl.BlockSpec((HY_FFN, C), lambda g: (0, g)), pl.BlockSpec(fr.shape, full)],
        out_specs=(pl.BlockSpec((None, n, C), lambda g: (g, 0, 0)),
                   pl.BlockSpec((None, 8, C), lambda g: (g, 0, 0))),
        compiler_params=_cparams("arbitrary"),
    )(feat, dec, w1, b1, w2, b2, w3, b3, w4, fr)


def _hy_feat(n):
    t = jnp.linspace(0.0, 1.0, n, dtype=F32)[:, None]
    phase = (2.0 * math.pi / n) * jnp.arange(n, dtype=F32)[:, None] * \
        jnp.linspace(1e-4, HY_BANDS - 1, HY_BANDS, dtype=F32)[None, :]
    feat = jnp.concatenate([t, jnp.cos(phase), -jnp.sin(phase)], axis=-1)
    feat = jnp.pad(feat, ((0, 0), (0, HY_FFN - HY_EMB)))
    deltas = jnp.abs(jnp.linspace(HY_MIN_DECAY, HY_MAX_DECAY, HY_CH, dtype=F32))
    return feat, jnp.exp(-t * deltas)


def _hy_circular(h, sums):
    n = h.shape[1]
    h = h.reshape(HY_ORDER, 2, n, HY_CH)
    kc = jnp.concatenate([h[:, 0], jnp.zeros((HY_ORDER, 1, HY_CH), F32), h[:, 1, :n - 1][:, ::-1]], axis=1)
    s = sums.reshape(HY_ORDER, 2, 8, HY_CH)[:, :, 0:1, :]
    return kc, 1.0 / (s[:, 0] + s[:, 1])


def _hy_dwconv_kernel(u_ref, w_ref, b_ref, o_ref):
    u = u_ref[...]
    L = u.shape[0]
    row = lax.broadcasted_iota(jnp.int32, (L, 1), 0)
    up = jnp.where(row == 0, 0.0, pltpu.roll(u, 1, 0))
    dn = jnp.where(row == L - 1, 0.0, pltpu.roll(u, L - 1, 0))
    o_ref[...] = up * w_ref[0:1, :] + u * w_ref[1:2, :] + dn * w_ref[2:3, :] + b_ref[...]


def _hy_dwconv(u, w, b, row_block, length):
    B = u.shape[0]
    cw = 256
    per = HY_CH // cw
    return pl.pallas_call(
        _hy_dwconv_kernel,
        out_shape=jax.ShapeDtypeStruct((3, B, length, HY_CH), F32),
        grid=(B, 3 * per),
        in_specs=[pl.BlockSpec((None, length, cw), lambda bb, j: (bb, row_block, j)),
                  pl.BlockSpec((3, cw), lambda bb, j: (0, j)),
                  pl.BlockSpec((1, cw), lambda bb, j: (0, j))],
        out_specs=pl.BlockSpec((None, None, length, cw), lambda bb, j: (j // per, bb, 0, j % per)),
        compiler_params=_cparams("arbitrary", "arbitrary"),
    )(u, w, b)


def _dft_tables(n):
    S = FFT_S
    M = 2 * n
    N1 = M // S
    H = N1 // 2
    s2 = np.arange(S)[:, None, None]
    k1 = np.arange(N1)[None, :, None]
    s1 = np.arange(N1)[None, None, :]
    ang = -2.0 * np.pi * ((k1 * (S * s1 + s2)) % M) / M
    fr, fi = np.cos(ang), np.sin(ang)
    g1f = np.concatenate([fr, fi], axis=1)
    frh, fih = fr[:, :, :H], fi[:, :, :H]
    g1d = np.concatenate([np.concatenate([frh, -fih], axis=2),
                          np.concatenate([fih, frh], axis=2)], axis=1)
    er = np.transpose(frh, (0, 2, 1)) / M
    ei = -np.transpose(fih, (0, 2, 1)) / M
    g3 = np.concatenate([np.concatenate([er, -ei], axis=2),
                         np.concatenate([ei, er], axis=2)], axis=1)
    a2 = -2.0 * np.pi * ((np.arange(S)[:, None] * np.arange(S)[None, :]) % S) / S
    f2r, f2i = np.cos(a2), np.sin(a2)
    g2 = np.block([[f2r, -f2i], [f2i, f2r]])
    g2i = np.block([[f2r, f2i], [-f2i, f2r]])
    cast = lambda a: jnp.asarray(a, dtype=F32).astype(BF16)
    return cast(g1d), cast(g1f), cast(g2), cast(g2i), cast(g3)


def _fft_s1_kernel(g_ref, x_ref, o_ref, *, ns):
    C = HY_CH
    for s in range(ns):
        xs = x_ref[:, s * C:(s + 1) * C].astype(BF16)
        o_ref[:, s * C:(s + 1) * C] = jnp.dot(g_ref[s], xs, preferred_element_type=F32).astype(o_ref.dtype)


def _fft_s1(g, x):
    P, Ri, W = x.shape
    S, Ro, _ = g.shape
    ns = 8
    return pl.pallas_call(
        functools.partial(_fft_s1_kernel, ns=ns),
        out_shape=jax.ShapeDtypeStruct((P, Ro, W), BF16),
        grid=(S // ns, P),
        in_specs=[pl.BlockSpec((ns, Ro, Ri), lambda j, p: (j, 0, 0)),
                  pl.BlockSpec((None, Ri, ns * HY_CH), lambda j, p: (p, 0, j))],
        out_specs=pl.BlockSpec((None, Ro, ns * HY_CH), lambda j, p: (p, 0, j)),
        compiler_params=_cparams("arbitrary", "arbitrary"),
    )(g, x)


def _fft_s2_filt_kernel(a_ref, g_ref, rn_ref, o_ref, *, nk):
    S = FFT_S
    rn = rn_ref[...]
    for t in range(nk):
        d = jnp.concatenate([a_ref[0, t], a_ref[1, t]], axis=0)
        y = jnp.dot(g_ref[...], d, preferred_element_type=F32)
        o_ref[0, t] = y[:S] * rn
        o_ref[1, t] = y[S:] * rn


def _fft_s2_filt(a, g2, rnorm):
    O, _, N1, S, C = a.shape
    nk = 4
    blk = (None, 2, nk, S, C)
    return pl.pallas_call(
        functools.partial(_fft_s2_filt_kernel, nk=nk),
        out_shape=jax.ShapeDtypeStruct(a.shape, F32),
        grid=(O, N1 // nk),
        in_specs=[pl.BlockSpec(blk, lambda o, j: (o, 0, j, 0, 0)),
                  pl.BlockSpec(g2.shape, lambda o, j: (0, 0)),
                  pl.BlockSpec((None, 1, C), lambda o, j: (o, 0, 0))],
        out_specs=pl.BlockSpec(blk, lambda o, j: (o, 0, j, 0, 0)),
        compiler_params=_cparams("arbitrary", "arbitrary"),
    )(a, g2, rnorm)


def _fft_s2_kernel(a_ref, g_ref, gi_ref, kf_ref, o_ref, *, nk):
    S = FFT_S
    for t in range(nk):
        d = jnp.concatenate([a_ref[0, t], a_ref[1, t]], axis=0)
        y = jnp.dot(g_ref[...], d, preferred_element_type=F32)
        yr, yi = y[:S], y[S:]
        kr, ki = kf_ref[0, t], kf_ref[1, t]
        p = jnp.concatenate([yr * kr - yi * ki, yr * ki + yi * kr], axis=0).astype(BF16)
        b = jnp.dot(gi_ref[...], p, preferred_element_type=F32)
        o_ref[0, t] = b[:S].astype(BF16)
        o_ref[1, t] = b[S:].astype(BF16)


def _fft_s2(a, g2, g2i, kf, order):
    P, _, N1, S, C = a.shape
    nk = 4
    blk = (None, 2, nk, S, C)
    return pl.pallas_call(
        functools.partial(_fft_s2_kernel, nk=nk),
        out_shape=jax.ShapeDtypeStruct(a.shape, BF16),
        grid=(N1 // nk, P),
        in_specs=[pl.BlockSpec(blk, lambda j, p: (p, 0, j, 0, 0)),
                  pl.BlockSpec(g2.shape, lambda j, p: (0, 0)),
                  pl.BlockSpec(g2i.shape, lambda j, p: (0, 0)),
                  pl.BlockSpec(blk, lambda j, p: (order, 0, j, 0, 0))],
        out_specs=pl.BlockSpec(blk, lambda j, p: (p, 0, j, 0, 0)),
        compiler_params=_cparams("arbitrary", "arbitrary"),
    )(a, g2, g2i, kf)


def _fft_s3_kernel(g_ref, b_ref, z_ref, gate_ref, skip_ref, o_ref, *, ns):
    C = HY_CH
    skip = skip_ref[...]
    for s in range(ns):
        sl = slice(s * C, (s + 1) * C)
        y = jnp.dot(g_ref[s], b_ref[:, sl], preferred_element_type=F32)
        o_ref[:, sl] = (gate_ref[:, sl] * (y + skip * z_ref[:, sl])).astype(o_ref.dtype)


def _fft_s3(g3, b, z, gate, skip, out_dtype):
    P, Ri, W = b.shape
    S, Ro, _ = g3.shape
    ns = 8
    dspec = pl.BlockSpec((None, Ro, ns * HY_CH), lambda j, p: (p, 0, j))
    return pl.pallas_call(
        functools.partial(_fft_s3_kernel, ns=ns),
        out_shape=jax.ShapeDtypeStruct((P, Ro, W), out_dtype),
        grid=(S // ns, P),
        in_specs=[pl.BlockSpec((ns, Ro, Ri), lambda j, p: (j, 0, 0)),
                  pl.BlockSpec((None, Ri, ns * HY_CH), lambda j, p: (p, 0, j)),
                  dspec, dspec,
                  pl.BlockSpec((1, HY_CH), lambda j, p: (0, 0))],
        out_specs=dspec,
        compiler_params=_cparams("arbitrary", "arbitrary"),
    )(g3, b, z, gate, skip)


def _dense_tables(m):
    M = 2 * m
    ang = -2.0 * np.pi * ((np.arange(M)[:, None] * np.arange(M)[None, :]) % M) / M
    fr, fi = np.cos(ang), np.sin(ang)
    gk = np.concatenate([fr, fi], axis=0)
    gd = np.block([[fr[:, :m], -fi[:, :m]], [fi[:, :m], fr[:, :m]]])
    er, ei = fr[:m, :] / M, -fi[:m, :] / M
    gi = np.block([[er, -ei], [ei, er]])
    cast = lambda a: jnp.asarray(a, dtype=F32).astype(BF16)
    return cast(gk), cast(gd), cast(gi)


def _dense_spec_kernel(k_ref, g_ref, rn_ref, o_ref):
    o_ref[...] = jnp.dot(g_ref[...], k_ref[...].astype(BF16), preferred_element_type=F32) * rn_ref[...]


def _dense_spec(kc, gk, rnorm):
    O, M, C = kc.shape
    return pl.pallas_call(
        _dense_spec_kernel,
        out_shape=jax.ShapeDtypeStruct((O, 2 * M, C), F32),
        grid=(O,),
        in_specs=[pl.BlockSpec((None, M, C), lambda o: (o, 0, 0)),
                  pl.BlockSpec(gk.shape, lambda o: (0, 0)),
                  pl.BlockSpec((None, 1, C), lambda o: (o, 0, 0))],
        out_specs=pl.BlockSpec((None, 2 * M, C), lambda o: (o, 0, 0)),
        compiler_params=_cparams("arbitrary"),
    )(kc, gk, rnorm)


def _dense_conv_kernel(x_ref, gd_ref, gi_ref, kf_ref, gate_ref, skip_ref, o_ref):
    x = x_ref[...]
    y = jnp.dot(gd_ref[...], x.astype(BF16), preferred_element_type=F32)
    M = y.shape[0] // 2
    yr, yi = y[:M], y[M:]
    kr, ki = kf_ref[:M], kf_ref[M:]
    p = jnp.concatenate([yr * kr - yi * ki, yr * ki + yi * kr], axis=0).astype(BF16)
    conv = jnp.dot(gi_ref[...], p, preferred_element_type=F32)
    o_ref[...] = (gate_ref[...] * (conv + skip_ref[...] * x)).astype(o_ref.dtype)


def _dense_conv(x, gd, gi, kf, order, gate, skip, out_dtype):
    P, R, C = x.shape
    dspec = pl.BlockSpec((None, R, C), lambda p: (p, 0, 0))
    return pl.pallas_call(
        _dense_conv_kernel,
        out_shape=jax.ShapeDtypeStruct((P, R, C), out_dtype),
        grid=(P,),
        in_specs=[dspec,
                  pl.BlockSpec(gd.shape, lambda p: (0, 0)),
                  pl.BlockSpec(gi.shape, lambda p: (0, 0)),
                  pl.BlockSpec((None,) + kf.shape[1:], lambda p: (order, 0, 0)),
                  dspec,
                  pl.BlockSpec((1, C), lambda p: (0, 0))],
        out_specs=dspec,
        compiler_params=_cparams("arbitrary"),
    )(x, gd, gi, kf, gate, skip)


def _merge_kernel(oa_ref, ob_ref, oc_ref, wa_ref, wb_ref, wc_ref, g0_ref, g1_ref, g2_ref, o_ref):
    ya = jnp.dot(oa_ref[...], wa_ref[...], preferred_element_type=F32)
    yb = jnp.dot(ob_ref[...], wb_ref[...], preferred_element_type=F32)
    yc = jnp.dot(oc_ref[...], wc_ref[...], preferred_element_type=F32)
    m = g0_ref[...].astype(F32) * ya + g1_ref[...].astype(F32) * yb + g2_ref[...].astype(F32) * yc
    o_ref[...] = m.astype(BF16)


def _merge(oa, ob, oc, w_ba, w_bb, w_bc, gates):
    M = oa.shape[0]
    D = D_MODEL
    tm = _pick(M, (1024, 512, 256))
    tn = 512
    nb = D // tn
    row = lambda i, j: (i, 0)
    col = lambda i, j: (0, j)
    return pl.pallas_call(
        _merge_kernel,
        out_shape=jax.ShapeDtypeStruct((M, D), BF16),
        grid=(M // tm, nb),
        in_specs=[pl.BlockSpec((tm, DA_W), row), pl.BlockSpec((tm, MLA_W), row), pl.BlockSpec((tm, HY_CH), row),
                  pl.BlockSpec((DA_W, tn), col), pl.BlockSpec((MLA_W, tn), col), pl.BlockSpec((HY_CH, tn), col),
                  pl.BlockSpec((tm, tn), lambda i, j: (i, j)),
                  pl.BlockSpec((tm, tn), lambda i, j: (i, nb + j)),
                  pl.BlockSpec((tm, tn), lambda i, j: (i, 2 * nb + j))],
        out_specs=pl.BlockSpec((tm, tn), lambda i, j: (i, j)),
        compiler_params=_cparams("arbitrary", "arbitrary"),
    )(oa, ob, oc, w_ba, w_bb, w_bc, gates, gates, gates)


def _wo_ln_kernel(m_ref, w_ref, xs_ref, mod_ref, g_ref, b_ref, xs1_ref, h2_ref, *, geom, groups):
    gpb, lat_groups, n_batch = geom
    y = jnp.dot(m_ref[...], w_ref[...], preferred_element_type=F32)
    g, b = g_ref[...], b_ref[...]
    for q in range(groups):
        r = _mod_row(pl.program_id(0) * groups + q, gpb, lat_groups, n_batch)
        mod = mod_ref[r]
        sl = slice(q * ROW_GROUP, (q + 1) * ROW_GROUP)
        x1 = _ln(DEEPNORM_ALPHA * xs_ref[sl, :] + mod[2:3, :] * y[sl, :]) * g + b
        xs1_ref[sl, :] = x1
        h2_ref[sl, :] = (_ln(x1) * (1.0 + mod[4:5, :]) + mod[3:4, :]).astype(BF16)


def _wo_ln(merged, w_o, xs, mod, ln_g, ln_b, geom):
    M, D = xs.shape
    tm = _pick(M, (512, 256))
    row = lambda i: (i, 0)
    full2 = lambda i: (0, 0)
    return pl.pallas_call(
        functools.partial(_wo_ln_kernel, geom=geom, groups=tm // ROW_GROUP),
        out_shape=(jax.ShapeDtypeStruct((M, D), F32), jax.ShapeDtypeStruct((M, D), BF16)),
        grid=(M // tm,),
        in_specs=[pl.BlockSpec((tm, D), row),
                  pl.BlockSpec((D, D), full2),
                  pl.BlockSpec((tm, D), row),
                  pl.BlockSpec(mod.shape, lambda i: (0, 0, 0)),
                  pl.BlockSpec((1, D), full2), pl.BlockSpec((1, D), full2)],
        out_specs=(pl.BlockSpec((tm, D), row), pl.BlockSpec((tm, D), row)),
        compiler_params=_cparams("arbitrary"),
    )(merged, w_o, xs, mod, ln_g, ln_b)


FFN_HALO = 16


def _ffn_up_kernel(hp_ref, h_ref, hn_ref, wa_ref, wv_ref, cw_ref, cb_ref, kp_ref, kn_ref, o_ref):
    tm, tn = o_ref.shape
    hm = h_ref[...]
    hext = jnp.concatenate([hp_ref[...], hm, hn_ref[...]], axis=0)
    a = jnp.dot(hext, wa_ref[...], preferred_element_type=F32)
    v = jnp.dot(hm, wv_ref[...], preferred_element_type=F32)
    ext = tm + 2 * FFN_HALO
    rep = tn // LANE
    keep_prev = jnp.tile(kp_ref[...], (1, rep))
    keep_next = jnp.tile(kn_ref[...], (1, rep))
    a_prev = pltpu.roll(a, 1, 0)[FFN_HALO:FFN_HALO + tm] * keep_prev
    a_next = pltpu.roll(a, ext - 1, 0)[FFN_HALO:FFN_HALO + tm] * keep_next
    cv = a_prev * cw_ref[0:1, :] + a[FFN_HALO:FFN_HALO + tm] * cw_ref[1:2, :] + a_next * cw_ref[2:3, :] + cb_ref[...]
    o_ref[...] = (cv * jax.nn.sigmoid(cv) * v).astype(BF16)


def _ffn_up(h2, w_up, conv_w, conv_b, keep_prev, keep_next):
    M, D = h2.shape
    tm = _pick(M, (1024, 512, 256))
    tn = 512
    nb = D_FF // tn
    hb = tm // FFN_HALO
    last = M // FFN_HALO - 1
    return pl.pallas_call(
        _ffn_up_kernel,
        out_shape=jax.ShapeDtypeStruct((M, D_FF), BF16),
        grid=(M // tm, nb),
        in_specs=[pl.BlockSpec((FFN_HALO, D), lambda i, j: (jnp.maximum(i * hb - 1, 0), 0)),
                  pl.BlockSpec((tm, D), lambda i, j: (i, 0)),
                  pl.BlockSpec((FFN_HALO, D), lambda i, j: (jnp.minimum((i + 1) * hb, last), 0)),
                  pl.BlockSpec((D, tn), lambda i, j: (0, j)),
                  pl.BlockSpec((D, tn), lambda i, j: (0, nb + j)),
                  pl.BlockSpec((3, tn), lambda i, j: (0, j)),
                  pl.BlockSpec((1, tn), lambda i, j: (0, j)),
                  pl.BlockSpec((tm, LANE), lambda i, j: (i, 0)),
                  pl.BlockSpec((tm, LANE), lambda i, j: (i, 0))],
        out_specs=pl.BlockSpec((tm, tn), lambda i, j: (i, j)),
        compiler_params=_cparams("arbitrary", "arbitrary"),
    )(h2, h2, h2, w_up, w_up, conv_w, conv_b, keep_prev, keep_next)


def _ffn_down_kernel(u_ref, w_ref, xs_ref, mod_ref, g_ref, b_ref, o_ref, acc_ref, *, geom, groups, nk):
    gpb, lat_groups, n_batch = geom
    k = pl.program_id(1)

    @pl.when(k == 0)
    def _():
        acc_ref[...] = jnp.zeros(acc_ref.shape, F32)

    acc_ref[...] += jnp.dot(u_ref[...], w_ref[...], preferred_element_type=F32)

    @pl.when(k == nk - 1)
    def _():
        g, b = g_ref[...], b_ref[...]
        for q in range(groups):
            r = _mod_row(pl.program_id(0) * groups + q, gpb, lat_groups, n_batch)
            mod = mod_ref[r]
            sl = slice(q * ROW_GROUP, (q + 1) * ROW_GROUP)
            o_ref[sl, :] = _ln(DEEPNORM_ALPHA * xs_ref[sl, :] + mod[5:6, :] * acc_ref[sl, :]) * g + b


def _ffn_down(u, w_down, xs1, mod, ln_g, ln_b, geom):
    M, D = xs1.shape
    tm = _pick(M, (512, 256))
    tk = 512
    nk = D_FF // tk
    row = lambda i, k: (i, 0)
    full2 = lambda i, k: (0, 0)
    return pl.pallas_call(
        functools.partial(_ffn_down_kernel, geom=geom, groups=tm // ROW_GROUP, nk=nk),
        out_shape=jax.ShapeDtypeStruct((M, D), F32),
        grid=(M // tm, nk),
        in_specs=[pl.BlockSpec((tm, tk), lambda i, k: (i, k)),
                  pl.BlockSpec((tk, D), lambda i, k: (k, 0)),
                  pl.BlockSpec((tm, D), row),
                  pl.BlockSpec(mod.shape, lambda i, k: (0, 0, 0)),
                  pl.BlockSpec((1, D), full2), pl.BlockSpec((1, D), full2)],
        out_specs=pl.BlockSpec((tm, D), row),
        scratch_shapes=[pltpu.VMEM((tm, D), F32)],
        compiler_params=_cparams("arbitrary", "arbitrary"),
    )(u, w_down, xs1, mod, ln_g, ln_b)


def _rope_tables(B, n, nc):
    half = DA_DQK // 2
    inv = ROPE_BASE ** (-jnp.arange(0, half, 2, dtype=F32) / half)
    t = jnp.arange(n, dtype=jnp.int32)
    ang_r = (t // GRID_W).astype(F32)[:, None] * inv[None, :]
    ang_c = (t % GRID_W).astype(F32)[:, None] * inv[None, :]
    ang = jnp.concatenate([ang_r, ang_r, ang_c, ang_c], axis=-1)
    cos, sin = jnp.cos(ang), jnp.sin(ang)
    upper = (jnp.arange(DA_DQK) % half) >= half // 2
    sa = jnp.where(upper, sin, 0.0)
    sb = jnp.where(upper, 0.0, -sin)

    def full(tab, fill):
        tab = jnp.concatenate([tab, jnp.full((nc, DA_DQK), fill, F32)], axis=0)
        tab = jnp.tile(tab, (B, LANE // DA_DQK))
        return tab

    return full(cos, 1.0), full(sa, 0.0), full(sb, 0.0)


def _conv_masks(B, n, nc):
    T = n + nc
    t = jnp.arange(T)
    keep_prev = ((t != 0) & (t != n)).astype(F32)
    keep_next = ((t != n - 1) & (t != T - 1)).astype(F32)
    widen = lambda m: jnp.tile(m[:, None], (B, LANE))
    return widen(keep_prev), widen(keep_next)


def _split_w_in(w):
    D = w.shape[0]
    o = 0
    def take(width):
        nonlocal o
        part = w[:, o:o + width]
        o += width
        return part
    regroup = lambda p: p.reshape(D, 2, DA_HEADS, DA_DQK).transpose(0, 2, 1, 3).reshape(D, DA_W)
    wq, wk, wv = regroup(take(DA_W)), regroup(take(DA_W)), take(DA_W)
    wcq, wckv, wkr = take(MLA_Q_RANK), take(MLA_KV_RANK), take(MLA_ROPE)
    why, wg = take(3 * HY_CH), take(N_BRANCH * D_MODEL)
    w_qkv = jnp.concatenate([wq, wk, wv], axis=1).astype(BF16)
    w_mla = jnp.concatenate([wcq, wckv, wkr, jnp.zeros((D, LANE - MLA_ROPE), w.dtype)], axis=1).astype(BF16)
    return w_qkv, w_mla, why.astype(BF16), wg.astype(BF16)


def _pad_w_uq(w):
    w = w.reshape(MLA_Q_RANK, MLA_HEADS, MLA_NOPE + MLA_ROPE)
    w = jnp.pad(w, ((0, 0), (0, 0), (0, MLA_QK_PAD - MLA_NOPE - MLA_ROPE)))
    return w.reshape(MLA_Q_RANK, MLA_HEADS * MLA_QK_PAD).astype(BF16)


def _split_w_ukv(w):
    w = w.reshape(MLA_KV_RANK, MLA_HEADS, 2, MLA_NOPE).transpose(0, 2, 1, 3)
    return w.reshape(MLA_KV_RANK, 2 * MLA_W).astype(BF16)


def _hyena(u3, layer, p, n, nc, with_ctx, tables):
    B = u3.shape[0]
    C = HY_CH
    P = B // 2
    g1d, g1f, g2, g2i, g3, gk, gd, gi = tables
    S = FFT_S
    N1 = 2 * n // S
    mlp = (p["hy_ffn_w1p"][layer], p["hy_ffn_b1"][layer][None], p["hy_ffn_w2"][layer], p["hy_ffn_b2"][layer][None],
           p["hy_ffn_w3"][layer], p["hy_ffn_b3"][layer][None], p["hy_ffn_w4"][layer], p["hy_freq"][layer][None])
    skip = p["hy_skip"][layer]
    cw, cb = p["hy_conv_w"][layer], p["hy_conv_b"][layer][None]

    kc, rnorm = _hy_circular(*_hy_filter(*_hy_feat(n), *mlp))
    kf = _fft_s1(g1f, kc.reshape(HY_ORDER, N1, S * C))
    kf = _fft_s2_filt(kf.reshape(HY_ORDER, 2, N1, S, C), g2, rnorm)
    dw = _hy_dwconv(u3, cw, cb, 0, n).reshape(3, P, N1, S * C)
    z = dw[0]
    for o in range(HY_ORDER):
        a = _fft_s1(g1d, z).reshape(P, 2, N1, S, C)
        b = _fft_s2(a, g2, g2i, kf, o).reshape(P, 2 * N1, S * C)
        z = _fft_s3(g3, b, z, dw[1 + o], skip[o][None], F32 if o + 1 < HY_ORDER else BF16)
    oc_lat = z.reshape(B, n, C)

    if with_ctx:
        kcc, rnc = _hy_circular(*_hy_filter(*_hy_feat(nc), *mlp))
        kfc = _dense_spec(kcc, gk, rnc)
        dwc = _hy_dwconv(u3, cw, cb, n // nc, nc).reshape(3, P, 2 * nc, C)
        zc = dwc[0]
        for o in range(HY_ORDER):
            zc = _dense_conv(zc, gd, gi, kfc, o, dwc[1 + o], skip[o][None], F32 if o + 1 < HY_ORDER else BF16)
        oc_ctx = zc.reshape(B, nc, C)
    else:
        oc_ctx = jnp.zeros((B, nc, C), BF16)
    return jnp.concatenate([oc_lat, oc_ctx], axis=1)


def kernel(x, c, ctx, c_ctx, ada_w, ada_b, w_in, da_lambda, da_subln_g, mla_q_g, mla_w_uq, mla_kv_g, mla_w_ukv, hy_conv_w, hy_conv_b, hy_ffn_w1, hy_ffn_b1, hy_ffn_w2, hy_ffn_b2, hy_ffn_w3, hy_ffn_b3, hy_ffn_w4, hy_freq, hy_skip, w_branch_a, w_branch_b, w_branch_c, w_out, ln1_g, ln1_b, ffn_w_up, ffn_conv_w, ffn_conv_b, ffn_w_down, ln2_g, ln2_b):
    B, n, D = x.shape
    nc = ctx.shape[1]
    T = n + nc
    rows = B * T
    assert D == D_MODEL and B % 2 == 0 and B < 8
    assert n % ROW_GROUP == 0 and nc % ROW_GROUP == 0 and n % nc == 0 and n % GRID_W == 0
    assert (2 * n) % (8 * FFT_S) == 0
    geom = (T // ROW_GROUP, n // ROW_GROUP, B)

    hy = dict(hy_ffn_w1p=jnp.pad(hy_ffn_w1, ((0, 0), (0, HY_FFN - HY_EMB), (0, 0))), hy_ffn_b1=hy_ffn_b1,
              hy_ffn_w2=hy_ffn_w2, hy_ffn_b2=hy_ffn_b2, hy_ffn_w3=hy_ffn_w3, hy_ffn_b3=hy_ffn_b3,
              hy_ffn_w4=hy_ffn_w4, hy_freq=hy_freq, hy_skip=hy_skip, hy_conv_w=hy_conv_w, hy_conv_b=hy_conv_b)
    tables = _dft_tables(n) + _dense_tables(nc)
    rope = _rope_tables(B, n, nc)
    keep_prev, keep_next = _conv_masks(B, n, nc)

    cc = jnp.concatenate([c, c_ctx[None], jnp.zeros((8 - B - 1, D), F32)], axis=0)
    mods = _ada(cc, ada_w, ada_b[:, None, :]).reshape(DEPTH, 8, 6, D)

    xs = jnp.concatenate([x, ctx], axis=1).reshape(rows, D)
    for i in range(DEPTH):
        last = i == DEPTH - 1
        lam_init = 0.8 - 0.6 * math.exp(-0.3 * i)
        lq1, lk1, lq2, lk2 = da_lambda[i].astype(F32)
        lam = (jnp.exp(jnp.sum(lq1 * lk1)) - jnp.exp(jnp.sum(lq2 * lk2)) + lam_init).reshape(1)
        mod = mods[i]
        w_qkv, w_mla, w_hy, w_g = _split_w_in(w_in[i])

        h = _lnmod(xs, mod, geom)
        qkv = _qkv_proj(h, w_qkv, rope)
        p_mla = _matmul(h, w_mla, F32, w_mla.shape[1])
        u_hy = _matmul(h, w_hy, F32, 768)
        gates = _matmul(h, w_g, BF16, 1024, act="sigmoid")

        q_m, k_m, v_m = _mla_prep(p_mla, mla_q_g[i][None], mla_kv_g[i][None], _pad_w_uq(mla_w_uq[i]),
                                  _split_w_ukv(mla_w_ukv[i]), rope)
        oa = _diff_attn(qkv.reshape(B, T, 3 * DA_W), lam, da_subln_g[i][None], n, not last, lam_init)
        ob = _mla_attn(q_m.reshape(B, T, -1), k_m.reshape(B, T, -1), v_m.reshape(B, T, -1), n, not last)
        oc = _hyena(u_hy.reshape(B, T, 3 * HY_CH), i, hy, n, nc, not last, tables)

        merged = _merge(oa.reshape(rows, DA_W), ob.reshape(rows, MLA_W), oc.reshape(rows, HY_CH),
                        w_branch_a[i].astype(BF16), w_branch_b[i].astype(BF16), w_branch_c[i].astype(BF16), gates)
        xs1, h2 = _wo_ln(merged, w_out[i].astype(BF16), xs, mod, ln1_g[i][None], ln1_b[i][None], geom)
        u = _ffn_up(h2, ffn_w_up[i].astype(BF16), ffn_conv_w[i], ffn_conv_b[i][None], keep_prev, keep_next)
        xs = _ffn_down(u, ffn_w_down[i].astype(BF16), xs1, mod, ln2_g[i][None], ln2_b[i][None], geom)
    return xs.reshape(B, T, D)[:, :n]
```

```python
import functools
import math

import numpy as np
import jax
import jax.numpy as jnp
from jax import lax
from jax.experimental import pallas as pl
from jax.experimental.pallas import tpu as pltpu

F32 = jnp.float32
BF16 = jnp.bfloat16

D_MODEL = 2048
DEPTH = 2
GRID_W = 64
ROPE_BASE = 10000.0
NORM_EPS = 1e-6
DA_HEADS = 6
DA_DQK = 64
DA_DV = 128
DA_W = DA_HEADS * DA_DV
MLA_HEADS = 6
MLA_Q_RANK = 512
MLA_KV_RANK = 256
MLA_NOPE = 128
MLA_ROPE = 64
MLA_DV = 128
MLA_W = MLA_HEADS * MLA_DV
MLA_QK_PAD = 256
HY_CH = 512
HY_ORDER = 2
HY_EMB = 33
HY_BANDS = (HY_EMB - 1) // 2
HY_FFN = 64
HY_MIN_DECAY = math.log(1e-2) / 1.5
HY_MAX_DECAY = math.log(1e-2) / 0.3
D_FF = 5632
N_BRANCH = 3
DEEPNORM_ALPHA = (2 * DEPTH) ** 0.25
LOG2E = 1.4426950408889634

ROW_GROUP = 256
LANE = 128
FFT_S = 128
VMEM_LIMIT = 52 * 1024 * 1024


def _cparams(*sem):
    return pltpu.CompilerParams(dimension_semantics=sem, vmem_limit_bytes=VMEM_LIMIT)


def _pick(total, prefs):
    for p in prefs:
        if total % p == 0:
            return p
    raise ValueError(f"no tile for {total} in {prefs}")


def _ln(x):
    mu = jnp.mean(x, axis=-1, keepdims=True)
    xc = x - mu
    var = jnp.mean(xc * xc, axis=-1, keepdims=True)
    return xc * lax.rsqrt(var + NORM_EPS)


def _rms(x):
    return x * lax.rsqrt(jnp.mean(x * x, axis=-1, keepdims=True) + NORM_EPS)


def _rope128(u, cos, sa, sb):
    return u * cos + pltpu.roll(u, 16, 1) * sa + pltpu.roll(u, LANE - 16, 1) * sb


def _ada_kernel(c_ref, w_ref, b_ref, o_ref):
    a = c_ref[...]
    a = a * jax.nn.sigmoid(a)
    o_ref[...] = jnp.dot(a.astype(BF16), w_ref[...].astype(BF16), preferred_element_type=F32) + b_ref[...]


def _ada(cc, ada_w, ada_b):
    L, D, N = ada_w.shape
    tn = 1024
    return pl.pallas_call(
        _ada_kernel,
        out_shape=jax.ShapeDtypeStruct((L, 8, N), F32),
        grid=(L, N // tn),
        in_specs=[pl.BlockSpec((8, D), lambda l, j: (0, 0)),
                  pl.BlockSpec((None, D, tn), lambda l, j: (l, 0, j)),
                  pl.BlockSpec((None, 1, tn), lambda l, j: (l, 0, j))],
        out_specs=pl.BlockSpec((None, 8, tn), lambda l, j: (l, 0, j)),
        compiler_params=_cparams("arbitrary", "arbitrary"),
    )(cc, ada_w, ada_b)


def _lnmod_kernel(x_ref, m_ref, o_ref):
    y = _ln(x_ref[...])
    o_ref[...] = (y * (1.0 + m_ref[1:2, :]) + m_ref[0:1, :]).astype(BF16)


def _mod_row(g, gpb, lat_groups, n_batch):
    return jnp.where(g % gpb < lat_groups, g // gpb, n_batch)


def _lnmod(xs, mod, geom):
    rows, D = xs.shape
    gpb, lat_groups, n_batch = geom
    return pl.pallas_call(
        _lnmod_kernel,
        out_shape=jax.ShapeDtypeStruct((rows, D), BF16),
        grid=(rows // ROW_GROUP,),
        in_specs=[pl.BlockSpec((ROW_GROUP, D), lambda i: (i, 0)),
                  pl.BlockSpec((None, 6, D), lambda i: (_mod_row(i, gpb, lat_groups, n_batch), 0, 0))],
        out_specs=pl.BlockSpec((ROW_GROUP, D), lambda i: (i, 0)),
        compiler_params=_cparams("arbitrary"),
    )(xs, mod)


def _mm_kernel(a_ref, w_ref, o_ref, *, act):
    acc = jnp.dot(a_ref[...], w_ref[...], preferred_element_type=F32)
    if act == "sigmoid":
        acc = jax.nn.sigmoid(acc)
    o_ref[...] = acc.astype(o_ref.dtype)


def _matmul(a, w, out_dtype, tn, act=None):
    M, K = a.shape
    N = w.shape[1]
    tm = _pick(M, (1024, 512, 256))
    return pl.pallas_call(
        functools.partial(_mm_kernel, act=act),
        out_shape=jax.ShapeDtypeStruct((M, N), out_dtype),
        grid=(M // tm, N // tn),
        in_specs=[pl.BlockSpec((tm, K), lambda i, j: (i, 0)),
                  pl.BlockSpec((K, tn), lambda i, j: (0, j))],
        out_specs=pl.BlockSpec((tm, tn), lambda i, j: (i, j)),
        compiler_params=_cparams("arbitrary", "arbitrary"),
    )(a, w)


def _qkv_kernel(h_ref, w_ref, cos_ref, sa_ref, sb_ref, o_ref, *, qscale):
    j = pl.program_id(1)
    acc = jnp.dot(h_ref[...], w_ref[...], preferred_element_type=F32)

    @pl.when(j < 2)
    def _():
        cos, sa, sb = cos_ref[...], sa_ref[...], sb_ref[...]
        scale = jnp.where(j == 0, qscale, 1.0).astype(F32)
        for c in range(DA_HEADS):
            u = acc[:, c * LANE:(c + 1) * LANE]
            o_ref[:, c * LANE:(c + 1) * LANE] = (_rope128(u, cos, sa, sb) * scale).astype(BF16)

    @pl.when(j == 2)
    def _():
        o_ref[...] = acc.astype(BF16)


def _qkv_proj(h, w_qkv, tabs):
    M, K = h.shape
    tm = _pick(M, (1024, 512, 256))
    tn = DA_W
    tab_spec = pl.BlockSpec((tm, LANE), lambda i, j: (i, 0))
    return pl.pallas_call(
        functools.partial(_qkv_kernel, qscale=DA_DQK ** -0.5 * LOG2E),
        out_shape=jax.ShapeDtypeStruct((M, 3 * DA_W), BF16),
        grid=(M // tm, 3),
        in_specs=[pl.BlockSpec((tm, K), lambda i, j: (i, 0)),
                  pl.BlockSpec((K, tn), lambda i, j: (0, j)),
                  tab_spec, tab_spec, tab_spec],
        out_specs=pl.BlockSpec((tm, tn), lambda i, j: (i, j)),
        compiler_params=_cparams("arbitrary", "arbitrary"),
    )(h, w_qkv, *tabs)


def _mla_prep_kernel(p_ref, qg_ref, kvg_ref, wuq_ref, wukv_ref, cos_ref, sa_ref, sb_ref,
                     q_ref, k_ref, v_ref, *, qscale):
    p = p_ref[...]
    cos, sa, sb = cos_ref[...], sa_ref[...], sb_ref[...]
    cq = p[:, :MLA_Q_RANK]
    ckv = p[:, MLA_Q_RANK:MLA_Q_RANK + MLA_KV_RANK]
    kr = p[:, MLA_Q_RANK + MLA_KV_RANK:]
    qn = (_rms(cq) * qg_ref[...]).astype(BF16)
    q = jnp.dot(qn, wuq_ref[...], preferred_element_type=F32)
    kvn = (_rms(ckv) * kvg_ref[...]).astype(BF16)
    kv = jnp.dot(kvn, wukv_ref[...], preferred_element_type=F32)
    krr = _rope128(kr, cos, sa, sb).astype(BF16)
    for h in range(MLA_HEADS):
        o = h * MLA_QK_PAD
        q_ref[:, o:o + LANE] = (q[:, o:o + LANE] * qscale).astype(BF16)
        q_ref[:, o + LANE:o + 2 * LANE] = (_rope128(q[:, o + LANE:o + 2 * LANE], cos, sa, sb) * qscale).astype(BF16)
        k_ref[:, o:o + LANE] = kv[:, h * LANE:(h + 1) * LANE].astype(BF16)
        k_ref[:, o + LANE:o + 2 * LANE] = krr
    v_ref[...] = kv[:, MLA_W:].astype(BF16)


def _mla_prep(p, q_g, kv_g, w_uq, w_ukv, tabs):
    M, W = p.shape
    tm = _pick(M, (512, 256))
    row = lambda i: (i, 0)
    full = lambda i: (0, 0)
    qk_w = MLA_HEADS * MLA_QK_PAD
    return pl.pallas_call(
        functools.partial(_mla_prep_kernel, qscale=(MLA_NOPE + MLA_ROPE) ** -0.5 * LOG2E),
        out_shape=(jax.ShapeDtypeStruct((M, qk_w), BF16),
                   jax.ShapeDtypeStruct((M, qk_w), BF16),
                   jax.ShapeDtypeStruct((M, MLA_W), BF16)),
        grid=(M // tm,),
        in_specs=[pl.BlockSpec((tm, W), row),
                  pl.BlockSpec((1, MLA_Q_RANK), full),
                  pl.BlockSpec((1, MLA_KV_RANK), full),
                  pl.BlockSpec(w_uq.shape, full),
                  pl.BlockSpec(w_ukv.shape, full),
                  pl.BlockSpec((tm, LANE), row), pl.BlockSpec((tm, LANE), row), pl.BlockSpec((tm, LANE), row)],
        out_specs=(pl.BlockSpec((tm, qk_w), row), pl.BlockSpec((tm, qk_w), row), pl.BlockSpec((tm, MLA_W), row)),
        compiler_params=_cparams("arbitrary"),
    )(p, q_g, kv_g, w_uq, w_ukv, *tabs)


_NT = (((1,), (1,)), ((), ()))
ATTN_TQ = 256
ATTN_CHAINS = 2
ATTN_TK = 256


def _skewed_pipeline(n_chains, n_chunks, stages):
    for t in range(n_chains + len(stages) - 1):
        active = [(s, t - s) for s in range(len(stages)) if 0 <= t - s < n_chains]
        for c in range(n_chunks):
            for s, chain in active:
                stages[s][0](chain, c)
        for s, chain in active:
            stages[s][1](chain)


def _acc(old, new, op):
    return new if old is None else op(old, new)


def _lane_halves(x, op):
    return op(x[:, :LANE], x[:, LANE:])


def _diff_attn_kernel(lam_ref, q_ref, k_ref, v_ref, g_ref, *rest, out_scale):
    o_ref = rest[-1]
    lam, g = lam_ref[0], g_ref[...]
    tq = min(ATTN_TQ, q_ref.shape[0])
    n_chains = q_ref.shape[0] // tq
    n_chunks = k_ref.shape[0] // ATTN_TK
    lane = lax.broadcasted_iota(jnp.int32, (1, LANE), 1)
    lo = (lane < DA_DQK).astype(F32)
    st = [dict(s=[], e=[], mx=[None, None], l=[None, None], o=None) for _ in range(n_chains)]
    rows = lambda i: slice(i * tq, (i + 1) * tq)
    keys = lambda c: slice(c * ATTN_TK, (c + 1) * ATTN_TK)

    def qk_chunk(i, c):
        d = st[i]
        if c == 0:
            qf = q_ref[rows(i), :].astype(F32)
            d["q"] = ((qf * lo).astype(BF16), (qf * (1.0 - lo)).astype(BF16))
        k = k_ref[keys(c), :]
        pair = []
        for m in range(2):
            s = lax.dot_general(d["q"][m], k, _NT, preferred_element_type=F32)
            d["mx"][m] = _acc(d["mx"][m], _lane_halves(s, jnp.maximum), jnp.maximum)
            pair.append(s)
        d["s"].append(pair)

    def qk_done(i):
        st[i]["m"] = [jnp.max(mx, axis=-1, keepdims=True) for mx in st[i]["mx"]]

    def exp_chunk(i, c):
        d = st[i]
        pair = []
        for m in range(2):
            e = jnp.exp2(d["s"][c][m] - d["m"][m])
            d["l"][m] = _acc(d["l"][m], _lane_halves(e, jnp.add), jnp.add)
            pair.append(e)
        d["s"][c] = None
        d["e"].append(pair)

    def exp_done(i):
        d = st[i]
        l1, l2 = [jnp.sum(l, axis=-1, keepdims=True) for l in d["l"]]
        d["r1"] = 1.0 / l1
        d["cf"] = lam * l1 / l2

    def pv_chunk(i, c):
        d = st[i]
        w = (d["e"][c][0] - d["cf"] * d["e"][c][1]).astype(BF16)
        d["e"][c] = None
        d["o"] = _acc(d["o"], jnp.dot(w, v_ref[keys(c), :], preferred_element_type=F32), jnp.add)

    def pv_done(i):
        o = st[i]["o"] * st[i]["r1"]
        o_ref[rows(i), :] = (_rms(o) * g * out_scale).astype(BF16)

    _skewed_pipeline(n_chains, n_chunks, [(qk_chunk, qk_done), (exp_chunk, exp_done), (pv_chunk, pv_done)])


def _mla_attn_kernel(q_ref, k_ref, v_ref, *rest):
    o_ref = rest[-1]
    tq = min(ATTN_TQ, q_ref.shape[0])
    n_chains = q_ref.shape[0] // tq
    n_chunks = k_ref.shape[0] // ATTN_TK
    st = [dict(s=[], mx=None, l=None, o=None) for _ in range(n_chains)]
    rows = lambda i: slice(i * tq, (i + 1) * tq)
    keys = lambda c: slice(c * ATTN_TK, (c + 1) * ATTN_TK)

    def qk_chunk(i, c):
        d = st[i]
        s = lax.dot_general(q_ref[rows(i), :], k_ref[keys(c), :], _NT, preferred_element_type=F32)
        d["mx"] = _acc(d["mx"], _lane_halves(s, jnp.maximum), jnp.maximum)
        d["s"].append(s)

    def qk_done(i):
        st[i]["m"] = jnp.max(st[i]["mx"], axis=-1, keepdims=True)

    def pv_chunk(i, c):
        d = st[i]
        e = jnp.exp2(d["s"][c] - d["m"])
        d["s"][c] = None
        d["l"] = _acc(d["l"], _lane_halves(e, jnp.add), jnp.add)
        d["o"] = _acc(d["o"], jnp.dot(e.astype(BF16), v_ref[keys(c), :], preferred_element_type=F32), jnp.add)

    def pv_done(i):
        d = st[i]
        o_ref[rows(i), :] = (d["o"] * (1.0 / jnp.sum(d["l"], axis=-1, keepdims=True))).astype(BF16)

    _skewed_pipeline(n_chains, n_chunks, [(qk_chunk, qk_done), (pv_chunk, pv_done)])


def _zero_rows_kernel(a_ref, o_ref):
    o_ref[...] = jnp.zeros(o_ref.shape, o_ref.dtype)


def _zero_rows(arr, row_block, length):
    B, _, W = arr.shape
    return pl.pallas_call(
        _zero_rows_kernel,
        out_shape=jax.ShapeDtypeStruct(arr.shape, arr.dtype),
        grid=(B,),
        in_specs=[pl.BlockSpec(memory_space=pl.ANY)],
        out_specs=pl.BlockSpec((None, length, W), lambda b: (b, row_block, 0)),
        input_output_aliases={0: 0},
        compiler_params=_cparams("arbitrary"),
    )(arr)


def _diff_attn(qkv, lam, subln_g, n, nc, ctx_queries, lam_init):
    B, T, _ = qkv.shape
    H = DA_HEADS
    tq = _pick(n, (ATTN_TQ * ATTN_CHAINS, ATTN_TQ))
    kern = functools.partial(_diff_attn_kernel, out_scale=1.0 - lam_init)
    smem = pl.BlockSpec(memory_space=pltpu.SMEM)
    gspec = pl.BlockSpec((1, DA_DV), lambda b, h, i: (0, 0))
    oa = pl.pallas_call(
        kern,
        out_shape=jax.ShapeDtypeStruct((B, T, DA_W), BF16),
        grid=(B, H, n // tq),
        in_specs=[smem,
                  pl.BlockSpec((None, tq, LANE), lambda b, h, i: (b, i, h)),
                  pl.BlockSpec((None, T, LANE), lambda b, h, i: (b, 0, H + h)),
                  pl.BlockSpec((None, T, LANE), lambda b, h, i: (b, 0, 2 * H + h)),
                  gspec],
        out_specs=pl.BlockSpec((None, tq, LANE), lambda b, h, i: (b, i, h)),
        compiler_params=_cparams("arbitrary", "arbitrary", "arbitrary"),
    )(lam, qkv, qkv, qkv, subln_g)
    if not ctx_queries:
        return _zero_rows(oa, n // nc, nc)
    cb = n // nc
    return pl.pallas_call(
        kern,
        out_shape=jax.ShapeDtypeStruct((B, T, DA_W), BF16),
        grid=(B, H, 1),
        in_specs=[smem,
                  pl.BlockSpec((None, nc, LANE), lambda b, h, i: (b, cb, h)),
                  pl.BlockSpec((None, nc, LANE), lambda b, h, i: (b, cb, H + h)),
                  pl.BlockSpec((None, nc, LANE), lambda b, h, i: (b, cb, 2 * H + h)),
                  gspec,
                  pl.BlockSpec(memory_space=pl.ANY)],
        out_specs=pl.BlockSpec((None, nc, LANE), lambda b, h, i: (b, cb, h)),
        input_output_aliases={5: 0},
        compiler_params=_cparams("arbitrary", "arbitrary", "arbitrary"),
    )(lam, qkv, qkv, qkv, subln_g, oa)


def _mla_attn(q, k, v, n, nc, ctx_queries):
    B, T, _ = q.shape
    H = MLA_HEADS
    tq = _pick(n, (ATTN_TQ * ATTN_CHAINS, ATTN_TQ))
    ob = pl.pallas_call(
        _mla_attn_kernel,
        out_shape=jax.ShapeDtypeStruct((B, T, MLA_W), BF16),
        grid=(B, H, n // tq),
        in_specs=[pl.BlockSpec((None, tq, MLA_QK_PAD), lambda b, h, i: (b, i, h)),
                  pl.BlockSpec((None, T, MLA_QK_PAD), lambda b, h, i: (b, 0, h)),
                  pl.BlockSpec((None, T, MLA_DV), lambda b, h, i: (b, 0, h))],
        out_specs=pl.BlockSpec((None, tq, MLA_DV), lambda b, h, i: (b, i, h)),
        compiler_params=_cparams("arbitrary", "arbitrary", "arbitrary"),
    )(q, k, v)
    if not ctx_queries:
        return _zero_rows(ob, n // nc, nc)
    cb = n // nc
    return pl.pallas_call(
        _mla_attn_kernel,
        out_shape=jax.ShapeDtypeStruct((B, T, MLA_W), BF16),
        grid=(B, H, 1),
        in_specs=[pl.BlockSpec((None, nc, MLA_QK_PAD), lambda b, h, i: (b, cb, h)),
                  pl.BlockSpec((None, nc, MLA_QK_PAD), lambda b, h, i: (b, cb, h)),
                  pl.BlockSpec((None, nc, MLA_DV), lambda b, h, i: (b, cb, h)),
                  pl.BlockSpec(memory_space=pl.ANY)],
        out_specs=pl.BlockSpec((None, nc, MLA_DV), lambda b, h, i: (b, cb, h)),
        input_output_aliases={3: 0},
        compiler_params=_cparams("arbitrary", "arbitrary", "arbitrary"),
    )(q, k, v, ob)


def _hy_filter_kernel(feat_ref, dec_ref, w1_ref, b1_ref, w2_ref, b2_ref, w3_ref, b3_ref, w4_ref, fr_ref,
                      k_ref, s_ref):
    hp = lax.Precision.HIGHEST
    d, r = pl.program_id(0), pl.program_id(1)
    fr = fr_ref[...]
    h = jnp.sin(fr * (jnp.dot(feat_ref[...], w1_ref[...], precision=hp, preferred_element_type=F32) + b1_ref[...]))
    h = jnp.sin(fr * (jnp.dot(h, w2_ref[...], precision=hp, preferred_element_type=F32) + b2_ref[...]))
    h = jnp.sin(fr * (jnp.dot(h, w3_ref[...], precision=hp, preferred_element_type=F32) + b3_ref[...]))
    h = jnp.dot(h, w4_ref[...], precision=hp, preferred_element_type=F32)
    row = lax.broadcasted_iota(jnp.int32, (h.shape[0], 1), 0)
    first_bwd = jnp.where((d == 1) & (r == 0), 1.0, 0.0)
    scale = dec_ref[...] * (1.0 - jnp.where(row == 0, 1.0, 0.0) * first_bwd)

    @pl.when((d == 0) & (r == 0))
    def _():
        s_ref[...] = jnp.zeros(s_ref.shape, F32)

    for o in range(HY_ORDER):
        ko = h[:, o * HY_CH:(o + 1) * HY_CH] * scale
        k_ref[o] = ko
        s_ref[o] += jnp.broadcast_to(jnp.sum(jnp.abs(ko), axis=0, keepdims=True), (8, HY_CH))


def _hy_filter(n, w1, b1, w2, b2, w3, b3, w4, fr):
    C = HY_CH
    t = jnp.linspace(0.0, 1.0, n, dtype=F32)
    pos = jnp.arange(n, dtype=F32)
    t2 = jnp.concatenate([t, t[::-1]])[:, None]
    pos2 = jnp.concatenate([pos, pos[::-1]])[:, None]
    phase = (2.0 * math.pi / n) * pos2 * jnp.linspace(1e-4, HY_BANDS - 1, HY_BANDS, dtype=F32)[None, :]
    feat = jnp.concatenate([t2, jnp.cos(phase), -jnp.sin(phase)], axis=-1)
    feat = jnp.pad(feat, ((0, 0), (0, HY_FFN - HY_EMB)))
    dec = jnp.exp(-t2 * jnp.abs(jnp.linspace(HY_MIN_DECAY, HY_MAX_DECAY, C, dtype=F32)))
    w4d = w4.reshape(HY_FFN, HY_ORDER, 2, C).transpose(2, 0, 1, 3).reshape(2, HY_FFN, HY_ORDER * C)
    rb = min(512, n)
    nb = n // rb
    full = lambda d, r: (0, 0)
    return pl.pallas_call(
        _hy_filter_kernel,
        out_shape=(jax.ShapeDtypeStruct((HY_ORDER, 2 * n, C), F32),
                   jax.ShapeDtypeStruct((HY_ORDER, 8, C), F32)),
        grid=(2, nb),
        in_specs=[pl.BlockSpec((rb, HY_FFN), lambda d, r: (d * nb + r, 0)),
                  pl.BlockSpec((rb, C), lambda d, r: (d * nb + r, 0)),
                  pl.BlockSpec(w1.shape, full), pl.BlockSpec(b1.shape, full),
                  pl.BlockSpec(w2.shape, full), pl.BlockSpec(b2.shape, full),
                  pl.BlockSpec(w3.shape, full), pl.BlockSpec(b3.shape, full),
                  pl.BlockSpec((None, HY_FFN, HY_ORDER * C), lambda d, r: (d, 0, 0)),
                  pl.BlockSpec(fr.shape, full)],
        out_specs=(pl.BlockSpec((HY_ORDER, rb, C), lambda d, r: (0, d * nb + r, 0)),
                   pl.BlockSpec((HY_ORDER, 8, C), lambda d, r: (0, 0, 0))),
        compiler_params=_cparams("arbitrary", "arbitrary"),
    )(feat, dec, w1, b1, w2, b2, w3, b3, w4d, fr)


def _hy_dwconv_kernel(u_ref, w_ref, b_ref, o_ref):
    u = u_ref[...]
    L = u.shape[0]
    row = lax.broadcasted_iota(jnp.int32, (L, 1), 0)
    up = jnp.where(row == 0, 0.0, pltpu.roll(u, 1, 0))
    dn = jnp.where(row == L - 1, 0.0, pltpu.roll(u, L - 1, 0))
    o_ref[...] = up * w_ref[0:1, :] + u * w_ref[1:2, :] + dn * w_ref[2:3, :] + b_ref[...]


def _hy_dwconv(u, w, b, row_block, length):
    B = u.shape[0]
    cw = 256
    per = HY_CH // cw
    return pl.pallas_call(
        _hy_dwconv_kernel,
        out_shape=jax.ShapeDtypeStruct((3, B, length, HY_CH), F32),
        grid=(B, 3 * per),
        in_specs=[pl.BlockSpec((None, length, cw), lambda bb, j: (bb, row_block, j)),
                  pl.BlockSpec((3, cw), lambda bb, j: (0, j)),
                  pl.BlockSpec((1, cw), lambda bb, j: (0, j))],
        out_specs=pl.BlockSpec((None, None, length, cw), lambda bb, j: (j // per, bb, 0, j % per)),
        compiler_params=_cparams("arbitrary", "arbitrary"),
    )(u, w, b)


def _dft_tables(n):
    S = FFT_S
    M = 2 * n
    N1 = M // S
    H = N1 // 2
    s2 = np.arange(S)[:, None, None]
    k1 = np.arange(N1)[None, :, None]
    s1 = np.arange(N1)[None, None, :]
    ang = -2.0 * np.pi * ((k1 * (S * s1 + s2)) % M) / M
    fr, fi = np.cos(ang), np.sin(ang)
    g1f = np.concatenate([fr, fi], axis=1)
    frh, fih = fr[:, :, :H], fi[:, :, :H]
    g1d = np.concatenate([np.concatenate([frh, -fih], axis=2),
                          np.concatenate([fih, frh], axis=2)], axis=1)
    er = np.transpose(frh, (0, 2, 1)) / M
    ei = -np.transpose(fih, (0, 2, 1)) / M
    g3 = np.concatenate([np.concatenate([er, -ei], axis=2),
                         np.concatenate([ei, er], axis=2)], axis=1)
    a2 = -2.0 * np.pi * ((np.arange(S)[:, None] * np.arange(S)[None, :]) % S) / S
    f2r, f2i = np.cos(a2), np.sin(a2)
    g2 = np.block([[f2r, -f2i], [f2i, f2r]])
    g2i = np.block([[f2r, f2i], [-f2i, f2r]])
    cast = lambda a: jnp.asarray(a, dtype=F32).astype(BF16)
    return cast(g1d), cast(g1f), cast(g2), cast(g2i), cast(g3)


FFT_NS = 16


def _fft_s1_kernel(g_ref, x_ref, o_ref):
    xt = jnp.swapaxes(x_ref[...], 0, 1)
    y = jnp.stack([jnp.dot(g_ref[j], xt[j].astype(BF16), preferred_element_type=F32)
                   for j in range(xt.shape[0])], axis=0)
    o_ref[...] = jnp.swapaxes(y, 0, 1).astype(o_ref.dtype)


def _fft_s1(g, x):
    P, Ri, S, C = x.shape
    Ro = g.shape[1]
    ns = FFT_NS
    return pl.pallas_call(
        _fft_s1_kernel,
        out_shape=jax.ShapeDtypeStruct((P, Ro, S, C), BF16),
        grid=(S // ns, P),
        in_specs=[pl.BlockSpec((ns, Ro, Ri), lambda j, p: (j, 0, 0)),
                  pl.BlockSpec((None, Ri, ns, C), lambda j, p: (p, 0, j, 0))],
        out_specs=pl.BlockSpec((None, Ro, ns, C), lambda j, p: (p, 0, j, 0)),
        compiler_params=_cparams("arbitrary", "arbitrary"),
    )(g, x)


def _fft_s2_filt_kernel(a_ref, g_ref, rn_ref, o_ref, *, nk):
    S = FFT_S
    rn = rn_ref[...]
    for t in range(nk):
        d = jnp.concatenate([a_ref[0, t], a_ref[1, t]], axis=0)
        y = jnp.dot(g_ref[...], d, preferred_element_type=F32)
        o_ref[0, t] = y[:S] * rn
        o_ref[1, t] = y[S:] * rn


def _fft_s2_filt(a, g2, rnorm):
    O, _, N1, S, C = a.shape
    nk = 4
    blk = (None, 2, nk, S, C)
    return pl.pallas_call(
        functools.partial(_fft_s2_filt_kernel, nk=nk),
        out_shape=jax.ShapeDtypeStruct(a.shape, F32),
        grid=(O, N1 // nk),
        in_specs=[pl.BlockSpec(blk, lambda o, j: (o, 0, j, 0, 0)),
                  pl.BlockSpec(g2.shape, lambda o, j: (0, 0)),
                  pl.BlockSpec((None, 1, C), lambda o, j: (o, 0, 0))],
        out_specs=pl.BlockSpec(blk, lambda o, j: (o, 0, j, 0, 0)),
        compiler_params=_cparams("arbitrary", "arbitrary"),
    )(a, g2, rnorm)


def _fft_s2_kernel(a_ref, g_ref, gi_ref, kf_ref, o_ref, *, nk):
    S = FFT_S
    for t in range(nk):
        d = jnp.concatenate([a_ref[0, t], a_ref[1, t]], axis=0)
        y = jnp.dot(g_ref[...], d, preferred_element_type=F32)
        yr, yi = y[:S], y[S:]
        kr, ki = kf_ref[0, t], kf_ref[1, t]
        p = jnp.concatenate([yr * kr - yi * ki, yr * ki + yi * kr], axis=0).astype(BF16)
        b = jnp.dot(gi_ref[...], p, preferred_element_type=F32)
        o_ref[0, t] = b[:S].astype(BF16)
        o_ref[1, t] = b[S:].astype(BF16)


def _fft_s2(a, g2, g2i, kf, order):
    P, _, N1, S, C = a.shape
    nk = 4
    blk = (None, 2, nk, S, C)
    return pl.pallas_call(
        functools.partial(_fft_s2_kernel, nk=nk),
        out_shape=jax.ShapeDtypeStruct(a.shape, BF16),
        grid=(N1 // nk, P),
        in_specs=[pl.BlockSpec(blk, lambda j, p: (p, 0, j, 0, 0)),
                  pl.BlockSpec(g2.shape, lambda j, p: (0, 0)),
                  pl.BlockSpec(g2i.shape, lambda j, p: (0, 0)),
                  pl.BlockSpec(blk, lambda j, p: (order, 0, j, 0, 0))],
        out_specs=pl.BlockSpec(blk, lambda j, p: (p, 0, j, 0, 0)),
        compiler_params=_cparams("arbitrary", "arbitrary"),
    )(a, g2, g2i, kf)


def _fft_s3_kernel(g_ref, b_ref, z_ref, gate_ref, skip_ref, o_ref):
    bt = jnp.swapaxes(b_ref[...].astype(F32), 0, 1).astype(BF16)
    y = jnp.stack([jnp.dot(g_ref[j], bt[j], preferred_element_type=F32) for j in range(bt.shape[0])], axis=0)
    y = jnp.swapaxes(y, 0, 1)
    o_ref[...] = (gate_ref[...] * (y + skip_ref[...] * z_ref[...])).astype(o_ref.dtype)


def _fft_s3(g3, b, z, gate, skip, out_dtype):
    P, Ri, S, C = b.shape
    Ro = g3.shape[1]
    ns = FFT_NS
    dspec = pl.BlockSpec((None, Ro, ns, C), lambda j, p: (p, 0, j, 0))
    return pl.pallas_call(
        _fft_s3_kernel,
        out_shape=jax.ShapeDtypeStruct((P, Ro, S, C), out_dtype),
        grid=(S // ns, P),
        in_specs=[pl.BlockSpec((ns, Ro, Ri), lambda j, p: (j, 0, 0)),
                  pl.BlockSpec((None, Ri, ns, C), lambda j, p: (p, 0, j, 0)),
                  dspec, dspec,
                  pl.BlockSpec((1, C), lambda j, p: (0, 0))],
        out_specs=dspec,
        compiler_params=_cparams("arbitrary", "arbitrary"),
    )(g3, b, z, gate, skip)


def _dense_tables(m):
    M = 2 * m
    ang = -2.0 * np.pi * ((np.arange(M)[:, None] * np.arange(M)[None, :]) % M) / M
    fr, fi = np.cos(ang), np.sin(ang)
    gk = np.concatenate([fr, fi], axis=0)
    gd = np.block([[fr[:, :m], -fi[:, :m]], [fi[:, :m], fr[:, :m]]])
    er, ei = fr[:m, :] / M, -fi[:m, :] / M
    gi = np.block([[er, -ei], [ei, er]])
    cast = lambda a: jnp.asarray(a, dtype=F32).astype(BF16)
    return cast(gk), cast(gd), cast(gi)


def _dense_spec_kernel(k_ref, g_ref, rn_ref, o_ref):
    o_ref[...] = jnp.dot(g_ref[...], k_ref[...].astype(BF16), preferred_element_type=F32) * rn_ref[...]


def _dense_spec(kc, gk, rnorm):
    O, M, C = kc.shape
    return pl.pallas_call(
        _dense_spec_kernel,
        out_shape=jax.ShapeDtypeStruct((O, 2 * M, C), F32),
        grid=(O,),
        in_specs=[pl.BlockSpec((None, M, C), lambda o: (o, 0, 0)),
                  pl.BlockSpec(gk.shape, lambda o: (0, 0)),
                  pl.BlockSpec((None, 1, C), lambda o: (o, 0, 0))],
        out_specs=pl.BlockSpec((None, 2 * M, C), lambda o: (o, 0, 0)),
        compiler_params=_cparams("arbitrary"),
    )(kc, gk, rnorm)


def _dense_conv_kernel(x_ref, gd_ref, gi_ref, kf_ref, gate_ref, skip_ref, o_ref):
    x = x_ref[...]
    y = jnp.dot(gd_ref[...], x.astype(BF16), preferred_element_type=F32)
    M = y.shape[0] // 2
    yr, yi = y[:M], y[M:]
    kr, ki = kf_ref[:M], kf_ref[M:]
    p = jnp.concatenate([yr * kr - yi * ki, yr * ki + yi * kr], axis=0).astype(BF16)
    conv = jnp.dot(gi_ref[...], p, preferred_element_type=F32)
    o_ref[...] = (gate_ref[...] * (conv + skip_ref[...] * x)).astype(o_ref.dtype)


def _dense_conv(x, gd, gi, kf, order, gate, skip, out_dtype):
    P, R, C = x.shape
    dspec = pl.BlockSpec((None, R, C), lambda p: (p, 0, 0))
    return pl.pallas_call(
        _dense_conv_kernel,
        out_shape=jax.ShapeDtypeStruct((P, R, C), out_dtype),
        grid=(P,),
        in_specs=[dspec,
                  pl.BlockSpec(gd.shape, lambda p: (0, 0)),
                  pl.BlockSpec(gi.shape, lambda p: (0, 0)),
                  pl.BlockSpec((None,) + kf.shape[1:], lambda p: (order, 0, 0)),
                  dspec,
                  pl.BlockSpec((1, C), lambda p: (0, 0))],
        out_specs=dspec,
        compiler_params=_cparams("arbitrary"),
    )(x, gd, gi, kf, gate, skip)


def _merge_kernel(oa_ref, ob_ref, oc_ref, wa_ref, wb_ref, wc_ref, g0_ref, g1_ref, g2_ref, o_ref):
    ya = jnp.dot(oa_ref[...], wa_ref[...], preferred_element_type=F32)
    yb = jnp.dot(ob_ref[...], wb_ref[...], preferred_element_type=F32)
    yc = jnp.dot(oc_ref[...], wc_ref[...], preferred_element_type=F32)
    m = g0_ref[...].astype(F32) * ya + g1_ref[...].astype(F32) * yb + g2_ref[...].astype(F32) * yc
    o_ref[...] = m.astype(BF16)


def _merge(oa, ob, oc, w_ba, w_bb, w_bc, gates):
    M = oa.shape[0]
    D = D_MODEL
    tm = _pick(M, (1024, 512, 256))
    tn = 512
    nb = D // tn
    row = lambda i, j: (i, 0)
    col = lambda i, j: (0, j)
    return pl.pallas_call(
        _merge_kernel,
        out_shape=jax.ShapeDtypeStruct((M, D), BF16),
        grid=(M // tm, nb),
        in_specs=[pl.BlockSpec((tm, DA_W), row), pl.BlockSpec((tm, MLA_W), row), pl.BlockSpec((tm, HY_CH), row),
                  pl.BlockSpec((DA_W, tn), col), pl.BlockSpec((MLA_W, tn), col), pl.BlockSpec((HY_CH, tn), col),
                  pl.BlockSpec((tm, tn), lambda i, j: (i, j)),
                  pl.BlockSpec((tm, tn), lambda i, j: (i, nb + j)),
                  pl.BlockSpec((tm, tn), lambda i, j: (i, 2 * nb + j))],
        out_specs=pl.BlockSpec((tm, tn), lambda i, j: (i, j)),
        compiler_params=_cparams("arbitrary", "arbitrary"),
    )(oa, ob, oc, w_ba, w_bb, w_bc, gates, gates, gates)


def _wo_ln_kernel(m_ref, w_ref, xs_ref, mod_ref, g_ref, b_ref, xs1_ref, h2_ref, *, geom, groups):
    gpb, lat_groups, n_batch = geom
    y = jnp.dot(m_ref[...], w_ref[...], preferred_element_type=F32)
    g, b = g_ref[...], b_ref[...]
    for q in range(groups):
        r = _mod_row(pl.program_id(0) * groups + q, gpb, lat_groups, n_batch)
        mod = mod_ref[r]
        sl = slice(q * ROW_GROUP, (q + 1) * ROW_GROUP)
        x1 = _ln(DEEPNORM_ALPHA * xs_ref[sl, :] + mod[2:3, :] * y[sl, :]) * g + b
        xs1_ref[sl, :] = x1
        h2_ref[sl, :] = (_ln(x1) * (1.0 + mod[4:5, :]) + mod[3:4, :]).astype(BF16)


def _wo_ln(merged, w_o, xs, mod, ln_g, ln_b, geom):
    M, D = xs.shape
    tm = _pick(M, (512, 256))
    row = lambda i: (i, 0)
    full2 = lambda i: (0, 0)
    return pl.pallas_call(
        functools.partial(_wo_ln_kernel, geom=geom, groups=tm // ROW_GROUP),
        out_shape=(jax.ShapeDtypeStruct((M, D), F32), jax.ShapeDtypeStruct((M, D), BF16)),
        grid=(M // tm,),
        in_specs=[pl.BlockSpec((tm, D), row),
                  pl.BlockSpec((D, D), full2),
                  pl.BlockSpec((tm, D), row),
                  pl.BlockSpec(mod.shape, lambda i: (0, 0, 0)),
                  pl.BlockSpec((1, D), full2), pl.BlockSpec((1, D), full2)],
        out_specs=(pl.BlockSpec((tm, D), row), pl.BlockSpec((tm, D), row)),
        compiler_params=_cparams("arbitrary"),
    )(merged, w_o, xs, mod, ln_g, ln_b)


FFN_HALO = 16


def _ffn_up_kernel(hp_ref, h_ref, hn_ref, wa_ref, wv_ref, cw_ref, cb_ref, kp_ref, kn_ref, o_ref):
    tm, tn = o_ref.shape
    hm = h_ref[...]
    hext = jnp.concatenate([hp_ref[...], hm, hn_ref[...]], axis=0)
    a = jnp.dot(hext, wa_ref[...], preferred_element_type=F32)
    v = jnp.dot(hm, wv_ref[...], preferred_element_type=F32)
    ext = tm + 2 * FFN_HALO
    rep = tn // LANE
    keep_prev = jnp.tile(kp_ref[...], (1, rep))
    keep_next = jnp.tile(kn_ref[...], (1, rep))
    a_prev = pltpu.roll(a, 1, 0)[FFN_HALO:FFN_HALO + tm] * keep_prev
    a_next = pltpu.roll(a, ext - 1, 0)[FFN_HALO:FFN_HALO + tm] * keep_next
    cv = a_prev * cw_ref[0:1, :] + a[FFN_HALO:FFN_HALO + tm] * cw_ref[1:2, :] + a_next * cw_ref[2:3, :] + cb_ref[...]
    o_ref[...] = (cv * jax.nn.sigmoid(cv) * v).astype(BF16)


def _ffn_up(h2, w_up, conv_w, conv_b, keep_prev, keep_next):
    M, D = h2.shape
    tm = _pick(M, (1024, 512, 256))
    tn = 512
    nb = D_FF // tn
    hb = tm // FFN_HALO
    last = M // FFN_HALO - 1
    return pl.pallas_call(
        _ffn_up_kernel,
        out_shape=jax.ShapeDtypeStruct((M, D_FF), BF16),
        grid=(M // tm, nb),
        in_specs=[pl.BlockSpec((FFN_HALO, D), lambda i, j: (jnp.maximum(i * hb - 1, 0), 0)),
                  pl.BlockSpec((tm, D), lambda i, j: (i, 0)),
                  pl.BlockSpec((FFN_HALO, D), lambda i, j: (jnp.minimum((i + 1) * hb, last), 0)),
                  pl.BlockSpec((D, tn), lambda i, j: (0, j)),
                  pl.BlockSpec((D, tn), lambda i, j: (0, nb + j)),
                  pl.BlockSpec((3, tn), lambda i, j: (0, j)),
                  pl.BlockSpec((1, tn), lambda i, j: (0, j)),
                  pl.BlockSpec((tm, LANE), lambda i, j: (i, 0)),
                  pl.BlockSpec((tm, LANE), lambda i, j: (i, 0))],
        out_specs=pl.BlockSpec((tm, tn), lambda i, j: (i, j)),
        compiler_params=_cparams("arbitrary", "arbitrary"),
    )(h2, h2, h2, w_up, w_up, conv_w, conv_b, keep_prev, keep_next)


def _ffn_down_kernel(u_ref, w_ref, xs_ref, mod_ref, g_ref, b_ref, o_ref, acc_ref, *, geom, groups, nk):
    gpb, lat_groups, n_batch = geom
    k = pl.program_id(1)

    @pl.when(k == 0)
    def _():
        acc_ref[...] = jnp.zeros(acc_ref.shape, F32)

    acc_ref[...] += jnp.dot(u_ref[...], w_ref[...], preferred_element_type=F32)

    @pl.when(k == nk - 1)
    def _():
        g, b = g_ref[...], b_ref[...]
        for q in range(groups):
            r = _mod_row(pl.program_id(0) * groups + q, gpb, lat_groups, n_batch)
            mod = mod_ref[r]
            sl = slice(q * ROW_GROUP, (q + 1) * ROW_GROUP)
            o_ref[sl, :] = _ln(DEEPNORM_ALPHA * xs_ref[sl, :] + mod[5:6, :] * acc_ref[sl, :]) * g + b


def _ffn_down(u, w_down, xs1, mod, ln_g, ln_b, geom):
    M, D = xs1.shape
    tm = _pick(M, (512, 256))
    tk = 512
    nk = D_FF // tk
    row = lambda i, k: (i, 0)
    full2 = lambda i, k: (0, 0)
    return pl.pallas_call(
        functools.partial(_ffn_down_kernel, geom=geom, groups=tm // ROW_GROUP, nk=nk),
        out_shape=jax.ShapeDtypeStruct((M, D), F32),
        grid=(M // tm, nk),
        in_specs=[pl.BlockSpec((tm, tk), lambda i, k: (i, k)),
                  pl.BlockSpec((tk, D), lambda i, k: (k, 0)),
                  pl.BlockSpec((tm, D), row),
                  pl.BlockSpec(mod.shape, lambda i, k: (0, 0, 0)),
                  pl.BlockSpec((1, D), full2), pl.BlockSpec((1, D), full2)],
        out_specs=pl.BlockSpec((tm, D), row),
        scratch_shapes=[pltpu.VMEM((tm, D), F32)],
        compiler_params=_cparams("arbitrary", "arbitrary"),
    )(u, w_down, xs1, mod, ln_g, ln_b)


def _rope_tables(B, n, nc):
    half = DA_DQK // 2
    inv = ROPE_BASE ** (-jnp.arange(0, half, 2, dtype=F32) / half)
    t = jnp.arange(n, dtype=jnp.int32)
    ang_r = (t // GRID_W).astype(F32)[:, None] * inv[None, :]
    ang_c = (t % GRID_W).astype(F32)[:, None] * inv[None, :]
    ang = jnp.concatenate([ang_r, ang_r, ang_c, ang_c], axis=-1)
    cos, sin = jnp.cos(ang), jnp.sin(ang)
    upper = (jnp.arange(DA_DQK) % half) >= half // 2
    sa = jnp.where(upper, sin, 0.0)
    sb = jnp.where(upper, 0.0, -sin)

    def full(tab, fill):
        tab = jnp.concatenate([tab, jnp.full((nc, DA_DQK), fill, F32)], axis=0)
        tab = jnp.tile(tab, (B, LANE // DA_DQK))
        return tab

    return full(cos, 1.0), full(sa, 0.0), full(sb, 0.0)


def _conv_masks(B, n, nc):
    T = n + nc
    t = jnp.arange(T)
    keep_prev = ((t != 0) & (t != n)).astype(F32)
    keep_next = ((t != n - 1) & (t != T - 1)).astype(F32)
    widen = lambda m: jnp.tile(m[:, None], (B, LANE))
    return widen(keep_prev), widen(keep_next)


def _split_w_in(w):
    D = w.shape[0]
    o = 0
    def take(width):
        nonlocal o
        part = w[:, o:o + width]
        o += width
        return part
    regroup = lambda p: p.reshape(D, 2, DA_HEADS, DA_DQK).transpose(0, 2, 1, 3).reshape(D, DA_W)
    wq, wk, wv = regroup(take(DA_W)), regroup(take(DA_W)), take(DA_W)
    wcq, wckv, wkr = take(MLA_Q_RANK), take(MLA_KV_RANK), take(MLA_ROPE)
    why, wg = take(3 * HY_CH), take(N_BRANCH * D_MODEL)
    w_qkv = jnp.concatenate([wq, wk, wv], axis=1).astype(BF16)
    w_mla = jnp.concatenate([wcq, wckv, wkr, jnp.zeros((D, LANE - MLA_ROPE), w.dtype)], axis=1).astype(BF16)
    return w_qkv, w_mla, why.astype(BF16), wg.astype(BF16)


def _pad_w_uq(w):
    w = w.reshape(MLA_Q_RANK, MLA_HEADS, MLA_NOPE + MLA_ROPE)
    w = jnp.pad(w, ((0, 0), (0, 0), (0, MLA_QK_PAD - MLA_NOPE - MLA_ROPE)))
    return w.reshape(MLA_Q_RANK, MLA_HEADS * MLA_QK_PAD).astype(BF16)


def _split_w_ukv(w):
    w = w.reshape(MLA_KV_RANK, MLA_HEADS, 2, MLA_NOPE).transpose(0, 2, 1, 3)
    return w.reshape(MLA_KV_RANK, 2 * MLA_W).astype(BF16)


def _hyena(u3, layer, p, n, nc, with_ctx, tables):
    B = u3.shape[0]
    C = HY_CH
    P = B // 2
    g1d, g1f, g2, g2i, g3, gk, gd, gi = tables
    S = FFT_S
    N1 = 2 * n // S
    mlp = (p["hy_ffn_w1p"][layer], p["hy_ffn_b1"][layer][None], p["hy_ffn_w2"][layer], p["hy_ffn_b2"][layer][None],
           p["hy_ffn_w3"][layer], p["hy_ffn_b3"][layer][None], p["hy_ffn_w4"][layer], p["hy_freq"][layer][None])
    skip = p["hy_skip"][layer]
    cw, cb = p["hy_conv_w"][layer], p["hy_conv_b"][layer][None]

    kc, sums = _hy_filter(n, *mlp)
    kf = _fft_s1(g1f, kc.reshape(HY_ORDER, N1, S, C))
    kf = _fft_s2_filt(kf.reshape(HY_ORDER, 2, N1, S, C), g2, 1.0 / sums[:, 0:1, :])
    dw = _hy_dwconv(u3, cw, cb, 0, n).reshape(3, P, N1, S, C)
    z = dw[0]
    for o in range(HY_ORDER):
        a = _fft_s1(g1d, z).reshape(P, 2, N1, S, C)
        b = _fft_s2(a, g2, g2i, kf, o).reshape(P, 2 * N1, S, C)
        z = _fft_s3(g3, b, z, dw[1 + o], skip[o][None], F32 if o + 1 < HY_ORDER else BF16)
    oc_lat = z.reshape(B, n, C)

    if with_ctx:
        kcc, sumc = _hy_filter(nc, *mlp)
        kfc = _dense_spec(kcc, gk, 1.0 / sumc[:, 0:1, :])
        dwc = _hy_dwconv(u3, cw, cb, n // nc, nc).reshape(3, P, 2 * nc, C)
        zc = dwc[0]
        for o in range(HY_ORDER):
            zc = _dense_conv(zc, gd, gi, kfc, o, dwc[1 + o], skip[o][None], F32 if o + 1 < HY_ORDER else BF16)
        oc_ctx = zc.reshape(B, nc, C)
    else:
        oc_ctx = jnp.zeros((B, nc, C), BF16)
    return jnp.concatenate([oc_lat, oc_ctx], axis=1)


def kernel(x, c, ctx, c_ctx, ada_w, ada_b, w_in, da_lambda, da_subln_g, mla_q_g, mla_w_uq, mla_kv_g, mla_w_ukv, hy_conv_w, hy_conv_b, hy_ffn_w1, hy_ffn_b1, hy_ffn_w2, hy_ffn_b2, hy_ffn_w3, hy_ffn_b3, hy_ffn_w4, hy_freq, hy_skip, w_branch_a, w_branch_b, w_branch_c, w_out, ln1_g, ln1_b, ffn_w_up, ffn_conv_w, ffn_conv_b, ffn_w_down, ln2_g, ln2_b):
    B, n, D = x.shape
    nc = ctx.shape[1]
    T = n + nc
    rows = B * T
    assert D == D_MODEL and B % 2 == 0 and B < 8
    assert n % ROW_GROUP == 0 and nc % ROW_GROUP == 0 and n % nc == 0 and n % GRID_W == 0
    assert (2 * n) % (8 * FFT_S) == 0 and T % ATTN_TK == 0 and nc % ATTN_TK == 0
    geom = (T // ROW_GROUP, n // ROW_GROUP, B)

    hy = dict(hy_ffn_w1p=jnp.pad(hy_ffn_w1, ((0, 0), (0, HY_FFN - HY_EMB), (0, 0))), hy_ffn_b1=hy_ffn_b1,
              hy_ffn_w2=hy_ffn_w2, hy_ffn_b2=hy_ffn_b2, hy_ffn_w3=hy_ffn_w3, hy_ffn_b3=hy_ffn_b3,
              hy_ffn_w4=hy_ffn_w4, hy_freq=hy_freq, hy_skip=hy_skip, hy_conv_w=hy_conv_w, hy_conv_b=hy_conv_b)
    tables = _dft_tables(n) + _dense_tables(nc)
    rope = _rope_tables(B, n, nc)
    keep_prev, keep_next = _conv_masks(B, n, nc)

    cc = jnp.concatenate([c, c_ctx[None], jnp.zeros((8 - B - 1, D), F32)], axis=0)
    mods = _ada(cc, ada_w, ada_b[:, None, :]).reshape(DEPTH, 8, 6, D)

    xs = jnp.concatenate([x, ctx], axis=1).reshape(rows, D)
    for i in range(DEPTH):
        last = i == DEPTH - 1
        lam_init = 0.8 - 0.6 * math.exp(-0.3 * i)
        lq1, lk1, lq2, lk2 = da_lambda[i].astype(F32)
        lam = (jnp.exp(jnp.sum(lq1 * lk1)) - jnp.exp(jnp.sum(lq2 * lk2)) + lam_init).reshape(1)
        mod = mods[i]
        w_qkv, w_mla, w_hy, w_g = _split_w_in(w_in[i])

        h = _lnmod(xs, mod, geom)
        qkv = _qkv_proj(h, w_qkv, rope)
        p_mla = _matmul(h, w_mla, F32, w_mla.shape[1])
        u_hy = _matmul(h, w_hy, F32, 768)
        gates = _matmul(h, w_g, BF16, 1024, act="sigmoid")

        q_m, k_m, v_m = _mla_prep(p_mla, mla_q_g[i][None], mla_kv_g[i][None], _pad_w_uq(mla_w_uq[i]),
                                  _split_w_ukv(mla_w_ukv[i]), rope)
        oa = _diff_attn(qkv.reshape(B, T, 3 * DA_W), lam, da_subln_g[i][None], n, nc, not last, lam_init)
        ob = _mla_attn(q_m.reshape(B, T, -1), k_m.reshape(B, T, -1), v_m.reshape(B, T, -1), n, nc, not last)
        oc = _hyena(u_hy.reshape(B, T, 3 * HY_CH), i, hy, n, nc, not last, tables)

        merged = _merge(oa.reshape(rows, DA_W), ob.reshape(rows, MLA_W), oc.reshape(rows, HY_CH),
                        w_branch_a[i].astype(BF16), w_branch_b[i].astype(BF16), w_branch_c[i].astype(BF16), gates)
        xs1, h2 = _wo_ln(merged, w_out[i].astype(BF16), xs, mod, ln1_g[i][None], ln1_b[i][None], geom)
        u = _ffn_up(h2, ffn_w_up[i].astype(BF16), ffn_conv_w[i], ffn_conv_b[i][None], keep_prev, keep_next)
        xs = _ffn_down(u, ffn_w_down[i].astype(BF16), xs1, mod, ln2_g[i][None], ln2_b[i][None], geom)
    return xs.reshape(B, T, D)[:, :n]
```

```python
import functools
import math

import numpy as np
import jax
import jax.numpy as jnp
from jax import lax
from jax.experimental import pallas as pl
from jax.experimental.pallas import tpu as pltpu

F32 = jnp.float32
BF16 = jnp.bfloat16

D_MODEL = 2048
DEPTH = 2
GRID_W = 64
ROPE_BASE = 10000.0
NORM_EPS = 1e-6
DA_HEADS = 6
DA_DQK = 64
DA_DV = 128
DA_W = DA_HEADS * DA_DV
MLA_HEADS = 6
MLA_Q_RANK = 512
MLA_KV_RANK = 256
MLA_NOPE = 128
MLA_ROPE = 64
MLA_DV = 128
MLA_W = MLA_HEADS * MLA_DV
MLA_QK_PAD = 256
HY_CH = 512
HY_ORDER = 2
HY_EMB = 33
HY_BANDS = (HY_EMB - 1) // 2
HY_FFN = 64
HY_MIN_DECAY = math.log(1e-2) / 1.5
HY_MAX_DECAY = math.log(1e-2) / 0.3
D_FF = 5632
N_BRANCH = 3
DEEPNORM_ALPHA = (2 * DEPTH) ** 0.25
LOG2E = 1.4426950408889634

ROW_GROUP = 256
LANE = 128
FFT_S = 128
VMEM_LIMIT = 52 * 1024 * 1024


def _cparams(*sem):
    return pltpu.CompilerParams(dimension_semantics=sem, vmem_limit_bytes=VMEM_LIMIT)


def _pick(total, prefs):
    for p in prefs:
        if total % p == 0:
            return p
    raise ValueError(f"no tile for {total} in {prefs}")


def _ln(x):
    mu = jnp.mean(x, axis=-1, keepdims=True)
    xc = x - mu
    var = jnp.mean(xc * xc, axis=-1, keepdims=True)
    return xc * lax.rsqrt(var + NORM_EPS)


def _rms(x):
    return x * lax.rsqrt(jnp.mean(x * x, axis=-1, keepdims=True) + NORM_EPS)


def _rope128(u, cos, sa, sb):
    return u * cos + pltpu.roll(u, 16, 1) * sa + pltpu.roll(u, LANE - 16, 1) * sb


def _ada_kernel(c_ref, w_ref, b_ref, o_ref):
    a = c_ref[...]
    a = a * jax.nn.sigmoid(a)
    o_ref[...] = jnp.dot(a.astype(BF16), w_ref[...].astype(BF16), preferred_element_type=F32) + b_ref[...]


def _ada(cc, ada_w, ada_b):
    L, D, N = ada_w.shape
    tn = 1024
    return pl.pallas_call(
        _ada_kernel,
        out_shape=jax.ShapeDtypeStruct((L, 8, N), F32),
        grid=(L, N // tn),
        in_specs=[pl.BlockSpec((8, D), lambda l, j: (0, 0)),
                  pl.BlockSpec((None, D, tn), lambda l, j: (l, 0, j)),
                  pl.BlockSpec((None, 1, tn), lambda l, j: (l, 0, j))],
        out_specs=pl.BlockSpec((None, 8, tn), lambda l, j: (l, 0, j)),
        compiler_params=_cparams("arbitrary", "arbitrary"),
    )(cc, ada_w, ada_b)


def _lnmod_kernel(x_ref, m_ref, o_ref):
    y = _ln(x_ref[...])
    o_ref[...] = (y * (1.0 + m_ref[1:2, :]) + m_ref[0:1, :]).astype(BF16)


def _mod_row(g, gpb, lat_groups, n_batch):
    return jnp.where(g % gpb < lat_groups, g // gpb, n_batch)


def _lnmod(xs, mod, geom):
    rows, D = xs.shape
    gpb, lat_groups, n_batch = geom
    return pl.pallas_call(
        _lnmod_kernel,
        out_shape=jax.ShapeDtypeStruct((rows, D), BF16),
        grid=(rows // ROW_GROUP,),
        in_specs=[pl.BlockSpec((ROW_GROUP, D), lambda i: (i, 0)),
                  pl.BlockSpec((None, 6, D), lambda i: (_mod_row(i, gpb, lat_groups, n_batch), 0, 0))],
        out_specs=pl.BlockSpec((ROW_GROUP, D), lambda i: (i, 0)),
        compiler_params=_cparams("arbitrary"),
    )(xs, mod)


def _mm_kernel(a_ref, w_ref, o_ref, *, act):
    acc = jnp.dot(a_ref[...], w_ref[...], preferred_element_type=F32)
    if act == "sigmoid":
        acc = jax.nn.sigmoid(acc)
    o_ref[...] = acc.astype(o_ref.dtype)


def _by_batch(a, nb):
    return a.reshape(nb, a.shape[0] // nb, a.shape[1])


def _matmul(a, w, out_dtype, tn, act=None, nb=1, tb=None):
    a = _by_batch(a, nb)
    K = a.shape[2]
    tb = tb or a.shape[1]
    N = w.shape[1]
    tm = _pick(tb, (1024, 512, 256))
    return pl.pallas_call(
        functools.partial(_mm_kernel, act=act),
        out_shape=jax.ShapeDtypeStruct((nb, tb, N), out_dtype),
        grid=(nb, tb // tm, N // tn),
        in_specs=[pl.BlockSpec((None, tm, K), lambda b, i, j: (b, i, 0)),
                  pl.BlockSpec((K, tn), lambda b, i, j: (0, j))],
        out_specs=pl.BlockSpec((None, tm, tn), lambda b, i, j: (b, i, j)),
        compiler_params=_cparams("arbitrary", "arbitrary", "arbitrary"),
    )(a, w).reshape(nb * tb, N)


def _qkv_kernel(h_ref, w_ref, cos_ref, sa_ref, sb_ref, o_ref, *, qscale):
    j = pl.program_id(1)
    acc = jnp.dot(h_ref[...], w_ref[...], preferred_element_type=F32)

    @pl.when(j < 2)
    def _():
        cos, sa, sb = cos_ref[...], sa_ref[...], sb_ref[...]
        scale = jnp.where(j == 0, qscale, 1.0).astype(F32)
        for c in range(DA_HEADS):
            u = acc[:, c * LANE:(c + 1) * LANE]
            o_ref[:, c * LANE:(c + 1) * LANE] = (_rope128(u, cos, sa, sb) * scale).astype(BF16)

    @pl.when(j == 2)
    def _():
        o_ref[...] = acc.astype(BF16)


def _qkv_proj(h, w_qkv, tabs):
    M, K = h.shape
    tm = _pick(M, (1024, 512, 256))
    tn = DA_W
    tab_spec = pl.BlockSpec((tm, LANE), lambda i, j: (i, 0))
    return pl.pallas_call(
        functools.partial(_qkv_kernel, qscale=DA_DQK ** -0.5 * LOG2E),
        out_shape=jax.ShapeDtypeStruct((M, 3 * DA_W), BF16),
        grid=(M // tm, 3),
        in_specs=[pl.BlockSpec((tm, K), lambda i, j: (i, 0)),
                  pl.BlockSpec((K, tn), lambda i, j: (0, j)),
                  tab_spec, tab_spec, tab_spec],
        out_specs=pl.BlockSpec((tm, tn), lambda i, j: (i, j)),
        compiler_params=_cparams("arbitrary", "arbitrary"),
    )(h, w_qkv, *tabs)


def _mla_prep_kernel(p_ref, qg_ref, kvg_ref, wuq_ref, wukv_ref, cos_ref, sa_ref, sb_ref,
                     q_ref, k_ref, v_ref, *, qscale):
    p = p_ref[...]
    cos, sa, sb = cos_ref[...], sa_ref[...], sb_ref[...]
    cq = p[:, :MLA_Q_RANK]
    ckv = p[:, MLA_Q_RANK:MLA_Q_RANK + MLA_KV_RANK]
    kr = p[:, MLA_Q_RANK + MLA_KV_RANK:]
    qn = (_rms(cq) * qg_ref[...]).astype(BF16)
    q = jnp.dot(qn, wuq_ref[...], preferred_element_type=F32)
    kvn = (_rms(ckv) * kvg_ref[...]).astype(BF16)
    kv = jnp.dot(kvn, wukv_ref[...], preferred_element_type=F32)
    krr = _rope128(kr, cos, sa, sb).astype(BF16)
    for h in range(MLA_HEADS):
        o = h * MLA_QK_PAD
        q_ref[:, o:o + LANE] = (q[:, o:o + LANE] * qscale).astype(BF16)
        q_ref[:, o + LANE:o + 2 * LANE] = (_rope128(q[:, o + LANE:o + 2 * LANE], cos, sa, sb) * qscale).astype(BF16)
        k_ref[:, o:o + LANE] = kv[:, h * LANE:(h + 1) * LANE].astype(BF16)
        k_ref[:, o + LANE:o + 2 * LANE] = krr
    v_ref[...] = kv[:, MLA_W:].astype(BF16)


def _mla_prep(p, q_g, kv_g, w_uq, w_ukv, tabs):
    M, W = p.shape
    tm = _pick(M, (512, 256))
    row = lambda i: (i, 0)
    full = lambda i: (0, 0)
    qk_w = MLA_HEADS * MLA_QK_PAD
    return pl.pallas_call(
        functools.partial(_mla_prep_kernel, qscale=(MLA_NOPE + MLA_ROPE) ** -0.5 * LOG2E),
        out_shape=(jax.ShapeDtypeStruct((M, qk_w), BF16),
                   jax.ShapeDtypeStruct((M, qk_w), BF16),
                   jax.ShapeDtypeStruct((M, MLA_W), BF16)),
        grid=(M // tm,),
        in_specs=[pl.BlockSpec((tm, W), row),
                  pl.BlockSpec((1, MLA_Q_RANK), full),
                  pl.BlockSpec((1, MLA_KV_RANK), full),
                  pl.BlockSpec(w_uq.shape, full),
                  pl.BlockSpec(w_ukv.shape, full),
                  pl.BlockSpec((tm, LANE), row), pl.BlockSpec((tm, LANE), row), pl.BlockSpec((tm, LANE), row)],
        out_specs=(pl.BlockSpec((tm, qk_w), row), pl.BlockSpec((tm, qk_w), row), pl.BlockSpec((tm, MLA_W), row)),
        compiler_params=_cparams("arbitrary"),
    )(p, q_g, kv_g, w_uq, w_ukv, *tabs)


_NT = (((1,), (1,)), ((), ()))
ATTN_TQ = 256
ATTN_CHAINS = 2
ATTN_TK = 256


def _skewed_pipeline(n_chains, n_chunks, stages):
    for t in range(n_chains + len(stages) - 1):
        active = [(s, t - s) for s in range(len(stages)) if 0 <= t - s < n_chains]
        for c in range(n_chunks):
            for s, chain in active:
                stages[s][0](chain, c)
        for s, chain in active:
            stages[s][1](chain)


def _acc(old, new, op):
    return new if old is None else op(old, new)


def _lane_halves(x, op):
    return op(x[:, :LANE], x[:, LANE:])


def _diff_attn_kernel(lam_ref, q_ref, k_ref, v_ref, g_ref, *rest, out_scale):
    o_ref = rest[-1]
    lam, g = lam_ref[0], g_ref[...]
    tq = min(ATTN_TQ, q_ref.shape[0])
    n_chains = q_ref.shape[0] // tq
    n_chunks = k_ref.shape[0] // ATTN_TK
    lane = lax.broadcasted_iota(jnp.int32, (1, LANE), 1)
    lo = (lane < DA_DQK).astype(F32)
    st = [dict(s=[], e=[], mx=[None, None], l=[None, None], o=None) for _ in range(n_chains)]
    rows = lambda i: slice(i * tq, (i + 1) * tq)
    keys = lambda c: slice(c * ATTN_TK, (c + 1) * ATTN_TK)

    def qk_chunk(i, c):
        d = st[i]
        if c == 0:
            qf = q_ref[rows(i), :].astype(F32)
            d["q"] = jnp.concatenate([(qf * lo).astype(BF16), (qf * (1.0 - lo)).astype(BF16)], axis=0)
        both = lax.dot_general(d["q"], k_ref[keys(c), :], _NT, preferred_element_type=F32)
        pair = []
        for m in range(2):
            s = both[m * tq:(m + 1) * tq]
            d["mx"][m] = _acc(d["mx"][m], _lane_halves(s, jnp.maximum), jnp.maximum)
            pair.append(s)
        d["s"].append(pair)

    def qk_done(i):
        st[i]["m"] = [jnp.max(mx, axis=-1, keepdims=True) for mx in st[i]["mx"]]

    def exp_chunk(i, c):
        d = st[i]
        pair = []
        for m in range(2):
            e = jnp.exp2(d["s"][c][m] - d["m"][m])
            d["l"][m] = _acc(d["l"][m], _lane_halves(e, jnp.add), jnp.add)
            pair.append(e)
        d["s"][c] = None
        d["e"].append(pair)

    def exp_done(i):
        d = st[i]
        l1, l2 = [jnp.sum(l, axis=-1, keepdims=True) for l in d["l"]]
        d["r1"] = 1.0 / l1
        d["cf"] = lam * l1 / l2

    def pv_chunk(i, c):
        d = st[i]
        w = (d["e"][c][0] - d["cf"] * d["e"][c][1]).astype(BF16)
        d["e"][c] = None
        d["o"] = _acc(d["o"], jnp.dot(w, v_ref[keys(c), :], preferred_element_type=F32), jnp.add)

    def pv_done(i):
        o = st[i]["o"] * st[i]["r1"]
        o_ref[rows(i), :] = (_rms(o) * g * out_scale).astype(BF16)

    _skewed_pipeline(n_chains, n_chunks, [(qk_chunk, qk_done), (exp_chunk, exp_done), (pv_chunk, pv_done)])


def _mla_attn_kernel(q_ref, k_ref, v_ref, *rest):
    o_ref = rest[-1]
    tq = min(ATTN_TQ, q_ref.shape[0])
    n_chains = q_ref.shape[0] // tq
    n_chunks = k_ref.shape[0] // ATTN_TK
    st = [dict(s=[], mx=None, l=None, o=None) for _ in range(n_chains)]
    rows = lambda i: slice(i * tq, (i + 1) * tq)
    keys = lambda c: slice(c * ATTN_TK, (c + 1) * ATTN_TK)

    def qk_chunk(i, c):
        d = st[i]
        s = lax.dot_general(q_ref[rows(i), :], k_ref[keys(c), :], _NT, preferred_element_type=F32)
        d["mx"] = _acc(d["mx"], _lane_halves(s, jnp.maximum), jnp.maximum)
        d["s"].append(s)

    def qk_done(i):
        st[i]["m"] = jnp.max(st[i]["mx"], axis=-1, keepdims=True)

    def pv_chunk(i, c):
        d = st[i]
        e = jnp.exp2(d["s"][c] - d["m"])
        d["s"][c] = None
        d["l"] = _acc(d["l"], _lane_halves(e, jnp.add), jnp.add)
        d["o"] = _acc(d["o"], jnp.dot(e.astype(BF16), v_ref[keys(c), :], preferred_element_type=F32), jnp.add)

    def pv_done(i):
        d = st[i]
        o_ref[rows(i), :] = (d["o"] * (1.0 / jnp.sum(d["l"], axis=-1, keepdims=True))).astype(BF16)

    _skewed_pipeline(n_chains, n_chunks, [(qk_chunk, qk_done), (pv_chunk, pv_done)])


def _diff_attn(qkv, lam, subln_g, n, nc, ctx_queries, lam_init):
    B, T, _ = qkv.shape
    H = DA_HEADS
    tq = _pick(n, (ATTN_TQ * ATTN_CHAINS, ATTN_TQ))
    kern = functools.partial(_diff_attn_kernel, out_scale=1.0 - lam_init)
    smem = pl.BlockSpec(memory_space=pltpu.SMEM)
    gspec = pl.BlockSpec((1, DA_DV), lambda b, h, i: (0, 0))
    oa = pl.pallas_call(
        kern,
        out_shape=jax.ShapeDtypeStruct((B, T if ctx_queries else n, DA_W), BF16),
        grid=(B, H, n // tq),
        in_specs=[smem,
                  pl.BlockSpec((None, tq, LANE), lambda b, h, i: (b, i, h)),
                  pl.BlockSpec((None, T, LANE), lambda b, h, i: (b, 0, H + h)),
                  pl.BlockSpec((None, T, LANE), lambda b, h, i: (b, 0, 2 * H + h)),
                  gspec],
        out_specs=pl.BlockSpec((None, tq, LANE), lambda b, h, i: (b, i, h)),
        compiler_params=_cparams("arbitrary", "arbitrary", "arbitrary"),
    )(lam, qkv, qkv, qkv, subln_g)
    if not ctx_queries:
        return oa
    cb = n // nc
    return pl.pallas_call(
        kern,
        out_shape=jax.ShapeDtypeStruct((B, T, DA_W), BF16),
        grid=(B, H, 1),
        in_specs=[smem,
                  pl.BlockSpec((None, nc, LANE), lambda b, h, i: (b, cb, h)),
                  pl.BlockSpec((None, nc, LANE), lambda b, h, i: (b, cb, H + h)),
                  pl.BlockSpec((None, nc, LANE), lambda b, h, i: (b, cb, 2 * H + h)),
                  gspec,
                  pl.BlockSpec(memory_space=pl.ANY)],
        out_specs=pl.BlockSpec((None, nc, LANE), lambda b, h, i: (b, cb, h)),
        input_output_aliases={5: 0},
        compiler_params=_cparams("arbitrary", "arbitrary", "arbitrary"),
    )(lam, qkv, qkv, qkv, subln_g, oa)


def _mla_attn(q, k, v, n, nc, ctx_queries):
    B, T, _ = q.shape
    H = MLA_HEADS
    tq = _pick(n, (ATTN_TQ * ATTN_CHAINS, ATTN_TQ))
    ob = pl.pallas_call(
        _mla_attn_kernel,
        out_shape=jax.ShapeDtypeStruct((B, T if ctx_queries else n, MLA_W), BF16),
        grid=(B, H, n // tq),
        in_specs=[pl.BlockSpec((None, tq, MLA_QK_PAD), lambda b, h, i: (b, i, h)),
                  pl.BlockSpec((None, T, MLA_QK_PAD), lambda b, h, i: (b, 0, h)),
                  pl.BlockSpec((None, T, MLA_DV), lambda b, h, i: (b, 0, h))],
        out_specs=pl.BlockSpec((None, tq, MLA_DV), lambda b, h, i: (b, i, h)),
        compiler_params=_cparams("arbitrary", "arbitrary", "arbitrary"),
    )(q, k, v)
    if not ctx_queries:
        return ob
    cb = n // nc
    return pl.pallas_call(
        _mla_attn_kernel,
        out_shape=jax.ShapeDtypeStruct((B, T, MLA_W), BF16),
        grid=(B, H, 1),
        in_specs=[pl.BlockSpec((None, nc, MLA_QK_PAD), lambda b, h, i: (b, cb, h)),
                  pl.BlockSpec((None, nc, MLA_QK_PAD), lambda b, h, i: (b, cb, h)),
                  pl.BlockSpec((None, nc, MLA_DV), lambda b, h, i: (b, cb, h)),
                  pl.BlockSpec(memory_space=pl.ANY)],
        out_specs=pl.BlockSpec((None, nc, MLA_DV), lambda b, h, i: (b, cb, h)),
        input_output_aliases={3: 0},
        compiler_params=_cparams("arbitrary", "arbitrary", "arbitrary"),
    )(q, k, v, ob)


def _hy_filter_kernel(feat_ref, dec_ref, w1_ref, b1_ref, w2_ref, b2_ref, w3_ref, b3_ref, w4_ref, fr_ref,
                      k_ref, s_ref):
    hp = lax.Precision.HIGHEST
    d, r = pl.program_id(0), pl.program_id(1)
    fr = fr_ref[...]
    h = jnp.sin(fr * (jnp.dot(feat_ref[...], w1_ref[...], precision=hp, preferred_element_type=F32) + b1_ref[...]))
    h = jnp.sin(fr * (jnp.dot(h, w2_ref[...], precision=hp, preferred_element_type=F32) + b2_ref[...]))
    h = jnp.sin(fr * (jnp.dot(h, w3_ref[...], precision=hp, preferred_element_type=F32) + b3_ref[...]))
    h = jnp.dot(h, w4_ref[...], precision=hp, preferred_element_type=F32)
    row = lax.broadcasted_iota(jnp.int32, (h.shape[0], 1), 0)
    first_bwd = jnp.where((d == 1) & (r == 0), 1.0, 0.0)
    scale = dec_ref[...] * (1.0 - jnp.where(row == 0, 1.0, 0.0) * first_bwd)

    @pl.when((d == 0) & (r == 0))
    def _():
        s_ref[...] = jnp.zeros(s_ref.shape, F32)

    for o in range(HY_ORDER):
        ko = h[:, o * HY_CH:(o + 1) * HY_CH] * scale
        k_ref[o] = ko
        s_ref[o] += jnp.broadcast_to(jnp.sum(jnp.abs(ko), axis=0, keepdims=True), (8, HY_CH))


def _hy_filter(n, w1, b1, w2, b2, w3, b3, w4, fr):
    C = HY_CH
    t = jnp.linspace(0.0, 1.0, n, dtype=F32)
    pos = jnp.arange(n, dtype=F32)
    t2 = jnp.concatenate([t, t[::-1]])[:, None]
    pos2 = jnp.concatenate([pos, pos[::-1]])[:, None]
    phase = (2.0 * math.pi / n) * pos2 * jnp.linspace(1e-4, HY_BANDS - 1, HY_BANDS, dtype=F32)[None, :]
    feat = jnp.concatenate([t2, jnp.cos(phase), -jnp.sin(phase)], axis=-1)
    feat = jnp.pad(feat, ((0, 0), (0, HY_FFN - HY_EMB)))
    dec = jnp.exp(-t2 * jnp.abs(jnp.linspace(HY_MIN_DECAY, HY_MAX_DECAY, C, dtype=F32)))
    w4d = w4.reshape(HY_FFN, HY_ORDER, 2, C).transpose(2, 0, 1, 3).reshape(2, HY_FFN, HY_ORDER * C)
    rb = min(512, n)
    nb = n // rb
    full = lambda d, r: (0, 0)
    return pl.pallas_call(
        _hy_filter_kernel,
        out_shape=(jax.ShapeDtypeStruct((HY_ORDER, 2 * n, C), F32),
                   jax.ShapeDtypeStruct((HY_ORDER, 8, C), F32)),
        grid=(2, nb),
        in_specs=[pl.BlockSpec((rb, HY_FFN), lambda d, r: (d * nb + r, 0)),
                  pl.BlockSpec((rb, C), lambda d, r: (d * nb + r, 0)),
                  pl.BlockSpec(w1.shape, full), pl.BlockSpec(b1.shape, full),
                  pl.BlockSpec(w2.shape, full), pl.BlockSpec(b2.shape, full),
                  pl.BlockSpec(w3.shape, full), pl.BlockSpec(b3.shape, full),
                  pl.BlockSpec((None, HY_FFN, HY_ORDER * C), lambda d, r: (d, 0, 0)),
                  pl.BlockSpec(fr.shape, full)],
        out_specs=(pl.BlockSpec((HY_ORDER, rb, C), lambda d, r: (0, d * nb + r, 0)),
                   pl.BlockSpec((HY_ORDER, 8, C), lambda d, r: (0, 0, 0))),
        compiler_params=_cparams("arbitrary", "arbitrary"),
    )(feat, dec, w1, b1, w2, b2, w3, b3, w4d, fr)


def _hy_dwconv_kernel(u_ref, w_ref, b_ref, o_ref):
    u = u_ref[...]
    L = u.shape[0]
    row = lax.broadcasted_iota(jnp.int32, (L, 1), 0)
    up = jnp.where(row == 0, 0.0, pltpu.roll(u, 1, 0))
    dn = jnp.where(row == L - 1, 0.0, pltpu.roll(u, L - 1, 0))
    o_ref[...] = up * w_ref[0:1, :] + u * w_ref[1:2, :] + dn * w_ref[2:3, :] + b_ref[...]


def _hy_dwconv(u, w, b, row_block, length):
    B = u.shape[0]
    cw = 256
    per = HY_CH // cw
    return pl.pallas_call(
        _hy_dwconv_kernel,
        out_shape=jax.ShapeDtypeStruct((3, B, length, HY_CH), F32),
        grid=(B, 3 * per),
        in_specs=[pl.BlockSpec((None, length, cw), lambda bb, j: (bb, row_block, j)),
                  pl.BlockSpec((3, cw), lambda bb, j: (0, j)),
                  pl.BlockSpec((1, cw), lambda bb, j: (0, j))],
        out_specs=pl.BlockSpec((None, None, length, cw), lambda bb, j: (j // per, bb, 0, j % per)),
        compiler_params=_cparams("arbitrary", "arbitrary"),
    )(u, w, b)


def _dft_tables(n):
    S = FFT_S
    M = 2 * n
    N1 = M // S
    H = N1 // 2
    s2 = np.arange(S)[:, None, None]
    k1 = np.arange(N1)[None, :, None]
    s1 = np.arange(N1)[None, None, :]
    ang = -2.0 * np.pi * ((k1 * (S * s1 + s2)) % M) / M
    fr, fi = np.cos(ang), np.sin(ang)
    g1f = np.concatenate([fr, fi], axis=1)
    frh, fih = fr[:, :, :H], fi[:, :, :H]
    g1d = np.concatenate([np.concatenate([frh, -fih], axis=2),
                          np.concatenate([fih, frh], axis=2)], axis=1)
    er = np.transpose(frh, (0, 2, 1)) / M
    ei = -np.transpose(fih, (0, 2, 1)) / M
    g3 = np.concatenate([np.concatenate([er, -ei], axis=2),
                         np.concatenate([ei, er], axis=2)], axis=1)
    a2 = -2.0 * np.pi * ((np.arange(S)[:, None] * np.arange(S)[None, :]) % S) / S
    f2r, f2i = np.cos(a2), np.sin(a2)
    g2 = np.block([[f2r, -f2i], [f2i, f2r]])
    g2i = np.block([[f2r, f2i], [-f2i, f2r]])
    cast = lambda a: jnp.asarray(a, dtype=F32).astype(BF16)
    return cast(g1d), cast(g1f), cast(g2), cast(g2i), cast(g3)


FFT_NS = 16


def _fft_s1_kernel(g_ref, x_ref, o_ref):
    xt = jnp.swapaxes(x_ref[...], 0, 1)
    y = jnp.stack([jnp.dot(g_ref[j], xt[j].astype(BF16), preferred_element_type=F32)
                   for j in range(xt.shape[0])], axis=0)
    o_ref[...] = jnp.swapaxes(y, 0, 1).astype(o_ref.dtype)


def _fft_s1(g, x):
    P, Ri, S, C = x.shape
    Ro = g.shape[1]
    ns = FFT_NS
    return pl.pallas_call(
        _fft_s1_kernel,
        out_shape=jax.ShapeDtypeStruct((P, Ro, S, C), BF16),
        grid=(S // ns, P),
        in_specs=[pl.BlockSpec((ns, Ro, Ri), lambda j, p: (j, 0, 0)),
                  pl.BlockSpec((None, Ri, ns, C), lambda j, p: (p, 0, j, 0))],
        out_specs=pl.BlockSpec((None, Ro, ns, C), lambda j, p: (p, 0, j, 0)),
        compiler_params=_cparams("arbitrary", "arbitrary"),
    )(g, x)


def _fft_s2_filt_kernel(a_ref, g_ref, rn_ref, o_ref, *, nk):
    S = FFT_S
    rn = rn_ref[...]
    for t in range(nk):
        d = jnp.concatenate([a_ref[0, t], a_ref[1, t]], axis=0)
        y = jnp.dot(g_ref[...], d, preferred_element_type=F32)
        o_ref[0, t] = y[:S] * rn
        o_ref[1, t] = y[S:] * rn


def _fft_s2_filt(a, g2, rnorm):
    O, _, N1, S, C = a.shape
    nk = 4
    blk = (None, 2, nk, S, C)
    return pl.pallas_call(
        functools.partial(_fft_s2_filt_kernel, nk=nk),
        out_shape=jax.ShapeDtypeStruct(a.shape, F32),
        grid=(O, N1 // nk),
        in_specs=[pl.BlockSpec(blk, lambda o, j: (o, 0, j, 0, 0)),
                  pl.BlockSpec(g2.shape, lambda o, j: (0, 0)),
                  pl.BlockSpec((None, 1, C), lambda o, j: (o, 0, 0))],
        out_specs=pl.BlockSpec(blk, lambda o, j: (o, 0, j, 0, 0)),
        compiler_params=_cparams("arbitrary", "arbitrary"),
    )(a, g2, rnorm)


def _fft_s2_kernel(a_ref, g_ref, gi_ref, kf_ref, o_ref, *, nk):
    S = FFT_S
    for t in range(nk):
        d = jnp.concatenate([a_ref[0, t], a_ref[1, t]], axis=0)
        y = jnp.dot(g_ref[...], d, preferred_element_type=F32)
        yr, yi = y[:S], y[S:]
        kr, ki = kf_ref[0, t], kf_ref[1, t]
        p = jnp.concatenate([yr * kr - yi * ki, yr * ki + yi * kr], axis=0).astype(BF16)
        b = jnp.dot(gi_ref[...], p, preferred_element_type=F32)
        o_ref[0, t] = b[:S].astype(BF16)
        o_ref[1, t] = b[S:].astype(BF16)


def _fft_s2(a, g2, g2i, kf, order):
    P, _, N1, S, C = a.shape
    nk = 4
    blk = (None, 2, nk, S, C)
    return pl.pallas_call(
        functools.partial(_fft_s2_kernel, nk=nk),
        out_shape=jax.ShapeDtypeStruct(a.shape, BF16),
        grid=(N1 // nk, P),
        in_specs=[pl.BlockSpec(blk, lambda j, p: (p, 0, j, 0, 0)),
                  pl.BlockSpec(g2.shape, lambda j, p: (0, 0)),
                  pl.BlockSpec(g2i.shape, lambda j, p: (0, 0)),
                  pl.BlockSpec(blk, lambda j, p: (order, 0, j, 0, 0))],
        out_specs=pl.BlockSpec(blk, lambda j, p: (p, 0, j, 0, 0)),
        compiler_params=_cparams("arbitrary", "arbitrary"),
    )(a, g2, g2i, kf)


def _fft_s3_kernel(g_ref, b_ref, z_ref, gate_ref, skip_ref, o_ref):
    bt = jnp.swapaxes(b_ref[...].astype(F32), 0, 1).astype(BF16)
    y = jnp.stack([jnp.dot(g_ref[j], bt[j], preferred_element_type=F32) for j in range(bt.shape[0])], axis=0)
    y = jnp.swapaxes(y, 0, 1)
    o_ref[...] = (gate_ref[...] * (y + skip_ref[...] * z_ref[...])).astype(o_ref.dtype)


def _fft_s3(g3, b, z, gate, skip, out_dtype):
    P, Ri, S, C = b.shape
    Ro = g3.shape[1]
    ns = FFT_NS
    dspec = pl.BlockSpec((None, Ro, ns, C), lambda j, p: (p, 0, j, 0))
    return pl.pallas_call(
        _fft_s3_kernel,
        out_shape=jax.ShapeDtypeStruct((P, Ro, S, C), out_dtype),
        grid=(S // ns, P),
        in_specs=[pl.BlockSpec((ns, Ro, Ri), lambda j, p: (j, 0, 0)),
                  pl.BlockSpec((None, Ri, ns, C), lambda j, p: (p, 0, j, 0)),
                  dspec, dspec,
                  pl.BlockSpec((1, C), lambda j, p: (0, 0))],
        out_specs=dspec,
        compiler_params=_cparams("arbitrary", "arbitrary"),
    )(g3, b, z, gate, skip)


def _dense_tables(m):
    M = 2 * m
    ang = -2.0 * np.pi * ((np.arange(M)[:, None] * np.arange(M)[None, :]) % M) / M
    fr, fi = np.cos(ang), np.sin(ang)
    gk = np.concatenate([fr, fi], axis=0)
    gd = np.block([[fr[:, :m], -fi[:, :m]], [fi[:, :m], fr[:, :m]]])
    er, ei = fr[:m, :] / M, -fi[:m, :] / M
    gi = np.block([[er, -ei], [ei, er]])
    cast = lambda a: jnp.asarray(a, dtype=F32).astype(BF16)
    return cast(gk), cast(gd), cast(gi)


def _dense_spec_kernel(k_ref, g_ref, rn_ref, o_ref):
    o_ref[...] = jnp.dot(g_ref[...], k_ref[...].astype(BF16), preferred_element_type=F32) * rn_ref[...]


def _dense_spec(kc, gk, rnorm):
    O, M, C = kc.shape
    return pl.pallas_call(
        _dense_spec_kernel,
        out_shape=jax.ShapeDtypeStruct((O, 2 * M, C), F32),
        grid=(O,),
        in_specs=[pl.BlockSpec((None, M, C), lambda o: (o, 0, 0)),
                  pl.BlockSpec(gk.shape, lambda o: (0, 0)),
                  pl.BlockSpec((None, 1, C), lambda o: (o, 0, 0))],
        out_specs=pl.BlockSpec((None, 2 * M, C), lambda o: (o, 0, 0)),
        compiler_params=_cparams("arbitrary"),
    )(kc, gk, rnorm)


def _dense_conv_kernel(x_ref, gd_ref, gi_ref, kf_ref, gate_ref, skip_ref, o_ref):
    x = x_ref[...]
    y = jnp.dot(gd_ref[...], x.astype(BF16), preferred_element_type=F32)
    M = y.shape[0] // 2
    yr, yi = y[:M], y[M:]
    kr, ki = kf_ref[:M], kf_ref[M:]
    p = jnp.concatenate([yr * kr - yi * ki, yr * ki + yi * kr], axis=0).astype(BF16)
    conv = jnp.dot(gi_ref[...], p, preferred_element_type=F32)
    o_ref[...] = (gate_ref[...] * (conv + skip_ref[...] * x)).astype(o_ref.dtype)


def _dense_conv(x, gd, gi, kf, order, gate, skip, out_dtype):
    P, R, C = x.shape
    dspec = pl.BlockSpec((None, R, C), lambda p: (p, 0, 0))
    return pl.pallas_call(
        _dense_conv_kernel,
        out_shape=jax.ShapeDtypeStruct((P, R, C), out_dtype),
        grid=(P,),
        in_specs=[dspec,
                  pl.BlockSpec(gd.shape, lambda p: (0, 0)),
                  pl.BlockSpec(gi.shape, lambda p: (0, 0)),
                  pl.BlockSpec((None,) + kf.shape[1:], lambda p: (order, 0, 0)),
                  dspec,
                  pl.BlockSpec((1, C), lambda p: (0, 0))],
        out_specs=dspec,
        compiler_params=_cparams("arbitrary"),
    )(x, gd, gi, kf, gate, skip)


def _merge_kernel(oa_ref, ob_ref, oc_ref, wa_ref, wb_ref, wc_ref, g0_ref, g1_ref, g2_ref, o_ref):
    ya = jnp.dot(oa_ref[...], wa_ref[...], preferred_element_type=F32)
    yb = jnp.dot(ob_ref[...], wb_ref[...], preferred_element_type=F32)
    yc = jnp.dot(oc_ref[...], wc_ref[...], preferred_element_type=F32)
    m = g0_ref[...].astype(F32) * ya + g1_ref[...].astype(F32) * yb + g2_ref[...].astype(F32) * yc
    o_ref[...] = m.astype(BF16)


def _merge(oa, ob, oc, w_ba, w_bb, w_bc, gates, nb, tb):
    D = D_MODEL
    tm = _pick(tb, (1024, 512, 256))
    tn = 512
    nt = D // tn
    oa, ob, oc, gates = (_by_batch(a, nb) for a in (oa, ob, oc, gates))
    row = lambda b, i, j: (b, i, 0)
    col = lambda b, i, j: (0, j)
    return pl.pallas_call(
        _merge_kernel,
        out_shape=jax.ShapeDtypeStruct((nb, tb, D), BF16),
        grid=(nb, tb // tm, nt),
        in_specs=[pl.BlockSpec((None, tm, DA_W), row), pl.BlockSpec((None, tm, MLA_W), row),
                  pl.BlockSpec((None, tm, HY_CH), row),
                  pl.BlockSpec((DA_W, tn), col), pl.BlockSpec((MLA_W, tn), col), pl.BlockSpec((HY_CH, tn), col),
                  pl.BlockSpec((None, tm, tn), lambda b, i, j: (b, i, j)),
                  pl.BlockSpec((None, tm, tn), lambda b, i, j: (b, i, nt + j)),
                  pl.BlockSpec((None, tm, tn), lambda b, i, j: (b, i, 2 * nt + j))],
        out_specs=pl.BlockSpec((None, tm, tn), lambda b, i, j: (b, i, j)),
        compiler_params=_cparams("arbitrary", "arbitrary", "arbitrary"),
    )(oa, ob, oc, w_ba, w_bb, w_bc, gates, gates, gates).reshape(nb * tb, D)


def _wo_ln_kernel(m_ref, w_ref, xs_ref, mod_ref, g_ref, b_ref, xs1_ref, h2_ref, *, sel, groups):
    y = jnp.dot(m_ref[...], w_ref[...], preferred_element_type=F32)
    g, b = g_ref[...], b_ref[...]
    for q in range(groups):
        mod = mod_ref[sel(pl.program_id(0), pl.program_id(1) * groups + q)]
        sl = slice(q * ROW_GROUP, (q + 1) * ROW_GROUP)
        x1 = _ln(DEEPNORM_ALPHA * xs_ref[sl, :] + mod[2:3, :] * y[sl, :]) * g + b
        xs1_ref[sl, :] = x1
        h2_ref[sl, :] = (_ln(x1) * (1.0 + mod[4:5, :]) + mod[3:4, :]).astype(BF16)


def _wo_ln(merged, w_o, xs, mod, ln_g, ln_b, sel, nb, tb):
    D = D_MODEL
    tm = _pick(tb, (512, 256))
    merged, xs = _by_batch(merged, nb), _by_batch(xs, nb)
    row = lambda b, i: (b, i, 0)
    full2 = lambda b, i: (0, 0)
    xs1, h2 = pl.pallas_call(
        functools.partial(_wo_ln_kernel, sel=sel, groups=tm // ROW_GROUP),
        out_shape=(jax.ShapeDtypeStruct((nb, tb, D), F32), jax.ShapeDtypeStruct((nb, tb, D), BF16)),
        grid=(nb, tb // tm),
        in_specs=[pl.BlockSpec((None, tm, D), row),
                  pl.BlockSpec((D, D), full2),
                  pl.BlockSpec((None, tm, D), row),
                  pl.BlockSpec(mod.shape, lambda b, i: (0, 0, 0)),
                  pl.BlockSpec((1, D), full2), pl.BlockSpec((1, D), full2)],
        out_specs=(pl.BlockSpec((None, tm, D), row), pl.BlockSpec((None, tm, D), row)),
        compiler_params=_cparams("arbitrary", "arbitrary"),
    )(merged, w_o, xs, mod, ln_g, ln_b)
    return xs1.reshape(nb * tb, D), h2.reshape(nb * tb, D)


FFN_HALO = 16


def _ffn_up_kernel(hp_ref, h_ref, hn_ref, wa_ref, wv_ref, cw_ref, cb_ref, kp_ref, kn_ref, o_ref):
    tm, tn = o_ref.shape
    hm = h_ref[...]
    hext = jnp.concatenate([hp_ref[...], hm, hn_ref[...]], axis=0)
    a = jnp.dot(hext, wa_ref[...], preferred_element_type=F32)
    v = jnp.dot(hm, wv_ref[...], preferred_element_type=F32)
    ext = tm + 2 * FFN_HALO
    rep = tn // LANE
    keep_prev = jnp.tile(kp_ref[...], (1, rep))
    keep_next = jnp.tile(kn_ref[...], (1, rep))
    a_prev = pltpu.roll(a, 1, 0)[FFN_HALO:FFN_HALO + tm] * keep_prev
    a_next = pltpu.roll(a, ext - 1, 0)[FFN_HALO:FFN_HALO + tm] * keep_next
    cv = a_prev * cw_ref[0:1, :] + a[FFN_HALO:FFN_HALO + tm] * cw_ref[1:2, :] + a_next * cw_ref[2:3, :] + cb_ref[...]
    o_ref[...] = (cv * jax.nn.sigmoid(cv) * v).astype(BF16)


def _ffn_up(h2, w_up, conv_w, conv_b, keep_prev, keep_next, nb, tb):
    D = D_MODEL
    tm = _pick(tb, (1024, 512, 256))
    tn = 512
    nt = D_FF // tn
    hb = tm // FFN_HALO
    last = tb // FFN_HALO - 1
    h2, keep_prev, keep_next = (_by_batch(a, nb) for a in (h2, keep_prev, keep_next))
    mask_spec = pl.BlockSpec((None, tm, LANE), lambda b, i, j: (b, i, 0))
    return pl.pallas_call(
        _ffn_up_kernel,
        out_shape=jax.ShapeDtypeStruct((nb, tb, D_FF), BF16),
        grid=(nb, tb // tm, nt),
        in_specs=[pl.BlockSpec((None, FFN_HALO, D), lambda b, i, j: (b, jnp.maximum(i * hb - 1, 0), 0)),
                  pl.BlockSpec((None, tm, D), lambda b, i, j: (b, i, 0)),
                  pl.BlockSpec((None, FFN_HALO, D), lambda b, i, j: (b, jnp.minimum((i + 1) * hb, last), 0)),
                  pl.BlockSpec((D, tn), lambda b, i, j: (0, j)),
                  pl.BlockSpec((D, tn), lambda b, i, j: (0, nt + j)),
                  pl.BlockSpec((3, tn), lambda b, i, j: (0, j)),
                  pl.BlockSpec((1, tn), lambda b, i, j: (0, j)),
                  mask_spec, mask_spec],
        out_specs=pl.BlockSpec((None, tm, tn), lambda b, i, j: (b, i, j)),
        compiler_params=_cparams("arbitrary", "arbitrary", "arbitrary"),
    )(h2, h2, h2, w_up, w_up, conv_w, conv_b, keep_prev, keep_next).reshape(nb * tb, D_FF)


def _ffn_down_kernel(u_ref, w_ref, xs_ref, mod_ref, g_ref, b_ref, o_ref, acc_ref, *, sel, groups, nk):
    k = pl.program_id(2)

    @pl.when(k == 0)
    def _():
        acc_ref[...] = jnp.zeros(acc_ref.shape, F32)

    acc_ref[...] += jnp.dot(u_ref[...], w_ref[...], preferred_element_type=F32)

    @pl.when(k == nk - 1)
    def _():
        g, b = g_ref[...], b_ref[...]
        for q in range(groups):
            mod = mod_ref[sel(pl.program_id(0), pl.program_id(1) * groups + q)]
            sl = slice(q * ROW_GROUP, (q + 1) * ROW_GROUP)
            o_ref[sl, :] = _ln(DEEPNORM_ALPHA * xs_ref[sl, :] + mod[5:6, :] * acc_ref[sl, :]) * g + b


def _ffn_down(u, w_down, xs1, mod, ln_g, ln_b, sel, nb, tb):
    D = D_MODEL
    tm = _pick(tb, (512, 256))
    tk = D_FF // 2
    nk = D_FF // tk
    u, xs1 = _by_batch(u, nb), _by_batch(xs1, nb)
    row = lambda b, i, k: (b, i, 0)
    full2 = lambda b, i, k: (0, 0)
    return pl.pallas_call(
        functools.partial(_ffn_down_kernel, sel=sel, groups=tm // ROW_GROUP, nk=nk),
        out_shape=jax.ShapeDtypeStruct((nb, tb, D), F32),
        grid=(nb, tb // tm, nk),
        in_specs=[pl.BlockSpec((None, tm, tk), lambda b, i, k: (b, i, k)),
                  pl.BlockSpec((tk, D), lambda b, i, k: (k, 0)),
                  pl.BlockSpec((None, tm, D), row),
                  pl.BlockSpec(mod.shape, lambda b, i, k: (0, 0, 0)),
                  pl.BlockSpec((1, D), full2), pl.BlockSpec((1, D), full2)],
        out_specs=pl.BlockSpec((None, tm, D), row),
        scratch_shapes=[pltpu.VMEM((tm, D), F32)],
        compiler_params=_cparams("arbitrary", "arbitrary", "arbitrary"),
    )(u, w_down, xs1, mod, ln_g, ln_b).reshape(nb * tb, D)


def _rope_tables(B, n, nc):
    half = DA_DQK // 2
    inv = ROPE_BASE ** (-jnp.arange(0, half, 2, dtype=F32) / half)
    t = jnp.arange(n, dtype=jnp.int32)
    ang_r = (t // GRID_W).astype(F32)[:, None] * inv[None, :]
    ang_c = (t % GRID_W).astype(F32)[:, None] * inv[None, :]
    ang = jnp.concatenate([ang_r, ang_r, ang_c, ang_c], axis=-1)
    cos, sin = jnp.cos(ang), jnp.sin(ang)
    upper = (jnp.arange(DA_DQK) % half) >= half // 2
    sa = jnp.where(upper, sin, 0.0)
    sb = jnp.where(upper, 0.0, -sin)

    def full(tab, fill):
        tab = jnp.concatenate([tab, jnp.full((nc, DA_DQK), fill, F32)], axis=0)
        tab = jnp.tile(tab, (B, LANE // DA_DQK))
        return tab

    return full(cos, 1.0), full(sa, 0.0), full(sb, 0.0)


def _conv_masks(B, n, nc):
    T = n + nc
    t = jnp.arange(T)
    keep_prev = ((t != 0) & (t != n)).astype(F32)
    keep_next = ((t != n - 1) & (t != T - 1)).astype(F32)
    widen = lambda m: jnp.tile(m[:, None], (B, LANE))
    return widen(keep_prev), widen(keep_next)


def _split_w_in(w):
    D = w.shape[0]
    o = 0
    def take(width):
        nonlocal o
        part = w[:, o:o + width]
        o += width
        return part
    regroup = lambda p: p.reshape(D, 2, DA_HEADS, DA_DQK).transpose(0, 2, 1, 3).reshape(D, DA_W)
    wq, wk, wv = regroup(take(DA_W)), regroup(take(DA_W)), take(DA_W)
    wcq, wckv, wkr = take(MLA_Q_RANK), take(MLA_KV_RANK), take(MLA_ROPE)
    why, wg = take(3 * HY_CH), take(N_BRANCH * D_MODEL)
    w_qkv = jnp.concatenate([wq, wk, wv], axis=1).astype(BF16)
    w_mla = jnp.concatenate([wcq, wckv, wkr, jnp.zeros((D, LANE - MLA_ROPE), w.dtype)], axis=1).astype(BF16)
    return w_qkv, w_mla, why.astype(BF16), wg.astype(BF16)


def _pad_w_uq(w):
    w = w.reshape(MLA_Q_RANK, MLA_HEADS, MLA_NOPE + MLA_ROPE)
    w = jnp.pad(w, ((0, 0), (0, 0), (0, MLA_QK_PAD - MLA_NOPE - MLA_ROPE)))
    return w.reshape(MLA_Q_RANK, MLA_HEADS * MLA_QK_PAD).astype(BF16)


def _split_w_ukv(w):
    w = w.reshape(MLA_KV_RANK, MLA_HEADS, 2, MLA_NOPE).transpose(0, 2, 1, 3)
    return w.reshape(MLA_KV_RANK, 2 * MLA_W).astype(BF16)


def _hyena(u3, layer, p, n, nc, with_ctx, tables):
    B = u3.shape[0]
    C = HY_CH
    P = B // 2
    g1d, g1f, g2, g2i, g3, gk, gd, gi = tables
    S = FFT_S
    N1 = 2 * n // S
    mlp = (p["hy_ffn_w1p"][layer], p["hy_ffn_b1"][layer][None], p["hy_ffn_w2"][layer], p["hy_ffn_b2"][layer][None],
           p["hy_ffn_w3"][layer], p["hy_ffn_b3"][layer][None], p["hy_ffn_w4"][layer], p["hy_freq"][layer][None])
    skip = p["hy_skip"][layer]
    cw, cb = p["hy_conv_w"][layer], p["hy_conv_b"][layer][None]

    kc, sums = _hy_filter(n, *mlp)
    kf = _fft_s1(g1f, kc.reshape(HY_ORDER, N1, S, C))
    kf = _fft_s2_filt(kf.reshape(HY_ORDER, 2, N1, S, C), g2, 1.0 / sums[:, 0:1, :])
    dw = _hy_dwconv(u3, cw, cb, 0, n).reshape(3, P, N1, S, C)
    z = dw[0]
    for o in range(HY_ORDER):
        a = _fft_s1(g1d, z).reshape(P, 2, N1, S, C)
        b = _fft_s2(a, g2, g2i, kf, o).reshape(P, 2 * N1, S, C)
        z = _fft_s3(g3, b, z, dw[1 + o], skip[o][None], F32 if o + 1 < HY_ORDER else BF16)
    oc_lat = z.reshape(B, n, C)

    if with_ctx:
        kcc, sumc = _hy_filter(nc, *mlp)
        kfc = _dense_spec(kcc, gk, 1.0 / sumc[:, 0:1, :])
        dwc = _hy_dwconv(u3, cw, cb, n // nc, nc).reshape(3, P, 2 * nc, C)
        zc = dwc[0]
        for o in range(HY_ORDER):
            zc = _dense_conv(zc, gd, gi, kfc, o, dwc[1 + o], skip[o][None], F32 if o + 1 < HY_ORDER else BF16)
        return jnp.concatenate([oc_lat, zc.reshape(B, nc, C)], axis=1)
    return oc_lat


def kernel(x, c, ctx, c_ctx, ada_w, ada_b, w_in, da_lambda, da_subln_g, mla_q_g, mla_w_uq, mla_kv_g, mla_w_ukv, hy_conv_w, hy_conv_b, hy_ffn_w1, hy_ffn_b1, hy_ffn_w2, hy_ffn_b2, hy_ffn_w3, hy_ffn_b3, hy_ffn_w4, hy_freq, hy_skip, w_branch_a, w_branch_b, w_branch_c, w_out, ln1_g, ln1_b, ffn_w_up, ffn_conv_w, ffn_conv_b, ffn_w_down, ln2_g, ln2_b):
    B, n, D = x.shape
    nc = ctx.shape[1]
    T = n + nc
    rows = B * T
    assert D == D_MODEL and B % 2 == 0 and B < 8
    assert n % ROW_GROUP == 0 and nc % ROW_GROUP == 0 and n % nc == 0 and n % GRID_W == 0
    assert (2 * n) % (8 * FFT_S) == 0 and T % ATTN_TK == 0 and nc % ATTN_TK == 0
    geom = (T // ROW_GROUP, n // ROW_GROUP, B)

    hy = dict(hy_ffn_w1p=jnp.pad(hy_ffn_w1, ((0, 0), (0, HY_FFN - HY_EMB), (0, 0))), hy_ffn_b1=hy_ffn_b1,
              hy_ffn_w2=hy_ffn_w2, hy_ffn_b2=hy_ffn_b2, hy_ffn_w3=hy_ffn_w3, hy_ffn_b3=hy_ffn_b3,
              hy_ffn_w4=hy_ffn_w4, hy_freq=hy_freq, hy_skip=hy_skip, hy_conv_w=hy_conv_w, hy_conv_b=hy_conv_b)
    tables = _dft_tables(n) + _dense_tables(nc)
    rope = _rope_tables(B, n, nc)
    keep_prev, keep_next = _conv_masks(B, n, nc)

    cc = jnp.concatenate([c, c_ctx[None], jnp.zeros((8 - B - 1, D), F32)], axis=0)
    mods = _ada(cc, ada_w, ada_b[:, None, :]).reshape(DEPTH, 8, 6, D)

    xs = jnp.concatenate([x, ctx], axis=1).reshape(rows, D)
    for i in range(DEPTH):
        last = i == DEPTH - 1
        lam_init = 0.8 - 0.6 * math.exp(-0.3 * i)
        lq1, lk1, lq2, lk2 = da_lambda[i].astype(F32)
        lam = (jnp.exp(jnp.sum(lq1 * lk1)) - jnp.exp(jnp.sum(lq2 * lk2)) + lam_init).reshape(1)
        mod = mods[i]
        w_qkv, w_mla, w_hy, w_g = _split_w_in(w_in[i])

        if last:
            nb, tb = B, n
            sel = lambda b, g: b
        else:
            nb, tb = 1, rows
            sel = lambda b, g: _mod_row(g, *geom)

        h = _lnmod(xs, mod, geom)
        qkv = _qkv_proj(h, w_qkv, rope)
        p_mla = _matmul(h, w_mla, F32, w_mla.shape[1])
        u_hy = _matmul(h, w_hy, F32, 768, nb=nb, tb=tb)
        gates = _matmul(h, w_g, BF16, 1024, act="sigmoid", nb=nb, tb=tb)

        q_m, k_m, v_m = _mla_prep(p_mla, mla_q_g[i][None], mla_kv_g[i][None], _pad_w_uq(mla_w_uq[i]),
                                  _split_w_ukv(mla_w_ukv[i]), rope)
        oa = _diff_attn(qkv.reshape(B, T, 3 * DA_W), lam, da_subln_g[i][None], n, nc, not last, lam_init)
        ob = _mla_attn(q_m.reshape(B, T, -1), k_m.reshape(B, T, -1), v_m.reshape(B, T, -1), n, nc, not last)
        oc = _hyena(u_hy.reshape(B, -1, 3 * HY_CH), i, hy, n, nc, not last, tables)

        merged = _merge(oa.reshape(-1, DA_W), ob.reshape(-1, MLA_W), oc.reshape(-1, HY_CH),
                        w_branch_a[i].astype(BF16), w_branch_b[i].astype(BF16), w_branch_c[i].astype(BF16), gates,
                        nb, tb)
        xs1, h2 = _wo_ln(merged, w_out[i].astype(BF16), xs, mod, ln1_g[i][None], ln1_b[i][None], sel, nb, tb)
        u = _ffn_up(h2, ffn_w_up[i].astype(BF16), ffn_conv_w[i], ffn_conv_b[i][None], keep_prev, keep_next, nb, tb)
        xs = _ffn_down(u, ffn_w_down[i].astype(BF16), xs1, mod, ln2_g[i][None], ln2_b[i][None], sel, nb, tb)
    return xs.reshape(B, n, D)
```

```python
import functools
import math

import numpy as np
import jax
import jax.numpy as jnp
from jax import lax
from jax.experimental import pallas as pl
from jax.experimental.pallas import tpu as pltpu

F32 = jnp.float32
BF16 = jnp.bfloat16

D_MODEL = 2048
DEPTH = 2
GRID_W = 64
ROPE_BASE = 10000.0
NORM_EPS = 1e-6
DA_HEADS = 6
DA_DQK = 64
DA_DV = 128
DA_W = DA_HEADS * DA_DV
MLA_HEADS = 6
MLA_Q_RANK = 512
MLA_KV_RANK = 256
MLA_NOPE = 128
MLA_ROPE = 64
MLA_DV = 128
MLA_W = MLA_HEADS * MLA_DV
MLA_QK_PAD = 256
HY_CH = 512
HY_ORDER = 2
HY_EMB = 33
HY_BANDS = (HY_EMB - 1) // 2
HY_FFN = 64
HY_MIN_DECAY = math.log(1e-2) / 1.5
HY_MAX_DECAY = math.log(1e-2) / 0.3
D_FF = 5632
N_BRANCH = 3
DEEPNORM_ALPHA = (2 * DEPTH) ** 0.25
LOG2E = 1.4426950408889634

ROW_GROUP = 256
LANE = 128
FFT_S = 128
VMEM_LIMIT = 52 * 1024 * 1024


def _cparams(*sem):
    return pltpu.CompilerParams(dimension_semantics=sem, vmem_limit_bytes=VMEM_LIMIT)


def _pick(total, prefs):
    for p in prefs:
        if total % p == 0:
            return p
    raise ValueError(f"no tile for {total} in {prefs}")


def _ln(x):
    mu = jnp.mean(x, axis=-1, keepdims=True)
    xc = x - mu
    var = jnp.mean(xc * xc, axis=-1, keepdims=True)
    return xc * lax.rsqrt(var + NORM_EPS)


def _rms(x):
    return x * lax.rsqrt(jnp.mean(x * x, axis=-1, keepdims=True) + NORM_EPS)


def _rope128(u, cos, sa, sb):
    return u * cos + pltpu.roll(u, 16, 1) * sa + pltpu.roll(u, LANE - 16, 1) * sb


def _ada_kernel(c_ref, w_ref, b_ref, o_ref):
    a = c_ref[...]
    a = a * jax.nn.sigmoid(a)
    o_ref[...] = jnp.dot(a.astype(BF16), w_ref[...].astype(BF16), preferred_element_type=F32) + b_ref[...]


def _ada(cc, ada_w, ada_b):
    L, D, N = ada_w.shape
    tn = 1024
    return pl.pallas_call(
        _ada_kernel,
        out_shape=jax.ShapeDtypeStruct((L, 8, N), F32),
        grid=(L, N // tn),
        in_specs=[pl.BlockSpec((8, D), lambda l, j: (0, 0)),
                  pl.BlockSpec((None, D, tn), lambda l, j: (l, 0, j)),
                  pl.BlockSpec((None, 1, tn), lambda l, j: (l, 0, j))],
        out_specs=pl.BlockSpec((None, 8, tn), lambda l, j: (l, 0, j)),
        compiler_params=_cparams("arbitrary", "arbitrary"),
    )(cc, ada_w, ada_b)


def _lnmod_kernel(x_ref, m_ref, o_ref):
    y = _ln(x_ref[...])
    o_ref[...] = (y * (1.0 + m_ref[1:2, :]) + m_ref[0:1, :]).astype(BF16)


def _mod_row(g, gpb, lat_groups, n_batch):
    return jnp.where(g % gpb < lat_groups, g // gpb, n_batch)


def _lnmod(xs, mod, geom):
    rows, D = xs.shape
    gpb, lat_groups, n_batch = geom
    return pl.pallas_call(
        _lnmod_kernel,
        out_shape=jax.ShapeDtypeStruct((rows, D), BF16),
        grid=(rows // ROW_GROUP,),
        in_specs=[pl.BlockSpec((ROW_GROUP, D), lambda i: (i, 0)),
                  pl.BlockSpec((None, 6, D), lambda i: (_mod_row(i, gpb, lat_groups, n_batch), 0, 0))],
        out_specs=pl.BlockSpec((ROW_GROUP, D), lambda i: (i, 0)),
        compiler_params=_cparams("arbitrary"),
    )(xs, mod)


def _mm_kernel(a_ref, w_ref, o_ref, *, act):
    acc = jnp.dot(a_ref[...], w_ref[...], preferred_element_type=F32)
    if act == "sigmoid":
        acc = jax.nn.sigmoid(acc)
    o_ref[...] = acc.astype(o_ref.dtype)


def _by_batch(a, nb):
    return a.reshape(nb, a.shape[0] // nb, a.shape[1])


def _matmul(a, w, out_dtype, tn, act=None, nb=1, tb=None):
    a = _by_batch(a, nb)
    K = a.shape[2]
    tb = tb or a.shape[1]
    N = w.shape[1]
    tm = _pick(tb, (1024, 512, 256))
    return pl.pallas_call(
        functools.partial(_mm_kernel, act=act),
        out_shape=jax.ShapeDtypeStruct((nb, tb, N), out_dtype),
        grid=(nb, tb // tm, N // tn),
        in_specs=[pl.BlockSpec((None, tm, K), lambda b, i, j: (b, i, 0)),
                  pl.BlockSpec((K, tn), lambda b, i, j: (0, j))],
        out_specs=pl.BlockSpec((None, tm, tn), lambda b, i, j: (b, i, j)),
        compiler_params=_cparams("arbitrary", "arbitrary", "arbitrary"),
    )(a, w).reshape(nb * tb, N)


def _qkv_kernel(h_ref, w_ref, cos_ref, sa_ref, sb_ref, o_ref, *, qscale):
    j = pl.program_id(1)
    acc = jnp.dot(h_ref[...], w_ref[...], preferred_element_type=F32)

    @pl.when(j < 2)
    def _():
        cos, sa, sb = cos_ref[...], sa_ref[...], sb_ref[...]
        scale = jnp.where(j == 0, qscale, 1.0).astype(F32)
        for c in range(DA_HEADS):
            u = acc[:, c * LANE:(c + 1) * LANE]
            o_ref[:, c * LANE:(c + 1) * LANE] = (_rope128(u, cos, sa, sb) * scale).astype(BF16)

    @pl.when(j == 2)
    def _():
        o_ref[...] = acc.astype(BF16)


def _qkv_proj(h, w_qkv, tabs):
    M, K = h.shape
    tm = _pick(M, (1024, 512, 256))
    tn = DA_W
    tab_spec = pl.BlockSpec((tm, LANE), lambda i, j: (i, 0))
    return pl.pallas_call(
        functools.partial(_qkv_kernel, qscale=DA_DQK ** -0.5 * LOG2E),
        out_shape=jax.ShapeDtypeStruct((M, 3 * DA_W), BF16),
        grid=(M // tm, 3),
        in_specs=[pl.BlockSpec((tm, K), lambda i, j: (i, 0)),
                  pl.BlockSpec((K, tn), lambda i, j: (0, j)),
                  tab_spec, tab_spec, tab_spec],
        out_specs=pl.BlockSpec((tm, tn), lambda i, j: (i, j)),
        compiler_params=_cparams("arbitrary", "arbitrary"),
    )(h, w_qkv, *tabs)


def _mla_prep_kernel(p_ref, qg_ref, kvg_ref, wuq_ref, wukv_ref, cos_ref, sa_ref, sb_ref,
                     q_ref, k_ref, v_ref, *, qscale):
    p = p_ref[...]
    cos, sa, sb = cos_ref[...], sa_ref[...], sb_ref[...]
    cq = p[:, :MLA_Q_RANK]
    ckv = p[:, MLA_Q_RANK:MLA_Q_RANK + MLA_KV_RANK]
    kr = p[:, MLA_Q_RANK + MLA_KV_RANK:]
    qn = (_rms(cq) * qg_ref[...]).astype(BF16)
    q = jnp.dot(qn, wuq_ref[...], preferred_element_type=F32)
    kvn = (_rms(ckv) * kvg_ref[...]).astype(BF16)
    kv = jnp.dot(kvn, wukv_ref[...], preferred_element_type=F32)
    krr = _rope128(kr, cos, sa, sb).astype(BF16)
    for h in range(MLA_HEADS):
        o = h * MLA_QK_PAD
        q_ref[:, o:o + LANE] = (q[:, o:o + LANE] * qscale).astype(BF16)
        q_ref[:, o + LANE:o + 2 * LANE] = (_rope128(q[:, o + LANE:o + 2 * LANE], cos, sa, sb) * qscale).astype(BF16)
        k_ref[:, o:o + LANE] = kv[:, h * LANE:(h + 1) * LANE].astype(BF16)
        k_ref[:, o + LANE:o + 2 * LANE] = krr
    v_ref[...] = kv[:, MLA_W:].astype(BF16)


def _mla_prep(p, q_g, kv_g, w_uq, w_ukv, tabs):
    M, W = p.shape
    tm = _pick(M, (512, 256))
    row = lambda i: (i, 0)
    full = lambda i: (0, 0)
    qk_w = MLA_HEADS * MLA_QK_PAD
    return pl.pallas_call(
        functools.partial(_mla_prep_kernel, qscale=(MLA_NOPE + MLA_ROPE) ** -0.5 * LOG2E),
        out_shape=(jax.ShapeDtypeStruct((M, qk_w), BF16),
                   jax.ShapeDtypeStruct((M, qk_w), BF16),
                   jax.ShapeDtypeStruct((M, MLA_W), BF16)),
        grid=(M // tm,),
        in_specs=[pl.BlockSpec((tm, W), row),
                  pl.BlockSpec((1, MLA_Q_RANK), full),
                  pl.BlockSpec((1, MLA_KV_RANK), full),
                  pl.BlockSpec(w_uq.shape, full),
                  pl.BlockSpec(w_ukv.shape, full),
                  pl.BlockSpec((tm, LANE), row), pl.BlockSpec((tm, LANE), row), pl.BlockSpec((tm, LANE), row)],
        out_specs=(pl.BlockSpec((tm, qk_w), row), pl.BlockSpec((tm, qk_w), row), pl.BlockSpec((tm, MLA_W), row)),
        compiler_params=_cparams("arbitrary"),
    )(p, q_g, kv_g, w_uq, w_ukv, *tabs)


_NT = (((1,), (1,)), ((), ()))
ATTN_TQ = 256
ATTN_CHAINS = 4
ATTN_TK = 256


def _skewed_pipeline(n_chains, n_chunks, stages):
    for t in range(n_chains + len(stages) - 1):
        active = [(s, t - s) for s in range(len(stages)) if 0 <= t - s < n_chains]
        for c in range(n_chunks):
            for s, chain in active:
                stages[s][0](chain, c)
        for s, chain in active:
            stages[s][1](chain)


def _acc(old, new, op):
    return new if old is None else op(old, new)


def _lane_halves(x, op):
    return op(x[:, :LANE], x[:, LANE:])


def _diff_attn_kernel(lam_ref, q_ref, k_ref, v_ref, g_ref, *rest, out_scale):
    o_ref = rest[-1]
    lam, g = lam_ref[0], g_ref[...]
    tq = min(ATTN_TQ, q_ref.shape[0])
    n_chains = q_ref.shape[0] // tq
    n_chunks = k_ref.shape[0] // ATTN_TK
    lane = lax.broadcasted_iota(jnp.int32, (1, LANE), 1)
    lo = (lane < DA_DQK).astype(F32)
    st = [dict(s=[], e=[], mx=[None, None], l=[None, None], o=None) for _ in range(n_chains)]
    rows = lambda i: slice(i * tq, (i + 1) * tq)
    keys = lambda c: slice(c * ATTN_TK, (c + 1) * ATTN_TK)

    def qk_chunk(i, c):
        d = st[i]
        if c == 0:
            qf = q_ref[rows(i), :].astype(F32)
            d["q"] = jnp.concatenate([(qf * lo).astype(BF16), (qf * (1.0 - lo)).astype(BF16)], axis=0)
        both = lax.dot_general(d["q"], k_ref[keys(c), :], _NT, preferred_element_type=F32)
        pair = []
        for m in range(2):
            s = both[m * tq:(m + 1) * tq]
            d["mx"][m] = _acc(d["mx"][m], _lane_halves(s, jnp.maximum), jnp.maximum)
            pair.append(s)
        d["s"].append(pair)

    def qk_done(i):
        st[i]["m"] = [jnp.max(mx, axis=-1, keepdims=True) for mx in st[i]["mx"]]

    def exp_chunk(i, c):
        d = st[i]
        pair = []
        for m in range(2):
            e = jnp.exp2(d["s"][c][m] - d["m"][m])
            d["l"][m] = _acc(d["l"][m], _lane_halves(e, jnp.add), jnp.add)
            pair.append(e)
        d["s"][c] = None
        d["e"].append(pair)

    def exp_done(i):
        d = st[i]
        l1, l2 = [jnp.sum(l, axis=-1, keepdims=True) for l in d["l"]]
        d["r1"] = 1.0 / l1
        d["cf"] = lam * l1 / l2

    def pv_chunk(i, c):
        d = st[i]
        w = (d["e"][c][0] - d["cf"] * d["e"][c][1]).astype(BF16)
        d["e"][c] = None
        d["o"] = _acc(d["o"], jnp.dot(w, v_ref[keys(c), :], preferred_element_type=F32), jnp.add)

    def pv_done(i):
        o = st[i]["o"] * st[i]["r1"]
        o_ref[rows(i), :] = (_rms(o) * g * out_scale).astype(BF16)

    _skewed_pipeline(n_chains, n_chunks, [(qk_chunk, qk_done), (exp_chunk, exp_done), (pv_chunk, pv_done)])


def _mla_attn_kernel(q_ref, k_ref, v_ref, *rest):
    o_ref = rest[-1]
    tq = min(ATTN_TQ, q_ref.shape[0])
    n_chains = q_ref.shape[0] // tq
    n_chunks = k_ref.shape[0] // ATTN_TK
    st = [dict(s=[], mx=None, l=None, o=None) for _ in range(n_chains)]
    rows = lambda i: slice(i * tq, (i + 1) * tq)
    keys = lambda c: slice(c * ATTN_TK, (c + 1) * ATTN_TK)

    def qk_chunk(i, c):
        d = st[i]
        s = lax.dot_general(q_ref[rows(i), :], k_ref[keys(c), :], _NT, preferred_element_type=F32)
        d["mx"] = _acc(d["mx"], _lane_halves(s, jnp.maximum), jnp.maximum)
        d["s"].append(s)

    def qk_done(i):
        st[i]["m"] = jnp.max(st[i]["mx"], axis=-1, keepdims=True)

    def pv_chunk(i, c):
        d = st[i]
        e = jnp.exp2(d["s"][c] - d["m"])
        d["s"][c] = None
        d["l"] = _acc(d["l"], _lane_halves(e, jnp.add), jnp.add)
        d["o"] = _acc(d["o"], jnp.dot(e.astype(BF16), v_ref[keys(c), :], preferred_element_type=F32), jnp.add)

    def pv_done(i):
        d = st[i]
        o_ref[rows(i), :] = (d["o"] * (1.0 / jnp.sum(d["l"], axis=-1, keepdims=True))).astype(BF16)

    _skewed_pipeline(n_chains, n_chunks, [(qk_chunk, qk_done), (pv_chunk, pv_done)])


def _diff_attn(qkv, lam, subln_g, n, nc, ctx_queries, lam_init):
    B, T, _ = qkv.shape
    H = DA_HEADS
    tq = _pick(n, (ATTN_TQ * ATTN_CHAINS, ATTN_TQ))
    kern = functools.partial(_diff_attn_kernel, out_scale=1.0 - lam_init)
    smem = pl.BlockSpec(memory_space=pltpu.SMEM)
    gspec = pl.BlockSpec((1, DA_DV), lambda b, h, i: (0, 0))
    oa = pl.pallas_call(
        kern,
        out_shape=jax.ShapeDtypeStruct((B, T if ctx_queries else n, DA_W), BF16),
        grid=(B, H, n // tq),
        in_specs=[smem,
                  pl.BlockSpec((None, tq, LANE), lambda b, h, i: (b, i, h)),
                  pl.BlockSpec((None, T, LANE), lambda b, h, i: (b, 0, H + h)),
                  pl.BlockSpec((None, T, LANE), lambda b, h, i: (b, 0, 2 * H + h)),
                  gspec],
        out_specs=pl.BlockSpec((None, tq, LANE), lambda b, h, i: (b, i, h)),
        compiler_params=_cparams("arbitrary", "arbitrary", "arbitrary"),
    )(lam, qkv, qkv, qkv, subln_g)
    if not ctx_queries:
        return oa
    cb = n // nc
    return pl.pallas_call(
        kern,
        out_shape=jax.ShapeDtypeStruct((B, T, DA_W), BF16),
        grid=(B, H, 1),
        in_specs=[smem,
                  pl.BlockSpec((None, nc, LANE), lambda b, h, i: (b, cb, h)),
                  pl.BlockSpec((None, nc, LANE), lambda b, h, i: (b, cb, H + h)),
                  pl.BlockSpec((None, nc, LANE), lambda b, h, i: (b, cb, 2 * H + h)),
                  gspec,
                  pl.BlockSpec(memory_space=pl.ANY)],
        out_specs=pl.BlockSpec((None, nc, LANE), lambda b, h, i: (b, cb, h)),
        input_output_aliases={5: 0},
        compiler_params=_cparams("arbitrary", "arbitrary", "arbitrary"),
    )(lam, qkv, qkv, qkv, subln_g, oa)


def _mla_attn(q, k, v, n, nc, ctx_queries):
    B, T, _ = q.shape
    H = MLA_HEADS
    tq = _pick(n, (ATTN_TQ * ATTN_CHAINS, ATTN_TQ))
    ob = pl.pallas_call(
        _mla_attn_kernel,
        out_shape=jax.ShapeDtypeStruct((B, T if ctx_queries else n, MLA_W), BF16),
        grid=(B, H, n // tq),
        in_specs=[pl.BlockSpec((None, tq, MLA_QK_PAD), lambda b, h, i: (b, i, h)),
                  pl.BlockSpec((None, T, MLA_QK_PAD), lambda b, h, i: (b, 0, h)),
                  pl.BlockSpec((None, T, MLA_DV), lambda b, h, i: (b, 0, h))],
        out_specs=pl.BlockSpec((None, tq, MLA_DV), lambda b, h, i: (b, i, h)),
        compiler_params=_cparams("arbitrary", "arbitrary", "arbitrary"),
    )(q, k, v)
    if not ctx_queries:
        return ob
    cb = n // nc
    return pl.pallas_call(
        _mla_attn_kernel,
        out_shape=jax.ShapeDtypeStruct((B, T, MLA_W), BF16),
        grid=(B, H, 1),
        in_specs=[pl.BlockSpec((None, nc, MLA_QK_PAD), lambda b, h, i: (b, cb, h)),
                  pl.BlockSpec((None, nc, MLA_QK_PAD), lambda b, h, i: (b, cb, h)),
                  pl.BlockSpec((None, nc, MLA_DV), lambda b, h, i: (b, cb, h)),
                  pl.BlockSpec(memory_space=pl.ANY)],
        out_specs=pl.BlockSpec((None, nc, MLA_DV), lambda b, h, i: (b, cb, h)),
        input_output_aliases={3: 0},
        compiler_params=_cparams("arbitrary", "arbitrary", "arbitrary"),
    )(q, k, v, ob)


def _hy_filter_kernel(feat_ref, dec_ref, w1_ref, b1_ref, w2_ref, b2_ref, w3_ref, b3_ref, w4_ref, fr_ref,
                      k_ref, s_ref):
    hp = lax.Precision.HIGHEST
    d, r = pl.program_id(0), pl.program_id(1)
    fr = fr_ref[...]
    h = jnp.sin(fr * (jnp.dot(feat_ref[...], w1_ref[...], precision=hp, preferred_element_type=F32) + b1_ref[...]))
    h = jnp.sin(fr * (jnp.dot(h, w2_ref[...], precision=hp, preferred_element_type=F32) + b2_ref[...]))
    h = jnp.sin(fr * (jnp.dot(h, w3_ref[...], precision=hp, preferred_element_type=F32) + b3_ref[...]))
    h = jnp.dot(h, w4_ref[...], precision=hp, preferred_element_type=F32)
    row = lax.broadcasted_iota(jnp.int32, (h.shape[0], 1), 0)
    first_bwd = jnp.where((d == 1) & (r == 0), 1.0, 0.0)
    scale = dec_ref[...] * (1.0 - jnp.where(row == 0, 1.0, 0.0) * first_bwd)

    @pl.when((d == 0) & (r == 0))
    def _():
        s_ref[...] = jnp.zeros(s_ref.shape, F32)

    for o in range(HY_ORDER):
        ko = h[:, o * HY_CH:(o + 1) * HY_CH] * scale
        k_ref[o] = ko
        s_ref[o] += jnp.broadcast_to(jnp.sum(jnp.abs(ko), axis=0, keepdims=True), (8, HY_CH))


def _hy_filter(n, w1, b1, w2, b2, w3, b3, w4, fr):
    C = HY_CH
    t = jnp.linspace(0.0, 1.0, n, dtype=F32)
    pos = jnp.arange(n, dtype=F32)
    t2 = jnp.concatenate([t, t[::-1]])[:, None]
    pos2 = jnp.concatenate([pos, pos[::-1]])[:, None]
    phase = (2.0 * math.pi / n) * pos2 * jnp.linspace(1e-4, HY_BANDS - 1, HY_BANDS, dtype=F32)[None, :]
    feat = jnp.concatenate([t2, jnp.cos(phase), -jnp.sin(phase)], axis=-1)
    feat = jnp.pad(feat, ((0, 0), (0, HY_FFN - HY_EMB)))
    dec = jnp.exp(-t2 * jnp.abs(jnp.linspace(HY_MIN_DECAY, HY_MAX_DECAY, C, dtype=F32)))
    w4d = w4.reshape(HY_FFN, HY_ORDER, 2, C).transpose(2, 0, 1, 3).reshape(2, HY_FFN, HY_ORDER * C)
    rb = min(512, n)
    nb = n // rb
    full = lambda d, r: (0, 0)
    return pl.pallas_call(
        _hy_filter_kernel,
        out_shape=(jax.ShapeDtypeStruct((HY_ORDER, 2 * n, C), F32),
                   jax.ShapeDtypeStruct((HY_ORDER, 8, C), F32)),
        grid=(2, nb),
        in_specs=[pl.BlockSpec((rb, HY_FFN), lambda d, r: (d * nb + r, 0)),
                  pl.BlockSpec((rb, C), lambda d, r: (d * nb + r, 0)),
                  pl.BlockSpec(w1.shape, full), pl.BlockSpec(b1.shape, full),
                  pl.BlockSpec(w2.shape, full), pl.BlockSpec(b2.shape, full),
                  pl.BlockSpec(w3.shape, full), pl.BlockSpec(b3.shape, full),
                  pl.BlockSpec((None, HY_FFN, HY_ORDER * C), lambda d, r: (d, 0, 0)),
                  pl.BlockSpec(fr.shape, full)],
        out_specs=(pl.BlockSpec((HY_ORDER, rb, C), lambda d, r: (0, d * nb + r, 0)),
                   pl.BlockSpec((HY_ORDER, 8, C), lambda d, r: (0, 0, 0))),
        compiler_params=_cparams("arbitrary", "arbitrary"),
    )(feat, dec, w1, b1, w2, b2, w3, b3, w4d, fr)


def _hy_dwconv_kernel(u_ref, w_ref, b_ref, o_ref):
    u = u_ref[...]
    L = u.shape[0]
    row = lax.broadcasted_iota(jnp.int32, (L, 1), 0)
    up = jnp.where(row == 0, 0.0, pltpu.roll(u, 1, 0))
    dn = jnp.where(row == L - 1, 0.0, pltpu.roll(u, L - 1, 0))
    o_ref[...] = up * w_ref[0:1, :] + u * w_ref[1:2, :] + dn * w_ref[2:3, :] + b_ref[...]


def _hy_dwconv(u, w, b, row_block, length):
    B = u.shape[0]
    cw = 256
    per = HY_CH // cw
    return pl.pallas_call(
        _hy_dwconv_kernel,
        out_shape=jax.ShapeDtypeStruct((3, B, length, HY_CH), F32),
        grid=(B, 3 * per),
        in_specs=[pl.BlockSpec((None, length, cw), lambda bb, j: (bb, row_block, j)),
                  pl.BlockSpec((3, cw), lambda bb, j: (0, j)),
                  pl.BlockSpec((1, cw), lambda bb, j: (0, j))],
        out_specs=pl.BlockSpec((None, None, length, cw), lambda bb, j: (j // per, bb, 0, j % per)),
        compiler_params=_cparams("arbitrary", "arbitrary"),
    )(u, w, b)


def _dft_tables(n):
    S = FFT_S
    M = 2 * n
    N1 = M // S
    H = N1 // 2
    s2 = np.arange(S)[:, None, None]
    k1 = np.arange(N1)[None, :, None]
    s1 = np.arange(N1)[None, None, :]
    ang = -2.0 * np.pi * ((k1 * (S * s1 + s2)) % M) / M
    fr, fi = np.cos(ang), np.sin(ang)
    g1f = np.concatenate([fr, fi], axis=1)
    frh, fih = fr[:, :, :H], fi[:, :, :H]
    g1d = np.concatenate([np.concatenate([frh, -fih], axis=2),
                          np.concatenate([fih, frh], axis=2)], axis=1)
    er = np.transpose(frh, (0, 2, 1)) / M
    ei = -np.transpose(fih, (0, 2, 1)) / M
    g3 = np.concatenate([np.concatenate([er, -ei], axis=2),
                         np.concatenate([ei, er], axis=2)], axis=1)
    a2 = -2.0 * np.pi * ((np.arange(S)[:, None] * np.arange(S)[None, :]) % S) / S
    f2r, f2i = np.cos(a2), np.sin(a2)
    g2 = np.block([[f2r, -f2i], [f2i, f2r]])
    g2i = np.block([[f2r, f2i], [-f2i, f2r]])
    cast = lambda a: jnp.asarray(a, dtype=F32).astype(BF16)
    return cast(g1d), cast(g1f), cast(g2), cast(g2i), cast(g3)


FFT_NS = 16


def _fft_s1_kernel(g_ref, x_ref, o_ref):
    xt = jnp.swapaxes(x_ref[...], 0, 1)
    y = jnp.stack([jnp.dot(g_ref[j], xt[j].astype(BF16), preferred_element_type=F32)
                   for j in range(xt.shape[0])], axis=0)
    o_ref[...] = jnp.swapaxes(y, 0, 1).astype(o_ref.dtype)


def _fft_s1(g, x):
    P, Ri, S, C = x.shape
    Ro = g.shape[1]
    ns = FFT_NS
    return pl.pallas_call(
        _fft_s1_kernel,
        out_shape=jax.ShapeDtypeStruct((P, Ro, S, C), BF16),
        grid=(S // ns, P),
        in_specs=[pl.BlockSpec((ns, Ro, Ri), lambda j, p: (j, 0, 0)),
                  pl.BlockSpec((None, Ri, ns, C), lambda j, p: (p, 0, j, 0))],
        out_specs=pl.BlockSpec((None, Ro, ns, C), lambda j, p: (p, 0, j, 0)),
        compiler_params=_cparams("arbitrary", "arbitrary"),
    )(g, x)


def _fft_s2_filt_kernel(a_ref, g_ref, rn_ref, o_ref, *, nk):
    S = FFT_S
    rn = rn_ref[...]
    for t in range(nk):
        d = jnp.concatenate([a_ref[0, t], a_ref[1, t]], axis=0)
        y = jnp.dot(g_ref[...], d, preferred_element_type=F32)
        o_ref[0, t] = y[:S] * rn
        o_ref[1, t] = y[S:] * rn


def _fft_s2_filt(a, g2, rnorm):
    O, _, N1, S, C = a.shape
    nk = 4
    blk = (None, 2, nk, S, C)
    return pl.pallas_call(
        functools.partial(_fft_s2_filt_kernel, nk=nk),
        out_shape=jax.ShapeDtypeStruct(a.shape, F32),
        grid=(O, N1 // nk),
        in_specs=[pl.BlockSpec(blk, lambda o, j: (o, 0, j, 0, 0)),
                  pl.BlockSpec(g2.shape, lambda o, j: (0, 0)),
                  pl.BlockSpec((None, 1, C), lambda o, j: (o, 0, 0))],
        out_specs=pl.BlockSpec(blk, lambda o, j: (o, 0, j, 0, 0)),
        compiler_params=_cparams("arbitrary", "arbitrary"),
    )(a, g2, rnorm)


def _fft_s2_kernel(a_ref, g_ref, gi_ref, kf_ref, o_ref, *, nk):
    S = FFT_S
    for t in range(nk):
        d = jnp.concatenate([a_ref[0, t], a_ref[1, t]], axis=0)
        y = jnp.dot(g_ref[...], d, preferred_element_type=F32)
        yr, yi = y[:S], y[S:]
        kr, ki = kf_ref[0, t], kf_ref[1, t]
        p = jnp.concatenate([yr * kr - yi * ki, yr * ki + yi * kr], axis=0).astype(BF16)
        b = jnp.dot(gi_ref[...], p, preferred_element_type=F32)
        o_ref[0, t] = b[:S].astype(BF16)
        o_ref[1, t] = b[S:].astype(BF16)


def _fft_s2(a, g2, g2i, kf, order):
    P, _, N1, S, C = a.shape
    nk = 4
    blk = (None, 2, nk, S, C)
    return pl.pallas_call(
        functools.partial(_fft_s2_kernel, nk=nk),
        out_shape=jax.ShapeDtypeStruct(a.shape, BF16),
        grid=(N1 // nk, P),
        in_specs=[pl.BlockSpec(blk, lambda j, p: (p, 0, j, 0, 0)),
                  pl.BlockSpec(g2.shape, lambda j, p: (0, 0)),
                  pl.BlockSpec(g2i.shape, lambda j, p: (0, 0)),
                  pl.BlockSpec(blk, lambda j, p: (order, 0, j, 0, 0))],
        out_specs=pl.BlockSpec(blk, lambda j, p: (p, 0, j, 0, 0)),
        compiler_params=_cparams("arbitrary", "arbitrary"),
    )(a, g2, g2i, kf)


def _fft_s3_kernel(g_ref, b_ref, z_ref, gate_ref, skip_ref, o_ref):
    bt = jnp.swapaxes(b_ref[...].astype(F32), 0, 1).astype(BF16)
    y = jnp.stack([jnp.dot(g_ref[j], bt[j], preferred_element_type=F32) for j in range(bt.shape[0])], axis=0)
    y = jnp.swapaxes(y, 0, 1)
    o_ref[...] = (gate_ref[...] * (y + skip_ref[...] * z_ref[...])).astype(o_ref.dtype)


def _fft_s3(g3, b, z, gate, skip, out_dtype):
    P, Ri, S, C = b.shape
    Ro = g3.shape[1]
    ns = FFT_NS
    dspec = pl.BlockSpec((None, Ro, ns, C), lambda j, p: (p, 0, j, 0))
    return pl.pallas_call(
        _fft_s3_kernel,
        out_shape=jax.ShapeDtypeStruct((P, Ro, S, C), out_dtype),
        grid=(S // ns, P),
        in_specs=[pl.BlockSpec((ns, Ro, Ri), lambda j, p: (j, 0, 0)),
                  pl.BlockSpec((None, Ri, ns, C), lambda j, p: (p, 0, j, 0)),
                  dspec, dspec,
                  pl.BlockSpec((1, C), lambda j, p: (0, 0))],
        out_specs=dspec,
        compiler_params=_cparams("arbitrary", "arbitrary"),
    )(g3, b, z, gate, skip)


def _dense_tables(m):
    M = 2 * m
    ang = -2.0 * np.pi * ((np.arange(M)[:, None] * np.arange(M)[None, :]) % M) / M
    fr, fi = np.cos(ang), np.sin(ang)
    gk = np.concatenate([fr, fi], axis=0)
    gd = np.block([[fr[:, :m], -fi[:, :m]], [fi[:, :m], fr[:, :m]]])
    er, ei = fr[:m, :] / M, -fi[:m, :] / M
    gi = np.block([[er, -ei], [ei, er]])
    cast = lambda a: jnp.asarray(a, dtype=F32).astype(BF16)
    return cast(gk), cast(gd), cast(gi)


def _dense_spec_kernel(k_ref, g_ref, rn_ref, o_ref):
    o_ref[...] = jnp.dot(g_ref[...], k_ref[...].astype(BF16), preferred_element_type=F32) * rn_ref[...]


def _dense_spec(kc, gk, rnorm):
    O, M, C = kc.shape
    return pl.pallas_call(
        _dense_spec_kernel,
        out_shape=jax.ShapeDtypeStruct((O, 2 * M, C), F32),
        grid=(O,),
        in_specs=[pl.BlockSpec((None, M, C), lambda o: (o, 0, 0)),
                  pl.BlockSpec(gk.shape, lambda o: (0, 0)),
                  pl.BlockSpec((None, 1, C), lambda o: (o, 0, 0))],
        out_specs=pl.BlockSpec((None, 2 * M, C), lambda o: (o, 0, 0)),
        compiler_params=_cparams("arbitrary"),
    )(kc, gk, rnorm)


def _dense_conv_kernel(x_ref, gd_ref, gi_ref, kf_ref, gate_ref, skip_ref, o_ref):
    x = x_ref[...]
    y = jnp.dot(gd_ref[...], x.astype(BF16), preferred_element_type=F32)
    M = y.shape[0] // 2
    yr, yi = y[:M], y[M:]
    kr, ki = kf_ref[:M], kf_ref[M:]
    p = jnp.concatenate([yr * kr - yi * ki, yr * ki + yi * kr], axis=0).astype(BF16)
    conv = jnp.dot(gi_ref[...], p, preferred_element_type=F32)
    o_ref[...] = (gate_ref[...] * (conv + skip_ref[...] * x)).astype(o_ref.dtype)


def _dense_conv(x, gd, gi, kf, order, gate, skip, out_dtype):
    P, R, C = x.shape
    dspec = pl.BlockSpec((None, R, C), lambda p: (p, 0, 0))
    return pl.pallas_call(
        _dense_conv_kernel,
        out_shape=jax.ShapeDtypeStruct((P, R, C), out_dtype),
        grid=(P,),
        in_specs=[dspec,
                  pl.BlockSpec(gd.shape, lambda p: (0, 0)),
                  pl.BlockSpec(gi.shape, lambda p: (0, 0)),
                  pl.BlockSpec((None,) + kf.shape[1:], lambda p: (order, 0, 0)),
                  dspec,
                  pl.BlockSpec((1, C), lambda p: (0, 0))],
        out_specs=dspec,
        compiler_params=_cparams("arbitrary"),
    )(x, gd, gi, kf, gate, skip)


def _merge_kernel(oa_ref, ob_ref, oc_ref, wa_ref, wb_ref, wc_ref, g0_ref, g1_ref, g2_ref, o_ref):
    ya = jnp.dot(oa_ref[...], wa_ref[...], preferred_element_type=F32)
    yb = jnp.dot(ob_ref[...], wb_ref[...], preferred_element_type=F32)
    yc = jnp.dot(oc_ref[...], wc_ref[...], preferred_element_type=F32)
    m = g0_ref[...].astype(F32) * ya + g1_ref[...].astype(F32) * yb + g2_ref[...].astype(F32) * yc
    o_ref[...] = m.astype(BF16)


def _merge(oa, ob, oc, w_ba, w_bb, w_bc, gates, nb, tb):
    D = D_MODEL
    tm = _pick(tb, (1024, 512, 256))
    tn = 512
    nt = D // tn
    oa, ob, oc, gates = (_by_batch(a, nb) for a in (oa, ob, oc, gates))
    row = lambda b, i, j: (b, i, 0)
    col = lambda b, i, j: (0, j)
    return pl.pallas_call(
        _merge_kernel,
        out_shape=jax.ShapeDtypeStruct((nb, tb, D), BF16),
        grid=(nb, tb // tm, nt),
        in_specs=[pl.BlockSpec((None, tm, DA_W), row), pl.BlockSpec((None, tm, MLA_W), row),
                  pl.BlockSpec((None, tm, HY_CH), row),
                  pl.BlockSpec((DA_W, tn), col), pl.BlockSpec((MLA_W, tn), col), pl.BlockSpec((HY_CH, tn), col),
                  pl.BlockSpec((None, tm, tn), lambda b, i, j: (b, i, j)),
                  pl.BlockSpec((None, tm, tn), lambda b, i, j: (b, i, nt + j)),
                  pl.BlockSpec((None, tm, tn), lambda b, i, j: (b, i, 2 * nt + j))],
        out_specs=pl.BlockSpec((None, tm, tn), lambda b, i, j: (b, i, j)),
        compiler_params=_cparams("arbitrary", "arbitrary", "arbitrary"),
    )(oa, ob, oc, w_ba, w_bb, w_bc, gates, gates, gates).reshape(nb * tb, D)


def _wo_ln_kernel(m_ref, w_ref, xs_ref, mod_ref, g_ref, b_ref, xs1_ref, h2_ref, *, sel, groups):
    y = jnp.dot(m_ref[...], w_ref[...], preferred_element_type=F32)
    g, b = g_ref[...], b_ref[...]
    for q in range(groups):
        mod = mod_ref[sel(pl.program_id(0), pl.program_id(1) * groups + q)]
        sl = slice(q * ROW_GROUP, (q + 1) * ROW_GROUP)
        x1 = _ln(DEEPNORM_ALPHA * xs_ref[sl, :] + mod[2:3, :] * y[sl, :]) * g + b
        xs1_ref[sl, :] = x1
        h2_ref[sl, :] = (_ln(x1) * (1.0 + mod[4:5, :]) + mod[3:4, :]).astype(BF16)


def _wo_ln(merged, w_o, xs, mod, ln_g, ln_b, sel, nb, tb):
    D = D_MODEL
    tm = _pick(tb, (512, 256))
    merged, xs = _by_batch(merged, nb), _by_batch(xs, nb)
    row = lambda b, i: (b, i, 0)
    full2 = lambda b, i: (0, 0)
    xs1, h2 = pl.pallas_call(
        functools.partial(_wo_ln_kernel, sel=sel, groups=tm // ROW_GROUP),
        out_shape=(jax.ShapeDtypeStruct((nb, tb, D), F32), jax.ShapeDtypeStruct((nb, tb, D), BF16)),
        grid=(nb, tb // tm),
        in_specs=[pl.BlockSpec((None, tm, D), row),
                  pl.BlockSpec((D, D), full2),
                  pl.BlockSpec((None, tm, D), row),
                  pl.BlockSpec(mod.shape, lambda b, i: (0, 0, 0)),
                  pl.BlockSpec((1, D), full2), pl.BlockSpec((1, D), full2)],
        out_specs=(pl.BlockSpec((None, tm, D), row), pl.BlockSpec((None, tm, D), row)),
        compiler_params=_cparams("arbitrary", "arbitrary"),
    )(merged, w_o, xs, mod, ln_g, ln_b)
    return xs1.reshape(nb * tb, D), h2.reshape(nb * tb, D)


FFN_HALO = 16


def _ffn_up_kernel(hp_ref, h_ref, hn_ref, wa_ref, wv_ref, cw_ref, cb_ref, kp_ref, kn_ref, o_ref, wa_s, wv_s):
    tm, tn = o_ref.shape

    @pl.when((pl.program_id(1) == 0) & (pl.program_id(2) == 0))
    def _():
        wa_s[...] = wa_ref[...].astype(BF16)
        wv_s[...] = wv_ref[...].astype(BF16)

    hm = h_ref[...]
    hext = jnp.concatenate([hp_ref[...], hm, hn_ref[...]], axis=0)
    a = jnp.dot(hext, wa_s[...], preferred_element_type=F32)
    v = jnp.dot(hm, wv_s[...], preferred_element_type=F32)
    ext = tm + 2 * FFN_HALO
    rep = tn // LANE
    keep_prev = jnp.tile(kp_ref[...], (1, rep))
    keep_next = jnp.tile(kn_ref[...], (1, rep))
    a_prev = pltpu.roll(a, 1, 0)[FFN_HALO:FFN_HALO + tm] * keep_prev
    a_next = pltpu.roll(a, ext - 1, 0)[FFN_HALO:FFN_HALO + tm] * keep_next
    cv = a_prev * cw_ref[0:1, :] + a[FFN_HALO:FFN_HALO + tm] * cw_ref[1:2, :] + a_next * cw_ref[2:3, :] + cb_ref[...]
    o_ref[...] = (cv * jax.nn.sigmoid(cv) * v).astype(BF16)


def _ffn_up(h2, w_up, layer, conv_w, conv_b, keep_prev, keep_next, nb, tb):
    D = D_MODEL
    tm = _pick(tb, (1024, 512, 256))
    tn = 512
    nt = D_FF // tn
    hb = tm // FFN_HALO
    last = tb // FFN_HALO - 1
    h2, keep_prev, keep_next = (_by_batch(a, nb) for a in (h2, keep_prev, keep_next))
    mask_spec = pl.BlockSpec((None, tm, LANE), lambda j, b, i: (b, i, 0))
    return pl.pallas_call(
        _ffn_up_kernel,
        out_shape=jax.ShapeDtypeStruct((nb, tb, D_FF), BF16),
        grid=(nt, nb, tb // tm),
        in_specs=[pl.BlockSpec((None, FFN_HALO, D), lambda j, b, i: (b, jnp.maximum(i * hb - 1, 0), 0)),
                  pl.BlockSpec((None, tm, D), lambda j, b, i: (b, i, 0)),
                  pl.BlockSpec((None, FFN_HALO, D), lambda j, b, i: (b, jnp.minimum((i + 1) * hb, last), 0)),
                  pl.BlockSpec((None, D, tn), lambda j, b, i: (layer, 0, j)),
                  pl.BlockSpec((None, D, tn), lambda j, b, i: (layer, 0, nt + j)),
                  pl.BlockSpec((3, tn), lambda j, b, i: (0, j)),
                  pl.BlockSpec((1, tn), lambda j, b, i: (0, j)),
                  mask_spec, mask_spec],
        out_specs=pl.BlockSpec((None, tm, tn), lambda j, b, i: (b, i, j)),
        scratch_shapes=[pltpu.VMEM((D, tn), BF16), pltpu.VMEM((D, tn), BF16)],
        compiler_params=_cparams("arbitrary", "arbitrary", "arbitrary"),
    )(h2, h2, h2, w_up, w_up, conv_w, conv_b, keep_prev, keep_next).reshape(nb * tb, D_FF)


def _ffn_down_kernel(u_ref, w_ref, xs_ref, mod_ref, g_ref, b_ref, o_ref, acc_ref, *, sel, groups, nk):
    k = pl.program_id(2)

    @pl.when(k == 0)
    def _():
        acc_ref[...] = jnp.zeros(acc_ref.shape, F32)

    acc_ref[...] += jnp.dot(u_ref[...], w_ref[...], preferred_element_type=F32)

    @pl.when(k == nk - 1)
    def _():
        g, b = g_ref[...], b_ref[...]
        for q in range(groups):
            mod = mod_ref[sel(pl.program_id(0), pl.program_id(1) * groups + q)]
            sl = slice(q * ROW_GROUP, (q + 1) * ROW_GROUP)
            o_ref[sl, :] = _ln(DEEPNORM_ALPHA * xs_ref[sl, :] + mod[5:6, :] * acc_ref[sl, :]) * g + b


def _ffn_down(u, w_down, xs1, mod, ln_g, ln_b, sel, nb, tb):
    D = D_MODEL
    tm = _pick(tb, (512, 256))
    tk = D_FF // 2
    nk = D_FF // tk
    u, xs1 = _by_batch(u, nb), _by_batch(xs1, nb)
    row = lambda b, i, k: (b, i, 0)
    full2 = lambda b, i, k: (0, 0)
    return pl.pallas_call(
        functools.partial(_ffn_down_kernel, sel=sel, groups=tm // ROW_GROUP, nk=nk),
        out_shape=jax.ShapeDtypeStruct((nb, tb, D), F32),
        grid=(nb, tb // tm, nk),
        in_specs=[pl.BlockSpec((None, tm, tk), lambda b, i, k: (b, i, k)),
                  pl.BlockSpec((tk, D), lambda b, i, k: (k, 0)),
                  pl.BlockSpec((None, tm, D), row),
                  pl.BlockSpec(mod.shape, lambda b, i, k: (0, 0, 0)),
                  pl.BlockSpec((1, D), full2), pl.BlockSpec((1, D), full2)],
        out_specs=pl.BlockSpec((None, tm, D), row),
        scratch_shapes=[pltpu.VMEM((tm, D), F32)],
        compiler_params=_cparams("arbitrary", "arbitrary", "arbitrary"),
    )(u, w_down, xs1, mod, ln_g, ln_b).reshape(nb * tb, D)


def _rope_tables(B, n, nc):
    half = DA_DQK // 2
    inv = ROPE_BASE ** (-jnp.arange(0, half, 2, dtype=F32) / half)
    t = jnp.arange(n, dtype=jnp.int32)
    ang_r = (t // GRID_W).astype(F32)[:, None] * inv[None, :]
    ang_c = (t % GRID_W).astype(F32)[:, None] * inv[None, :]
    ang = jnp.concatenate([ang_r, ang_r, ang_c, ang_c], axis=-1)
    cos, sin = jnp.cos(ang), jnp.sin(ang)
    upper = (jnp.arange(DA_DQK) % half) >= half // 2
    sa = jnp.where(upper, sin, 0.0)
    sb = jnp.where(upper, 0.0, -sin)

    def full(tab, fill):
        tab = jnp.concatenate([tab, jnp.full((nc, DA_DQK), fill, F32)], axis=0)
        tab = jnp.tile(tab, (B, LANE // DA_DQK))
        return tab

    return full(cos, 1.0), full(sa, 0.0), full(sb, 0.0)


def _conv_masks(B, n, nc):
    T = n + nc
    t = jnp.arange(T)
    keep_prev = ((t != 0) & (t != n)).astype(F32)
    keep_next = ((t != n - 1) & (t != T - 1)).astype(F32)
    widen = lambda m: jnp.tile(m[:, None], (B, LANE))
    return widen(keep_prev), widen(keep_next)


W_IN_WIDTHS = (DA_W, DA_W, DA_W, MLA_Q_RANK, MLA_KV_RANK, MLA_ROPE, 3 * HY_CH, N_BRANCH * D_MODEL)
W_IN_OFFS = tuple(sum(W_IN_WIDTHS[:j]) for j in range(len(W_IN_WIDTHS)))
W_MLA_COLS = MLA_Q_RANK + MLA_KV_RANK + LANE


def _prep_w_in_kernel(w_ref, qkv_ref, mla_ref, hy_ref, g_ref):
    o_q, o_k, o_v, o_cq, _, o_kr, o_hy, o_g = W_IN_OFFS
    win = lambda start: w_ref[:, start:start + LANE]
    lane = lax.broadcasted_iota(jnp.int32, (1, LANE), 1)
    lower = lane < DA_DQK
    for part, base in enumerate((o_q, o_k)):
        for h in range(DA_HEADS):
            m1 = win(base + h * DA_DQK)
            m2 = win(base + DA_HEADS * DA_DQK + (h - 1) * DA_DQK)
            dst = part * DA_W + h * LANE
            qkv_ref[:, dst:dst + LANE] = jnp.where(lower, m1, m2).astype(BF16)
    qkv_ref[:, 2 * DA_W:] = w_ref[:, o_v:o_v + DA_W].astype(BF16)
    mla_ref[:, :o_kr - o_cq] = w_ref[:, o_cq:o_kr].astype(BF16)
    mla_ref[:, o_kr - o_cq:] = jnp.where(lower, win(o_kr), 0.0).astype(BF16)
    for j in range(3 * HY_CH // LANE):
        hy_ref[:, j * LANE:(j + 1) * LANE] = win(o_hy + j * LANE).astype(BF16)
    for j in range(N_BRANCH * D_MODEL // LANE):
        g_ref[:, j * LANE:(j + 1) * LANE] = win(o_g + j * LANE).astype(BF16)


def _prep_w_in(w_in, layer):
    _, D, W = w_in.shape
    assert W == sum(W_IN_WIDTHS)
    rb = 256
    widths = (3 * DA_W, W_MLA_COLS, 3 * HY_CH, N_BRANCH * D_MODEL)
    return pl.pallas_call(
        _prep_w_in_kernel,
        out_shape=tuple(jax.ShapeDtypeStruct((D, w), BF16) for w in widths),
        grid=(D // rb,),
        in_specs=[pl.BlockSpec((None, rb, W), lambda r: (layer, r, 0))],
        out_specs=tuple(pl.BlockSpec((rb, w), lambda r: (r, 0)) for w in widths),
        compiler_params=_cparams("arbitrary"),
    )(w_in)


def _pad_w_uq(w):
    w = w.reshape(MLA_Q_RANK, MLA_HEADS, MLA_NOPE + MLA_ROPE)
    w = jnp.pad(w, ((0, 0), (0, 0), (0, MLA_QK_PAD - MLA_NOPE - MLA_ROPE)))
    return w.reshape(MLA_Q_RANK, MLA_HEADS * MLA_QK_PAD).astype(BF16)


def _split_w_ukv(w):
    w = w.reshape(MLA_KV_RANK, MLA_HEADS, 2, MLA_NOPE).transpose(0, 2, 1, 3)
    return w.reshape(MLA_KV_RANK, 2 * MLA_W).astype(BF16)


def _hyena(u3, layer, p, n, nc, with_ctx, tables):
    B = u3.shape[0]
    C = HY_CH
    P = B // 2
    g1d, g1f, g2, g2i, g3, gk, gd, gi = tables
    S = FFT_S
    N1 = 2 * n // S
    mlp = (p["hy_ffn_w1p"][layer], p["hy_ffn_b1"][layer][None], p["hy_ffn_w2"][layer], p["hy_ffn_b2"][layer][None],
           p["hy_ffn_w3"][layer], p["hy_ffn_b3"][layer][None], p["hy_ffn_w4"][layer], p["hy_freq"][layer][None])
    skip = p["hy_skip"][layer]
    cw, cb = p["hy_conv_w"][layer], p["hy_conv_b"][layer][None]

    kc, sums = _hy_filter(n, *mlp)
    kf = _fft_s1(g1f, kc.reshape(HY_ORDER, N1, S, C))
    kf = _fft_s2_filt(kf.reshape(HY_ORDER, 2, N1, S, C), g2, 1.0 / sums[:, 0:1, :])
    dw = _hy_dwconv(u3, cw, cb, 0, n).reshape(3, P, N1, S, C)
    z = dw[0]
    for o in range(HY_ORDER):
        a = _fft_s1(g1d, z).reshape(P, 2, N1, S, C)
        b = _fft_s2(a, g2, g2i, kf, o).reshape(P, 2 * N1, S, C)
        z = _fft_s3(g3, b, z, dw[1 + o], skip[o][None], F32 if o + 1 < HY_ORDER else BF16)
    oc_lat = z.reshape(B, n, C)

    if with_ctx:
        kcc, sumc = _hy_filter(nc, *mlp)
        kfc = _dense_spec(kcc, gk, 1.0 / sumc[:, 0:1, :])
        dwc = _hy_dwconv(u3, cw, cb, n // nc, nc).reshape(3, P, 2 * nc, C)
        zc = dwc[0]
        for o in range(HY_ORDER):
            zc = _dense_conv(zc, gd, gi, kfc, o, dwc[1 + o], skip[o][None], F32 if o + 1 < HY_ORDER else BF16)
        return jnp.concatenate([oc_lat, zc.reshape(B, nc, C)], axis=1)
    return oc_lat


def kernel(x, c, ctx, c_ctx, ada_w, ada_b, w_in, da_lambda, da_subln_g, mla_q_g, mla_w_uq, mla_kv_g, mla_w_ukv, hy_conv_w, hy_conv_b, hy_ffn_w1, hy_ffn_b1, hy_ffn_w2, hy_ffn_b2, hy_ffn_w3, hy_ffn_b3, hy_ffn_w4, hy_freq, hy_skip, w_branch_a, w_branch_b, w_branch_c, w_out, ln1_g, ln1_b, ffn_w_up, ffn_conv_w, ffn_conv_b, ffn_w_down, ln2_g, ln2_b):
    B, n, D = x.shape
    nc = ctx.shape[1]
    T = n + nc
    rows = B * T
    assert D == D_MODEL and B % 2 == 0 and B < 8
    assert n % ROW_GROUP == 0 and nc % ROW_GROUP == 0 and n % nc == 0 and n % GRID_W == 0
    assert (2 * n) % (8 * FFT_S) == 0 and T % ATTN_TK == 0 and nc % ATTN_TK == 0
    geom = (T // ROW_GROUP, n // ROW_GROUP, B)

    hy = dict(hy_ffn_w1p=jnp.pad(hy_ffn_w1, ((0, 0), (0, HY_FFN - HY_EMB), (0, 0))), hy_ffn_b1=hy_ffn_b1,
              hy_ffn_w2=hy_ffn_w2, hy_ffn_b2=hy_ffn_b2, hy_ffn_w3=hy_ffn_w3, hy_ffn_b3=hy_ffn_b3,
              hy_ffn_w4=hy_ffn_w4, hy_freq=hy_freq, hy_skip=hy_skip, hy_conv_w=hy_conv_w, hy_conv_b=hy_conv_b)
    tables = _dft_tables(n) + _dense_tables(nc)
    rope = _rope_tables(B, n, nc)
    keep_prev, keep_next = _conv_masks(B, n, nc)

    cc = jnp.concatenate([c, c_ctx[None], jnp.zeros((8 - B - 1, D), F32)], axis=0)
    mods = _ada(cc, ada_w, ada_b[:, None, :]).reshape(DEPTH, 8, 6, D)

    xs = jnp.concatenate([x, ctx], axis=1).reshape(rows, D)
    for i in range(DEPTH):
        last = i == DEPTH - 1
        lam_init = 0.8 - 0.6 * math.exp(-0.3 * i)
        lq1, lk1, lq2, lk2 = da_lambda[i].astype(F32)
        lam = (jnp.exp(jnp.sum(lq1 * lk1)) - jnp.exp(jnp.sum(lq2 * lk2)) + lam_init).reshape(1)
        mod = mods[i]
        w_qkv, w_mla, w_hy, w_g = _prep_w_in(w_in, i)

        if last:
            nb, tb = B, n
            sel = lambda b, g: b
        else:
            nb, tb = 1, rows
            sel = lambda b, g: _mod_row(g, *geom)

        h = _lnmod(xs, mod, geom)
        qkv = _qkv_proj(h, w_qkv, rope)
        p_mla = _matmul(h, w_mla, F32, w_mla.shape[1])
        u_hy = _matmul(h, w_hy, F32, 768, nb=nb, tb=tb)
        gates = _matmul(h, w_g, BF16, 1024, act="sigmoid", nb=nb, tb=tb)

        q_m, k_m, v_m = _mla_prep(p_mla, mla_q_g[i][None], mla_kv_g[i][None], _pad_w_uq(mla_w_uq[i]),
                                  _split_w_ukv(mla_w_ukv[i]), rope)
        oa = _diff_attn(qkv.reshape(B, T, 3 * DA_W), lam, da_subln_g[i][None], n, nc, not last, lam_init)
        ob = _mla_attn(q_m.reshape(B, T, -1), k_m.reshape(B, T, -1), v_m.reshape(B, T, -1), n, nc, not last)
        oc = _hyena(u_hy.reshape(B, -1, 3 * HY_CH), i, hy, n, nc, not last, tables)

        merged = _merge(oa.reshape(-1, DA_W), ob.reshape(-1, MLA_W), oc.reshape(-1, HY_CH),
                        w_branch_a[i].astype(BF16), w_branch_b[i].astype(BF16), w_branch_c[i].astype(BF16), gates,
                        nb, tb)
        xs1, h2 = _wo_ln(merged, w_out[i].astype(BF16), xs, mod, ln1_g[i][None], ln1_b[i][None], sel, nb, tb)
        u = _ffn_up(h2, ffn_w_up, i, ffn_conv_w[i], ffn_conv_b[i][None], keep_prev, keep_next, nb, tb)
        xs = _ffn_down(u, ffn_w_down[i].astype(BF16), xs1, mod, ln2_g[i][None], ln2_b[i][None], sel, nb, tb)
    return xs.reshape(B, n, D)
```

```python
import functools
import math

import numpy as np
import jax
import jax.numpy as jnp
from jax import lax
from jax.experimental import pallas as pl
from jax.experimental.pallas import tpu as pltpu

F32 = jnp.float32
BF16 = jnp.bfloat16

D_MODEL = 2048
DEPTH = 2
GRID_W = 64
ROPE_BASE = 10000.0
NORM_EPS = 1e-6
DA_HEADS = 6
DA_DQK = 64
DA_DV = 128
DA_W = DA_HEADS * DA_DV
MLA_HEADS = 6
MLA_Q_RANK = 512
MLA_KV_RANK = 256
MLA_NOPE = 128
MLA_ROPE = 64
MLA_DV = 128
MLA_W = MLA_HEADS * MLA_DV
MLA_QK_PAD = 256
HY_CH = 512
HY_ORDER = 2
HY_EMB = 33
HY_BANDS = (HY_EMB - 1) // 2
HY_FFN = 64
HY_MIN_DECAY = math.log(1e-2) / 1.5
HY_MAX_DECAY = math.log(1e-2) / 0.3
D_FF = 5632
N_BRANCH = 3
DEEPNORM_ALPHA = (2 * DEPTH) ** 0.25
LOG2E = 1.4426950408889634

ROW_GROUP = 256
LANE = 128
FFT_S = 128
VMEM_LIMIT = 52 * 1024 * 1024


def _cparams(*sem):
    return pltpu.CompilerParams(dimension_semantics=sem, vmem_limit_bytes=VMEM_LIMIT)


def _pick(total, prefs):
    for p in prefs:
        if total % p == 0:
            return p
    raise ValueError(f"no tile for {total} in {prefs}")


def _ln(x):
    mu = jnp.mean(x, axis=-1, keepdims=True)
    xc = x - mu
    var = jnp.mean(xc * xc, axis=-1, keepdims=True)
    return xc * lax.rsqrt(var + NORM_EPS)


def _rms(x):
    return x * lax.rsqrt(jnp.mean(x * x, axis=-1, keepdims=True) + NORM_EPS)


def _rope128(u, cos, sa, sb):
    return u * cos + pltpu.roll(u, 16, 1) * sa + pltpu.roll(u, LANE - 16, 1) * sb


def _ada_kernel(c_ref, w_ref, b_ref, o_ref):
    a = c_ref[...]
    a = a * jax.nn.sigmoid(a)
    o_ref[...] = jnp.dot(a.astype(BF16), w_ref[...].astype(BF16), preferred_element_type=F32) + b_ref[...]


def _ada(cc, ada_w, ada_b):
    L, D, N = ada_w.shape
    tn = 1024
    return pl.pallas_call(
        _ada_kernel,
        out_shape=jax.ShapeDtypeStruct((L, 8, N), F32),
        grid=(L, N // tn),
        in_specs=[pl.BlockSpec((8, D), lambda l, j: (0, 0)),
                  pl.BlockSpec((None, D, tn), lambda l, j: (l, 0, j)),
                  pl.BlockSpec((None, 1, tn), lambda l, j: (l, 0, j))],
        out_specs=pl.BlockSpec((None, 8, tn), lambda l, j: (l, 0, j)),
        compiler_params=_cparams("arbitrary", "arbitrary"),
    )(cc, ada_w, ada_b)


def _lnmod_kernel(x_ref, m_ref, o_ref):
    y = _ln(x_ref[...])
    o_ref[...] = (y * (1.0 + m_ref[1:2, :]) + m_ref[0:1, :]).astype(BF16)


def _mod_row(g, gpb, lat_groups, n_batch):
    return jnp.where(g % gpb < lat_groups, g // gpb, n_batch)


def _lnmod(xs, mod, geom):
    rows, D = xs.shape
    gpb, lat_groups, n_batch = geom
    return pl.pallas_call(
        _lnmod_kernel,
        out_shape=jax.ShapeDtypeStruct((rows, D), BF16),
        grid=(rows // ROW_GROUP,),
        in_specs=[pl.BlockSpec((ROW_GROUP, D), lambda i: (i, 0)),
                  pl.BlockSpec((None, 6, D), lambda i: (_mod_row(i, gpb, lat_groups, n_batch), 0, 0))],
        out_specs=pl.BlockSpec((ROW_GROUP, D), lambda i: (i, 0)),
        compiler_params=_cparams("arbitrary"),
    )(xs, mod)


def _mm_kernel(a_ref, w_ref, o_ref, *, act):
    acc = jnp.dot(a_ref[...], w_ref[...], preferred_element_type=F32)
    if act == "sigmoid":
        acc = jax.nn.sigmoid(acc)
    o_ref[...] = acc.astype(o_ref.dtype)


def _by_batch(a, nb):
    return a.reshape(nb, a.shape[0] // nb, a.shape[1])


def _matmul(a, w, out_dtype, tn, act=None, nb=1, tb=None):
    a = _by_batch(a, nb)
    K = a.shape[2]
    tb = tb or a.shape[1]
    N = w.shape[1]
    tm = _pick(tb, (1024, 512, 256))
    return pl.pallas_call(
        functools.partial(_mm_kernel, act=act),
        out_shape=jax.ShapeDtypeStruct((nb, tb, N), out_dtype),
        grid=(nb, tb // tm, N // tn),
        in_specs=[pl.BlockSpec((None, tm, K), lambda b, i, j: (b, i, 0)),
                  pl.BlockSpec((K, tn), lambda b, i, j: (0, j))],
        out_specs=pl.BlockSpec((None, tm, tn), lambda b, i, j: (b, i, j)),
        compiler_params=_cparams("arbitrary", "arbitrary", "arbitrary"),
    )(a, w).reshape(nb * tb, N)


def _qkv_kernel(h_ref, w_ref, cos_ref, sa_ref, sb_ref, o_ref, *, qscale):
    j = pl.program_id(1)
    acc = jnp.dot(h_ref[...], w_ref[...], preferred_element_type=F32)

    @pl.when(j < 2)
    def _():
        cos, sa, sb = cos_ref[...], sa_ref[...], sb_ref[...]
        scale = jnp.where(j == 0, qscale, 1.0).astype(F32)
        for c in range(DA_HEADS):
            u = acc[:, c * LANE:(c + 1) * LANE]
            o_ref[:, c * LANE:(c + 1) * LANE] = (_rope128(u, cos, sa, sb) * scale).astype(BF16)

    @pl.when(j == 2)
    def _():
        o_ref[...] = acc.astype(BF16)


def _qkv_proj(h, w_qkv, tabs):
    M, K = h.shape
    tm = _pick(M, (1024, 512, 256))
    tn = DA_W
    tab_spec = pl.BlockSpec((tm, LANE), lambda i, j: (i, 0))
    return pl.pallas_call(
        functools.partial(_qkv_kernel, qscale=DA_DQK ** -0.5 * LOG2E),
        out_shape=jax.ShapeDtypeStruct((M, 3 * DA_W), BF16),
        grid=(M // tm, 3),
        in_specs=[pl.BlockSpec((tm, K), lambda i, j: (i, 0)),
                  pl.BlockSpec((K, tn), lambda i, j: (0, j)),
                  tab_spec, tab_spec, tab_spec],
        out_specs=pl.BlockSpec((tm, tn), lambda i, j: (i, j)),
        compiler_params=_cparams("arbitrary", "arbitrary"),
    )(h, w_qkv, *tabs)


def _mla_prep_kernel(p_ref, qg_ref, kvg_ref, wuq_ref, wukv_ref, cos_ref, sa_ref, sb_ref,
                     q_ref, k_ref, v_ref, *, qscale):
    p = p_ref[...]
    cos, sa, sb = cos_ref[...], sa_ref[...], sb_ref[...]
    cq = p[:, :MLA_Q_RANK]
    ckv = p[:, MLA_Q_RANK:MLA_Q_RANK + MLA_KV_RANK]
    kr = p[:, MLA_Q_RANK + MLA_KV_RANK:]
    qn = (_rms(cq) * qg_ref[...]).astype(BF16)
    q = jnp.dot(qn, wuq_ref[...], preferred_element_type=F32)
    kvn = (_rms(ckv) * kvg_ref[...]).astype(BF16)
    kv = jnp.dot(kvn, wukv_ref[...], preferred_element_type=F32)
    krr = _rope128(kr, cos, sa, sb).astype(BF16)
    for h in range(MLA_HEADS):
        o = h * MLA_QK_PAD
        q_ref[:, o:o + LANE] = (q[:, o:o + LANE] * qscale).astype(BF16)
        q_ref[:, o + LANE:o + 2 * LANE] = (_rope128(q[:, o + LANE:o + 2 * LANE], cos, sa, sb) * qscale).astype(BF16)
        k_ref[:, o:o + LANE] = kv[:, h * LANE:(h + 1) * LANE].astype(BF16)
        k_ref[:, o + LANE:o + 2 * LANE] = krr
    v_ref[...] = kv[:, MLA_W:].astype(BF16)


def _mla_prep(p, q_g, kv_g, w_uq, w_ukv, tabs):
    M, W = p.shape
    tm = _pick(M, (512, 256))
    row = lambda i: (i, 0)
    full = lambda i: (0, 0)
    qk_w = MLA_HEADS * MLA_QK_PAD
    return pl.pallas_call(
        functools.partial(_mla_prep_kernel, qscale=(MLA_NOPE + MLA_ROPE) ** -0.5 * LOG2E),
        out_shape=(jax.ShapeDtypeStruct((M, qk_w), BF16),
                   jax.ShapeDtypeStruct((M, qk_w), BF16),
                   jax.ShapeDtypeStruct((M, MLA_W), BF16)),
        grid=(M // tm,),
        in_specs=[pl.BlockSpec((tm, W), row),
                  pl.BlockSpec((1, MLA_Q_RANK), full),
                  pl.BlockSpec((1, MLA_KV_RANK), full),
                  pl.BlockSpec(w_uq.shape, full),
                  pl.BlockSpec(w_ukv.shape, full),
                  pl.BlockSpec((tm, LANE), row), pl.BlockSpec((tm, LANE), row), pl.BlockSpec((tm, LANE), row)],
        out_specs=(pl.BlockSpec((tm, qk_w), row), pl.BlockSpec((tm, qk_w), row), pl.BlockSpec((tm, MLA_W), row)),
        compiler_params=_cparams("arbitrary"),
    )(p, q_g, kv_g, w_uq, w_ukv, *tabs)


_NT = (((1,), (1,)), ((), ()))
ATTN_TQ = 256
ATTN_CHAINS = 4
ATTN_TK = 256


def _skewed_pipeline(n_chains, n_chunks, stages):
    for t in range(n_chains + len(stages) - 1):
        active = [(s, t - s) for s in range(len(stages)) if 0 <= t - s < n_chains]
        for c in range(n_chunks):
            for s, chain in active:
                stages[s][0](chain, c)
        for s, chain in active:
            stages[s][1](chain)


def _acc(old, new, op):
    return new if old is None else op(old, new)


def _lane_halves(x, op):
    return op(x[:, :LANE], x[:, LANE:])


def _diff_attn_kernel(lam_ref, q_ref, k_ref, v_ref, g_ref, *rest, out_scale):
    o_ref = rest[-1]
    lam, g = lam_ref[0], g_ref[...]
    tq = min(ATTN_TQ, q_ref.shape[0])
    n_chains = q_ref.shape[0] // tq
    n_chunks = k_ref.shape[0] // ATTN_TK
    lane = lax.broadcasted_iota(jnp.int32, (1, LANE), 1)
    lo = (lane < DA_DQK).astype(F32)
    st = [dict(s=[], e=[], mx=[None, None], l=[None, None], o=None) for _ in range(n_chains)]
    rows = lambda i: slice(i * tq, (i + 1) * tq)
    keys = lambda c: slice(c * ATTN_TK, (c + 1) * ATTN_TK)

    def qk_chunk(i, c):
        d = st[i]
        if c == 0:
            qf = q_ref[rows(i), :].astype(F32)
            d["q"] = jnp.concatenate([(qf * lo).astype(BF16), (qf * (1.0 - lo)).astype(BF16)], axis=0)
        both = lax.dot_general(d["q"], k_ref[keys(c), :], _NT, preferred_element_type=F32)
        pair = []
        for m in range(2):
            s = both[m * tq:(m + 1) * tq]
            d["mx"][m] = _acc(d["mx"][m], _lane_halves(s, jnp.maximum), jnp.maximum)
            pair.append(s)
        d["s"].append(pair)

    def qk_done(i):
        st[i]["m"] = [jnp.max(mx, axis=-1, keepdims=True) for mx in st[i]["mx"]]

    def exp_chunk(i, c):
        d = st[i]
        pair = []
        for m in range(2):
            e = jnp.exp2(d["s"][c][m] - d["m"][m])
            d["l"][m] = _acc(d["l"][m], _lane_halves(e, jnp.add), jnp.add)
            pair.append(e)
        d["s"][c] = None
        d["e"].append(pair)

    def exp_done(i):
        d = st[i]
        l1, l2 = [jnp.sum(l, axis=-1, keepdims=True) for l in d["l"]]
        d["r1"] = 1.0 / l1
        d["cf"] = lam * l1 / l2

    def pv_chunk(i, c):
        d = st[i]
        w = (d["e"][c][0] - d["cf"] * d["e"][c][1]).astype(BF16)
        d["e"][c] = None
        d["o"] = _acc(d["o"], jnp.dot(w, v_ref[keys(c), :], preferred_element_type=F32), jnp.add)

    def pv_done(i):
        o = st[i]["o"] * st[i]["r1"]
        o_ref[rows(i), :] = (_rms(o) * g * out_scale).astype(BF16)

    _skewed_pipeline(n_chains, n_chunks, [(qk_chunk, qk_done), (exp_chunk, exp_done), (pv_chunk, pv_done)])


def _mla_attn_kernel(q_ref, k_ref, v_ref, *rest):
    o_ref = rest[-1]
    tq = min(ATTN_TQ, q_ref.shape[0])
    n_chains = q_ref.shape[0] // tq
    n_chunks = k_ref.shape[0] // ATTN_TK
    st = [dict(s=[], mx=None, l=None, o=None) for _ in range(n_chains)]
    rows = lambda i: slice(i * tq, (i + 1) * tq)
    keys = lambda c: slice(c * ATTN_TK, (c + 1) * ATTN_TK)

    def qk_chunk(i, c):
        d = st[i]
        s = lax.dot_general(q_ref[rows(i), :], k_ref[keys(c), :], _NT, preferred_element_type=F32)
        d["mx"] = _acc(d["mx"], _lane_halves(s, jnp.maximum), jnp.maximum)
        d["s"].append(s)

    def qk_done(i):
        st[i]["m"] = jnp.max(st[i]["mx"], axis=-1, keepdims=True)

    def pv_chunk(i, c):
        d = st[i]
        e = jnp.exp2(d["s"][c] - d["m"])
        d["s"][c] = None
        d["l"] = _acc(d["l"], _lane_halves(e, jnp.add), jnp.add)
        d["o"] = _acc(d["o"], jnp.dot(e.astype(BF16), v_ref[keys(c), :], preferred_element_type=F32), jnp.add)

    def pv_done(i):
        d = st[i]
        o_ref[rows(i), :] = (d["o"] * (1.0 / jnp.sum(d["l"], axis=-1, keepdims=True))).astype(BF16)

    _skewed_pipeline(n_chains, n_chunks, [(qk_chunk, qk_done), (pv_chunk, pv_done)])


def _diff_attn(qkv, lam, subln_g, n, nc, ctx_queries, lam_init):
    B, T, _ = qkv.shape
    H = DA_HEADS
    tq = _pick(n, (ATTN_TQ * ATTN_CHAINS, ATTN_TQ))
    kern = functools.partial(_diff_attn_kernel, out_scale=1.0 - lam_init)
    smem = pl.BlockSpec(memory_space=pltpu.SMEM)
    gspec = pl.BlockSpec((1, DA_DV), lambda b, h, i: (0, 0))
    oa = pl.pallas_call(
        kern,
        out_shape=jax.ShapeDtypeStruct((B, T if ctx_queries else n, DA_W), BF16),
        grid=(B, H, n // tq),
        in_specs=[smem,
                  pl.BlockSpec((None, tq, LANE), lambda b, h, i: (b, i, h)),
                  pl.BlockSpec((None, T, LANE), lambda b, h, i: (b, 0, H + h)),
                  pl.BlockSpec((None, T, LANE), lambda b, h, i: (b, 0, 2 * H + h)),
                  gspec],
        out_specs=pl.BlockSpec((None, tq, LANE), lambda b, h, i: (b, i, h)),
        compiler_params=_cparams("arbitrary", "arbitrary", "arbitrary"),
    )(lam, qkv, qkv, qkv, subln_g)
    if not ctx_queries:
        return oa
    cb = n // nc
    return pl.pallas_call(
        kern,
        out_shape=jax.ShapeDtypeStruct((B, T, DA_W), BF16),
        grid=(B, H, 1),
        in_specs=[smem,
                  pl.BlockSpec((None, nc, LANE), lambda b, h, i: (b, cb, h)),
                  pl.BlockSpec((None, nc, LANE), lambda b, h, i: (b, cb, H + h)),
                  pl.BlockSpec((None, nc, LANE), lambda b, h, i: (b, cb, 2 * H + h)),
                  gspec,
                  pl.BlockSpec(memory_space=pl.ANY)],
        out_specs=pl.BlockSpec((None, nc, LANE), lambda b, h, i: (b, cb, h)),
        input_output_aliases={5: 0},
        compiler_params=_cparams("arbitrary", "arbitrary", "arbitrary"),
    )(lam, qkv, qkv, qkv, subln_g, oa)


def _mla_attn(q, k, v, n, nc, ctx_queries):
    B, T, _ = q.shape
    H = MLA_HEADS
    tq = _pick(n, (ATTN_TQ * ATTN_CHAINS, ATTN_TQ))
    ob = pl.pallas_call(
        _mla_attn_kernel,
        out_shape=jax.ShapeDtypeStruct((B, T if ctx_queries else n, MLA_W), BF16),
        grid=(B, H, n // tq),
        in_specs=[pl.BlockSpec((None, tq, MLA_QK_PAD), lambda b, h, i: (b, i, h)),
                  pl.BlockSpec((None, T, MLA_QK_PAD), lambda b, h, i: (b, 0, h)),
                  pl.BlockSpec((None, T, MLA_DV), lambda b, h, i: (b, 0, h))],
        out_specs=pl.BlockSpec((None, tq, MLA_DV), lambda b, h, i: (b, i, h)),
        compiler_params=_cparams("arbitrary", "arbitrary", "arbitrary"),
    )(q, k, v)
    if not ctx_queries:
        return ob
    cb = n // nc
    return pl.pallas_call(
        _mla_attn_kernel,
        out_shape=jax.ShapeDtypeStruct((B, T, MLA_W), BF16),
        grid=(B, H, 1),
        in_specs=[pl.BlockSpec((None, nc, MLA_QK_PAD), lambda b, h, i: (b, cb, h)),
                  pl.BlockSpec((None, nc, MLA_QK_PAD), lambda b, h, i: (b, cb, h)),
                  pl.BlockSpec((None, nc, MLA_DV), lambda b, h, i: (b, cb, h)),
                  pl.BlockSpec(memory_space=pl.ANY)],
        out_specs=pl.BlockSpec((None, nc, MLA_DV), lambda b, h, i: (b, cb, h)),
        input_output_aliases={3: 0},
        compiler_params=_cparams("arbitrary", "arbitrary", "arbitrary"),
    )(q, k, v, ob)


def _hy_filter_kernel(feat_ref, dec_ref, w1_ref, b1_ref, w2_ref, b2_ref, w3_ref, b3_ref, w4_ref, fr_ref,
                      k_ref, s_ref):
    hp = lax.Precision.HIGHEST
    d, r = pl.program_id(0), pl.program_id(1)
    fr = fr_ref[...]
    h = jnp.sin(fr * (jnp.dot(feat_ref[...], w1_ref[...], precision=hp, preferred_element_type=F32) + b1_ref[...]))
    h = jnp.sin(fr * (jnp.dot(h, w2_ref[...], precision=hp, preferred_element_type=F32) + b2_ref[...]))
    h = jnp.sin(fr * (jnp.dot(h, w3_ref[...], precision=hp, preferred_element_type=F32) + b3_ref[...]))
    h = jnp.dot(h, w4_ref[...], precision=hp, preferred_element_type=F32)
    row = lax.broadcasted_iota(jnp.int32, (h.shape[0], 1), 0)
    first_bwd = jnp.where((d == 1) & (r == 0), 1.0, 0.0)
    scale = dec_ref[...] * (1.0 - jnp.where(row == 0, 1.0, 0.0) * first_bwd)

    @pl.when((d == 0) & (r == 0))
    def _():
        s_ref[...] = jnp.zeros(s_ref.shape, F32)

    for o in range(HY_ORDER):
        ko = h[:, o * HY_CH:(o + 1) * HY_CH] * scale
        k_ref[o] = ko
        s_ref[o] += jnp.broadcast_to(jnp.sum(jnp.abs(ko), axis=0, keepdims=True), (8, HY_CH))


def _hy_filter(n, w1, b1, w2, b2, w3, b3, w4, fr):
    C = HY_CH
    t = jnp.linspace(0.0, 1.0, n, dtype=F32)
    pos = jnp.arange(n, dtype=F32)
    t2 = jnp.concatenate([t, t[::-1]])[:, None]
    pos2 = jnp.concatenate([pos, pos[::-1]])[:, None]
    phase = (2.0 * math.pi / n) * pos2 * jnp.linspace(1e-4, HY_BANDS - 1, HY_BANDS, dtype=F32)[None, :]
    feat = jnp.concatenate([t2, jnp.cos(phase), -jnp.sin(phase)], axis=-1)
    feat = jnp.pad(feat, ((0, 0), (0, HY_FFN - HY_EMB)))
    dec = jnp.exp(-t2 * jnp.abs(jnp.linspace(HY_MIN_DECAY, HY_MAX_DECAY, C, dtype=F32)))
    w4d = w4.reshape(HY_FFN, HY_ORDER, 2, C).transpose(2, 0, 1, 3).reshape(2, HY_FFN, HY_ORDER * C)
    rb = min(512, n)
    nb = n // rb
    full = lambda d, r: (0, 0)
    return pl.pallas_call(
        _hy_filter_kernel,
        out_shape=(jax.ShapeDtypeStruct((HY_ORDER, 2 * n, C), F32),
                   jax.ShapeDtypeStruct((HY_ORDER, 8, C), F32)),
        grid=(2, nb),
        in_specs=[pl.BlockSpec((rb, HY_FFN), lambda d, r: (d * nb + r, 0)),
                  pl.BlockSpec((rb, C), lambda d, r: (d * nb + r, 0)),
                  pl.BlockSpec(w1.shape, full), pl.BlockSpec(b1.shape, full),
                  pl.BlockSpec(w2.shape, full), pl.BlockSpec(b2.shape, full),
                  pl.BlockSpec(w3.shape, full), pl.BlockSpec(b3.shape, full),
                  pl.BlockSpec((None, HY_FFN, HY_ORDER * C), lambda d, r: (d, 0, 0)),
                  pl.BlockSpec(fr.shape, full)],
        out_specs=(pl.BlockSpec((HY_ORDER, rb, C), lambda d, r: (0, d * nb + r, 0)),
                   pl.BlockSpec((HY_ORDER, 8, C), lambda d, r: (0, 0, 0))),
        compiler_params=_cparams("arbitrary", "arbitrary"),
    )(feat, dec, w1, b1, w2, b2, w3, b3, w4d, fr)


def _hy_dwconv_kernel(*refs):
    for k in range(3):
        u_ref, w_ref, b_ref, o_ref = refs[3 * k], refs[3 * k + 1], refs[3 * k + 2], refs[9 + k]
        u = u_ref[...]
        L = u.shape[0]
        row = lax.broadcasted_iota(jnp.int32, (L, 1), 0)
        up = jnp.where(row == 0, 0.0, pltpu.roll(u, 1, 0))
        dn = jnp.where(row == L - 1, 0.0, pltpu.roll(u, L - 1, 0))
        o_ref[...] = up * w_ref[0:1, :] + u * w_ref[1:2, :] + dn * w_ref[2:3, :] + b_ref[...]


def _hy_dwconv(u, w, b, row_block, length):
    B = u.shape[0]
    cw = LANE
    per = HY_CH // cw
    in_specs, args = [], []
    for k in range(3):
        in_specs += [pl.BlockSpec((None, length, cw), lambda bb, j, k=k: (bb, row_block, k * per + j)),
                     pl.BlockSpec((3, cw), lambda bb, j, k=k: (0, k * per + j)),
                     pl.BlockSpec((1, cw), lambda bb, j, k=k: (0, k * per + j))]
        args += [u, w, b]
    ospec = pl.BlockSpec((None, length, cw), lambda bb, j: (bb, 0, j))
    return pl.pallas_call(
        _hy_dwconv_kernel,
        out_shape=(jax.ShapeDtypeStruct((B, length, HY_CH), F32),) * 3,
        grid=(B, per),
        in_specs=in_specs,
        out_specs=(ospec,) * 3,
        compiler_params=_cparams("arbitrary", "arbitrary"),
    )(*args)


def _dft_tables(n):
    S = FFT_S
    M = 2 * n
    N1 = M // S
    H = N1 // 2
    s2 = np.arange(S)[:, None, None]
    k1 = np.arange(N1)[None, :, None]
    s1 = np.arange(N1)[None, None, :]
    ang = -2.0 * np.pi * ((k1 * (S * s1 + s2)) % M) / M
    fr, fi = np.cos(ang), np.sin(ang)
    g1f = np.concatenate([fr, fi], axis=1)
    frh, fih = fr[:, :, :H], fi[:, :, :H]
    g1d = np.concatenate([np.concatenate([frh, -fih], axis=2),
                          np.concatenate([fih, frh], axis=2)], axis=1)
    er = np.transpose(frh, (0, 2, 1)) / M
    ei = -np.transpose(fih, (0, 2, 1)) / M
    g3 = np.concatenate([np.concatenate([er, -ei], axis=2),
                         np.concatenate([ei, er], axis=2)], axis=1)
    a2 = -2.0 * np.pi * ((np.arange(S)[:, None] * np.arange(S)[None, :]) % S) / S
    f2r, f2i = np.cos(a2), np.sin(a2)
    g2 = np.block([[f2r, -f2i], [f2i, f2r]])
    g2i = np.block([[f2r, f2i], [-f2i, f2r]])
    cast = lambda a: jnp.asarray(a, dtype=F32).astype(BF16)
    return cast(g1d), cast(g1f), cast(g2), cast(g2i), cast(g3)


FFT_NS = 16


def _fft_s1_kernel(g_ref, x_ref, o_ref):
    xt = jnp.swapaxes(x_ref[...], 0, 1)
    y = jnp.stack([jnp.dot(g_ref[j], xt[j].astype(BF16), preferred_element_type=F32)
                   for j in range(xt.shape[0])], axis=0)
    o_ref[...] = jnp.swapaxes(y, 0, 1).astype(o_ref.dtype)


def _fft_s1(g, x):
    P, Ri, S, C = x.shape
    Ro = g.shape[1]
    ns = FFT_NS
    return pl.pallas_call(
        _fft_s1_kernel,
        out_shape=jax.ShapeDtypeStruct((P, Ro, S, C), BF16),
        grid=(S // ns, P),
        in_specs=[pl.BlockSpec((ns, Ro, Ri), lambda j, p: (j, 0, 0)),
                  pl.BlockSpec((None, Ri, ns, C), lambda j, p: (p, 0, j, 0))],
        out_specs=pl.BlockSpec((None, Ro, ns, C), lambda j, p: (p, 0, j, 0)),
        compiler_params=_cparams("arbitrary", "arbitrary"),
    )(g, x)


def _fft_s2_filt_kernel(a_ref, g_ref, rn_ref, o_ref, *, nk):
    S = FFT_S
    rn = rn_ref[...]
    for t in range(nk):
        d = jnp.concatenate([a_ref[0, t], a_ref[1, t]], axis=0)
        y = jnp.dot(g_ref[...], d, preferred_element_type=F32)
        o_ref[0, t] = y[:S] * rn
        o_ref[1, t] = y[S:] * rn


def _fft_s2_filt(a, g2, rnorm):
    O, _, N1, S, C = a.shape
    nk = 4
    blk = (None, 2, nk, S, C)
    return pl.pallas_call(
        functools.partial(_fft_s2_filt_kernel, nk=nk),
        out_shape=jax.ShapeDtypeStruct(a.shape, F32),
        grid=(O, N1 // nk),
        in_specs=[pl.BlockSpec(blk, lambda o, j: (o, 0, j, 0, 0)),
                  pl.BlockSpec(g2.shape, lambda o, j: (0, 0)),
                  pl.BlockSpec((None, 1, C), lambda o, j: (o, 0, 0))],
        out_specs=pl.BlockSpec(blk, lambda o, j: (o, 0, j, 0, 0)),
        compiler_params=_cparams("arbitrary", "arbitrary"),
    )(a, g2, rnorm)


def _fft_s2_kernel(a_ref, g_ref, gi_ref, kf_ref, o_ref, *, nk):
    S = FFT_S
    for t in range(nk):
        d = jnp.concatenate([a_ref[0, t], a_ref[1, t]], axis=0)
        y = jnp.dot(g_ref[...], d, preferred_element_type=F32)
        yr, yi = y[:S], y[S:]
        kr, ki = kf_ref[0, t], kf_ref[1, t]
        p = jnp.concatenate([yr * kr - yi * ki, yr * ki + yi * kr], axis=0).astype(BF16)
        b = jnp.dot(gi_ref[...], p, preferred_element_type=F32)
        o_ref[0, t] = b[:S].astype(BF16)
        o_ref[1, t] = b[S:].astype(BF16)


def _fft_s2(a, g2, g2i, kf, order):
    P, _, N1, S, C = a.shape
    nk = 4
    blk = (None, 2, nk, S, C)
    return pl.pallas_call(
        functools.partial(_fft_s2_kernel, nk=nk),
        out_shape=jax.ShapeDtypeStruct(a.shape, BF16),
        grid=(N1 // nk, P),
        in_specs=[pl.BlockSpec(blk, lambda j, p: (p, 0, j, 0, 0)),
                  pl.BlockSpec(g2.shape, lambda j, p: (0, 0)),
                  pl.BlockSpec(g2i.shape, lambda j, p: (0, 0)),
                  pl.BlockSpec(blk, lambda j, p: (order, 0, j, 0, 0))],
        out_specs=pl.BlockSpec(blk, lambda j, p: (p, 0, j, 0, 0)),
        compiler_params=_cparams("arbitrary", "arbitrary"),
    )(a, g2, g2i, kf)


def _fft_s3_kernel(g_ref, b_ref, z_ref, gate_ref, skip_ref, o_ref):
    bt = jnp.swapaxes(b_ref[...].astype(F32), 0, 1).astype(BF16)
    y = jnp.stack([jnp.dot(g_ref[j], bt[j], preferred_element_type=F32) for j in range(bt.shape[0])], axis=0)
    y = jnp.swapaxes(y, 0, 1)
    o_ref[...] = (gate_ref[...] * (y + skip_ref[...] * z_ref[...])).astype(o_ref.dtype)


def _fft_s3(g3, b, z, gate, skip, out_dtype):
    P, Ri, S, C = b.shape
    Ro = g3.shape[1]
    ns = FFT_NS
    dspec = pl.BlockSpec((None, Ro, ns, C), lambda j, p: (p, 0, j, 0))
    return pl.pallas_call(
        _fft_s3_kernel,
        out_shape=jax.ShapeDtypeStruct((P, Ro, S, C), out_dtype),
        grid=(S // ns, P),
        in_specs=[pl.BlockSpec((ns, Ro, Ri), lambda j, p: (j, 0, 0)),
                  pl.BlockSpec((None, Ri, ns, C), lambda j, p: (p, 0, j, 0)),
                  dspec, dspec,
                  pl.BlockSpec((1, C), lambda j, p: (0, 0))],
        out_specs=dspec,
        compiler_params=_cparams("arbitrary", "arbitrary"),
    )(g3, b, z, gate, skip)


def _dense_tables(m):
    M = 2 * m
    ang = -2.0 * np.pi * ((np.arange(M)[:, None] * np.arange(M)[None, :]) % M) / M
    fr, fi = np.cos(ang), np.sin(ang)
    gk = np.concatenate([fr, fi], axis=0)
    gd = np.block([[fr[:, :m], -fi[:, :m]], [fi[:, :m], fr[:, :m]]])
    er, ei = fr[:m, :] / M, -fi[:m, :] / M
    gi = np.block([[er, -ei], [ei, er]])
    cast = lambda a: jnp.asarray(a, dtype=F32).astype(BF16)
    return cast(gk), cast(gd), cast(gi)


def _dense_spec_kernel(k_ref, g_ref, rn_ref, o_ref):
    o_ref[...] = jnp.dot(g_ref[...], k_ref[...].astype(BF16), preferred_element_type=F32) * rn_ref[...]


def _dense_spec(kc, gk, rnorm):
    O, M, C = kc.shape
    return pl.pallas_call(
        _dense_spec_kernel,
        out_shape=jax.ShapeDtypeStruct((O, 2 * M, C), F32),
        grid=(O,),
        in_specs=[pl.BlockSpec((None, M, C), lambda o: (o, 0, 0)),
                  pl.BlockSpec(gk.shape, lambda o: (0, 0)),
                  pl.BlockSpec((None, 1, C), lambda o: (o, 0, 0))],
        out_specs=pl.BlockSpec((None, 2 * M, C), lambda o: (o, 0, 0)),
        compiler_params=_cparams("arbitrary"),
    )(kc, gk, rnorm)


def _dense_conv_kernel(x_ref, gd_ref, gi_ref, kf_ref, gate_ref, skip_ref, o_ref):
    x = x_ref[...]
    y = jnp.dot(gd_ref[...], x.astype(BF16), preferred_element_type=F32)
    M = y.shape[0] // 2
    yr, yi = y[:M], y[M:]
    kr, ki = kf_ref[:M], kf_ref[M:]
    p = jnp.concatenate([yr * kr - yi * ki, yr * ki + yi * kr], axis=0).astype(BF16)
    conv = jnp.dot(gi_ref[...], p, preferred_element_type=F32)
    o_ref[...] = (gate_ref[...] * (conv + skip_ref[...] * x)).astype(o_ref.dtype)


def _dense_conv(x, gd, gi, kf, order, gate, skip, out_dtype):
    P, R, C = x.shape
    dspec = pl.BlockSpec((None, R, C), lambda p: (p, 0, 0))
    return pl.pallas_call(
        _dense_conv_kernel,
        out_shape=jax.ShapeDtypeStruct((P, R, C), out_dtype),
        grid=(P,),
        in_specs=[dspec,
                  pl.BlockSpec(gd.shape, lambda p: (0, 0)),
                  pl.BlockSpec(gi.shape, lambda p: (0, 0)),
                  pl.BlockSpec((None,) + kf.shape[1:], lambda p: (order, 0, 0)),
                  dspec,
                  pl.BlockSpec((1, C), lambda p: (0, 0))],
        out_specs=dspec,
        compiler_params=_cparams("arbitrary"),
    )(x, gd, gi, kf, gate, skip)


def _merge_kernel(oa_ref, ob_ref, oc_ref, wa_ref, wb_ref, wc_ref, g0_ref, g1_ref, g2_ref, o_ref):
    ya = jnp.dot(oa_ref[...], wa_ref[...], preferred_element_type=F32)
    yb = jnp.dot(ob_ref[...], wb_ref[...], preferred_element_type=F32)
    yc = jnp.dot(oc_ref[...], wc_ref[...], preferred_element_type=F32)
    m = g0_ref[...].astype(F32) * ya + g1_ref[...].astype(F32) * yb + g2_ref[...].astype(F32) * yc
    o_ref[...] = m.astype(BF16)


def _merge(oa, ob, oc, w_ba, w_bb, w_bc, gates, nb, tb):
    D = D_MODEL
    tm = _pick(tb, (1024, 512, 256))
    tn = 512
    nt = D // tn
    oa, ob, oc, gates = (_by_batch(a, nb) for a in (oa, ob, oc, gates))
    row = lambda b, i, j: (b, i, 0)
    col = lambda b, i, j: (0, j)
    return pl.pallas_call(
        _merge_kernel,
        out_shape=jax.ShapeDtypeStruct((nb, tb, D), BF16),
        grid=(nb, tb // tm, nt),
        in_specs=[pl.BlockSpec((None, tm, DA_W), row), pl.BlockSpec((None, tm, MLA_W), row),
                  pl.BlockSpec((None, tm, HY_CH), row),
                  pl.BlockSpec((DA_W, tn), col), pl.BlockSpec((MLA_W, tn), col), pl.BlockSpec((HY_CH, tn), col),
                  pl.BlockSpec((None, tm, tn), lambda b, i, j: (b, i, j)),
                  pl.BlockSpec((None, tm, tn), lambda b, i, j: (b, i, nt + j)),
                  pl.BlockSpec((None, tm, tn), lambda b, i, j: (b, i, 2 * nt + j))],
        out_specs=pl.BlockSpec((None, tm, tn), lambda b, i, j: (b, i, j)),
        compiler_params=_cparams("arbitrary", "arbitrary", "arbitrary"),
    )(oa, ob, oc, w_ba, w_bb, w_bc, gates, gates, gates).reshape(nb * tb, D)


def _wo_ln_kernel(m_ref, w_ref, xs_ref, mod_ref, g_ref, b_ref, xs1_ref, h2_ref, *, sel, groups):
    y = jnp.dot(m_ref[...], w_ref[...], preferred_element_type=F32)
    g, b = g_ref[...], b_ref[...]
    for q in range(groups):
        mod = mod_ref[sel(pl.program_id(0), pl.program_id(1) * groups + q)]
        sl = slice(q * ROW_GROUP, (q + 1) * ROW_GROUP)
        x1 = _ln(DEEPNORM_ALPHA * xs_ref[sl, :] + mod[2:3, :] * y[sl, :]) * g + b
        xs1_ref[sl, :] = x1
        h2_ref[sl, :] = (_ln(x1) * (1.0 + mod[4:5, :]) + mod[3:4, :]).astype(BF16)


def _wo_ln(merged, w_o, xs, mod, ln_g, ln_b, sel, nb, tb):
    D = D_MODEL
    tm = _pick(tb, (512, 256))
    merged, xs = _by_batch(merged, nb), _by_batch(xs, nb)
    row = lambda b, i: (b, i, 0)
    full2 = lambda b, i: (0, 0)
    xs1, h2 = pl.pallas_call(
        functools.partial(_wo_ln_kernel, sel=sel, groups=tm // ROW_GROUP),
        out_shape=(jax.ShapeDtypeStruct((nb, tb, D), F32), jax.ShapeDtypeStruct((nb, tb, D), BF16)),
        grid=(nb, tb // tm),
        in_specs=[pl.BlockSpec((None, tm, D), row),
                  pl.BlockSpec((D, D), full2),
                  pl.BlockSpec((None, tm, D), row),
                  pl.BlockSpec(mod.shape, lambda b, i: (0, 0, 0)),
                  pl.BlockSpec((1, D), full2), pl.BlockSpec((1, D), full2)],
        out_specs=(pl.BlockSpec((None, tm, D), row), pl.BlockSpec((None, tm, D), row)),
        compiler_params=_cparams("arbitrary", "arbitrary"),
    )(merged, w_o, xs, mod, ln_g, ln_b)
    return xs1.reshape(nb * tb, D), h2.reshape(nb * tb, D)


FFN_HALO = 16
FFN_SPLIT = 2


def _ffn_up_kernel(hp_ref, h_ref, hn_ref, wa_ref, wv_ref, cw_ref, cb_ref, kp_ref, kn_ref, o_ref, wa_s, wv_s):
    tm, tn = o_ref.shape

    @pl.when((pl.program_id(1) == 0) & (pl.program_id(2) == 0))
    def _():
        wa_s[...] = wa_ref[...].astype(BF16)
        wv_s[...] = wv_ref[...].astype(BF16)

    hm = h_ref[...]
    hext = jnp.concatenate([hp_ref[...], hm, hn_ref[...]], axis=0)
    rep = tn // LANE
    th = tm // FFN_SPLIT
    ext = th + 2 * FFN_HALO
    for half in range(FFN_SPLIT):
        r0 = half * th
        a = jnp.dot(hext[r0:r0 + ext], wa_s[...], preferred_element_type=F32)
        v = jnp.dot(hm[r0:r0 + th], wv_s[...], preferred_element_type=F32)
        keep_prev = jnp.tile(kp_ref[r0:r0 + th, :], (1, rep))
        keep_next = jnp.tile(kn_ref[r0:r0 + th, :], (1, rep))
        a_prev = pltpu.roll(a, 1, 0)[FFN_HALO:FFN_HALO + th] * keep_prev
        a_next = pltpu.roll(a, ext - 1, 0)[FFN_HALO:FFN_HALO + th] * keep_next
        cv = a_prev * cw_ref[0:1, :] + a[FFN_HALO:FFN_HALO + th] * cw_ref[1:2, :] + a_next * cw_ref[2:3, :] + cb_ref[...]
        o_ref[r0:r0 + th, :] = (cv * jax.nn.sigmoid(cv) * v).astype(BF16)


def _ffn_up(h2, w_up, layer, conv_w, conv_b, keep_prev, keep_next, nb, tb):
    D = D_MODEL
    tm = _pick(tb, (1024, 512, 256))
    tn = 512
    nt = D_FF // tn
    hb = tm // FFN_HALO
    last = tb // FFN_HALO - 1
    h2, keep_prev, keep_next = (_by_batch(a, nb) for a in (h2, keep_prev, keep_next))
    mask_spec = pl.BlockSpec((None, tm, LANE), lambda j, b, i: (b, i, 0))
    return pl.pallas_call(
        _ffn_up_kernel,
        out_shape=jax.ShapeDtypeStruct((nb, tb, D_FF), BF16),
        grid=(nt, nb, tb // tm),
        in_specs=[pl.BlockSpec((None, FFN_HALO, D), lambda j, b, i: (b, jnp.maximum(i * hb - 1, 0), 0)),
                  pl.BlockSpec((None, tm, D), lambda j, b, i: (b, i, 0)),
                  pl.BlockSpec((None, FFN_HALO, D), lambda j, b, i: (b, jnp.minimum((i + 1) * hb, last), 0)),
                  pl.BlockSpec((None, D, tn), lambda j, b, i: (layer, 0, j)),
                  pl.BlockSpec((None, D, tn), lambda j, b, i: (layer, 0, nt + j)),
                  pl.BlockSpec((3, tn), lambda j, b, i: (0, j)),
                  pl.BlockSpec((1, tn), lambda j, b, i: (0, j)),
                  mask_spec, mask_spec],
        out_specs=pl.BlockSpec((None, tm, tn), lambda j, b, i: (b, i, j)),
        scratch_shapes=[pltpu.VMEM((D, tn), BF16), pltpu.VMEM((D, tn), BF16)],
        compiler_params=_cparams("arbitrary", "arbitrary", "arbitrary"),
    )(h2, h2, h2, w_up, w_up, conv_w, conv_b, keep_prev, keep_next).reshape(nb * tb, D_FF)


def _ffn_down_kernel(u_ref, w_ref, xs_ref, mod_ref, g_ref, b_ref, o_ref, acc_ref, *, sel, groups, nk):
    k = pl.program_id(2)

    @pl.when(k == 0)
    def _():
        acc_ref[...] = jnp.zeros(acc_ref.shape, F32)

    acc_ref[...] += jnp.dot(u_ref[...], w_ref[...], preferred_element_type=F32)

    @pl.when(k == nk - 1)
    def _():
        g, b = g_ref[...], b_ref[...]
        for q in range(groups):
            mod = mod_ref[sel(pl.program_id(0), pl.program_id(1) * groups + q)]
            sl = slice(q * ROW_GROUP, (q + 1) * ROW_GROUP)
            o_ref[sl, :] = _ln(DEEPNORM_ALPHA * xs_ref[sl, :] + mod[5:6, :] * acc_ref[sl, :]) * g + b


def _ffn_down(u, w_down, xs1, mod, ln_g, ln_b, sel, nb, tb):
    D = D_MODEL
    tm = _pick(tb, (512, 256))
    tk = D_FF // 2
    nk = D_FF // tk
    u, xs1 = _by_batch(u, nb), _by_batch(xs1, nb)
    row = lambda b, i, k: (b, i, 0)
    full2 = lambda b, i, k: (0, 0)
    return pl.pallas_call(
        functools.partial(_ffn_down_kernel, sel=sel, groups=tm // ROW_GROUP, nk=nk),
        out_shape=jax.ShapeDtypeStruct((nb, tb, D), F32),
        grid=(nb, tb // tm, nk),
        in_specs=[pl.BlockSpec((None, tm, tk), lambda b, i, k: (b, i, k)),
                  pl.BlockSpec((tk, D), lambda b, i, k: (k, 0)),
                  pl.BlockSpec((None, tm, D), row),
                  pl.BlockSpec(mod.shape, lambda b, i, k: (0, 0, 0)),
                  pl.BlockSpec((1, D), full2), pl.BlockSpec((1, D), full2)],
        out_specs=pl.BlockSpec((None, tm, D), row),
        scratch_shapes=[pltpu.VMEM((tm, D), F32)],
        compiler_params=_cparams("arbitrary", "arbitrary", "arbitrary"),
    )(u, w_down, xs1, mod, ln_g, ln_b).reshape(nb * tb, D)


def _rope_tables(B, n, nc):
    half = DA_DQK // 2
    inv = ROPE_BASE ** (-jnp.arange(0, half, 2, dtype=F32) / half)
    t = jnp.arange(n, dtype=jnp.int32)
    ang_r = (t // GRID_W).astype(F32)[:, None] * inv[None, :]
    ang_c = (t % GRID_W).astype(F32)[:, None] * inv[None, :]
    ang = jnp.concatenate([ang_r, ang_r, ang_c, ang_c], axis=-1)
    cos, sin = jnp.cos(ang), jnp.sin(ang)
    upper = (jnp.arange(DA_DQK) % half) >= half // 2
    sa = jnp.where(upper, sin, 0.0)
    sb = jnp.where(upper, 0.0, -sin)

    def full(tab, fill):
        tab = jnp.concatenate([tab, jnp.full((nc, DA_DQK), fill, F32)], axis=0)
        tab = jnp.tile(tab, (B, LANE // DA_DQK))
        return tab

    return full(cos, 1.0), full(sa, 0.0), full(sb, 0.0)


def _conv_masks(B, n, nc):
    T = n + nc
    t = jnp.arange(T)
    keep_prev = ((t != 0) & (t != n)).astype(F32)
    keep_next = ((t != n - 1) & (t != T - 1)).astype(F32)
    widen = lambda m: jnp.tile(m[:, None], (B, LANE))
    return widen(keep_prev), widen(keep_next)


W_IN_WIDTHS = (DA_W, DA_W, DA_W, MLA_Q_RANK, MLA_KV_RANK, MLA_ROPE, 3 * HY_CH, N_BRANCH * D_MODEL)
W_IN_OFFS = tuple(sum(W_IN_WIDTHS[:j]) for j in range(len(W_IN_WIDTHS)))
W_MLA_COLS = MLA_Q_RANK + MLA_KV_RANK + LANE


def _prep_w_in_kernel(w_ref, qkv_ref, mla_ref, hy_ref, g_ref):
    o_q, o_k, o_v, o_cq, _, o_kr, o_hy, o_g = W_IN_OFFS
    win = lambda start: w_ref[:, start:start + LANE]
    lane = lax.broadcasted_iota(jnp.int32, (1, LANE), 1)
    lower = lane < DA_DQK
    for part, base in enumerate((o_q, o_k)):
        for h in range(DA_HEADS):
            m1 = win(base + h * DA_DQK)
            m2 = win(base + DA_HEADS * DA_DQK + (h - 1) * DA_DQK)
            dst = part * DA_W + h * LANE
            qkv_ref[:, dst:dst + LANE] = jnp.where(lower, m1, m2).astype(BF16)
    qkv_ref[:, 2 * DA_W:] = w_ref[:, o_v:o_v + DA_W].astype(BF16)
    mla_ref[:, :o_kr - o_cq] = w_ref[:, o_cq:o_kr].astype(BF16)
    mla_ref[:, o_kr - o_cq:] = jnp.where(lower, win(o_kr), 0.0).astype(BF16)
    for j in range(3 * HY_CH // LANE):
        hy_ref[:, j * LANE:(j + 1) * LANE] = win(o_hy + j * LANE).astype(BF16)
    for j in range(N_BRANCH * D_MODEL // LANE):
        g_ref[:, j * LANE:(j + 1) * LANE] = win(o_g + j * LANE).astype(BF16)


def _prep_w_in(w_in, layer):
    _, D, W = w_in.shape
    assert W == sum(W_IN_WIDTHS)
    rb = 256
    widths = (3 * DA_W, W_MLA_COLS, 3 * HY_CH, N_BRANCH * D_MODEL)
    return pl.pallas_call(
        _prep_w_in_kernel,
        out_shape=tuple(jax.ShapeDtypeStruct((D, w), BF16) for w in widths),
        grid=(D // rb,),
        in_specs=[pl.BlockSpec((None, rb, W), lambda r: (layer, r, 0))],
        out_specs=tuple(pl.BlockSpec((rb, w), lambda r: (r, 0)) for w in widths),
        compiler_params=_cparams("arbitrary"),
    )(w_in)


def _pad_w_uq(w):
    w = w.reshape(MLA_Q_RANK, MLA_HEADS, MLA_NOPE + MLA_ROPE)
    w = jnp.pad(w, ((0, 0), (0, 0), (0, MLA_QK_PAD - MLA_NOPE - MLA_ROPE)))
    return w.reshape(MLA_Q_RANK, MLA_HEADS * MLA_QK_PAD).astype(BF16)


def _split_w_ukv(w):
    w = w.reshape(MLA_KV_RANK, MLA_HEADS, 2, MLA_NOPE).transpose(0, 2, 1, 3)
    return w.reshape(MLA_KV_RANK, 2 * MLA_W).astype(BF16)


def _hyena(u3, layer, p, n, nc, with_ctx, tables):
    B = u3.shape[0]
    C = HY_CH
    P = B // 2
    g1d, g1f, g2, g2i, g3, gk, gd, gi = tables
    S = FFT_S
    N1 = 2 * n // S
    mlp = (p["hy_ffn_w1p"][layer], p["hy_ffn_b1"][layer][None], p["hy_ffn_w2"][layer], p["hy_ffn_b2"][layer][None],
           p["hy_ffn_w3"][layer], p["hy_ffn_b3"][layer][None], p["hy_ffn_w4"][layer], p["hy_freq"][layer][None])
    skip = p["hy_skip"][layer]
    cw, cb = p["hy_conv_w"][layer], p["hy_conv_b"][layer][None]

    kc, sums = _hy_filter(n, *mlp)
    kf = _fft_s1(g1f, kc.reshape(HY_ORDER, N1, S, C))
    kf = _fft_s2_filt(kf.reshape(HY_ORDER, 2, N1, S, C), g2, 1.0 / sums[:, 0:1, :])
    dw = [a.reshape(P, N1, S, C) for a in _hy_dwconv(u3, cw, cb, 0, n)]
    z = dw[0]
    for o in range(HY_ORDER):
        a = _fft_s1(g1d, z).reshape(P, 2, N1, S, C)
        b = _fft_s2(a, g2, g2i, kf, o).reshape(P, 2 * N1, S, C)
        z = _fft_s3(g3, b, z, dw[1 + o], skip[o][None], F32 if o + 1 < HY_ORDER else BF16)
    oc_lat = z.reshape(B, n, C)

    if with_ctx:
        kcc, sumc = _hy_filter(nc, *mlp)
        kfc = _dense_spec(kcc, gk, 1.0 / sumc[:, 0:1, :])
        dwc = [a.reshape(P, 2 * nc, C) for a in _hy_dwconv(u3, cw, cb, n // nc, nc)]
        zc = dwc[0]
        for o in range(HY_ORDER):
            zc = _dense_conv(zc, gd, gi, kfc, o, dwc[1 + o], skip[o][None], F32 if o + 1 < HY_ORDER else BF16)
        return jnp.concatenate([oc_lat, zc.reshape(B, nc, C)], axis=1)
    return oc_lat


def kernel(x, c, ctx, c_ctx, ada_w, ada_b, w_in, da_lambda, da_subln_g, mla_q_g, mla_w_uq, mla_kv_g, mla_w_ukv, hy_conv_w, hy_conv_b, hy_ffn_w1, hy_ffn_b1, hy_ffn_w2, hy_ffn_b2, hy_ffn_w3, hy_ffn_b3, hy_ffn_w4, hy_freq, hy_skip, w_branch_a, w_branch_b, w_branch_c, w_out, ln1_g, ln1_b, ffn_w_up, ffn_conv_w, ffn_conv_b, ffn_w_down, ln2_g, ln2_b):
    B, n, D = x.shape
    nc = ctx.shape[1]
    T = n + nc
    rows = B * T
    assert D == D_MODEL and B % 2 == 0 and B < 8
    assert n % ROW_GROUP == 0 and nc % ROW_GROUP == 0 and n % nc == 0 and n % GRID_W == 0
    assert (2 * n) % (8 * FFT_S) == 0 and T % ATTN_TK == 0 and nc % ATTN_TK == 0
    geom = (T // ROW_GROUP, n // ROW_GROUP, B)

    hy = dict(hy_ffn_w1p=jnp.pad(hy_ffn_w1, ((0, 0), (0, HY_FFN - HY_EMB), (0, 0))), hy_ffn_b1=hy_ffn_b1,
              hy_ffn_w2=hy_ffn_w2, hy_ffn_b2=hy_ffn_b2, hy_ffn_w3=hy_ffn_w3, hy_ffn_b3=hy_ffn_b3,
              hy_ffn_w4=hy_ffn_w4, hy_freq=hy_freq, hy_skip=hy_skip, hy_conv_w=hy_conv_w, hy_conv_b=hy_conv_b)
    tables = _dft_tables(n) + _dense_tables(nc)
    rope = _rope_tables(B, n, nc)
    keep_prev, keep_next = _conv_masks(B, n, nc)

    cc = jnp.concatenate([c, c_ctx[None], jnp.zeros((8 - B - 1, D), F32)], axis=0)
    mods = _ada(cc, ada_w, ada_b[:, None, :]).reshape(DEPTH, 8, 6, D)

    xs = jnp.concatenate([x, ctx], axis=1).reshape(rows, D)
    for i in range(DEPTH):
        last = i == DEPTH - 1
        lam_init = 0.8 - 0.6 * math.exp(-0.3 * i)
        lq1, lk1, lq2, lk2 = da_lambda[i].astype(F32)
        lam = (jnp.exp(jnp.sum(lq1 * lk1)) - jnp.exp(jnp.sum(lq2 * lk2)) + lam_init).reshape(1)
        mod = mods[i]
        w_qkv, w_mla, w_hy, w_g = _prep_w_in(w_in, i)

        if last:
            nb, tb = B, n
            sel = lambda b, g: b
        else:
            nb, tb = 1, rows
            sel = lambda b, g: _mod_row(g, *geom)

        h = _lnmod(xs, mod, geom)
        qkv = _qkv_proj(h, w_qkv, rope)
        p_mla = _matmul(h, w_mla, F32, w_mla.shape[1])
        u_hy = _matmul(h, w_hy, F32, 768, nb=nb, tb=tb)
        gates = _matmul(h, w_g, BF16, 1024, act="sigmoid", nb=nb, tb=tb)

        q_m, k_m, v_m = _mla_prep(p_mla, mla_q_g[i][None], mla_kv_g[i][None], _pad_w_uq(mla_w_uq[i]),
                                  _split_w_ukv(mla_w_ukv[i]), rope)
        oa = _diff_attn(qkv.reshape(B, T, 3 * DA_W), lam, da_subln_g[i][None], n, nc, not last, lam_init)
        ob = _mla_attn(q_m.reshape(B, T, -1), k_m.reshape(B, T, -1), v_m.reshape(B, T, -1), n, nc, not last)
        oc = _hyena(u_hy.reshape(B, -1, 3 * HY_CH), i, hy, n, nc, not last, tables)

        merged = _merge(oa.reshape(-1, DA_W), ob.reshape(-1, MLA_W), oc.reshape(-1, HY_CH),
                        w_branch_a[i].astype(BF16), w_branch_b[i].astype(BF16), w_branch_c[i].astype(BF16), gates,
                        nb, tb)
        xs1, h2 = _wo_ln(merged, w_out[i].astype(BF16), xs, mod, ln1_g[i][None], ln1_b[i][None], sel, nb, tb)
        u = _ffn_up(h2, ffn_w_up, i, ffn_conv_w[i], ffn_conv_b[i][None], keep_prev, keep_next, nb, tb)
        xs = _ffn_down(u, ffn_w_down[i].astype(BF16), xs1, mod, ln2_g[i][None], ln2_b[i][None], sel, nb, tb)
    return xs.reshape(B, n, D)
```

```python
import functools
import math

import numpy as np
import jax
import jax.numpy as jnp
from jax import lax
from jax.experimental import pallas as pl
from jax.experimental.pallas import tpu as pltpu

F32 = jnp.float32
BF16 = jnp.bfloat16

D_MODEL = 2048
DEPTH = 2
GRID_W = 64
ROPE_BASE = 10000.0
NORM_EPS = 1e-6
DA_HEADS = 6
DA_DQK = 64
DA_DV = 128
DA_W = DA_HEADS * DA_DV
MLA_HEADS = 6
MLA_Q_RANK = 512
MLA_KV_RANK = 256
MLA_NOPE = 128
MLA_ROPE = 64
MLA_DV = 128
MLA_W = MLA_HEADS * MLA_DV
MLA_QK_PAD = 256
HY_CH = 512
HY_ORDER = 2
HY_EMB = 33
HY_BANDS = (HY_EMB - 1) // 2
HY_FFN = 64
HY_MIN_DECAY = math.log(1e-2) / 1.5
HY_MAX_DECAY = math.log(1e-2) / 0.3
D_FF = 5632
N_BRANCH = 3
DEEPNORM_ALPHA = (2 * DEPTH) ** 0.25
LOG2E = 1.4426950408889634

ROW_GROUP = 256
LANE = 128
FFT_S = 128
VMEM_LIMIT = 52 * 1024 * 1024


def _cparams(*sem):
    return pltpu.CompilerParams(dimension_semantics=sem, vmem_limit_bytes=VMEM_LIMIT)


def _pick(total, prefs):
    for p in prefs:
        if total % p == 0:
            return p
    raise ValueError(f"no tile for {total} in {prefs}")


def _ln(x):
    mu = jnp.mean(x, axis=-1, keepdims=True)
    xc = x - mu
    var = jnp.mean(xc * xc, axis=-1, keepdims=True)
    return xc * lax.rsqrt(var + NORM_EPS)


def _rms(x):
    return x * lax.rsqrt(jnp.mean(x * x, axis=-1, keepdims=True) + NORM_EPS)


def _rope128(u, cos, sa, sb):
    return u * cos + pltpu.roll(u, 16, 1) * sa + pltpu.roll(u, LANE - 16, 1) * sb


def _ada_kernel(c_ref, w_ref, b_ref, o_ref):
    a = c_ref[...]
    a = a * jax.nn.sigmoid(a)
    o_ref[...] = jnp.dot(a.astype(BF16), w_ref[...].astype(BF16), preferred_element_type=F32) + b_ref[...]


def _ada(cc, ada_w, ada_b):
    L, D, N = ada_w.shape
    tn = 1024
    return pl.pallas_call(
        _ada_kernel,
        out_shape=jax.ShapeDtypeStruct((L, 8, N), F32),
        grid=(L, N // tn),
        in_specs=[pl.BlockSpec((8, D), lambda l, j: (0, 0)),
                  pl.BlockSpec((None, D, tn), lambda l, j: (l, 0, j)),
                  pl.BlockSpec((None, 1, tn), lambda l, j: (l, 0, j))],
        out_specs=pl.BlockSpec((None, 8, tn), lambda l, j: (l, 0, j)),
        compiler_params=_cparams("arbitrary", "arbitrary"),
    )(cc, ada_w, ada_b)


def _lnmod_kernel(x_ref, m_ref, o_ref):
    y = _ln(x_ref[...])
    o_ref[...] = (y * (1.0 + m_ref[1:2, :]) + m_ref[0:1, :]).astype(BF16)


def _mod_row(g, gpb, lat_groups, n_batch):
    return jnp.where(g % gpb < lat_groups, g // gpb, n_batch)


def _lnmod(xs, mod, geom):
    rows, D = xs.shape
    gpb, lat_groups, n_batch = geom
    return pl.pallas_call(
        _lnmod_kernel,
        out_shape=jax.ShapeDtypeStruct((rows, D), BF16),
        grid=(rows // ROW_GROUP,),
        in_specs=[pl.BlockSpec((ROW_GROUP, D), lambda i: (i, 0)),
                  pl.BlockSpec((None, 6, D), lambda i: (_mod_row(i, gpb, lat_groups, n_batch), 0, 0))],
        out_specs=pl.BlockSpec((ROW_GROUP, D), lambda i: (i, 0)),
        compiler_params=_cparams("arbitrary"),
    )(xs, mod)


def _mm_kernel(a_ref, w_ref, o_ref, *, act):
    acc = jnp.dot(a_ref[...], w_ref[...], preferred_element_type=F32)
    if act == "sigmoid":
        acc = jax.nn.sigmoid(acc)
    o_ref[...] = acc.astype(o_ref.dtype)


def _by_batch(a, nb):
    return a.reshape(nb, a.shape[0] // nb, a.shape[1])


def _matmul(a, w, out_dtype, tn, act=None, nb=1, tb=None):
    a = _by_batch(a, nb)
    K = a.shape[2]
    tb = tb or a.shape[1]
    N = w.shape[1]
    tm = _pick(tb, (1024, 512, 256))
    return pl.pallas_call(
        functools.partial(_mm_kernel, act=act),
        out_shape=jax.ShapeDtypeStruct((nb, tb, N), out_dtype),
        grid=(nb, tb // tm, N // tn),
        in_specs=[pl.BlockSpec((None, tm, K), lambda b, i, j: (b, i, 0)),
                  pl.BlockSpec((K, tn), lambda b, i, j: (0, j))],
        out_specs=pl.BlockSpec((None, tm, tn), lambda b, i, j: (b, i, j)),
        compiler_params=_cparams("arbitrary", "arbitrary", "arbitrary"),
    )(a, w).reshape(nb * tb, N)


def _qkv_kernel(h_ref, w_ref, cos_ref, sa_ref, sb_ref, o_ref, *, qscale):
    j = pl.program_id(1)
    acc = jnp.dot(h_ref[...], w_ref[...], preferred_element_type=F32)

    @pl.when(j < 2)
    def _():
        cos, sa, sb = cos_ref[...], sa_ref[...], sb_ref[...]
        scale = jnp.where(j == 0, qscale, 1.0).astype(F32)
        for c in range(DA_HEADS):
            u = acc[:, c * LANE:(c + 1) * LANE]
            o_ref[:, c * LANE:(c + 1) * LANE] = (_rope128(u, cos, sa, sb) * scale).astype(BF16)

    @pl.when(j == 2)
    def _():
        o_ref[...] = acc.astype(BF16)


def _qkv_proj(h, w_qkv, tabs):
    M, K = h.shape
    tm = _pick(M, (1024, 512, 256))
    tn = DA_W
    tab_spec = pl.BlockSpec((tm, LANE), lambda i, j: (i, 0))
    return pl.pallas_call(
        functools.partial(_qkv_kernel, qscale=DA_DQK ** -0.5 * LOG2E),
        out_shape=jax.ShapeDtypeStruct((M, 3 * DA_W), BF16),
        grid=(M // tm, 3),
        in_specs=[pl.BlockSpec((tm, K), lambda i, j: (i, 0)),
                  pl.BlockSpec((K, tn), lambda i, j: (0, j)),
                  tab_spec, tab_spec, tab_spec],
        out_specs=pl.BlockSpec((tm, tn), lambda i, j: (i, j)),
        compiler_params=_cparams("arbitrary", "arbitrary"),
    )(h, w_qkv, *tabs)


def _mla_prep_kernel(p_ref, qg_ref, kvg_ref, wuq_ref, wukv_ref, cos_ref, sa_ref, sb_ref,
                     q_ref, k_ref, v_ref, *, qscale):
    p = p_ref[...]
    cos, sa, sb = cos_ref[...], sa_ref[...], sb_ref[...]
    cq = p[:, :MLA_Q_RANK]
    ckv = p[:, MLA_Q_RANK:MLA_Q_RANK + MLA_KV_RANK]
    kr = p[:, MLA_Q_RANK + MLA_KV_RANK:]
    qn = (_rms(cq) * qg_ref[...]).astype(BF16)
    q = jnp.dot(qn, wuq_ref[...], preferred_element_type=F32)
    kvn = (_rms(ckv) * kvg_ref[...]).astype(BF16)
    kv = jnp.dot(kvn, wukv_ref[...], preferred_element_type=F32)
    krr = _rope128(kr, cos, sa, sb).astype(BF16)
    for h in range(MLA_HEADS):
        o = h * MLA_QK_PAD
        q_ref[:, o:o + LANE] = (q[:, o:o + LANE] * qscale).astype(BF16)
        q_ref[:, o + LANE:o + 2 * LANE] = (_rope128(q[:, o + LANE:o + 2 * LANE], cos, sa, sb) * qscale).astype(BF16)
        k_ref[:, o:o + LANE] = kv[:, h * LANE:(h + 1) * LANE].astype(BF16)
        k_ref[:, o + LANE:o + 2 * LANE] = krr
    v_ref[...] = kv[:, MLA_W:].astype(BF16)


def _mla_prep(p, q_g, kv_g, w_uq, w_ukv, tabs):
    M, W = p.shape
    tm = _pick(M, (512, 256))
    row = lambda i: (i, 0)
    full = lambda i: (0, 0)
    qk_w = MLA_HEADS * MLA_QK_PAD
    return pl.pallas_call(
        functools.partial(_mla_prep_kernel, qscale=(MLA_NOPE + MLA_ROPE) ** -0.5 * LOG2E),
        out_shape=(jax.ShapeDtypeStruct((M, qk_w), BF16),
                   jax.ShapeDtypeStruct((M, qk_w), BF16),
                   jax.ShapeDtypeStruct((M, MLA_W), BF16)),
        grid=(M // tm,),
        in_specs=[pl.BlockSpec((tm, W), row),
                  pl.BlockSpec((1, MLA_Q_RANK), full),
                  pl.BlockSpec((1, MLA_KV_RANK), full),
                  pl.BlockSpec(w_uq.shape, full),
                  pl.BlockSpec(w_ukv.shape, full),
                  pl.BlockSpec((tm, LANE), row), pl.BlockSpec((tm, LANE), row), pl.BlockSpec((tm, LANE), row)],
        out_specs=(pl.BlockSpec((tm, qk_w), row), pl.BlockSpec((tm, qk_w), row), pl.BlockSpec((tm, MLA_W), row)),
        compiler_params=_cparams("arbitrary"),
    )(p, q_g, kv_g, w_uq, w_ukv, *tabs)


_NT = (((1,), (1,)), ((), ()))
ATTN_TQ = 256
ATTN_CHAINS = 8
ATTN_TK = 256


def _skewed_pipeline(n_chains, n_chunks, stages):
    for t in range(n_chains + len(stages) - 1):
        active = [(s, t - s) for s in range(len(stages)) if 0 <= t - s < n_chains]
        for c in range(n_chunks):
            for s, chain in active:
                stages[s][0](chain, c)
        for s, chain in active:
            stages[s][1](chain)


def _acc(old, new, op):
    return new if old is None else op(old, new)


def _lane_halves(x, op):
    return op(x[:, :LANE], x[:, LANE:])


def _diff_attn_kernel(lam_ref, q_ref, k_ref, v_ref, g_ref, *rest, out_scale):
    o_ref = rest[-1]
    lam, g = lam_ref[0], g_ref[...]
    tq = min(ATTN_TQ, q_ref.shape[0])
    n_chains = q_ref.shape[0] // tq
    n_chunks = k_ref.shape[0] // ATTN_TK
    lane = lax.broadcasted_iota(jnp.int32, (1, LANE), 1)
    lo = (lane < DA_DQK).astype(F32)
    st = [dict(s=[], e=[], mx=[None, None], l=[None, None], o=None) for _ in range(n_chains)]
    rows = lambda i: slice(i * tq, (i + 1) * tq)
    keys = lambda c: slice(c * ATTN_TK, (c + 1) * ATTN_TK)

    def qk_chunk(i, c):
        d = st[i]
        if c == 0:
            qf = q_ref[rows(i), :].astype(F32)
            d["q"] = jnp.concatenate([(qf * lo).astype(BF16), (qf * (1.0 - lo)).astype(BF16)], axis=0)
        both = lax.dot_general(d["q"], k_ref[keys(c), :], _NT, preferred_element_type=F32)
        pair = []
        for m in range(2):
            s = both[m * tq:(m + 1) * tq]
            d["mx"][m] = _acc(d["mx"][m], _lane_halves(s, jnp.maximum), jnp.maximum)
            pair.append(s)
        d["s"].append(pair)

    def qk_done(i):
        st[i]["m"] = [jnp.max(mx, axis=-1, keepdims=True) for mx in st[i]["mx"]]

    def exp_chunk(i, c):
        d = st[i]
        pair = []
        for m in range(2):
            e = jnp.exp2(d["s"][c][m] - d["m"][m])
            d["l"][m] = _acc(d["l"][m], _lane_halves(e, jnp.add), jnp.add)
            pair.append(e)
        d["s"][c] = None
        d["e"].append(pair)

    def exp_done(i):
        d = st[i]
        l1, l2 = [jnp.sum(l, axis=-1, keepdims=True) for l in d["l"]]
        d["r1"] = 1.0 / l1
        d["cf"] = lam * l1 / l2

    def pv_chunk(i, c):
        d = st[i]
        w = (d["e"][c][0] - d["cf"] * d["e"][c][1]).astype(BF16)
        d["e"][c] = None
        d["o"] = _acc(d["o"], jnp.dot(w, v_ref[keys(c), :], preferred_element_type=F32), jnp.add)

    def pv_done(i):
        o = st[i]["o"] * st[i]["r1"]
        o_ref[rows(i), :] = (_rms(o) * g * out_scale).astype(BF16)

    _skewed_pipeline(n_chains, n_chunks, [(qk_chunk, qk_done), (exp_chunk, exp_done), (pv_chunk, pv_done)])


def _mla_attn_kernel(q_ref, k_ref, v_ref, *rest):
    o_ref = rest[-1]
    tq = min(ATTN_TQ, q_ref.shape[0])
    n_chains = q_ref.shape[0] // tq
    n_chunks = k_ref.shape[0] // ATTN_TK
    st = [dict(s=[], mx=None, l=None, o=None) for _ in range(n_chains)]
    rows = lambda i: slice(i * tq, (i + 1) * tq)
    keys = lambda c: slice(c * ATTN_TK, (c + 1) * ATTN_TK)

    def qk_chunk(i, c):
        d = st[i]
        s = lax.dot_general(q_ref[rows(i), :], k_ref[keys(c), :], _NT, preferred_element_type=F32)
        d["mx"] = _acc(d["mx"], _lane_halves(s, jnp.maximum), jnp.maximum)
        d["s"].append(s)

    def qk_done(i):
        st[i]["m"] = jnp.max(st[i]["mx"], axis=-1, keepdims=True)

    def pv_chunk(i, c):
        d = st[i]
        e = jnp.exp2(d["s"][c] - d["m"])
        d["s"][c] = None
        d["l"] = _acc(d["l"], _lane_halves(e, jnp.add), jnp.add)
        d["o"] = _acc(d["o"], jnp.dot(e.astype(BF16), v_ref[keys(c), :], preferred_element_type=F32), jnp.add)

    def pv_done(i):
        d = st[i]
        o_ref[rows(i), :] = (d["o"] * (1.0 / jnp.sum(d["l"], axis=-1, keepdims=True))).astype(BF16)

    _skewed_pipeline(n_chains, n_chunks, [(qk_chunk, qk_done), (pv_chunk, pv_done)])


def _diff_attn(qkv, lam, subln_g, n, nc, ctx_queries, lam_init):
    B, T, _ = qkv.shape
    H = DA_HEADS
    tq = _pick(n, (ATTN_TQ * ATTN_CHAINS, ATTN_TQ))
    kern = functools.partial(_diff_attn_kernel, out_scale=1.0 - lam_init)
    smem = pl.BlockSpec(memory_space=pltpu.SMEM)
    gspec = pl.BlockSpec((1, DA_DV), lambda b, h, i: (0, 0))
    oa = pl.pallas_call(
        kern,
        out_shape=jax.ShapeDtypeStruct((B, T if ctx_queries else n, DA_W), BF16),
        grid=(B, H, n // tq),
        in_specs=[smem,
                  pl.BlockSpec((None, tq, LANE), lambda b, h, i: (b, i, h)),
                  pl.BlockSpec((None, T, LANE), lambda b, h, i: (b, 0, H + h)),
                  pl.BlockSpec((None, T, LANE), lambda b, h, i: (b, 0, 2 * H + h)),
                  gspec],
        out_specs=pl.BlockSpec((None, tq, LANE), lambda b, h, i: (b, i, h)),
        compiler_params=_cparams("arbitrary", "arbitrary", "arbitrary"),
    )(lam, qkv, qkv, qkv, subln_g)
    if not ctx_queries:
        return oa
    cb = n // nc
    return pl.pallas_call(
        kern,
        out_shape=jax.ShapeDtypeStruct((B, T, DA_W), BF16),
        grid=(B, H, 1),
        in_specs=[smem,
                  pl.BlockSpec((None, nc, LANE), lambda b, h, i: (b, cb, h)),
                  pl.BlockSpec((None, nc, LANE), lambda b, h, i: (b, cb, H + h)),
                  pl.BlockSpec((None, nc, LANE), lambda b, h, i: (b, cb, 2 * H + h)),
                  gspec,
                  pl.BlockSpec(memory_space=pl.ANY)],
        out_specs=pl.BlockSpec((None, nc, LANE), lambda b, h, i: (b, cb, h)),
        input_output_aliases={5: 0},
        compiler_params=_cparams("arbitrary", "arbitrary", "arbitrary"),
    )(lam, qkv, qkv, qkv, subln_g, oa)


def _mla_attn(q, k, v, n, nc, ctx_queries):
    B, T, _ = q.shape
    H = MLA_HEADS
    tq = _pick(n, (ATTN_TQ * ATTN_CHAINS, ATTN_TQ))
    ob = pl.pallas_call(
        _mla_attn_kernel,
        out_shape=jax.ShapeDtypeStruct((B, T if ctx_queries else n, MLA_W), BF16),
        grid=(B, H, n // tq),
        in_specs=[pl.BlockSpec((None, tq, MLA_QK_PAD), lambda b, h, i: (b, i, h)),
                  pl.BlockSpec((None, T, MLA_QK_PAD), lambda b, h, i: (b, 0, h)),
                  pl.BlockSpec((None, T, MLA_DV), lambda b, h, i: (b, 0, h))],
        out_specs=pl.BlockSpec((None, tq, MLA_DV), lambda b, h, i: (b, i, h)),
        compiler_params=_cparams("arbitrary", "arbitrary", "arbitrary"),
    )(q, k, v)
    if not ctx_queries:
        return ob
    cb = n // nc
    return pl.pallas_call(
        _mla_attn_kernel,
        out_shape=jax.ShapeDtypeStruct((B, T, MLA_W), BF16),
        grid=(B, H, 1),
        in_specs=[pl.BlockSpec((None, nc, MLA_QK_PAD), lambda b, h, i: (b, cb, h)),
                  pl.BlockSpec((None, nc, MLA_QK_PAD), lambda b, h, i: (b, cb, h)),
                  pl.BlockSpec((None, nc, MLA_DV), lambda b, h, i: (b, cb, h)),
                  pl.BlockSpec(memory_space=pl.ANY)],
        out_specs=pl.BlockSpec((None, nc, MLA_DV), lambda b, h, i: (b, cb, h)),
        input_output_aliases={3: 0},
        compiler_params=_cparams("arbitrary", "arbitrary", "arbitrary"),
    )(q, k, v, ob)


def _hy_filter_kernel(feat_ref, dec_ref, w1_ref, b1_ref, w2_ref, b2_ref, w3_ref, b3_ref, w4_ref, fr_ref,
                      k_ref, s_ref):
    hp = lax.Precision.HIGHEST
    d, r = pl.program_id(0), pl.program_id(1)
    fr = fr_ref[...]
    h = jnp.sin(fr * (jnp.dot(feat_ref[...], w1_ref[...], precision=hp, preferred_element_type=F32) + b1_ref[...]))
    h = jnp.sin(fr * (jnp.dot(h, w2_ref[...], precision=hp, preferred_element_type=F32) + b2_ref[...]))
    h = jnp.sin(fr * (jnp.dot(h, w3_ref[...], precision=hp, preferred_element_type=F32) + b3_ref[...]))
    h = jnp.dot(h, w4_ref[...], precision=hp, preferred_element_type=F32)
    row = lax.broadcasted_iota(jnp.int32, (h.shape[0], 1), 0)
    first_bwd = jnp.where((d == 1) & (r == 0), 1.0, 0.0)
    scale = dec_ref[...] * (1.0 - jnp.where(row == 0, 1.0, 0.0) * first_bwd)

    @pl.when((d == 0) & (r == 0))
    def _():
        s_ref[...] = jnp.zeros(s_ref.shape, F32)

    for o in range(HY_ORDER):
        ko = h[:, o * HY_CH:(o + 1) * HY_CH] * scale
        k_ref[o] = ko
        s_ref[o] += jnp.broadcast_to(jnp.sum(jnp.abs(ko), axis=0, keepdims=True), (8, HY_CH))


def _hy_filter(n, w1, b1, w2, b2, w3, b3, w4, fr):
    C = HY_CH
    t = jnp.linspace(0.0, 1.0, n, dtype=F32)
    pos = jnp.arange(n, dtype=F32)
    t2 = jnp.concatenate([t, t[::-1]])[:, None]
    pos2 = jnp.concatenate([pos, pos[::-1]])[:, None]
    phase = (2.0 * math.pi / n) * pos2 * jnp.linspace(1e-4, HY_BANDS - 1, HY_BANDS, dtype=F32)[None, :]
    feat = jnp.concatenate([t2, jnp.cos(phase), -jnp.sin(phase)], axis=-1)
    feat = jnp.pad(feat, ((0, 0), (0, HY_FFN - HY_EMB)))
    dec = jnp.exp(-t2 * jnp.abs(jnp.linspace(HY_MIN_DECAY, HY_MAX_DECAY, C, dtype=F32)))
    w4d = w4.reshape(HY_FFN, HY_ORDER, 2, C).transpose(2, 0, 1, 3).reshape(2, HY_FFN, HY_ORDER * C)
    rb = min(512, n)
    nb = n // rb
    full = lambda d, r: (0, 0)
    return pl.pallas_call(
        _hy_filter_kernel,
        out_shape=(jax.ShapeDtypeStruct((HY_ORDER, 2 * n, C), F32),
                   jax.ShapeDtypeStruct((HY_ORDER, 8, C), F32)),
        grid=(2, nb),
        in_specs=[pl.BlockSpec((rb, HY_FFN), lambda d, r: (d * nb + r, 0)),
                  pl.BlockSpec((rb, C), lambda d, r: (d * nb + r, 0)),
                  pl.BlockSpec(w1.shape, full), pl.BlockSpec(b1.shape, full),
                  pl.BlockSpec(w2.shape, full), pl.BlockSpec(b2.shape, full),
                  pl.BlockSpec(w3.shape, full), pl.BlockSpec(b3.shape, full),
                  pl.BlockSpec((None, HY_FFN, HY_ORDER * C), lambda d, r: (d, 0, 0)),
                  pl.BlockSpec(fr.shape, full)],
        out_specs=(pl.BlockSpec((HY_ORDER, rb, C), lambda d, r: (0, d * nb + r, 0)),
                   pl.BlockSpec((HY_ORDER, 8, C), lambda d, r: (0, 0, 0))),
        compiler_params=_cparams("arbitrary", "arbitrary"),
    )(feat, dec, w1, b1, w2, b2, w3, b3, w4d, fr)


def _hy_dwconv_kernel(*refs):
    for k in range(3):
        u_ref, w_ref, b_ref, o_ref = refs[3 * k], refs[3 * k + 1], refs[3 * k + 2], refs[9 + k]
        u = u_ref[...]
        L = u.shape[0]
        row = lax.broadcasted_iota(jnp.int32, (L, 1), 0)
        up = jnp.where(row == 0, 0.0, pltpu.roll(u, 1, 0))
        dn = jnp.where(row == L - 1, 0.0, pltpu.roll(u, L - 1, 0))
        o_ref[...] = up * w_ref[0:1, :] + u * w_ref[1:2, :] + dn * w_ref[2:3, :] + b_ref[...]


def _hy_dwconv(u, w, b, row_block, length):
    B = u.shape[0]
    cw = LANE
    per = HY_CH // cw
    in_specs, args = [], []
    for k in range(3):
        in_specs += [pl.BlockSpec((None, length, cw), lambda bb, j, k=k: (bb, row_block, k * per + j)),
                     pl.BlockSpec((3, cw), lambda bb, j, k=k: (0, k * per + j)),
                     pl.BlockSpec((1, cw), lambda bb, j, k=k: (0, k * per + j))]
        args += [u, w, b]
    ospec = pl.BlockSpec((None, length, cw), lambda bb, j: (bb, 0, j))
    return pl.pallas_call(
        _hy_dwconv_kernel,
        out_shape=(jax.ShapeDtypeStruct((B, length, HY_CH), F32),) * 3,
        grid=(B, per),
        in_specs=in_specs,
        out_specs=(ospec,) * 3,
        compiler_params=_cparams("arbitrary", "arbitrary"),
    )(*args)


def _dft_tables(n):
    S = FFT_S
    M = 2 * n
    N1 = M // S
    H = N1 // 2
    s2 = np.arange(S)[:, None, None]
    k1 = np.arange(N1)[None, :, None]
    s1 = np.arange(N1)[None, None, :]
    ang = -2.0 * np.pi * ((k1 * (S * s1 + s2)) % M) / M
    fr, fi = np.cos(ang), np.sin(ang)
    g1f = np.concatenate([fr, fi], axis=1)
    frh, fih = fr[:, :, :H], fi[:, :, :H]
    g1d = np.concatenate([np.concatenate([frh, -fih], axis=2),
                          np.concatenate([fih, frh], axis=2)], axis=1)
    er = np.transpose(frh, (0, 2, 1)) / M
    ei = -np.transpose(fih, (0, 2, 1)) / M
    g3 = np.concatenate([np.concatenate([er, -ei], axis=2),
                         np.concatenate([ei, er], axis=2)], axis=1)
    a2 = -2.0 * np.pi * ((np.arange(S)[:, None] * np.arange(S)[None, :]) % S) / S
    f2r, f2i = np.cos(a2), np.sin(a2)
    g2 = np.block([[f2r, -f2i], [f2i, f2r]])
    g2i = np.block([[f2r, f2i], [-f2i, f2r]])
    cast = lambda a: jnp.asarray(a, dtype=F32).astype(BF16)
    return cast(g1d), cast(g1f), cast(g2), cast(g2i), cast(g3)


FFT_NS = 16


def _fft_s1_kernel(g_ref, x_ref, o_ref):
    xt = jnp.swapaxes(x_ref[...], 0, 1)
    y = jnp.stack([jnp.dot(g_ref[j], xt[j].astype(BF16), preferred_element_type=F32)
                   for j in range(xt.shape[0])], axis=0)
    o_ref[...] = jnp.swapaxes(y, 0, 1).astype(o_ref.dtype)


def _fft_s1(g, x):
    P, Ri, S, C = x.shape
    Ro = g.shape[1]
    ns = FFT_NS
    return pl.pallas_call(
        _fft_s1_kernel,
        out_shape=jax.ShapeDtypeStruct((P, Ro, S, C), BF16),
        grid=(S // ns, P),
        in_specs=[pl.BlockSpec((ns, Ro, Ri), lambda j, p: (j, 0, 0)),
                  pl.BlockSpec((None, Ri, ns, C), lambda j, p: (p, 0, j, 0))],
        out_specs=pl.BlockSpec((None, Ro, ns, C), lambda j, p: (p, 0, j, 0)),
        compiler_params=_cparams("arbitrary", "arbitrary"),
    )(g, x)


def _fft_s2_filt_kernel(a_ref, g_ref, rn_ref, o_ref, *, nk):
    S = FFT_S
    rn = rn_ref[...]
    for t in range(nk):
        d = jnp.concatenate([a_ref[0, t], a_ref[1, t]], axis=0)
        y = jnp.dot(g_ref[...], d, preferred_element_type=F32)
        o_ref[0, t] = y[:S] * rn
        o_ref[1, t] = y[S:] * rn


def _fft_s2_filt(a, g2, rnorm):
    O, _, N1, S, C = a.shape
    nk = 4
    blk = (None, 2, nk, S, C)
    return pl.pallas_call(
        functools.partial(_fft_s2_filt_kernel, nk=nk),
        out_shape=jax.ShapeDtypeStruct(a.shape, F32),
        grid=(O, N1 // nk),
        in_specs=[pl.BlockSpec(blk, lambda o, j: (o, 0, j, 0, 0)),
                  pl.BlockSpec(g2.shape, lambda o, j: (0, 0)),
                  pl.BlockSpec((None, 1, C), lambda o, j: (o, 0, 0))],
        out_specs=pl.BlockSpec(blk, lambda o, j: (o, 0, j, 0, 0)),
        compiler_params=_cparams("arbitrary", "arbitrary"),
    )(a, g2, rnorm)


def _fft_s2_kernel(a_ref, g_ref, gi_ref, kf_ref, o_ref, *, nk):
    S = FFT_S
    for t in range(nk):
        d = jnp.concatenate([a_ref[0, t], a_ref[1, t]], axis=0)
        y = jnp.dot(g_ref[...], d, preferred_element_type=F32)
        yr, yi = y[:S], y[S:]
        kr, ki = kf_ref[0, t], kf_ref[1, t]
        p = jnp.concatenate([yr * kr - yi * ki, yr * ki + yi * kr], axis=0).astype(BF16)
        b = jnp.dot(gi_ref[...], p, preferred_element_type=F32)
        o_ref[0, t] = b[:S].astype(BF16)
        o_ref[1, t] = b[S:].astype(BF16)


def _fft_s2(a, g2, g2i, kf, order):
    P, _, N1, S, C = a.shape
    nk = 4
    blk = (None, 2, nk, S, C)
    return pl.pallas_call(
        functools.partial(_fft_s2_kernel, nk=nk),
        out_shape=jax.ShapeDtypeStruct(a.shape, BF16),
        grid=(N1 // nk, P),
        in_specs=[pl.BlockSpec(blk, lambda j, p: (p, 0, j, 0, 0)),
                  pl.BlockSpec(g2.shape, lambda j, p: (0, 0)),
                  pl.BlockSpec(g2i.shape, lambda j, p: (0, 0)),
                  pl.BlockSpec(blk, lambda j, p: (order, 0, j, 0, 0))],
        out_specs=pl.BlockSpec(blk, lambda j, p: (p, 0, j, 0, 0)),
        compiler_params=_cparams("arbitrary", "arbitrary"),
    )(a, g2, g2i, kf)


def _fft_s3_kernel(g_ref, b_ref, z_ref, gate_ref, skip_ref, o_ref):
    bt = jnp.swapaxes(b_ref[...].astype(F32), 0, 1).astype(BF16)
    y = jnp.stack([jnp.dot(g_ref[j], bt[j], preferred_element_type=F32) for j in range(bt.shape[0])], axis=0)
    y = jnp.swapaxes(y, 0, 1)
    o_ref[...] = (gate_ref[...] * (y + skip_ref[...] * z_ref[...])).astype(o_ref.dtype)


def _fft_s3(g3, b, z, gate, skip, out_dtype):
    P, Ri, S, C = b.shape
    Ro = g3.shape[1]
    ns = FFT_NS
    dspec = pl.BlockSpec((None, Ro, ns, C), lambda j, p: (p, 0, j, 0))
    return pl.pallas_call(
        _fft_s3_kernel,
        out_shape=jax.ShapeDtypeStruct((P, Ro, S, C), out_dtype),
        grid=(S // ns, P),
        in_specs=[pl.BlockSpec((ns, Ro, Ri), lambda j, p: (j, 0, 0)),
                  pl.BlockSpec((None, Ri, ns, C), lambda j, p: (p, 0, j, 0)),
                  dspec, dspec,
                  pl.BlockSpec((1, C), lambda j, p: (0, 0))],
        out_specs=dspec,
        compiler_params=_cparams("arbitrary", "arbitrary"),
    )(g3, b, z, gate, skip)


def _dense_tables(m):
    M = 2 * m
    ang = -2.0 * np.pi * ((np.arange(M)[:, None] * np.arange(M)[None, :]) % M) / M
    fr, fi = np.cos(ang), np.sin(ang)
    gk = np.concatenate([fr, fi], axis=0)
    gd = np.block([[fr[:, :m], -fi[:, :m]], [fi[:, :m], fr[:, :m]]])
    er, ei = fr[:m, :] / M, -fi[:m, :] / M
    gi = np.block([[er, -ei], [ei, er]])
    cast = lambda a: jnp.asarray(a, dtype=F32).astype(BF16)
    return cast(gk), cast(gd), cast(gi)


def _dense_spec_kernel(k_ref, g_ref, rn_ref, o_ref):
    o_ref[...] = jnp.dot(g_ref[...], k_ref[...].astype(BF16), preferred_element_type=F32) * rn_ref[...]


def _dense_spec(kc, gk, rnorm):
    O, M, C = kc.shape
    return pl.pallas_call(
        _dense_spec_kernel,
        out_shape=jax.ShapeDtypeStruct((O, 2 * M, C), F32),
        grid=(O,),
        in_specs=[pl.BlockSpec((None, M, C), lambda o: (o, 0, 0)),
                  pl.BlockSpec(gk.shape, lambda o: (0, 0)),
                  pl.BlockSpec((None, 1, C), lambda o: (o, 0, 0))],
        out_specs=pl.BlockSpec((None, 2 * M, C), lambda o: (o, 0, 0)),
        compiler_params=_cparams("arbitrary"),
    )(kc, gk, rnorm)


def _dense_conv_kernel(x_ref, gd_ref, gi_ref, kf_ref, gate_ref, skip_ref, o_ref):
    x = x_ref[...]
    y = jnp.dot(gd_ref[...], x.astype(BF16), preferred_element_type=F32)
    M = y.shape[0] // 2
    yr, yi = y[:M], y[M:]
    kr, ki = kf_ref[:M], kf_ref[M:]
    p = jnp.concatenate([yr * kr - yi * ki, yr * ki + yi * kr], axis=0).astype(BF16)
    conv = jnp.dot(gi_ref[...], p, preferred_element_type=F32)
    o_ref[...] = (gate_ref[...] * (conv + skip_ref[...] * x)).astype(o_ref.dtype)


def _dense_conv(x, gd, gi, kf, order, gate, skip, out_dtype):
    P, R, C = x.shape
    dspec = pl.BlockSpec((None, R, C), lambda p: (p, 0, 0))
    return pl.pallas_call(
        _dense_conv_kernel,
        out_shape=jax.ShapeDtypeStruct((P, R, C), out_dtype),
        grid=(P,),
        in_specs=[dspec,
                  pl.BlockSpec(gd.shape, lambda p: (0, 0)),
                  pl.BlockSpec(gi.shape, lambda p: (0, 0)),
                  pl.BlockSpec((None,) + kf.shape[1:], lambda p: (order, 0, 0)),
                  dspec,
                  pl.BlockSpec((1, C), lambda p: (0, 0))],
        out_specs=dspec,
        compiler_params=_cparams("arbitrary"),
    )(x, gd, gi, kf, gate, skip)


def _merge_kernel(oa_ref, ob_ref, oc_ref, wa_ref, wb_ref, wc_ref, g0_ref, g1_ref, g2_ref, o_ref):
    ya = jnp.dot(oa_ref[...], wa_ref[...], preferred_element_type=F32)
    yb = jnp.dot(ob_ref[...], wb_ref[...], preferred_element_type=F32)
    yc = jnp.dot(oc_ref[...], wc_ref[...], preferred_element_type=F32)
    m = g0_ref[...].astype(F32) * ya + g1_ref[...].astype(F32) * yb + g2_ref[...].astype(F32) * yc
    o_ref[...] = m.astype(BF16)


def _merge(oa, ob, oc, w_ba, w_bb, w_bc, gates, nb, tb):
    D = D_MODEL
    tm = _pick(tb, (1024, 512, 256))
    tn = 512
    nt = D // tn
    oa, ob, oc, gates = (_by_batch(a, nb) for a in (oa, ob, oc, gates))
    row = lambda b, i, j: (b, i, 0)
    col = lambda b, i, j: (0, j)
    return pl.pallas_call(
        _merge_kernel,
        out_shape=jax.ShapeDtypeStruct((nb, tb, D), BF16),
        grid=(nb, tb // tm, nt),
        in_specs=[pl.BlockSpec((None, tm, DA_W), row), pl.BlockSpec((None, tm, MLA_W), row),
                  pl.BlockSpec((None, tm, HY_CH), row),
                  pl.BlockSpec((DA_W, tn), col), pl.BlockSpec((MLA_W, tn), col), pl.BlockSpec((HY_CH, tn), col),
                  pl.BlockSpec((None, tm, tn), lambda b, i, j: (b, i, j)),
                  pl.BlockSpec((None, tm, tn), lambda b, i, j: (b, i, nt + j)),
                  pl.BlockSpec((None, tm, tn), lambda b, i, j: (b, i, 2 * nt + j))],
        out_specs=pl.BlockSpec((None, tm, tn), lambda b, i, j: (b, i, j)),
        compiler_params=_cparams("arbitrary", "arbitrary", "arbitrary"),
    )(oa, ob, oc, w_ba, w_bb, w_bc, gates, gates, gates).reshape(nb * tb, D)


def _wo_ln_kernel(m_ref, w_ref, xs_ref, mod_ref, g_ref, b_ref, xs1_ref, h2_ref, *, sel, groups):
    y = jnp.dot(m_ref[...], w_ref[...], preferred_element_type=F32)
    g, b = g_ref[...], b_ref[...]
    for q in range(groups):
        mod = mod_ref[sel(pl.program_id(0), pl.program_id(1) * groups + q)]
        sl = slice(q * ROW_GROUP, (q + 1) * ROW_GROUP)
        x1 = _ln(DEEPNORM_ALPHA * xs_ref[sl, :] + mod[2:3, :] * y[sl, :]) * g + b
        xs1_ref[sl, :] = x1
        h2_ref[sl, :] = (_ln(x1) * (1.0 + mod[4:5, :]) + mod[3:4, :]).astype(BF16)


def _wo_ln(merged, w_o, xs, mod, ln_g, ln_b, sel, nb, tb):
    D = D_MODEL
    tm = _pick(tb, (512, 256))
    merged, xs = _by_batch(merged, nb), _by_batch(xs, nb)
    row = lambda b, i: (b, i, 0)
    full2 = lambda b, i: (0, 0)
    xs1, h2 = pl.pallas_call(
        functools.partial(_wo_ln_kernel, sel=sel, groups=tm // ROW_GROUP),
        out_shape=(jax.ShapeDtypeStruct((nb, tb, D), F32), jax.ShapeDtypeStruct((nb, tb, D), BF16)),
        grid=(nb, tb // tm),
        in_specs=[pl.BlockSpec((None, tm, D), row),
                  pl.BlockSpec((D, D), full2),
                  pl.BlockSpec((None, tm, D), row),
                  pl.BlockSpec(mod.shape, lambda b, i: (0, 0, 0)),
                  pl.BlockSpec((1, D), full2), pl.BlockSpec((1, D), full2)],
        out_specs=(pl.BlockSpec((None, tm, D), row), pl.BlockSpec((None, tm, D), row)),
        compiler_params=_cparams("arbitrary", "arbitrary"),
    )(merged, w_o, xs, mod, ln_g, ln_b)
    return xs1.reshape(nb * tb, D), h2.reshape(nb * tb, D)


FFN_HALO = 16


def _ffn_up_kernel(hp_ref, h_ref, hn_ref, wa_ref, wv_ref, cw_ref, cb_ref, kp_ref, kn_ref, o_ref, wa_s, wv_s):
    tm, tn = o_ref.shape

    @pl.when((pl.program_id(1) == 0) & (pl.program_id(2) == 0))
    def _():
        wa_s[...] = wa_ref[...].astype(BF16)
        wv_s[...] = wv_ref[...].astype(BF16)

    hm = h_ref[...]
    hext = jnp.concatenate([hp_ref[...], hm, hn_ref[...]], axis=0)
    a = jnp.dot(hext, wa_s[...], preferred_element_type=F32)
    v = jnp.dot(hm, wv_s[...], preferred_element_type=F32)
    ext = tm + 2 * FFN_HALO
    rep = tn // LANE
    keep_prev = jnp.tile(kp_ref[...], (1, rep))
    keep_next = jnp.tile(kn_ref[...], (1, rep))
    a_prev = pltpu.roll(a, 1, 0)[FFN_HALO:FFN_HALO + tm] * keep_prev
    a_next = pltpu.roll(a, ext - 1, 0)[FFN_HALO:FFN_HALO + tm] * keep_next
    cv = a_prev * cw_ref[0:1, :] + a[FFN_HALO:FFN_HALO + tm] * cw_ref[1:2, :] + a_next * cw_ref[2:3, :] + cb_ref[...]
    o_ref[...] = (cv * jax.nn.sigmoid(cv) * v).astype(BF16)


def _ffn_up(h2, w_up, layer, conv_w, conv_b, keep_prev, keep_next, nb, tb):
    D = D_MODEL
    tm = _pick(tb, (1024, 512, 256))
    tn = 512
    nt = D_FF // tn
    hb = tm // FFN_HALO
    last = tb // FFN_HALO - 1
    h2, keep_prev, keep_next = (_by_batch(a, nb) for a in (h2, keep_prev, keep_next))
    mask_spec = pl.BlockSpec((None, tm, LANE), lambda j, b, i: (b, i, 0))
    return pl.pallas_call(
        _ffn_up_kernel,
        out_shape=jax.ShapeDtypeStruct((nb, tb, D_FF), BF16),
        grid=(nt, nb, tb // tm),
        in_specs=[pl.BlockSpec((None, FFN_HALO, D), lambda j, b, i: (b, jnp.maximum(i * hb - 1, 0), 0)),
                  pl.BlockSpec((None, tm, D), lambda j, b, i: (b, i, 0)),
                  pl.BlockSpec((None, FFN_HALO, D), lambda j, b, i: (b, jnp.minimum((i + 1) * hb, last), 0)),
                  pl.BlockSpec((None, D, tn), lambda j, b, i: (layer, 0, j)),
                  pl.BlockSpec((None, D, tn), lambda j, b, i: (layer, 0, nt + j)),
                  pl.BlockSpec((3, tn), lambda j, b, i: (0, j)),
                  pl.BlockSpec((1, tn), lambda j, b, i: (0, j)),
                  mask_spec, mask_spec],
        out_specs=pl.BlockSpec((None, tm, tn), lambda j, b, i: (b, i, j)),
        scratch_shapes=[pltpu.VMEM((D, tn), BF16), pltpu.VMEM((D, tn), BF16)],
        compiler_params=_cparams("arbitrary", "arbitrary", "arbitrary"),
    )(h2, h2, h2, w_up, w_up, conv_w, conv_b, keep_prev, keep_next).reshape(nb * tb, D_FF)


def _ffn_down_kernel(u_ref, w_ref, xs_ref, mod_ref, g_ref, b_ref, o_ref, acc_ref, *, sel, groups, nk):
    k = pl.program_id(2)

    @pl.when(k == 0)
    def _():
        acc_ref[...] = jnp.zeros(acc_ref.shape, F32)

    acc_ref[...] += jnp.dot(u_ref[...], w_ref[...], preferred_element_type=F32)

    @pl.when(k == nk - 1)
    def _():
        g, b = g_ref[...], b_ref[...]
        for q in range(groups):
            mod = mod_ref[sel(pl.program_id(0), pl.program_id(1) * groups + q)]
            sl = slice(q * ROW_GROUP, (q + 1) * ROW_GROUP)
            o_ref[sl, :] = _ln(DEEPNORM_ALPHA * xs_ref[sl, :] + mod[5:6, :] * acc_ref[sl, :]) * g + b


def _ffn_down(u, w_down, xs1, mod, ln_g, ln_b, sel, nb, tb):
    D = D_MODEL
    tm = _pick(tb, (512, 256))
    tk = D_FF // 2
    nk = D_FF // tk
    u, xs1 = _by_batch(u, nb), _by_batch(xs1, nb)
    row = lambda b, i, k: (b, i, 0)
    full2 = lambda b, i, k: (0, 0)
    return pl.pallas_call(
        functools.partial(_ffn_down_kernel, sel=sel, groups=tm // ROW_GROUP, nk=nk),
        out_shape=jax.ShapeDtypeStruct((nb, tb, D), F32),
        grid=(nb, tb // tm, nk),
        in_specs=[pl.BlockSpec((None, tm, tk), lambda b, i, k: (b, i, k)),
                  pl.BlockSpec((tk, D), lambda b, i, k: (k, 0)),
                  pl.BlockSpec((None, tm, D), row),
                  pl.BlockSpec(mod.shape, lambda b, i, k: (0, 0, 0)),
                  pl.BlockSpec((1, D), full2), pl.BlockSpec((1, D), full2)],
        out_specs=pl.BlockSpec((None, tm, D), row),
        scratch_shapes=[pltpu.VMEM((tm, D), F32)],
        compiler_params=_cparams("arbitrary", "arbitrary", "arbitrary"),
    )(u, w_down, xs1, mod, ln_g, ln_b).reshape(nb * tb, D)


def _rope_tables(B, n, nc):
    half = DA_DQK // 2
    inv = ROPE_BASE ** (-jnp.arange(0, half, 2, dtype=F32) / half)
    t = jnp.arange(n, dtype=jnp.int32)
    ang_r = (t // GRID_W).astype(F32)[:, None] * inv[None, :]
    ang_c = (t % GRID_W).astype(F32)[:, None] * inv[None, :]
    ang = jnp.concatenate([ang_r, ang_r, ang_c, ang_c], axis=-1)
    cos, sin = jnp.cos(ang), jnp.sin(ang)
    upper = (jnp.arange(DA_DQK) % half) >= half // 2
    sa = jnp.where(upper, sin, 0.0)
    sb = jnp.where(upper, 0.0, -sin)

    def full(tab, fill):
        tab = jnp.concatenate([tab, jnp.full((nc, DA_DQK), fill, F32)], axis=0)
        tab = jnp.tile(tab, (B, LANE // DA_DQK))
        return tab

    return full(cos, 1.0), full(sa, 0.0), full(sb, 0.0)


def _conv_masks(B, n, nc):
    T = n + nc
    t = jnp.arange(T)
    keep_prev = ((t != 0) & (t != n)).astype(F32)
    keep_next = ((t != n - 1) & (t != T - 1)).astype(F32)
    widen = lambda m: jnp.tile(m[:, None], (B, LANE))
    return widen(keep_prev), widen(keep_next)


W_IN_WIDTHS = (DA_W, DA_W, DA_W, MLA_Q_RANK, MLA_KV_RANK, MLA_ROPE, 3 * HY_CH, N_BRANCH * D_MODEL)
W_IN_OFFS = tuple(sum(W_IN_WIDTHS[:j]) for j in range(len(W_IN_WIDTHS)))
W_MLA_COLS = MLA_Q_RANK + MLA_KV_RANK + LANE


def _prep_w_in_kernel(w_ref, qkv_ref, mla_ref, hy_ref, g_ref):
    o_q, o_k, o_v, o_cq, _, o_kr, o_hy, o_g = W_IN_OFFS
    win = lambda start: w_ref[:, start:start + LANE]
    lane = lax.broadcasted_iota(jnp.int32, (1, LANE), 1)
    lower = lane < DA_DQK
    for part, base in enumerate((o_q, o_k)):
        for h in range(DA_HEADS):
            m1 = win(base + h * DA_DQK)
            m2 = win(base + DA_HEADS * DA_DQK + (h - 1) * DA_DQK)
            dst = part * DA_W + h * LANE
            qkv_ref[:, dst:dst + LANE] = jnp.where(lower, m1, m2).astype(BF16)
    qkv_ref[:, 2 * DA_W:] = w_ref[:, o_v:o_v + DA_W].astype(BF16)
    mla_ref[:, :o_kr - o_cq] = w_ref[:, o_cq:o_kr].astype(BF16)
    mla_ref[:, o_kr - o_cq:] = jnp.where(lower, win(o_kr), 0.0).astype(BF16)
    for j in range(3 * HY_CH // LANE):
        hy_ref[:, j * LANE:(j + 1) * LANE] = win(o_hy + j * LANE).astype(BF16)
    for j in range(N_BRANCH * D_MODEL // LANE):
        g_ref[:, j * LANE:(j + 1) * LANE] = win(o_g + j * LANE).astype(BF16)


def _prep_w_in(w_in, layer):
    L, D, W = w_in.shape
    assert W == sum(W_IN_WIDTHS)
    rb = 256
    nr = D // rb
    widths = (3 * DA_W, W_MLA_COLS, 3 * HY_CH, N_BRANCH * D_MODEL)
    return pl.pallas_call(
        _prep_w_in_kernel,
        out_shape=tuple(jax.ShapeDtypeStruct((D, w), BF16) for w in widths),
        grid=(nr,),
        in_specs=[pl.BlockSpec((rb, W), lambda r: (layer * nr + r, 0))],
        out_specs=tuple(pl.BlockSpec((rb, w), lambda r: (r, 0)) for w in widths),
        compiler_params=_cparams("arbitrary"),
    )(w_in.reshape(L * D, W))


def _pad_w_uq(w):
    w = w.reshape(MLA_Q_RANK, MLA_HEADS, MLA_NOPE + MLA_ROPE)
    w = jnp.pad(w, ((0, 0), (0, 0), (0, MLA_QK_PAD - MLA_NOPE - MLA_ROPE)))
    return w.reshape(MLA_Q_RANK, MLA_HEADS * MLA_QK_PAD).astype(BF16)


def _split_w_ukv(w):
    w = w.reshape(MLA_KV_RANK, MLA_HEADS, 2, MLA_NOPE).transpose(0, 2, 1, 3)
    return w.reshape(MLA_KV_RANK, 2 * MLA_W).astype(BF16)


def _hyena(u3, layer, p, n, nc, with_ctx, tables):
    B = u3.shape[0]
    C = HY_CH
    P = B // 2
    g1d, g1f, g2, g2i, g3, gk, gd, gi = tables
    S = FFT_S
    N1 = 2 * n // S
    mlp = (p["hy_ffn_w1p"][layer], p["hy_ffn_b1"][layer][None], p["hy_ffn_w2"][layer], p["hy_ffn_b2"][layer][None],
           p["hy_ffn_w3"][layer], p["hy_ffn_b3"][layer][None], p["hy_ffn_w4"][layer], p["hy_freq"][layer][None])
    skip = p["hy_skip"][layer]
    cw, cb = p["hy_conv_w"][layer], p["hy_conv_b"][layer][None]

    kc, sums = _hy_filter(n, *mlp)
    kf = _fft_s1(g1f, kc.reshape(HY_ORDER, N1, S, C))
    kf = _fft_s2_filt(kf.reshape(HY_ORDER, 2, N1, S, C), g2, 1.0 / sums[:, 0:1, :])
    dw = [a.reshape(P, N1, S, C) for a in _hy_dwconv(u3, cw, cb, 0, n)]
    z = dw[0]
    for o in range(HY_ORDER):
        a = _fft_s1(g1d, z).reshape(P, 2, N1, S, C)
        b = _fft_s2(a, g2, g2i, kf, o).reshape(P, 2 * N1, S, C)
        z = _fft_s3(g3, b, z, dw[1 + o], skip[o][None], F32 if o + 1 < HY_ORDER else BF16)
    oc_lat = z.reshape(B, n, C)

    if with_ctx:
        kcc, sumc = _hy_filter(nc, *mlp)
        kfc = _dense_spec(kcc, gk, 1.0 / sumc[:, 0:1, :])
        dwc = [a.reshape(P, 2 * nc, C) for a in _hy_dwconv(u3, cw, cb, n // nc, nc)]
        zc = dwc[0]
        for o in range(HY_ORDER):
            zc = _dense_conv(zc, gd, gi, kfc, o, dwc[1 + o], skip[o][None], F32 if o + 1 < HY_ORDER else BF16)
        return jnp.concatenate([oc_lat, zc.reshape(B, nc, C)], axis=1)
    return oc_lat


def kernel(x, c, ctx, c_ctx, ada_w, ada_b, w_in, da_lambda, da_subln_g, mla_q_g, mla_w_uq, mla_kv_g, mla_w_ukv, hy_conv_w, hy_conv_b, hy_ffn_w1, hy_ffn_b1, hy_ffn_w2, hy_ffn_b2, hy_ffn_w3, hy_ffn_b3, hy_ffn_w4, hy_freq, hy_skip, w_branch_a, w_branch_b, w_branch_c, w_out, ln1_g, ln1_b, ffn_w_up, ffn_conv_w, ffn_conv_b, ffn_w_down, ln2_g, ln2_b):
    B, n, D = x.shape
    nc = ctx.shape[1]
    T = n + nc
    rows = B * T
    assert D == D_MODEL and B % 2 == 0 and B < 8
    assert n % ROW_GROUP == 0 and nc % ROW_GROUP == 0 and n % nc == 0 and n % GRID_W == 0
    assert (2 * n) % (8 * FFT_S) == 0 and T % ATTN_TK == 0 and nc % ATTN_TK == 0
    geom = (T // ROW_GROUP, n // ROW_GROUP, B)

    hy = dict(hy_ffn_w1p=jnp.pad(hy_ffn_w1, ((0, 0), (0, HY_FFN - HY_EMB), (0, 0))), hy_ffn_b1=hy_ffn_b1,
              hy_ffn_w2=hy_ffn_w2, hy_ffn_b2=hy_ffn_b2, hy_ffn_w3=hy_ffn_w3, hy_ffn_b3=hy_ffn_b3,
              hy_ffn_w4=hy_ffn_w4, hy_freq=hy_freq, hy_skip=hy_skip, hy_conv_w=hy_conv_w, hy_conv_b=hy_conv_b)
    tables = _dft_tables(n) + _dense_tables(nc)
    rope = _rope_tables(B, n, nc)
    keep_prev, keep_next = _conv_masks(B, n, nc)

    cc = jnp.concatenate([c, c_ctx[None], jnp.zeros((8 - B - 1, D), F32)], axis=0)
    mods = _ada(cc, ada_w, ada_b[:, None, :]).reshape(DEPTH, 8, 6, D)

    xs = jnp.concatenate([x, ctx], axis=1).reshape(rows, D)
    for i in range(DEPTH):
        last = i == DEPTH - 1
        lam_init = 0.8 - 0.6 * math.exp(-0.3 * i)
        lq1, lk1, lq2, lk2 = da_lambda[i].astype(F32)
        lam = (jnp.exp(jnp.sum(lq1 * lk1)) - jnp.exp(jnp.sum(lq2 * lk2)) + lam_init).reshape(1)
        mod = mods[i]
        w_qkv, w_mla, w_hy, w_g = _prep_w_in(w_in, i)

        if last:
            nb, tb = B, n
            sel = lambda b, g: b
        else:
            nb, tb = 1, rows
            sel = lambda b, g: _mod_row(g, *geom)

        h = _lnmod(xs, mod, geom)
        qkv = _qkv_proj(h, w_qkv, rope)
        p_mla = _matmul(h, w_mla, F32, w_mla.shape[1])
        u_hy = _matmul(h, w_hy, F32, 768, nb=nb, tb=tb)
        gates = _matmul(h, w_g, BF16, 1024, act="sigmoid", nb=nb, tb=tb)

        q_m, k_m, v_m = _mla_prep(p_mla, mla_q_g[i][None], mla_kv_g[i][None], _pad_w_uq(mla_w_uq[i]),
                                  _split_w_ukv(mla_w_ukv[i]), rope)
        oa = _diff_attn(qkv.reshape(B, T, 3 * DA_W), lam, da_subln_g[i][None], n, nc, not last, lam_init)
        ob = _mla_attn(q_m.reshape(B, T, -1), k_m.reshape(B, T, -1), v_m.reshape(B, T, -1), n, nc, not last)
        oc = _hyena(u_hy.reshape(B, -1, 3 * HY_CH), i, hy, n, nc, not last, tables)

        merged = _merge(oa.reshape(-1, DA_W), ob.reshape(-1, MLA_W), oc.reshape(-1, HY_CH),
                        w_branch_a[i].astype(BF16), w_branch_b[i].astype(BF16), w_branch_c[i].astype(BF16), gates,
                        nb, tb)
        xs1, h2 = _wo_ln(merged, w_out[i].astype(BF16), xs, mod, ln1_g[i][None], ln1_b[i][None], sel, nb, tb)
        u = _ffn_up(h2, ffn_w_up, i, ffn_conv_w[i], ffn_conv_b[i][None], keep_prev, keep_next, nb, tb)
        xs = _ffn_down(u, ffn_w_down[i].astype(BF16), xs1, mod, ln2_g[i][None], ln2_b[i][None], sel, nb, tb)
    return xs.reshape(B, n, D)
```

```python
import functools
import math

import numpy as np
import jax
import jax.numpy as jnp
from jax import lax
from jax.experimental import pallas as pl
from jax.experimental.pallas import tpu as pltpu

F32 = jnp.float32
BF16 = jnp.bfloat16

D_MODEL = 2048
DEPTH = 2
GRID_W = 64
ROPE_BASE = 10000.0
NORM_EPS = 1e-6
DA_HEADS = 6
DA_DQK = 64
DA_DV = 128
DA_W = DA_HEADS * DA_DV
MLA_HEADS = 6
MLA_Q_RANK = 512
MLA_KV_RANK = 256
MLA_NOPE = 128
MLA_ROPE = 64
MLA_DV = 128
MLA_W = MLA_HEADS * MLA_DV
MLA_QK_PAD = 256
HY_CH = 512
HY_ORDER = 2
HY_EMB = 33
HY_BANDS = (HY_EMB - 1) // 2
HY_FFN = 64
HY_MIN_DECAY = math.log(1e-2) / 1.5
HY_MAX_DECAY = math.log(1e-2) / 0.3
D_FF = 5632
N_BRANCH = 3
DEEPNORM_ALPHA = (2 * DEPTH) ** 0.25
LOG2E = 1.4426950408889634

ROW_GROUP = 256
LANE = 128
FFT_S = 128
VMEM_LIMIT = 52 * 1024 * 1024


def _cparams(*sem):
    return pltpu.CompilerParams(dimension_semantics=sem, vmem_limit_bytes=VMEM_LIMIT)


def _pick(total, prefs):
    for p in prefs:
        if total % p == 0:
            return p
    raise ValueError(f"no tile for {total} in {prefs}")


def _ln(x):
    mu = jnp.mean(x, axis=-1, keepdims=True)
    xc = x - mu
    var = jnp.mean(xc * xc, axis=-1, keepdims=True)
    return xc * lax.rsqrt(var + NORM_EPS)


def _rms(x):
    return x * lax.rsqrt(jnp.mean(x * x, axis=-1, keepdims=True) + NORM_EPS)


def _rope128(u, cos, sa, sb):
    return u * cos + pltpu.roll(u, 16, 1) * sa + pltpu.roll(u, LANE - 16, 1) * sb


def _ada_kernel(c_ref, w_ref, b_ref, o_ref):
    a = c_ref[...]
    a = a * jax.nn.sigmoid(a)
    o_ref[...] = jnp.dot(a.astype(BF16), w_ref[...].astype(BF16), preferred_element_type=F32) + b_ref[...]


def _ada(cc, ada_w, ada_b):
    L, D, N = ada_w.shape
    tn = 1024
    return pl.pallas_call(
        _ada_kernel,
        out_shape=jax.ShapeDtypeStruct((L, 8, N), F32),
        grid=(L, N // tn),
        in_specs=[pl.BlockSpec((8, D), lambda l, j: (0, 0)),
                  pl.BlockSpec((None, D, tn), lambda l, j: (l, 0, j)),
                  pl.BlockSpec((None, 1, tn), lambda l, j: (l, 0, j))],
        out_specs=pl.BlockSpec((None, 8, tn), lambda l, j: (l, 0, j)),
        compiler_params=_cparams("arbitrary", "arbitrary"),
    )(cc, ada_w, ada_b)


def _lnmod_kernel(x_ref, m_ref, o_ref):
    y = _ln(x_ref[...])
    o_ref[...] = (y * (1.0 + m_ref[1:2, :]) + m_ref[0:1, :]).astype(BF16)


def _mod_row(g, gpb, lat_groups, n_batch):
    return jnp.where(g % gpb < lat_groups, g // gpb, n_batch)


def _lnmod(xs, mod, geom):
    rows, D = xs.shape
    gpb, lat_groups, n_batch = geom
    return pl.pallas_call(
        _lnmod_kernel,
        out_shape=jax.ShapeDtypeStruct((rows, D), BF16),
        grid=(rows // ROW_GROUP,),
        in_specs=[pl.BlockSpec((ROW_GROUP, D), lambda i: (i, 0)),
                  pl.BlockSpec((None, 6, D), lambda i: (_mod_row(i, gpb, lat_groups, n_batch), 0, 0))],
        out_specs=pl.BlockSpec((ROW_GROUP, D), lambda i: (i, 0)),
        compiler_params=_cparams("arbitrary"),
    )(xs, mod)


def _mm_kernel(a_ref, w_ref, o_ref, *, act):
    acc = jnp.dot(a_ref[...], w_ref[...], preferred_element_type=F32)
    if act == "sigmoid":
        acc = jax.nn.sigmoid(acc)
    o_ref[...] = acc.astype(o_ref.dtype)


def _by_batch(a, nb):
    return a.reshape(nb, a.shape[0] // nb, a.shape[1])


def _matmul(a, w, out_dtype, tn, act=None, nb=1, tb=None):
    a = _by_batch(a, nb)
    K = a.shape[2]
    tb = tb or a.shape[1]
    N = w.shape[1]
    tm = _pick(tb, (1024, 512, 256))
    return pl.pallas_call(
        functools.partial(_mm_kernel, act=act),
        out_shape=jax.ShapeDtypeStruct((nb, tb, N), out_dtype),
        grid=(nb, tb // tm, N // tn),
        in_specs=[pl.BlockSpec((None, tm, K), lambda b, i, j: (b, i, 0)),
                  pl.BlockSpec((K, tn), lambda b, i, j: (0, j))],
        out_specs=pl.BlockSpec((None, tm, tn), lambda b, i, j: (b, i, j)),
        compiler_params=_cparams("arbitrary", "arbitrary", "arbitrary"),
    )(a, w).reshape(nb * tb, N)


def _qkv_kernel(h_ref, w_ref, cos_ref, sa_ref, sb_ref, o_ref, *, qscale):
    j = pl.program_id(1)
    acc = jnp.dot(h_ref[...], w_ref[...], preferred_element_type=F32)

    @pl.when(j < 2)
    def _():
        cos, sa, sb = cos_ref[...], sa_ref[...], sb_ref[...]
        scale = jnp.where(j == 0, qscale, 1.0).astype(F32)
        for c in range(DA_HEADS):
            u = acc[:, c * LANE:(c + 1) * LANE]
            o_ref[:, c * LANE:(c + 1) * LANE] = (_rope128(u, cos, sa, sb) * scale).astype(BF16)

    @pl.when(j == 2)
    def _():
        o_ref[...] = acc.astype(BF16)


def _qkv_proj(h, w_qkv, tabs):
    M, K = h.shape
    tm = _pick(M, (1024, 512, 256))
    tn = DA_W
    tab_spec = pl.BlockSpec((tm, LANE), lambda i, j: (i, 0))
    return pl.pallas_call(
        functools.partial(_qkv_kernel, qscale=DA_DQK ** -0.5 * LOG2E),
        out_shape=jax.ShapeDtypeStruct((M, 3 * DA_W), BF16),
        grid=(M // tm, 3),
        in_specs=[pl.BlockSpec((tm, K), lambda i, j: (i, 0)),
                  pl.BlockSpec((K, tn), lambda i, j: (0, j)),
                  tab_spec, tab_spec, tab_spec],
        out_specs=pl.BlockSpec((tm, tn), lambda i, j: (i, j)),
        compiler_params=_cparams("arbitrary", "arbitrary"),
    )(h, w_qkv, *tabs)


def _mla_prep_kernel(p_ref, qg_ref, kvg_ref, wuq_ref, wukv_ref, cos_ref, sa_ref, sb_ref,
                     q_ref, k_ref, v_ref, *, qscale):
    p = p_ref[...]
    cos, sa, sb = cos_ref[...], sa_ref[...], sb_ref[...]
    cq = p[:, :MLA_Q_RANK]
    ckv = p[:, MLA_Q_RANK:MLA_Q_RANK + MLA_KV_RANK]
    kr = p[:, MLA_Q_RANK + MLA_KV_RANK:]
    qn = (_rms(cq) * qg_ref[...]).astype(BF16)
    q = jnp.dot(qn, wuq_ref[...], preferred_element_type=F32)
    kvn = (_rms(ckv) * kvg_ref[...]).astype(BF16)
    kv = jnp.dot(kvn, wukv_ref[...], preferred_element_type=F32)
    krr = _rope128(kr, cos, sa, sb).astype(BF16)
    for h in range(MLA_HEADS):
        o = h * MLA_QK_PAD
        q_ref[:, o:o + LANE] = (q[:, o:o + LANE] * qscale).astype(BF16)
        q_ref[:, o + LANE:o + 2 * LANE] = (_rope128(q[:, o + LANE:o + 2 * LANE], cos, sa, sb) * qscale).astype(BF16)
        k_ref[:, o:o + LANE] = kv[:, h * LANE:(h + 1) * LANE].astype(BF16)
        k_ref[:, o + LANE:o + 2 * LANE] = krr
    v_ref[...] = kv[:, MLA_W:].astype(BF16)


def _mla_prep(p, q_g, kv_g, w_uq, w_ukv, tabs):
    M, W = p.shape
    tm = _pick(M, (512, 256))
    row = lambda i: (i, 0)
    full = lambda i: (0, 0)
    qk_w = MLA_HEADS * MLA_QK_PAD
    return pl.pallas_call(
        functools.partial(_mla_prep_kernel, qscale=(MLA_NOPE + MLA_ROPE) ** -0.5 * LOG2E),
        out_shape=(jax.ShapeDtypeStruct((M, qk_w), BF16),
                   jax.ShapeDtypeStruct((M, qk_w), BF16),
                   jax.ShapeDtypeStruct((M, MLA_W), BF16)),
        grid=(M // tm,),
        in_specs=[pl.BlockSpec((tm, W), row),
                  pl.BlockSpec((1, MLA_Q_RANK), full),
                  pl.BlockSpec((1, MLA_KV_RANK), full),
                  pl.BlockSpec(w_uq.shape, full),
                  pl.BlockSpec(w_ukv.shape, full),
                  pl.BlockSpec((tm, LANE), row), pl.BlockSpec((tm, LANE), row), pl.BlockSpec((tm, LANE), row)],
        out_specs=(pl.BlockSpec((tm, qk_w), row), pl.BlockSpec((tm, qk_w), row), pl.BlockSpec((tm, MLA_W), row)),
        compiler_params=_cparams("arbitrary"),
    )(p, q_g, kv_g, w_uq, w_ukv, *tabs)


_NT = (((1,), (1,)), ((), ()))
ATTN_TQ = 256
ATTN_CHAINS = 8
ATTN_TK = 256


def _skewed_pipeline(n_chains, n_chunks, stages):
    for t in range(n_chains + len(stages) - 1):
        active = [(s, t - s) for s in range(len(stages)) if 0 <= t - s < n_chains]
        for c in range(n_chunks):
            for s, chain in active:
                stages[s][0](chain, c)
        for s, chain in active:
            stages[s][1](chain)


def _acc(old, new, op):
    return new if old is None else op(old, new)


def _lane_halves(x, op):
    return op(x[:, :LANE], x[:, LANE:])


def _diff_attn_kernel(lam_ref, q_ref, k_ref, v_ref, g_ref, *rest, out_scale):
    o_ref = rest[-1]
    lam, g = lam_ref[0], g_ref[...]
    tq = min(ATTN_TQ, q_ref.shape[0])
    n_chains = q_ref.shape[0] // tq
    n_chunks = k_ref.shape[0] // ATTN_TK
    lane = lax.broadcasted_iota(jnp.int32, (1, LANE), 1)
    lo = (lane < DA_DQK).astype(F32)
    st = [dict(s=[], e=[], mx=[None, None], l=[None, None], o=None) for _ in range(n_chains)]
    rows = lambda i: slice(i * tq, (i + 1) * tq)
    keys = lambda c: slice(c * ATTN_TK, (c + 1) * ATTN_TK)

    def qk_chunk(i, c):
        d = st[i]
        if c == 0:
            qf = q_ref[rows(i), :].astype(F32)
            d["q"] = jnp.concatenate([(qf * lo).astype(BF16), (qf * (1.0 - lo)).astype(BF16)], axis=0)
        both = lax.dot_general(d["q"], k_ref[keys(c), :], _NT, preferred_element_type=F32)
        pair = []
        for m in range(2):
            s = both[m * tq:(m + 1) * tq]
            d["mx"][m] = _acc(d["mx"][m], _lane_halves(s, jnp.maximum), jnp.maximum)
            pair.append(s)
        d["s"].append(pair)

    def qk_done(i):
        st[i]["m"] = [jnp.max(mx, axis=-1, keepdims=True) for mx in st[i]["mx"]]

    def exp_chunk(i, c):
        d = st[i]
        pair = []
        for m in range(2):
            e = jnp.exp2(d["s"][c][m] - d["m"][m])
            d["l"][m] = _acc(d["l"][m], _lane_halves(e, jnp.add), jnp.add)
            pair.append(e)
        d["s"][c] = None
        d["e"].append(pair)

    def exp_done(i):
        d = st[i]
        l1, l2 = [jnp.sum(l, axis=-1, keepdims=True) for l in d["l"]]
        d["r1"] = 1.0 / l1
        d["cf"] = lam * l1 / l2

    def pv_chunk(i, c):
        d = st[i]
        w = (d["e"][c][0] - d["cf"] * d["e"][c][1]).astype(BF16)
        d["e"][c] = None
        d["o"] = _acc(d["o"], jnp.dot(w, v_ref[keys(c), :], preferred_element_type=F32), jnp.add)

    def pv_done(i):
        o = st[i]["o"] * st[i]["r1"]
        o_ref[rows(i), :] = (_rms(o) * g * out_scale).astype(BF16)

    _skewed_pipeline(n_chains, n_chunks, [(qk_chunk, qk_done), (exp_chunk, exp_done), (pv_chunk, pv_done)])


def _mla_attn_kernel(q_ref, k_ref, v_ref, *rest):
    o_ref = rest[-1]
    tq = min(ATTN_TQ, q_ref.shape[0])
    n_chains = q_ref.shape[0] // tq
    n_chunks = k_ref.shape[0] // ATTN_TK
    st = [dict(s=[], mx=None, l=None, o=None) for _ in range(n_chains)]
    rows = lambda i: slice(i * tq, (i + 1) * tq)
    keys = lambda c: slice(c * ATTN_TK, (c + 1) * ATTN_TK)

    def qk_chunk(i, c):
        d = st[i]
        s = lax.dot_general(q_ref[rows(i), :], k_ref[keys(c), :], _NT, preferred_element_type=F32)
        d["mx"] = _acc(d["mx"], _lane_halves(s, jnp.maximum), jnp.maximum)
        d["s"].append(s)

    def qk_done(i):
        st[i]["m"] = jnp.max(st[i]["mx"], axis=-1, keepdims=True)

    def pv_chunk(i, c):
        d = st[i]
        e = jnp.exp2(d["s"][c] - d["m"])
        d["s"][c] = None
        d["l"] = _acc(d["l"], _lane_halves(e, jnp.add), jnp.add)
        d["o"] = _acc(d["o"], jnp.dot(e.astype(BF16), v_ref[keys(c), :], preferred_element_type=F32), jnp.add)

    def pv_done(i):
        d = st[i]
        o_ref[rows(i), :] = (d["o"] * (1.0 / jnp.sum(d["l"], axis=-1, keepdims=True))).astype(BF16)

    _skewed_pipeline(n_chains, n_chunks, [(qk_chunk, qk_done), (pv_chunk, pv_done)])


def _diff_attn(qkv, lam, subln_g, n, nc, ctx_queries, lam_init):
    B, T, _ = qkv.shape
    H = DA_HEADS
    tq = _pick(n, (ATTN_TQ * ATTN_CHAINS, ATTN_TQ))
    kern = functools.partial(_diff_attn_kernel, out_scale=1.0 - lam_init)
    smem = pl.BlockSpec(memory_space=pltpu.SMEM)
    gspec = pl.BlockSpec((1, DA_DV), lambda b, h, i: (0, 0))
    oa = pl.pallas_call(
        kern,
        out_shape=jax.ShapeDtypeStruct((B, T if ctx_queries else n, DA_W), BF16),
        grid=(B, H, n // tq),
        in_specs=[smem,
                  pl.BlockSpec((None, tq, LANE), lambda b, h, i: (b, i, h)),
                  pl.BlockSpec((None, T, LANE), lambda b, h, i: (b, 0, H + h)),
                  pl.BlockSpec((None, T, LANE), lambda b, h, i: (b, 0, 2 * H + h)),
                  gspec],
        out_specs=pl.BlockSpec((None, tq, LANE), lambda b, h, i: (b, i, h)),
        compiler_params=_cparams("arbitrary", "arbitrary", "arbitrary"),
    )(lam, qkv, qkv, qkv, subln_g)
    if not ctx_queries:
        return oa
    cb = n // nc
    return pl.pallas_call(
        kern,
        out_shape=jax.ShapeDtypeStruct((B, T, DA_W), BF16),
        grid=(B, H, 1),
        in_specs=[smem,
                  pl.BlockSpec((None, nc, LANE), lambda b, h, i: (b, cb, h)),
                  pl.BlockSpec((None, nc, LANE), lambda b, h, i: (b, cb, H + h)),
                  pl.BlockSpec((None, nc, LANE), lambda b, h, i: (b, cb, 2 * H + h)),
                  gspec,
                  pl.BlockSpec(memory_space=pl.ANY)],
        out_specs=pl.BlockSpec((None, nc, LANE), lambda b, h, i: (b, cb, h)),
        input_output_aliases={5: 0},
        compiler_params=_cparams("arbitrary", "arbitrary", "arbitrary"),
    )(lam, qkv, qkv, qkv, subln_g, oa)


def _mla_attn(q, k, v, n, nc, ctx_queries):
    B, T, _ = q.shape
    H = MLA_HEADS
    tq = _pick(n, (ATTN_TQ * ATTN_CHAINS, ATTN_TQ))
    ob = pl.pallas_call(
        _mla_attn_kernel,
        out_shape=jax.ShapeDtypeStruct((B, T if ctx_queries else n, MLA_W), BF16),
        grid=(B, H, n // tq),
        in_specs=[pl.BlockSpec((None, tq, MLA_QK_PAD), lambda b, h, i: (b, i, h)),
                  pl.BlockSpec((None, T, MLA_QK_PAD), lambda b, h, i: (b, 0, h)),
                  pl.BlockSpec((None, T, MLA_DV), lambda b, h, i: (b, 0, h))],
        out_specs=pl.BlockSpec((None, tq, MLA_DV), lambda b, h, i: (b, i, h)),
        compiler_params=_cparams("arbitrary", "arbitrary", "arbitrary"),
    )(q, k, v)
    if not ctx_queries:
        return ob
    cb = n // nc
    return pl.pallas_call(
        _mla_attn_kernel,
        out_shape=jax.ShapeDtypeStruct((B, T, MLA_W), BF16),
        grid=(B, H, 1),
        in_specs=[pl.BlockSpec((None, nc, MLA_QK_PAD), lambda b, h, i: (b, cb, h)),
                  pl.BlockSpec((None, nc, MLA_QK_PAD), lambda b, h, i: (b, cb, h)),
                  pl.BlockSpec((None, nc, MLA_DV), lambda b, h, i: (b, cb, h)),
                  pl.BlockSpec(memory_space=pl.ANY)],
        out_specs=pl.BlockSpec((None, nc, MLA_DV), lambda b, h, i: (b, cb, h)),
        input_output_aliases={3: 0},
        compiler_params=_cparams("arbitrary", "arbitrary", "arbitrary"),
    )(q, k, v, ob)


def _hy_filter_kernel(feat_ref, dec_ref, w1_ref, b1_ref, w2_ref, b2_ref, w3_ref, b3_ref, w4_ref, fr_ref,
                      k_ref, s_ref):
    hp = lax.Precision.HIGHEST
    d, r = pl.program_id(0), pl.program_id(1)
    fr = fr_ref[...]
    h = jnp.sin(fr * (jnp.dot(feat_ref[...], w1_ref[...], precision=hp, preferred_element_type=F32) + b1_ref[...]))
    h = jnp.sin(fr * (jnp.dot(h, w2_ref[...], precision=hp, preferred_element_type=F32) + b2_ref[...]))
    h = jnp.sin(fr * (jnp.dot(h, w3_ref[...], precision=hp, preferred_element_type=F32) + b3_ref[...]))
    h = jnp.dot(h, w4_ref[...], precision=hp, preferred_element_type=F32)
    row = lax.broadcasted_iota(jnp.int32, (h.shape[0], 1), 0)
    first_bwd = jnp.where((d == 1) & (r == 0), 1.0, 0.0)
    scale = dec_ref[...] * (1.0 - jnp.where(row == 0, 1.0, 0.0) * first_bwd)

    @pl.when((d == 0) & (r == 0))
    def _():
        s_ref[...] = jnp.zeros(s_ref.shape, F32)

    for o in range(HY_ORDER):
        ko = h[:, o * HY_CH:(o + 1) * HY_CH] * scale
        k_ref[o] = ko
        s_ref[o] += jnp.broadcast_to(jnp.sum(jnp.abs(ko), axis=0, keepdims=True), (8, HY_CH))


def _hy_filter(n, w1, b1, w2, b2, w3, b3, w4, fr):
    C = HY_CH
    t = jnp.linspace(0.0, 1.0, n, dtype=F32)
    pos = jnp.arange(n, dtype=F32)
    t2 = jnp.concatenate([t, t[::-1]])[:, None]
    pos2 = jnp.concatenate([pos, pos[::-1]])[:, None]
    phase = (2.0 * math.pi / n) * pos2 * jnp.linspace(1e-4, HY_BANDS - 1, HY_BANDS, dtype=F32)[None, :]
    feat = jnp.concatenate([t2, jnp.cos(phase), -jnp.sin(phase)], axis=-1)
    feat = jnp.pad(feat, ((0, 0), (0, HY_FFN - HY_EMB)))
    dec = jnp.exp(-t2 * jnp.abs(jnp.linspace(HY_MIN_DECAY, HY_MAX_DECAY, C, dtype=F32)))
    w4d = w4.reshape(HY_FFN, HY_ORDER, 2, C).transpose(2, 0, 1, 3).reshape(2, HY_FFN, HY_ORDER * C)
    rb = min(512, n)
    nb = n // rb
    full = lambda d, r: (0, 0)
    return pl.pallas_call(
        _hy_filter_kernel,
        out_shape=(jax.ShapeDtypeStruct((HY_ORDER, 2 * n, C), F32),
                   jax.ShapeDtypeStruct((HY_ORDER, 8, C), F32)),
        grid=(2, nb),
        in_specs=[pl.BlockSpec((rb, HY_FFN), lambda d, r: (d * nb + r, 0)),
                  pl.BlockSpec((rb, C), lambda d, r: (d * nb + r, 0)),
                  pl.BlockSpec(w1.shape, full), pl.BlockSpec(b1.shape, full),
                  pl.BlockSpec(w2.shape, full), pl.BlockSpec(b2.shape, full),
                  pl.BlockSpec(w3.shape, full), pl.BlockSpec(b3.shape, full),
                  pl.BlockSpec((None, HY_FFN, HY_ORDER * C), lambda d, r: (d, 0, 0)),
                  pl.BlockSpec(fr.shape, full)],
        out_specs=(pl.BlockSpec((HY_ORDER, rb, C), lambda d, r: (0, d * nb + r, 0)),
                   pl.BlockSpec((HY_ORDER, 8, C), lambda d, r: (0, 0, 0))),
        compiler_params=_cparams("arbitrary", "arbitrary"),
    )(feat, dec, w1, b1, w2, b2, w3, b3, w4d, fr)


def _hy_dwconv_kernel(*refs):
    for k in range(3):
        u_ref, w_ref, b_ref, o_ref = refs[3 * k], refs[3 * k + 1], refs[3 * k + 2], refs[9 + k]
        u = u_ref[...]
        L = u.shape[0]
        row = lax.broadcasted_iota(jnp.int32, (L, 1), 0)
        up = jnp.where(row == 0, 0.0, pltpu.roll(u, 1, 0))
        dn = jnp.where(row == L - 1, 0.0, pltpu.roll(u, L - 1, 0))
        o_ref[...] = up * w_ref[0:1, :] + u * w_ref[1:2, :] + dn * w_ref[2:3, :] + b_ref[...]


def _hy_dwconv(u, w, b, row_block, length):
    B = u.shape[0]
    cw = LANE
    per = HY_CH // cw
    in_specs, args = [], []
    for k in range(3):
        in_specs += [pl.BlockSpec((None, length, cw), lambda bb, j, k=k: (bb, row_block, k * per + j)),
                     pl.BlockSpec((3, cw), lambda bb, j, k=k: (0, k * per + j)),
                     pl.BlockSpec((1, cw), lambda bb, j, k=k: (0, k * per + j))]
        args += [u, w, b]
    ospec = pl.BlockSpec((None, length, cw), lambda bb, j: (bb, 0, j))
    return pl.pallas_call(
        _hy_dwconv_kernel,
        out_shape=(jax.ShapeDtypeStruct((B, length, HY_CH), F32),) * 3,
        grid=(B, per),
        in_specs=in_specs,
        out_specs=(ospec,) * 3,
        compiler_params=_cparams("arbitrary", "arbitrary"),
    )(*args)


def _dft_tables(n):
    S = FFT_S
    M = 2 * n
    N1 = M // S
    H = N1 // 2
    s2 = np.arange(S)[:, None, None]
    k1 = np.arange(N1)[None, :, None]
    s1 = np.arange(N1)[None, None, :]
    ang = -2.0 * np.pi * ((k1 * (S * s1 + s2)) % M) / M
    fr, fi = np.cos(ang), np.sin(ang)
    g1f = np.concatenate([fr, fi], axis=1)
    frh, fih = fr[:, :, :H], fi[:, :, :H]
    g1d = np.concatenate([np.concatenate([frh, -fih], axis=2),
                          np.concatenate([fih, frh], axis=2)], axis=1)
    er = np.transpose(frh, (0, 2, 1)) / M
    ei = -np.transpose(fih, (0, 2, 1)) / M
    g3 = np.concatenate([np.concatenate([er, -ei], axis=2),
                         np.concatenate([ei, er], axis=2)], axis=1)
    a2 = -2.0 * np.pi * ((np.arange(S)[:, None] * np.arange(S)[None, :]) % S) / S
    f2r, f2i = np.cos(a2), np.sin(a2)
    g2 = np.block([[f2r, -f2i], [f2i, f2r]])
    g2i = np.block([[f2r, f2i], [-f2i, f2r]])
    cast = lambda a: jnp.asarray(a, dtype=F32).astype(BF16)
    return cast(g1d), cast(g1f), cast(g2), cast(g2i), cast(g3)


FFT_NS = 16


def _fft_s1_kernel(g_ref, x_ref, o_ref):
    xt = jnp.swapaxes(x_ref[...], 0, 1)
    y = jnp.stack([jnp.dot(g_ref[j], xt[j].astype(BF16), preferred_element_type=F32)
                   for j in range(xt.shape[0])], axis=0)
    o_ref[...] = jnp.swapaxes(y, 0, 1).astype(o_ref.dtype)


def _fft_s1(g, x):
    P, Ri, S, C = x.shape
    Ro = g.shape[1]
    ns = FFT_NS
    return pl.pallas_call(
        _fft_s1_kernel,
        out_shape=jax.ShapeDtypeStruct((P, Ro, S, C), BF16),
        grid=(S // ns, P),
        in_specs=[pl.BlockSpec((ns, Ro, Ri), lambda j, p: (j, 0, 0)),
                  pl.BlockSpec((None, Ri, ns, C), lambda j, p: (p, 0, j, 0))],
        out_specs=pl.BlockSpec((None, Ro, ns, C), lambda j, p: (p, 0, j, 0)),
        compiler_params=_cparams("arbitrary", "arbitrary"),
    )(g, x)


def _fft_s2_filt_kernel(a_ref, g_ref, rn_ref, o_ref, *, nk):
    S = FFT_S
    rn = rn_ref[...]
    for t in range(nk):
        d = jnp.concatenate([a_ref[0, t], a_ref[1, t]], axis=0)
        y = jnp.dot(g_ref[...], d, preferred_element_type=F32)
        o_ref[0, t] = y[:S] * rn
        o_ref[1, t] = y[S:] * rn


def _fft_s2_filt(a, g2, rnorm):
    O, _, N1, S, C = a.shape
    nk = 4
    blk = (None, 2, nk, S, C)
    return pl.pallas_call(
        functools.partial(_fft_s2_filt_kernel, nk=nk),
        out_shape=jax.ShapeDtypeStruct(a.shape, F32),
        grid=(O, N1 // nk),
        in_specs=[pl.BlockSpec(blk, lambda o, j: (o, 0, j, 0, 0)),
                  pl.BlockSpec(g2.shape, lambda o, j: (0, 0)),
                  pl.BlockSpec((None, 1, C), lambda o, j: (o, 0, 0))],
        out_specs=pl.BlockSpec(blk, lambda o, j: (o, 0, j, 0, 0)),
        compiler_params=_cparams("arbitrary", "arbitrary"),
    )(a, g2, rnorm)


def _fft_s2_kernel(a_ref, g_ref, gi_ref, kf_ref, o_ref, *, nk):
    S = FFT_S
    for t in range(nk):
        d = jnp.concatenate([a_ref[0, t], a_ref[1, t]], axis=0)
        y = jnp.dot(g_ref[...], d, preferred_element_type=F32)
        yr, yi = y[:S], y[S:]
        kr, ki = kf_ref[0, t], kf_ref[1, t]
        p = jnp.concatenate([yr * kr - yi * ki, yr * ki + yi * kr], axis=0).astype(BF16)
        b = jnp.dot(gi_ref[...], p, preferred_element_type=F32)
        o_ref[0, t] = b[:S].astype(BF16)
        o_ref[1, t] = b[S:].astype(BF16)


def _fft_s2(a, g2, g2i, kf, order):
    P, _, N1, S, C = a.shape
    nk = 4
    blk = (None, 2, nk, S, C)
    return pl.pallas_call(
        functools.partial(_fft_s2_kernel, nk=nk),
        out_shape=jax.ShapeDtypeStruct(a.shape, BF16),
        grid=(N1 // nk, P),
        in_specs=[pl.BlockSpec(blk, lambda j, p: (p, 0, j, 0, 0)),
                  pl.BlockSpec(g2.shape, lambda j, p: (0, 0)),
                  pl.BlockSpec(g2i.shape, lambda j, p: (0, 0)),
                  pl.BlockSpec(blk, lambda j, p: (order, 0, j, 0, 0))],
        out_specs=pl.BlockSpec(blk, lambda j, p: (p, 0, j, 0, 0)),
        compiler_params=_cparams("arbitrary", "arbitrary"),
    )(a, g2, g2i, kf)


def _fft_s3_kernel(g_ref, b_ref, z_ref, gate_ref, skip_ref, o_ref):
    bt = jnp.swapaxes(b_ref[...].astype(F32), 0, 1).astype(BF16)
    y = jnp.stack([jnp.dot(g_ref[j], bt[j], preferred_element_type=F32) for j in range(bt.shape[0])], axis=0)
    y = jnp.swapaxes(y, 0, 1)
    o_ref[...] = (gate_ref[...] * (y + skip_ref[...] * z_ref[...])).astype(o_ref.dtype)


def _fft_s3(g3, b, z, gate, skip, out_dtype):
    P, Ri, S, C = b.shape
    Ro = g3.shape[1]
    ns = FFT_NS
    dspec = pl.BlockSpec((None, Ro, ns, C), lambda j, p: (p, 0, j, 0))
    return pl.pallas_call(
        _fft_s3_kernel,
        out_shape=jax.ShapeDtypeStruct((P, Ro, S, C), out_dtype),
        grid=(S // ns, P),
        in_specs=[pl.BlockSpec((ns, Ro, Ri), lambda j, p: (j, 0, 0)),
                  pl.BlockSpec((None, Ri, ns, C), lambda j, p: (p, 0, j, 0)),
                  dspec, dspec,
                  pl.BlockSpec((1, C), lambda j, p: (0, 0))],
        out_specs=dspec,
        compiler_params=_cparams("arbitrary", "arbitrary"),
    )(g3, b, z, gate, skip)


def _dense_tables(m):
    M = 2 * m
    ang = -2.0 * np.pi * ((np.arange(M)[:, None] * np.arange(M)[None, :]) % M) / M
    fr, fi = np.cos(ang), np.sin(ang)
    gk = np.concatenate([fr, fi], axis=0)
    gd = np.block([[fr[:, :m], -fi[:, :m]], [fi[:, :m], fr[:, :m]]])
    er, ei = fr[:m, :] / M, -fi[:m, :] / M
    gi = np.block([[er, -ei], [ei, er]])
    cast = lambda a: jnp.asarray(a, dtype=F32).astype(BF16)
    return cast(gk), cast(gd), cast(gi)


def _dense_spec_kernel(k_ref, g_ref, rn_ref, o_ref):
    o_ref[...] = jnp.dot(g_ref[...], k_ref[...].astype(BF16), preferred_element_type=F32) * rn_ref[...]


def _dense_spec(kc, gk, rnorm):
    O, M, C = kc.shape
    return pl.pallas_call(
        _dense_spec_kernel,
        out_shape=jax.ShapeDtypeStruct((O, 2 * M, C), F32),
        grid=(O,),
        in_specs=[pl.BlockSpec((None, M, C), lambda o: (o, 0, 0)),
                  pl.BlockSpec(gk.shape, lambda o: (0, 0)),
                  pl.BlockSpec((None, 1, C), lambda o: (o, 0, 0))],
        out_specs=pl.BlockSpec((None, 2 * M, C), lambda o: (o, 0, 0)),
        compiler_params=_cparams("arbitrary"),
    )(kc, gk, rnorm)


def _dense_conv_kernel(x_ref, gd_ref, gi_ref, kf_ref, gate_ref, skip_ref, o_ref):
    x = x_ref[...]
    y = jnp.dot(gd_ref[...], x.astype(BF16), preferred_element_type=F32)
    M = y.shape[0] // 2
    yr, yi = y[:M], y[M:]
    kr, ki = kf_ref[:M], kf_ref[M:]
    p = jnp.concatenate([yr * kr - yi * ki, yr * ki + yi * kr], axis=0).astype(BF16)
    conv = jnp.dot(gi_ref[...], p, preferred_element_type=F32)
    o_ref[...] = (gate_ref[...] * (conv + skip_ref[...] * x)).astype(o_ref.dtype)


def _dense_conv(x, gd, gi, kf, order, gate, skip, out_dtype):
    P, R, C = x.shape
    dspec = pl.BlockSpec((None, R, C), lambda p: (p, 0, 0))
    return pl.pallas_call(
        _dense_conv_kernel,
        out_shape=jax.ShapeDtypeStruct((P, R, C), out_dtype),
        grid=(P,),
        in_specs=[dspec,
                  pl.BlockSpec(gd.shape, lambda p: (0, 0)),
                  pl.BlockSpec(gi.shape, lambda p: (0, 0)),
                  pl.BlockSpec((None,) + kf.shape[1:], lambda p: (order, 0, 0)),
                  dspec,
                  pl.BlockSpec((1, C), lambda p: (0, 0))],
        out_specs=dspec,
        compiler_params=_cparams("arbitrary"),
    )(x, gd, gi, kf, gate, skip)


def _merge_kernel(oa_ref, ob_ref, oc_ref, wa_ref, wb_ref, wc_ref, g0_ref, g1_ref, g2_ref, o_ref):
    ya = jnp.dot(oa_ref[...], wa_ref[...], preferred_element_type=F32)
    yb = jnp.dot(ob_ref[...], wb_ref[...], preferred_element_type=F32)
    yc = jnp.dot(oc_ref[...], wc_ref[...], preferred_element_type=F32)
    m = g0_ref[...].astype(F32) * ya + g1_ref[...].astype(F32) * yb + g2_ref[...].astype(F32) * yc
    o_ref[...] = m.astype(BF16)


def _merge(oa, ob, oc, w_ba, w_bb, w_bc, gates, nb, tb):
    D = D_MODEL
    tm = _pick(tb, (1024, 512, 256))
    tn = 512
    nt = D // tn
    oa, ob, oc, gates = (_by_batch(a, nb) for a in (oa, ob, oc, gates))
    row = lambda b, i, j: (b, i, 0)
    col = lambda b, i, j: (0, j)
    return pl.pallas_call(
        _merge_kernel,
        out_shape=jax.ShapeDtypeStruct((nb, tb, D), BF16),
        grid=(nb, tb // tm, nt),
        in_specs=[pl.BlockSpec((None, tm, DA_W), row), pl.BlockSpec((None, tm, MLA_W), row),
                  pl.BlockSpec((None, tm, HY_CH), row),
                  pl.BlockSpec((DA_W, tn), col), pl.BlockSpec((MLA_W, tn), col), pl.BlockSpec((HY_CH, tn), col),
                  pl.BlockSpec((None, tm, tn), lambda b, i, j: (b, i, j)),
                  pl.BlockSpec((None, tm, tn), lambda b, i, j: (b, i, nt + j)),
                  pl.BlockSpec((None, tm, tn), lambda b, i, j: (b, i, 2 * nt + j))],
        out_specs=pl.BlockSpec((None, tm, tn), lambda b, i, j: (b, i, j)),
        compiler_params=_cparams("arbitrary", "arbitrary", "arbitrary"),
    )(oa, ob, oc, w_ba, w_bb, w_bc, gates, gates, gates).reshape(nb * tb, D)


def _wo_ln_kernel(m_ref, w_ref, xs_ref, mod_ref, g_ref, b_ref, xs1_ref, h2_ref, *, sel, groups):
    y = jnp.dot(m_ref[...], w_ref[...], preferred_element_type=F32)
    g, b = g_ref[...], b_ref[...]
    for q in range(groups):
        mod = mod_ref[sel(pl.program_id(0), pl.program_id(1) * groups + q)]
        sl = slice(q * ROW_GROUP, (q + 1) * ROW_GROUP)
        x1 = _ln(DEEPNORM_ALPHA * xs_ref[sl, :] + mod[2:3, :] * y[sl, :]) * g + b
        xs1_ref[sl, :] = x1
        h2_ref[sl, :] = (_ln(x1) * (1.0 + mod[4:5, :]) + mod[3:4, :]).astype(BF16)


def _wo_ln(merged, w_o, xs, mod, ln_g, ln_b, sel, nb, tb):
    D = D_MODEL
    tm = _pick(tb, (512, 256))
    merged, xs = _by_batch(merged, nb), _by_batch(xs, nb)
    row = lambda b, i: (b, i, 0)
    full2 = lambda b, i: (0, 0)
    xs1, h2 = pl.pallas_call(
        functools.partial(_wo_ln_kernel, sel=sel, groups=tm // ROW_GROUP),
        out_shape=(jax.ShapeDtypeStruct((nb, tb, D), F32), jax.ShapeDtypeStruct((nb, tb, D), BF16)),
        grid=(nb, tb // tm),
        in_specs=[pl.BlockSpec((None, tm, D), row),
                  pl.BlockSpec((D, D), full2),
                  pl.BlockSpec((None, tm, D), row),
                  pl.BlockSpec(mod.shape, lambda b, i: (0, 0, 0)),
                  pl.BlockSpec((1, D), full2), pl.BlockSpec((1, D), full2)],
        out_specs=(pl.BlockSpec((None, tm, D), row), pl.BlockSpec((None, tm, D), row)),
        compiler_params=_cparams("arbitrary", "arbitrary"),
    )(merged, w_o, xs, mod, ln_g, ln_b)
    return xs1.reshape(nb * tb, D), h2.reshape(nb * tb, D)


FFN_HALO = 16


def _ffn_up_kernel(hp_ref, h_ref, hn_ref, wa_ref, wv_ref, cw_ref, cb_ref, kp_ref, kn_ref, o_ref, wa_s, wv_s):
    tm, tn = o_ref.shape

    @pl.when((pl.program_id(1) == 0) & (pl.program_id(2) == 0))
    def _():
        wa_s[...] = wa_ref[...].astype(BF16)
        wv_s[...] = wv_ref[...].astype(BF16)

    hm = h_ref[...]
    hext = jnp.concatenate([hp_ref[...], hm, hn_ref[...]], axis=0)
    a = jnp.dot(hext, wa_s[...], preferred_element_type=F32)
    v = jnp.dot(hm, wv_s[...], preferred_element_type=F32)
    ext = tm + 2 * FFN_HALO
    rep = tn // LANE
    keep_prev = jnp.tile(kp_ref[...], (1, rep))
    keep_next = jnp.tile(kn_ref[...], (1, rep))
    a_prev = pltpu.roll(a, 1, 0)[FFN_HALO:FFN_HALO + tm] * keep_prev
    a_next = pltpu.roll(a, ext - 1, 0)[FFN_HALO:FFN_HALO + tm] * keep_next
    cv = a_prev * cw_ref[0:1, :] + a[FFN_HALO:FFN_HALO + tm] * cw_ref[1:2, :] + a_next * cw_ref[2:3, :] + cb_ref[...]
    o_ref[...] = (cv * jax.nn.sigmoid(cv) * v).astype(BF16)


def _ffn_up(h2, w_up, layer, conv_w, conv_b, keep_prev, keep_next, nb, tb):
    D = D_MODEL
    tm = _pick(tb, (1024, 512, 256))
    tn = 512
    nt = D_FF // tn
    hb = tm // FFN_HALO
    last = tb // FFN_HALO - 1
    h2, keep_prev, keep_next = (_by_batch(a, nb) for a in (h2, keep_prev, keep_next))
    mask_spec = pl.BlockSpec((None, tm, LANE), lambda j, b, i: (b, i, 0))
    return pl.pallas_call(
        _ffn_up_kernel,
        out_shape=jax.ShapeDtypeStruct((nb, tb, D_FF), BF16),
        grid=(nt, nb, tb // tm),
        in_specs=[pl.BlockSpec((None, FFN_HALO, D), lambda j, b, i: (b, jnp.maximum(i * hb - 1, 0), 0)),
                  pl.BlockSpec((None, tm, D), lambda j, b, i: (b, i, 0)),
                  pl.BlockSpec((None, FFN_HALO, D), lambda j, b, i: (b, jnp.minimum((i + 1) * hb, last), 0)),
                  pl.BlockSpec((None, D, tn), lambda j, b, i: (layer, 0, j)),
                  pl.BlockSpec((None, D, tn), lambda j, b, i: (layer, 0, nt + j)),
                  pl.BlockSpec((3, tn), lambda j, b, i: (0, j)),
                  pl.BlockSpec((1, tn), lambda j, b, i: (0, j)),
                  mask_spec, mask_spec],
        out_specs=pl.BlockSpec((None, tm, tn), lambda j, b, i: (b, i, j)),
        scratch_shapes=[pltpu.VMEM((D, tn), BF16), pltpu.VMEM((D, tn), BF16)],
        compiler_params=_cparams("arbitrary", "arbitrary", "arbitrary"),
    )(h2, h2, h2, w_up, w_up, conv_w, conv_b, keep_prev, keep_next).reshape(nb * tb, D_FF)


def _ffn_down_kernel(u_ref, w_ref, xs_ref, mod_ref, g_ref, b_ref, o_ref, acc_ref, *, sel, groups, nk):
    k = pl.program_id(2)

    @pl.when(k == 0)
    def _():
        acc_ref[...] = jnp.zeros(acc_ref.shape, F32)

    acc_ref[...] += jnp.dot(u_ref[...], w_ref[...], preferred_element_type=F32)

    @pl.when(k == nk - 1)
    def _():
        g, b = g_ref[...], b_ref[...]
        for q in range(groups):
            mod = mod_ref[sel(pl.program_id(0), pl.program_id(1) * groups + q)]
            sl = slice(q * ROW_GROUP, (q + 1) * ROW_GROUP)
            o_ref[sl, :] = _ln(DEEPNORM_ALPHA * xs_ref[sl, :] + mod[5:6, :] * acc_ref[sl, :]) * g + b


def _ffn_down(u, w_down, xs1, mod, ln_g, ln_b, sel, nb, tb):
    D = D_MODEL
    tm = _pick(tb, (512, 256))
    tk = D_FF // 2
    nk = D_FF // tk
    u, xs1 = _by_batch(u, nb), _by_batch(xs1, nb)
    row = lambda b, i, k: (b, i, 0)
    full2 = lambda b, i, k: (0, 0)
    return pl.pallas_call(
        functools.partial(_ffn_down_kernel, sel=sel, groups=tm // ROW_GROUP, nk=nk),
        out_shape=jax.ShapeDtypeStruct((nb, tb, D), F32),
        grid=(nb, tb // tm, nk),
        in_specs=[pl.BlockSpec((None, tm, tk), lambda b, i, k: (b, i, k)),
                  pl.BlockSpec((tk, D), lambda b, i, k: (k, 0)),
                  pl.BlockSpec((None, tm, D), row),
                  pl.BlockSpec(mod.shape, lambda b, i, k: (0, 0, 0)),
                  pl.BlockSpec((1, D), full2), pl.BlockSpec((1, D), full2)],
        out_specs=pl.BlockSpec((None, tm, D), row),
        scratch_shapes=[pltpu.VMEM((tm, D), F32)],
        compiler_params=_cparams("arbitrary", "arbitrary", "arbitrary"),
    )(u, w_down, xs1, mod, ln_g, ln_b).reshape(nb * tb, D)


def _rope_tables(B, n, nc):
    half = DA_DQK // 2
    inv = ROPE_BASE ** (-jnp.arange(0, half, 2, dtype=F32) / half)
    t = jnp.arange(n, dtype=jnp.int32)
    ang_r = (t // GRID_W).astype(F32)[:, None] * inv[None, :]
    ang_c = (t % GRID_W).astype(F32)[:, None] * inv[None, :]
    ang = jnp.concatenate([ang_r, ang_r, ang_c, ang_c], axis=-1)
    cos, sin = jnp.cos(ang), jnp.sin(ang)
    upper = (jnp.arange(DA_DQK) % half) >= half // 2
    sa = jnp.where(upper, sin, 0.0)
    sb = jnp.where(upper, 0.0, -sin)

    def full(tab, fill):
        tab = jnp.concatenate([tab, jnp.full((nc, DA_DQK), fill, F32)], axis=0)
        tab = jnp.tile(tab, (B, LANE // DA_DQK))
        return tab

    return full(cos, 1.0), full(sa, 0.0), full(sb, 0.0)


def _conv_masks(B, n, nc):
    T = n + nc
    t = jnp.arange(T)
    keep_prev = ((t != 0) & (t != n)).astype(F32)
    keep_next = ((t != n - 1) & (t != T - 1)).astype(F32)
    widen = lambda m: jnp.tile(m[:, None], (B, LANE))
    return widen(keep_prev), widen(keep_next)


W_IN_WIDTHS = (DA_W, DA_W, DA_W, MLA_Q_RANK, MLA_KV_RANK, MLA_ROPE, 3 * HY_CH, N_BRANCH * D_MODEL)
W_IN_OFFS = tuple(sum(W_IN_WIDTHS[:j]) for j in range(len(W_IN_WIDTHS)))
W_MLA_COLS = MLA_Q_RANK + MLA_KV_RANK + LANE


W_HALF = LANE // 2


def _prep_w_in_kernel(a_ref, b_ref, o_ref, *, zero_b_at):
    b = b_ref[...]
    if zero_b_at is not None:
        b = jnp.where(pl.program_id(0) == zero_b_at, 0.0, b)
    o_ref[...] = jnp.transpose(jnp.concatenate([a_ref[...], b], axis=0)).astype(BF16)


def _prep_group(w_t, layer, n_blocks, src, zero_b_at=None):
    D = w_t.shape[2]
    return pl.pallas_call(
        functools.partial(_prep_w_in_kernel, zero_b_at=zero_b_at),
        out_shape=jax.ShapeDtypeStruct((D, n_blocks * LANE), BF16),
        grid=(n_blocks,),
        in_specs=[pl.BlockSpec((None, W_HALF, D), lambda c: (layer, src(c)[0], 0)),
                  pl.BlockSpec((None, W_HALF, D), lambda c: (layer, src(c)[1], 0))],
        out_specs=pl.BlockSpec((D, LANE), lambda c: (0, c)),
        compiler_params=_cparams("arbitrary"),
    )(w_t, w_t)


def _prep_w_in(w_in, layer):
    assert w_in.shape[2] == sum(W_IN_WIDTHS) and all(o % W_HALF == 0 for o in W_IN_OFFS)
    w_t = jnp.swapaxes(w_in, 1, 2)
    o_q, o_k, o_v, o_cq, _, o_kr, o_hy, o_g = (o // W_HALF for o in W_IN_OFFS)
    H = DA_HEADS

    def src_qkv(c):
        part, h = c // H, c % H
        a = jnp.where(part < 2, part * 2 * H + h, o_v + 2 * h)
        return a, jnp.where(part < 2, a + H, a + 1)

    pairs = lambda first: (lambda c: (first + 2 * c, first + 2 * c + 1))
    n_mla = W_MLA_COLS // LANE
    src_mla = lambda c: (o_cq + 2 * c, jnp.minimum(o_cq + 2 * c + 1, o_kr))
    return (_prep_group(w_t, layer, 3 * H, src_qkv),
            _prep_group(w_t, layer, n_mla, src_mla, zero_b_at=n_mla - 1),
            _prep_group(w_t, layer, 3 * HY_CH // LANE, pairs(o_hy)),
            _prep_group(w_t, layer, N_BRANCH * D_MODEL // LANE, pairs(o_g)))


def _pad_w_uq(w):
    w = w.reshape(MLA_Q_RANK, MLA_HEADS, MLA_NOPE + MLA_ROPE)
    w = jnp.pad(w, ((0, 0), (0, 0), (0, MLA_QK_PAD - MLA_NOPE - MLA_ROPE)))
    return w.reshape(MLA_Q_RANK, MLA_HEADS * MLA_QK_PAD).astype(BF16)


def _split_w_ukv(w):
    w = w.reshape(MLA_KV_RANK, MLA_HEADS, 2, MLA_NOPE).transpose(0, 2, 1, 3)
    return w.reshape(MLA_KV_RANK, 2 * MLA_W).astype(BF16)


def _hyena(u3, layer, p, n, nc, with_ctx, tables):
    B = u3.shape[0]
    C = HY_CH
    P = B // 2
    g1d, g1f, g2, g2i, g3, gk, gd, gi = tables
    S = FFT_S
    N1 = 2 * n // S
    mlp = (p["hy_ffn_w1p"][layer], p["hy_ffn_b1"][layer][None], p["hy_ffn_w2"][layer], p["hy_ffn_b2"][layer][None],
           p["hy_ffn_w3"][layer], p["hy_ffn_b3"][layer][None], p["hy_ffn_w4"][layer], p["hy_freq"][layer][None])
    skip = p["hy_skip"][layer]
    cw, cb = p["hy_conv_w"][layer], p["hy_conv_b"][layer][None]

    kc, sums = _hy_filter(n, *mlp)
    kf = _fft_s1(g1f, kc.reshape(HY_ORDER, N1, S, C))
    kf = _fft_s2_filt(kf.reshape(HY_ORDER, 2, N1, S, C), g2, 1.0 / sums[:, 0:1, :])
    dw = [a.reshape(P, N1, S, C) for a in _hy_dwconv(u3, cw, cb, 0, n)]
    z = dw[0]
    for o in range(HY_ORDER):
        a = _fft_s1(g1d, z).reshape(P, 2, N1, S, C)
        b = _fft_s2(a, g2, g2i, kf, o).reshape(P, 2 * N1, S, C)
        z = _fft_s3(g3, b, z, dw[1 + o], skip[o][None], F32 if o + 1 < HY_ORDER else BF16)
    oc_lat = z.reshape(B, n, C)

    if with_ctx:
        kcc, sumc = _hy_filter(nc, *mlp)
        kfc = _dense_spec(kcc, gk, 1.0 / sumc[:, 0:1, :])
        dwc = [a.reshape(P, 2 * nc, C) for a in _hy_dwconv(u3, cw, cb, n // nc, nc)]
        zc = dwc[0]
        for o in range(HY_ORDER):
            zc = _dense_conv(zc, gd, gi, kfc, o, dwc[1 + o], skip[o][None], F32 if o + 1 < HY_ORDER else BF16)
        return jnp.concatenate([oc_lat, zc.reshape(B, nc, C)], axis=1)
    return oc_lat


def kernel(x, c, ctx, c_ctx, ada_w, ada_b, w_in, da_lambda, da_subln_g, mla_q_g, mla_w_uq, mla_kv_g, mla_w_ukv, hy_conv_w, hy_conv_b, hy_ffn_w1, hy_ffn_b1, hy_ffn_w2, hy_ffn_b2, hy_ffn_w3, hy_ffn_b3, hy_ffn_w4, hy_freq, hy_skip, w_branch_a, w_branch_b, w_branch_c, w_out, ln1_g, ln1_b, ffn_w_up, ffn_conv_w, ffn_conv_b, ffn_w_down, ln2_g, ln2_b):
    B, n, D = x.shape
    nc = ctx.shape[1]
    T = n + nc
    rows = B * T
    assert D == D_MODEL and B % 2 == 0 and B < 8
    assert n % ROW_GROUP == 0 and nc % ROW_GROUP == 0 and n % nc == 0 and n % GRID_W == 0
    assert (2 * n) % (8 * FFT_S) == 0 and T % ATTN_TK == 0 and nc % ATTN_TK == 0
    geom = (T // ROW_GROUP, n // ROW_GROUP, B)

    hy = dict(hy_ffn_w1p=jnp.pad(hy_ffn_w1, ((0, 0), (0, HY_FFN - HY_EMB), (0, 0))), hy_ffn_b1=hy_ffn_b1,
              hy_ffn_w2=hy_ffn_w2, hy_ffn_b2=hy_ffn_b2, hy_ffn_w3=hy_ffn_w3, hy_ffn_b3=hy_ffn_b3,
              hy_ffn_w4=hy_ffn_w4, hy_freq=hy_freq, hy_skip=hy_skip, hy_conv_w=hy_conv_w, hy_conv_b=hy_conv_b)
    tables = _dft_tables(n) + _dense_tables(nc)
    rope = _rope_tables(B, n, nc)
    keep_prev, keep_next = _conv_masks(B, n, nc)

    cc = jnp.concatenate([c, c_ctx[None], jnp.zeros((8 - B - 1, D), F32)], axis=0)
    mods = _ada(cc, ada_w, ada_b[:, None, :]).reshape(DEPTH, 8, 6, D)

    xs = jnp.concatenate([x, ctx], axis=1).reshape(rows, D)
    for i in range(DEPTH):
        last = i == DEPTH - 1
        lam_init = 0.8 - 0.6 * math.exp(-0.3 * i)
        lq1, lk1, lq2, lk2 = da_lambda[i].astype(F32)
        lam = (jnp.exp(jnp.sum(lq1 * lk1)) - jnp.exp(jnp.sum(lq2 * lk2)) + lam_init).reshape(1)
        mod = mods[i]
        w_qkv, w_mla, w_hy, w_g = _prep_w_in(w_in, i)

        if last:
            nb, tb = B, n
            sel = lambda b, g: b
        else:
            nb, tb = 1, rows
            sel = lambda b, g: _mod_row(g, *geom)

        h = _lnmod(xs, mod, geom)
        qkv = _qkv_proj(h, w_qkv, rope)
        p_mla = _matmul(h, w_mla, F32, w_mla.shape[1])
        u_hy = _matmul(h, w_hy, F32, 768, nb=nb, tb=tb)
        gates = _matmul(h, w_g, BF16, 1024, act="sigmoid", nb=nb, tb=tb)

        q_m, k_m, v_m = _mla_prep(p_mla, mla_q_g[i][None], mla_kv_g[i][None], _pad_w_uq(mla_w_uq[i]),
                                  _split_w_ukv(mla_w_ukv[i]), rope)
        oa = _diff_attn(qkv.reshape(B, T, 3 * DA_W), lam, da_subln_g[i][None], n, nc, not last, lam_init)
        ob = _mla_attn(q_m.reshape(B, T, -1), k_m.reshape(B, T, -1), v_m.reshape(B, T, -1), n, nc, not last)
        oc = _hyena(u_hy.reshape(B, -1, 3 * HY_CH), i, hy, n, nc, not last, tables)

        merged = _merge(oa.reshape(-1, DA_W), ob.reshape(-1, MLA_W), oc.reshape(-1, HY_CH),
                        w_branch_a[i].astype(BF16), w_branch_b[i].astype(BF16), w_branch_c[i].astype(BF16), gates,
                        nb, tb)
        xs1, h2 = _wo_ln(merged, w_out[i].astype(BF16), xs, mod, ln1_g[i][None], ln1_b[i][None], sel, nb, tb)
        u = _ffn_up(h2, ffn_w_up, i, ffn_conv_w[i], ffn_conv_b[i][None], keep_prev, keep_next, nb, tb)
        xs = _ffn_down(u, ffn_w_down[i].astype(BF16), xs1, mod, ln2_g[i][None], ln2_b[i][None], sel, nb, tb)
    return xs.reshape(B, n, D)
```

```python
import functools
import math

import numpy as np
import jax
import jax.numpy as jnp
from jax import lax
from jax.experimental import pallas as pl
from jax.experimental.pallas import tpu as pltpu

F32 = jnp.float32
BF16 = jnp.bfloat16

D_MODEL = 2048
DEPTH = 2
GRID_W = 64
ROPE_BASE = 10000.0
NORM_EPS = 1e-6
DA_HEADS = 6
DA_DQK = 64
DA_DV = 128
DA_W = DA_HEADS * DA_DV
MLA_HEADS = 6
MLA_Q_RANK = 512
MLA_KV_RANK = 256
MLA_NOPE = 128
MLA_ROPE = 64
MLA_DV = 128
MLA_W = MLA_HEADS * MLA_DV
MLA_QK_PAD = 256
HY_CH = 512
HY_ORDER = 2
HY_EMB = 33
HY_BANDS = (HY_EMB - 1) // 2
HY_FFN = 64
HY_MIN_DECAY = math.log(1e-2) / 1.5
HY_MAX_DECAY = math.log(1e-2) / 0.3
D_FF = 5632
N_BRANCH = 3
DEEPNORM_ALPHA = (2 * DEPTH) ** 0.25
LOG2E = 1.4426950408889634

ROW_GROUP = 256
LANE = 128
FFT_S = 128
VMEM_LIMIT = 52 * 1024 * 1024


def _cparams(*sem):
    return pltpu.CompilerParams(dimension_semantics=sem, vmem_limit_bytes=VMEM_LIMIT)


def _pick(total, prefs):
    for p in prefs:
        if total % p == 0:
            return p
    raise ValueError(f"no tile for {total} in {prefs}")


def _ln(x):
    mu = jnp.mean(x, axis=-1, keepdims=True)
    xc = x - mu
    var = jnp.mean(xc * xc, axis=-1, keepdims=True)
    return xc * lax.rsqrt(var + NORM_EPS)


def _rms(x):
    return x * lax.rsqrt(jnp.mean(x * x, axis=-1, keepdims=True) + NORM_EPS)


def _rope128(u, cos, sa, sb):
    return u * cos + pltpu.roll(u, 16, 1) * sa + pltpu.roll(u, LANE - 16, 1) * sb


def _ada_kernel(c_ref, w_ref, b_ref, o_ref):
    a = c_ref[...]
    a = a * jax.nn.sigmoid(a)
    o_ref[...] = jnp.dot(a.astype(BF16), w_ref[...].astype(BF16), preferred_element_type=F32) + b_ref[...]


def _ada(cc, ada_w, ada_b):
    L, D, N = ada_w.shape
    tn = 1024
    return pl.pallas_call(
        _ada_kernel,
        out_shape=jax.ShapeDtypeStruct((L, 8, N), F32),
        grid=(L, N // tn),
        in_specs=[pl.BlockSpec((8, D), lambda l, j: (0, 0)),
                  pl.BlockSpec((None, D, tn), lambda l, j: (l, 0, j)),
                  pl.BlockSpec((None, 1, tn), lambda l, j: (l, 0, j))],
        out_specs=pl.BlockSpec((None, 8, tn), lambda l, j: (l, 0, j)),
        compiler_params=_cparams("arbitrary", "arbitrary"),
    )(cc, ada_w, ada_b)


def _lnmod_kernel(x_ref, m_ref, o_ref):
    y = _ln(x_ref[...])
    o_ref[...] = (y * (1.0 + m_ref[1:2, :]) + m_ref[0:1, :]).astype(BF16)


def _mod_row(g, gpb, lat_groups, n_batch):
    return jnp.where(g % gpb < lat_groups, g // gpb, n_batch)


def _lnmod(xs, mod, geom):
    rows, D = xs.shape
    gpb, lat_groups, n_batch = geom
    return pl.pallas_call(
        _lnmod_kernel,
        out_shape=jax.ShapeDtypeStruct((rows, D), BF16),
        grid=(rows // ROW_GROUP,),
        in_specs=[pl.BlockSpec((ROW_GROUP, D), lambda i: (i, 0)),
                  pl.BlockSpec((None, 6, D), lambda i: (_mod_row(i, gpb, lat_groups, n_batch), 0, 0))],
        out_specs=pl.BlockSpec((ROW_GROUP, D), lambda i: (i, 0)),
        compiler_params=_cparams("arbitrary"),
    )(xs, mod)


def _mm_kernel(a_ref, w_ref, o_ref, *, act):
    acc = jnp.dot(a_ref[...], w_ref[...], preferred_element_type=F32)
    if act == "sigmoid":
        acc = jax.nn.sigmoid(acc)
    o_ref[...] = acc.astype(o_ref.dtype)


def _by_batch(a, nb):
    return a.reshape(nb, a.shape[0] // nb, a.shape[1])


def _matmul(a, w, out_dtype, tn, act=None, nb=1, tb=None):
    a = _by_batch(a, nb)
    K = a.shape[2]
    tb = tb or a.shape[1]
    N = w.shape[1]
    tm = _pick(tb, (1024, 512, 256))
    return pl.pallas_call(
        functools.partial(_mm_kernel, act=act),
        out_shape=jax.ShapeDtypeStruct((nb, tb, N), out_dtype),
        grid=(nb, tb // tm, N // tn),
        in_specs=[pl.BlockSpec((None, tm, K), lambda b, i, j: (b, i, 0)),
                  pl.BlockSpec((K, tn), lambda b, i, j: (0, j))],
        out_specs=pl.BlockSpec((None, tm, tn), lambda b, i, j: (b, i, j)),
        compiler_params=_cparams("arbitrary", "arbitrary", "arbitrary"),
    )(a, w).reshape(nb * tb, N)


def _qkv_kernel(h_ref, w_ref, cos_ref, sa_ref, sb_ref, o_ref, *, qscale):
    j = pl.program_id(1)
    acc = jnp.dot(h_ref[...], w_ref[...], preferred_element_type=F32)

    @pl.when(j < 2)
    def _():
        cos, sa, sb = cos_ref[...], sa_ref[...], sb_ref[...]
        scale = jnp.where(j == 0, qscale, 1.0).astype(F32)
        for c in range(DA_HEADS):
            u = acc[:, c * LANE:(c + 1) * LANE]
            o_ref[:, c * LANE:(c + 1) * LANE] = (_rope128(u, cos, sa, sb) * scale).astype(BF16)

    @pl.when(j == 2)
    def _():
        o_ref[...] = acc.astype(BF16)


def _qkv_proj(h, w_qkv, tabs):
    M, K = h.shape
    tm = _pick(M, (1024, 512, 256))
    tn = DA_W
    tab_spec = pl.BlockSpec((tm, LANE), lambda i, j: (i, 0))
    return pl.pallas_call(
        functools.partial(_qkv_kernel, qscale=DA_DQK ** -0.5 * LOG2E),
        out_shape=jax.ShapeDtypeStruct((M, 3 * DA_W), BF16),
        grid=(M // tm, 3),
        in_specs=[pl.BlockSpec((tm, K), lambda i, j: (i, 0)),
                  pl.BlockSpec((K, tn), lambda i, j: (0, j)),
                  tab_spec, tab_spec, tab_spec],
        out_specs=pl.BlockSpec((tm, tn), lambda i, j: (i, j)),
        compiler_params=_cparams("arbitrary", "arbitrary"),
    )(h, w_qkv, *tabs)


def _mla_prep_kernel(p_ref, qg_ref, kvg_ref, wuq_ref, wukv_ref, cos_ref, sa_ref, sb_ref,
                     q_ref, k_ref, v_ref, *, qscale):
    p = p_ref[...]
    cos, sa, sb = cos_ref[...], sa_ref[...], sb_ref[...]
    cq = p[:, :MLA_Q_RANK]
    ckv = p[:, MLA_Q_RANK:MLA_Q_RANK + MLA_KV_RANK]
    kr = p[:, MLA_Q_RANK + MLA_KV_RANK:]
    qn = (_rms(cq) * qg_ref[...]).astype(BF16)
    q = jnp.dot(qn, wuq_ref[...], preferred_element_type=F32)
    kvn = (_rms(ckv) * kvg_ref[...]).astype(BF16)
    kv = jnp.dot(kvn, wukv_ref[...], preferred_element_type=F32)
    krr = _rope128(kr, cos, sa, sb).astype(BF16)
    for h in range(MLA_HEADS):
        o = h * MLA_QK_PAD
        q_ref[:, o:o + LANE] = (q[:, o:o + LANE] * qscale).astype(BF16)
        q_ref[:, o + LANE:o + 2 * LANE] = (_rope128(q[:, o + LANE:o + 2 * LANE], cos, sa, sb) * qscale).astype(BF16)
        k_ref[:, o:o + LANE] = kv[:, h * LANE:(h + 1) * LANE].astype(BF16)
        k_ref[:, o + LANE:o + 2 * LANE] = krr
    v_ref[...] = kv[:, MLA_W:].astype(BF16)


def _mla_prep(p, q_g, kv_g, w_uq, w_ukv, tabs):
    M, W = p.shape
    tm = _pick(M, (512, 256))
    row = lambda i: (i, 0)
    full = lambda i: (0, 0)
    qk_w = MLA_HEADS * MLA_QK_PAD
    return pl.pallas_call(
        functools.partial(_mla_prep_kernel, qscale=(MLA_NOPE + MLA_ROPE) ** -0.5 * LOG2E),
        out_shape=(jax.ShapeDtypeStruct((M, qk_w), BF16),
                   jax.ShapeDtypeStruct((M, qk_w), BF16),
                   jax.ShapeDtypeStruct((M, MLA_W), BF16)),
        grid=(M // tm,),
        in_specs=[pl.BlockSpec((tm, W), row),
                  pl.BlockSpec((1, MLA_Q_RANK), full),
                  pl.BlockSpec((1, MLA_KV_RANK), full),
                  pl.BlockSpec(w_uq.shape, full),
                  pl.BlockSpec(w_ukv.shape, full),
                  pl.BlockSpec((tm, LANE), row), pl.BlockSpec((tm, LANE), row), pl.BlockSpec((tm, LANE), row)],
        out_specs=(pl.BlockSpec((tm, qk_w), row), pl.BlockSpec((tm, qk_w), row), pl.BlockSpec((tm, MLA_W), row)),
        compiler_params=_cparams("arbitrary"),
    )(p, q_g, kv_g, w_uq, w_ukv, *tabs)


_NT = (((1,), (1,)), ((), ()))
ATTN_TQ = 256
ATTN_CHAINS = 8
ATTN_TK = 256


def _skewed_pipeline(n_chains, n_chunks, stages):
    for t in range(n_chains + len(stages) - 1):
        active = [(s, t - s) for s in range(len(stages)) if 0 <= t - s < n_chains]
        for c in range(n_chunks):
            for s, chain in active:
                stages[s][0](chain, c)
        for s, chain in active:
            stages[s][1](chain)


def _acc(old, new, op):
    return new if old is None else op(old, new)


def _lane_halves(x, op):
    return op(x[:, :LANE], x[:, LANE:])


def _attn_steps(run, q_rows, n_keys, lat_steps, n_lat, ctx_rows):
    if ctx_rows == 0:
        run(q_rows, 0, n_keys)
        return
    step = pl.program_id(2)

    @pl.when(step < lat_steps)
    def _():
        run(q_rows, 0, n_keys)

    @pl.when(step == lat_steps)
    def _():
        run(ctx_rows, n_lat, n_keys - n_lat)


def _diff_attn_kernel(lam_ref, q_ref, k_ref, v_ref, g_ref, o_ref, *, out_scale, lat_steps, n_lat, ctx_rows):
    run = functools.partial(_diff_attn_run, lam_ref, q_ref, k_ref, v_ref, g_ref, o_ref, out_scale)
    _attn_steps(run, q_ref.shape[0], k_ref.shape[0], lat_steps, n_lat, ctx_rows)


def _diff_attn_run(lam_ref, q_ref, k_ref, v_ref, g_ref, o_ref, out_scale, q_rows, key0, n_keys):
    lam, g = lam_ref[0], g_ref[...]
    tq = min(ATTN_TQ, q_rows)
    n_chains = q_rows // tq
    n_chunks = n_keys // ATTN_TK
    lane = lax.broadcasted_iota(jnp.int32, (1, LANE), 1)
    lo = (lane < DA_DQK).astype(F32)
    st = [dict(s=[], e=[], mx=[None, None], l=[None, None], o=None) for _ in range(n_chains)]
    rows = lambda i: slice(i * tq, (i + 1) * tq)
    keys = lambda c: slice(key0 + c * ATTN_TK, key0 + (c + 1) * ATTN_TK)

    def qk_chunk(i, c):
        d = st[i]
        if c == 0:
            qf = q_ref[rows(i), :].astype(F32)
            d["q"] = jnp.concatenate([(qf * lo).astype(BF16), (qf * (1.0 - lo)).astype(BF16)], axis=0)
        both = lax.dot_general(d["q"], k_ref[keys(c), :], _NT, preferred_element_type=F32)
        pair = []
        for m in range(2):
            s = both[m * tq:(m + 1) * tq]
            d["mx"][m] = _acc(d["mx"][m], _lane_halves(s, jnp.maximum), jnp.maximum)
            pair.append(s)
        d["s"].append(pair)

    def qk_done(i):
        st[i]["m"] = [jnp.max(mx, axis=-1, keepdims=True) for mx in st[i]["mx"]]

    def exp_chunk(i, c):
        d = st[i]
        pair = []
        for m in range(2):
            e = jnp.exp2(d["s"][c][m] - d["m"][m])
            d["l"][m] = _acc(d["l"][m], _lane_halves(e, jnp.add), jnp.add)
            pair.append(e)
        d["s"][c] = None
        d["e"].append(pair)

    def exp_done(i):
        d = st[i]
        l1, l2 = [jnp.sum(l, axis=-1, keepdims=True) for l in d["l"]]
        d["r1"] = 1.0 / l1
        d["cf"] = lam * l1 / l2

    def pv_chunk(i, c):
        d = st[i]
        w = (d["e"][c][0] - d["cf"] * d["e"][c][1]).astype(BF16)
        d["e"][c] = None
        d["o"] = _acc(d["o"], jnp.dot(w, v_ref[keys(c), :], preferred_element_type=F32), jnp.add)

    def pv_done(i):
        o = st[i]["o"] * st[i]["r1"]
        o_ref[rows(i), :] = (_rms(o) * g * out_scale).astype(BF16)

    _skewed_pipeline(n_chains, n_chunks, [(qk_chunk, qk_done), (exp_chunk, exp_done), (pv_chunk, pv_done)])


def _mla_attn_kernel(q_ref, k_ref, v_ref, o_ref, *, lat_steps, n_lat, ctx_rows):
    run = functools.partial(_mla_attn_run, q_ref, k_ref, v_ref, o_ref)
    _attn_steps(run, q_ref.shape[0], k_ref.shape[0], lat_steps, n_lat, ctx_rows)


def _mla_attn_run(q_ref, k_ref, v_ref, o_ref, q_rows, key0, n_keys):
    tq = min(ATTN_TQ, q_rows)
    n_chains = q_rows // tq
    n_chunks = n_keys // ATTN_TK
    st = [dict(s=[], mx=None, l=None, o=None) for _ in range(n_chains)]
    rows = lambda i: slice(i * tq, (i + 1) * tq)
    keys = lambda c: slice(key0 + c * ATTN_TK, key0 + (c + 1) * ATTN_TK)

    def qk_chunk(i, c):
        d = st[i]
        s = lax.dot_general(q_ref[rows(i), :], k_ref[keys(c), :], _NT, preferred_element_type=F32)
        d["mx"] = _acc(d["mx"], _lane_halves(s, jnp.maximum), jnp.maximum)
        d["s"].append(s)

    def qk_done(i):
        st[i]["m"] = jnp.max(st[i]["mx"], axis=-1, keepdims=True)

    def pv_chunk(i, c):
        d = st[i]
        e = jnp.exp2(d["s"][c] - d["m"])
        d["s"][c] = None
        d["l"] = _acc(d["l"], _lane_halves(e, jnp.add), jnp.add)
        d["o"] = _acc(d["o"], jnp.dot(e.astype(BF16), v_ref[keys(c), :], preferred_element_type=F32), jnp.add)

    def pv_done(i):
        d = st[i]
        o_ref[rows(i), :] = (d["o"] * (1.0 / jnp.sum(d["l"], axis=-1, keepdims=True))).astype(BF16)

    _skewed_pipeline(n_chains, n_chunks, [(qk_chunk, qk_done), (pv_chunk, pv_done)])


def _diff_attn(qkv, lam, subln_g, n, nc, ctx_queries, lam_init):
    B, T, _ = qkv.shape
    H = DA_HEADS
    tq = _pick(n, (ATTN_TQ * ATTN_CHAINS, ATTN_TQ))
    lat_steps = n // tq
    kern = functools.partial(_diff_attn_kernel, out_scale=1.0 - lam_init, lat_steps=lat_steps, n_lat=n,
                             ctx_rows=nc if ctx_queries else 0)
    return pl.pallas_call(
        kern,
        out_shape=jax.ShapeDtypeStruct((B, T if ctx_queries else n, DA_W), BF16),
        grid=(B, H, lat_steps + int(ctx_queries)),
        in_specs=[pl.BlockSpec(memory_space=pltpu.SMEM),
                  pl.BlockSpec((None, tq, LANE), lambda b, h, i: (b, i, h)),
                  pl.BlockSpec((None, T, LANE), lambda b, h, i: (b, 0, H + h)),
                  pl.BlockSpec((None, T, LANE), lambda b, h, i: (b, 0, 2 * H + h)),
                  pl.BlockSpec((1, DA_DV), lambda b, h, i: (0, 0))],
        out_specs=pl.BlockSpec((None, tq, LANE), lambda b, h, i: (b, i, h)),
        compiler_params=_cparams("arbitrary", "arbitrary", "arbitrary"),
    )(lam, qkv, qkv, qkv, subln_g)


def _mla_attn(q, k, v, n, nc, ctx_queries):
    B, T, _ = q.shape
    H = MLA_HEADS
    tq = _pick(n, (ATTN_TQ * ATTN_CHAINS, ATTN_TQ))
    lat_steps = n // tq
    kern = functools.partial(_mla_attn_kernel, lat_steps=lat_steps, n_lat=n, ctx_rows=nc if ctx_queries else 0)
    return pl.pallas_call(
        kern,
        out_shape=jax.ShapeDtypeStruct((B, T if ctx_queries else n, MLA_W), BF16),
        grid=(B, H, lat_steps + int(ctx_queries)),
        in_specs=[pl.BlockSpec((None, tq, MLA_QK_PAD), lambda b, h, i: (b, i, h)),
                  pl.BlockSpec((None, T, MLA_QK_PAD), lambda b, h, i: (b, 0, h)),
                  pl.BlockSpec((None, T, MLA_DV), lambda b, h, i: (b, 0, h))],
        out_specs=pl.BlockSpec((None, tq, MLA_DV), lambda b, h, i: (b, i, h)),
        compiler_params=_cparams("arbitrary", "arbitrary", "arbitrary"),
    )(q, k, v)


def _hy_filter_kernel(feat_ref, dec_ref, w1_ref, b1_ref, w2_ref, b2_ref, w3_ref, b3_ref, w4_ref, fr_ref,
                      k_ref, s_ref):
    hp = lax.Precision.HIGHEST
    d, r = pl.program_id(0), pl.program_id(1)
    fr = fr_ref[...]
    h = jnp.sin(fr * (jnp.dot(feat_ref[...], w1_ref[...], precision=hp, preferred_element_type=F32) + b1_ref[...]))
    h = jnp.sin(fr * (jnp.dot(h, w2_ref[...], precision=hp, preferred_element_type=F32) + b2_ref[...]))
    h = jnp.sin(fr * (jnp.dot(h, w3_ref[...], precision=hp, preferred_element_type=F32) + b3_ref[...]))
    h = jnp.dot(h, w4_ref[...], precision=hp, preferred_element_type=F32)
    row = lax.broadcasted_iota(jnp.int32, (h.shape[0], 1), 0)
    first_bwd = jnp.where((d == 1) & (r == 0), 1.0, 0.0)
    scale = dec_ref[...] * (1.0 - jnp.where(row == 0, 1.0, 0.0) * first_bwd)

    @pl.when((d == 0) & (r == 0))
    def _():
        s_ref[...] = jnp.zeros(s_ref.shape, F32)

    for o in range(HY_ORDER):
        ko = h[:, o * HY_CH:(o + 1) * HY_CH] * scale
        k_ref[o] = ko
        s_ref[o] += jnp.broadcast_to(jnp.sum(jnp.abs(ko), axis=0, keepdims=True), (8, HY_CH))


def _hy_filter(n, w1, b1, w2, b2, w3, b3, w4, fr):
    C = HY_CH
    t = jnp.linspace(0.0, 1.0, n, dtype=F32)
    pos = jnp.arange(n, dtype=F32)
    t2 = jnp.concatenate([t, t[::-1]])[:, None]
    pos2 = jnp.concatenate([pos, pos[::-1]])[:, None]
    phase = (2.0 * math.pi / n) * pos2 * jnp.linspace(1e-4, HY_BANDS - 1, HY_BANDS, dtype=F32)[None, :]
    feat = jnp.concatenate([t2, jnp.cos(phase), -jnp.sin(phase)], axis=-1)
    feat = jnp.pad(feat, ((0, 0), (0, HY_FFN - HY_EMB)))
    dec = jnp.exp(-t2 * jnp.abs(jnp.linspace(HY_MIN_DECAY, HY_MAX_DECAY, C, dtype=F32)))
    w4d = w4.reshape(HY_FFN, HY_ORDER, 2, C).transpose(2, 0, 1, 3).reshape(2, HY_FFN, HY_ORDER * C)
    rb = min(512, n)
    nb = n // rb
    full = lambda d, r: (0, 0)
    return pl.pallas_call(
        _hy_filter_kernel,
        out_shape=(jax.ShapeDtypeStruct((HY_ORDER, 2 * n, C), F32),
                   jax.ShapeDtypeStruct((HY_ORDER, 8, C), F32)),
        grid=(2, nb),
        in_specs=[pl.BlockSpec((rb, HY_FFN), lambda d, r: (d * nb + r, 0)),
                  pl.BlockSpec((rb, C), lambda d, r: (d * nb + r, 0)),
                  pl.BlockSpec(w1.shape, full), pl.BlockSpec(b1.shape, full),
                  pl.BlockSpec(w2.shape, full), pl.BlockSpec(b2.shape, full),
                  pl.BlockSpec(w3.shape, full), pl.BlockSpec(b3.shape, full),
                  pl.BlockSpec((None, HY_FFN, HY_ORDER * C), lambda d, r: (d, 0, 0)),
                  pl.BlockSpec(fr.shape, full)],
        out_specs=(pl.BlockSpec((HY_ORDER, rb, C), lambda d, r: (0, d * nb + r, 0)),
                   pl.BlockSpec((HY_ORDER, 8, C), lambda d, r: (0, 0, 0))),
        compiler_params=_cparams("arbitrary", "arbitrary"),
    )(feat, dec, w1, b1, w2, b2, w3, b3, w4d, fr)


def _hy_dwconv_kernel(*refs):
    for k in range(3):
        u_ref, w_ref, b_ref, o_ref = refs[3 * k], refs[3 * k + 1], refs[3 * k + 2], refs[9 + k]
        u = u_ref[...]
        L = u.shape[0]
        row = lax.broadcasted_iota(jnp.int32, (L, 1), 0)
        up = jnp.where(row == 0, 0.0, pltpu.roll(u, 1, 0))
        dn = jnp.where(row == L - 1, 0.0, pltpu.roll(u, L - 1, 0))
        o_ref[...] = up * w_ref[0:1, :] + u * w_ref[1:2, :] + dn * w_ref[2:3, :] + b_ref[...]


def _hy_dwconv(u, w, b, row_block, length):
    B = u.shape[0]
    cw = LANE
    per = HY_CH // cw
    in_specs, args = [], []
    for k in range(3):
        in_specs += [pl.BlockSpec((None, length, cw), lambda bb, j, k=k: (bb, row_block, k * per + j)),
                     pl.BlockSpec((3, cw), lambda bb, j, k=k: (0, k * per + j)),
                     pl.BlockSpec((1, cw), lambda bb, j, k=k: (0, k * per + j))]
        args += [u, w, b]
    ospec = pl.BlockSpec((None, length, cw), lambda bb, j: (bb, 0, j))
    return pl.pallas_call(
        _hy_dwconv_kernel,
        out_shape=(jax.ShapeDtypeStruct((B, length, HY_CH), F32),) * 3,
        grid=(B, per),
        in_specs=in_specs,
        out_specs=(ospec,) * 3,
        compiler_params=_cparams("arbitrary", "arbitrary"),
    )(*args)


def _dft_tables(n):
    S = FFT_S
    M = 2 * n
    N1 = M // S
    H = N1 // 2
    s2 = np.arange(S)[:, None, None]
    k1 = np.arange(N1)[None, :, None]
    s1 = np.arange(N1)[None, None, :]
    ang = -2.0 * np.pi * ((k1 * (S * s1 + s2)) % M) / M
    fr, fi = np.cos(ang), np.sin(ang)
    g1f = np.concatenate([fr, fi], axis=1)
    frh, fih = fr[:, :, :H], fi[:, :, :H]
    g1d = np.concatenate([np.concatenate([frh, -fih], axis=2),
                          np.concatenate([fih, frh], axis=2)], axis=1)
    er = np.transpose(frh, (0, 2, 1)) / M
    ei = -np.transpose(fih, (0, 2, 1)) / M
    g3 = np.concatenate([np.concatenate([er, -ei], axis=2),
                         np.concatenate([ei, er], axis=2)], axis=1)
    a2 = -2.0 * np.pi * ((np.arange(S)[:, None] * np.arange(S)[None, :]) % S) / S
    f2r, f2i = np.cos(a2), np.sin(a2)
    g2 = np.block([[f2r, -f2i], [f2i, f2r]])
    g2i = np.block([[f2r, f2i], [-f2i, f2r]])
    cast = lambda a: jnp.asarray(a, dtype=F32).astype(BF16)
    return cast(g1d), cast(g1f), cast(g2), cast(g2i), cast(g3)


FFT_NS = 16


def _fft_s1_kernel(g_ref, x_ref, o_ref):
    xt = jnp.swapaxes(x_ref[...], 0, 1)
    y = jnp.stack([jnp.dot(g_ref[j], xt[j].astype(BF16), preferred_element_type=F32)
                   for j in range(xt.shape[0])], axis=0)
    o_ref[...] = jnp.swapaxes(y, 0, 1).astype(o_ref.dtype)


def _fft_s1(g, x):
    P, Ri, S, C = x.shape
    Ro = g.shape[1]
    ns = FFT_NS
    return pl.pallas_call(
        _fft_s1_kernel,
        out_shape=jax.ShapeDtypeStruct((P, Ro, S, C), BF16),
        grid=(S // ns, P),
        in_specs=[pl.BlockSpec((ns, Ro, Ri), lambda j, p: (j, 0, 0)),
                  pl.BlockSpec((None, Ri, ns, C), lambda j, p: (p, 0, j, 0))],
        out_specs=pl.BlockSpec((None, Ro, ns, C), lambda j, p: (p, 0, j, 0)),
        compiler_params=_cparams("arbitrary", "arbitrary"),
    )(g, x)


def _fft_s2_filt_kernel(a_ref, g_ref, rn_ref, o_ref, *, nk):
    S = FFT_S
    rn = rn_ref[...]
    for t in range(nk):
        d = jnp.concatenate([a_ref[0, t], a_ref[1, t]], axis=0)
        y = jnp.dot(g_ref[...], d, preferred_element_type=F32)
        o_ref[0, t] = y[:S] * rn
        o_ref[1, t] = y[S:] * rn


def _fft_s2_filt(a, g2, rnorm):
    O, _, N1, S, C = a.shape
    nk = 4
    blk = (None, 2, nk, S, C)
    return pl.pallas_call(
        functools.partial(_fft_s2_filt_kernel, nk=nk),
        out_shape=jax.ShapeDtypeStruct(a.shape, F32),
        grid=(O, N1 // nk),
        in_specs=[pl.BlockSpec(blk, lambda o, j: (o, 0, j, 0, 0)),
                  pl.BlockSpec(g2.shape, lambda o, j: (0, 0)),
                  pl.BlockSpec((None, 1, C), lambda o, j: (o, 0, 0))],
        out_specs=pl.BlockSpec(blk, lambda o, j: (o, 0, j, 0, 0)),
        compiler_params=_cparams("arbitrary", "arbitrary"),
    )(a, g2, rnorm)


def _fft_s2_kernel(a_ref, g_ref, gi_ref, kf_ref, o_ref, *, nk):
    S = FFT_S
    for t in range(nk):
        d = jnp.concatenate([a_ref[0, t], a_ref[1, t]], axis=0)
        y = jnp.dot(g_ref[...], d, preferred_element_type=F32)
        yr, yi = y[:S], y[S:]
        kr, ki = kf_ref[0, t], kf_ref[1, t]
        p = jnp.concatenate([yr * kr - yi * ki, yr * ki + yi * kr], axis=0).astype(BF16)
        b = jnp.dot(gi_ref[...], p, preferred_element_type=F32)
        o_ref[0, t] = b[:S].astype(BF16)
        o_ref[1, t] = b[S:].astype(BF16)


def _fft_s2(a, g2, g2i, kf, order):
    P, _, N1, S, C = a.shape
    nk = 4
    blk = (None, 2, nk, S, C)
    return pl.pallas_call(
        functools.partial(_fft_s2_kernel, nk=nk),
        out_shape=jax.ShapeDtypeStruct(a.shape, BF16),
        grid=(N1 // nk, P),
        in_specs=[pl.BlockSpec(blk, lambda j, p: (p, 0, j, 0, 0)),
                  pl.BlockSpec(g2.shape, lambda j, p: (0, 0)),
                  pl.BlockSpec(g2i.shape, lambda j, p: (0, 0)),
                  pl.BlockSpec(blk, lambda j, p: (order, 0, j, 0, 0))],
        out_specs=pl.BlockSpec(blk, lambda j, p: (p, 0, j, 0, 0)),
        compiler_params=_cparams("arbitrary", "arbitrary"),
    )(a, g2, g2i, kf)


def _fft_s3_kernel(g_ref, b_ref, z_ref, gate_ref, skip_ref, o_ref):
    bt = jnp.swapaxes(b_ref[...].astype(F32), 0, 1).astype(BF16)
    y = jnp.stack([jnp.dot(g_ref[j], bt[j], preferred_element_type=F32) for j in range(bt.shape[0])], axis=0)
    y = jnp.swapaxes(y, 0, 1)
    o_ref[...] = (gate_ref[...] * (y + skip_ref[...] * z_ref[...])).astype(o_ref.dtype)


def _fft_s3(g3, b, z, gate, skip, out_dtype):
    P, Ri, S, C = b.shape
    Ro = g3.shape[1]
    ns = FFT_NS
    dspec = pl.BlockSpec((None, Ro, ns, C), lambda j, p: (p, 0, j, 0))
    return pl.pallas_call(
        _fft_s3_kernel,
        out_shape=jax.ShapeDtypeStruct((P, Ro, S, C), out_dtype),
        grid=(S // ns, P),
        in_specs=[pl.BlockSpec((ns, Ro, Ri), lambda j, p: (j, 0, 0)),
                  pl.BlockSpec((None, Ri, ns, C), lambda j, p: (p, 0, j, 0)),
                  dspec, dspec,
                  pl.BlockSpec((1, C), lambda j, p: (0, 0))],
        out_specs=dspec,
        compiler_params=_cparams("arbitrary", "arbitrary"),
    )(g3, b, z, gate, skip)


def _dense_tables(m):
    M = 2 * m
    ang = -2.0 * np.pi * ((np.arange(M)[:, None] * np.arange(M)[None, :]) % M) / M
    fr, fi = np.cos(ang), np.sin(ang)
    gk = np.concatenate([fr, fi], axis=0)
    gd = np.block([[fr[:, :m], -fi[:, :m]], [fi[:, :m], fr[:, :m]]])
    er, ei = fr[:m, :] / M, -fi[:m, :] / M
    gi = np.block([[er, -ei], [ei, er]])
    cast = lambda a: jnp.asarray(a, dtype=F32).astype(BF16)
    return cast(gk), cast(gd), cast(gi)


def _dense_spec_kernel(k_ref, g_ref, rn_ref, o_ref):
    o_ref[...] = jnp.dot(g_ref[...], k_ref[...].astype(BF16), preferred_element_type=F32) * rn_ref[...]


def _dense_spec(kc, gk, rnorm):
    O, M, C = kc.shape
    return pl.pallas_call(
        _dense_spec_kernel,
        out_shape=jax.ShapeDtypeStruct((O, 2 * M, C), F32),
        grid=(O,),
        in_specs=[pl.BlockSpec((None, M, C), lambda o: (o, 0, 0)),
                  pl.BlockSpec(gk.shape, lambda o: (0, 0)),
                  pl.BlockSpec((None, 1, C), lambda o: (o, 0, 0))],
        out_specs=pl.BlockSpec((None, 2 * M, C), lambda o: (o, 0, 0)),
        compiler_params=_cparams("arbitrary"),
    )(kc, gk, rnorm)


def _dense_conv_kernel(x_ref, gd_ref, gi_ref, kf_ref, gate_ref, skip_ref, o_ref):
    x = x_ref[...]
    y = jnp.dot(gd_ref[...], x.astype(BF16), preferred_element_type=F32)
    M = y.shape[0] // 2
    yr, yi = y[:M], y[M:]
    kr, ki = kf_ref[:M], kf_ref[M:]
    p = jnp.concatenate([yr * kr - yi * ki, yr * ki + yi * kr], axis=0).astype(BF16)
    conv = jnp.dot(gi_ref[...], p, preferred_element_type=F32)
    o_ref[...] = (gate_ref[...] * (conv + skip_ref[...] * x)).astype(o_ref.dtype)


def _dense_conv(x, gd, gi, kf, order, gate, skip, out_dtype):
    P, R, C = x.shape
    dspec = pl.BlockSpec((None, R, C), lambda p: (p, 0, 0))
    return pl.pallas_call(
        _dense_conv_kernel,
        out_shape=jax.ShapeDtypeStruct((P, R, C), out_dtype),
        grid=(P,),
        in_specs=[dspec,
                  pl.BlockSpec(gd.shape, lambda p: (0, 0)),
                  pl.BlockSpec(gi.shape, lambda p: (0, 0)),
                  pl.BlockSpec((None,) + kf.shape[1:], lambda p: (order, 0, 0)),
                  dspec,
                  pl.BlockSpec((1, C), lambda p: (0, 0))],
        out_specs=dspec,
        compiler_params=_cparams("arbitrary"),
    )(x, gd, gi, kf, gate, skip)


def _merge_kernel(oa_ref, ob_ref, oc_ref, wa_ref, wb_ref, wc_ref, g0_ref, g1_ref, g2_ref, o_ref):
    ya = jnp.dot(oa_ref[...], wa_ref[...], preferred_element_type=F32)
    yb = jnp.dot(ob_ref[...], wb_ref[...], preferred_element_type=F32)
    yc = jnp.dot(oc_ref[...], wc_ref[...], preferred_element_type=F32)
    m = g0_ref[...].astype(F32) * ya + g1_ref[...].astype(F32) * yb + g2_ref[...].astype(F32) * yc
    o_ref[...] = m.astype(BF16)


def _merge(oa, ob, oc, w_ba, w_bb, w_bc, gates, nb, tb):
    D = D_MODEL
    tm = _pick(tb, (1024, 512, 256))
    tn = 512
    nt = D // tn
    oa, ob, oc, gates = (_by_batch(a, nb) for a in (oa, ob, oc, gates))
    row = lambda b, i, j: (b, i, 0)
    col = lambda b, i, j: (0, j)
    return pl.pallas_call(
        _merge_kernel,
        out_shape=jax.ShapeDtypeStruct((nb, tb, D), BF16),
        grid=(nb, tb // tm, nt),
        in_specs=[pl.BlockSpec((None, tm, DA_W), row), pl.BlockSpec((None, tm, MLA_W), row),
                  pl.BlockSpec((None, tm, HY_CH), row),
                  pl.BlockSpec((DA_W, tn), col), pl.BlockSpec((MLA_W, tn), col), pl.BlockSpec((HY_CH, tn), col),
                  pl.BlockSpec((None, tm, tn), lambda b, i, j: (b, i, j)),
                  pl.BlockSpec((None, tm, tn), lambda b, i, j: (b, i, nt + j)),
                  pl.BlockSpec((None, tm, tn), lambda b, i, j: (b, i, 2 * nt + j))],
        out_specs=pl.BlockSpec((None, tm, tn), lambda b, i, j: (b, i, j)),
        compiler_params=_cparams("arbitrary", "arbitrary", "arbitrary"),
    )(oa, ob, oc, w_ba, w_bb, w_bc, gates, gates, gates).reshape(nb * tb, D)


def _wo_ln_kernel(m_ref, w_ref, xs_ref, mod_ref, g_ref, b_ref, xs1_ref, h2_ref, *, sel, groups):
    y = jnp.dot(m_ref[...], w_ref[...], preferred_element_type=F32)
    g, b = g_ref[...], b_ref[...]
    for q in range(groups):
        mod = mod_ref[sel(pl.program_id(0), pl.program_id(1) * groups + q)]
        sl = slice(q * ROW_GROUP, (q + 1) * ROW_GROUP)
        x1 = _ln(DEEPNORM_ALPHA * xs_ref[sl, :] + mod[2:3, :] * y[sl, :]) * g + b
        xs1_ref[sl, :] = x1
        h2_ref[sl, :] = (_ln(x1) * (1.0 + mod[4:5, :]) + mod[3:4, :]).astype(BF16)


def _wo_ln(merged, w_o, xs, mod, ln_g, ln_b, sel, nb, tb):
    D = D_MODEL
    tm = _pick(tb, (512, 256))
    merged, xs = _by_batch(merged, nb), _by_batch(xs, nb)
    row = lambda b, i: (b, i, 0)
    full2 = lambda b, i: (0, 0)
    xs1, h2 = pl.pallas_call(
        functools.partial(_wo_ln_kernel, sel=sel, groups=tm // ROW_GROUP),
        out_shape=(jax.ShapeDtypeStruct((nb, tb, D), F32), jax.ShapeDtypeStruct((nb, tb, D), BF16)),
        grid=(nb, tb // tm),
        in_specs=[pl.BlockSpec((None, tm, D), row),
                  pl.BlockSpec((D, D), full2),
                  pl.BlockSpec((None, tm, D), row),
                  pl.BlockSpec(mod.shape, lambda b, i: (0, 0, 0)),
                  pl.BlockSpec((1, D), full2), pl.BlockSpec((1, D), full2)],
        out_specs=(pl.BlockSpec((None, tm, D), row), pl.BlockSpec((None, tm, D), row)),
        compiler_params=_cparams("arbitrary", "arbitrary"),
    )(merged, w_o, xs, mod, ln_g, ln_b)
    return xs1.reshape(nb * tb, D), h2.reshape(nb * tb, D)


FFN_HALO = 16


def _ffn_up_kernel(hp_ref, h_ref, hn_ref, wa_ref, wv_ref, cw_ref, cb_ref, kp_ref, kn_ref, o_ref, wa_s, wv_s):
    tm, tn = o_ref.shape

    @pl.when((pl.program_id(1) == 0) & (pl.program_id(2) == 0))
    def _():
        wa_s[...] = wa_ref[...].astype(BF16)
        wv_s[...] = wv_ref[...].astype(BF16)

    hm = h_ref[...]
    hext = jnp.concatenate([hp_ref[...], hm, hn_ref[...]], axis=0)
    a = jnp.dot(hext, wa_s[...], preferred_element_type=F32)
    v = jnp.dot(hm, wv_s[...], preferred_element_type=F32)
    ext = tm + 2 * FFN_HALO
    rep = tn // LANE
    keep_prev = jnp.tile(kp_ref[...], (1, rep))
    keep_next = jnp.tile(kn_ref[...], (1, rep))
    a_prev = pltpu.roll(a, 1, 0)[FFN_HALO:FFN_HALO + tm] * keep_prev
    a_next = pltpu.roll(a, ext - 1, 0)[FFN_HALO:FFN_HALO + tm] * keep_next
    cv = a_prev * cw_ref[0:1, :] + a[FFN_HALO:FFN_HALO + tm] * cw_ref[1:2, :] + a_next * cw_ref[2:3, :] + cb_ref[...]
    o_ref[...] = (cv * jax.nn.sigmoid(cv) * v).astype(BF16)


def _ffn_up(h2, w_up, layer, conv_w, conv_b, keep_prev, keep_next, nb, tb):
    D = D_MODEL
    tm = _pick(tb, (1024, 512, 256))
    tn = 512
    nt = D_FF // tn
    hb = tm // FFN_HALO
    last = tb // FFN_HALO - 1
    h2, keep_prev, keep_next = (_by_batch(a, nb) for a in (h2, keep_prev, keep_next))
    mask_spec = pl.BlockSpec((None, tm, LANE), lambda j, b, i: (b, i, 0))
    return pl.pallas_call(
        _ffn_up_kernel,
        out_shape=jax.ShapeDtypeStruct((nb, tb, D_FF), BF16),
        grid=(nt, nb, tb // tm),
        in_specs=[pl.BlockSpec((None, FFN_HALO, D), lambda j, b, i: (b, jnp.maximum(i * hb - 1, 0), 0)),
                  pl.BlockSpec((None, tm, D), lambda j, b, i: (b, i, 0)),
                  pl.BlockSpec((None, FFN_HALO, D), lambda j, b, i: (b, jnp.minimum((i + 1) * hb, last), 0)),
                  pl.BlockSpec((None, D, tn), lambda j, b, i: (layer, 0, j)),
                  pl.BlockSpec((None, D, tn), lambda j, b, i: (layer, 0, nt + j)),
                  pl.BlockSpec((3, tn), lambda j, b, i: (0, j)),
                  pl.BlockSpec((1, tn), lambda j, b, i: (0, j)),
                  mask_spec, mask_spec],
        out_specs=pl.BlockSpec((None, tm, tn), lambda j, b, i: (b, i, j)),
        scratch_shapes=[pltpu.VMEM((D, tn), BF16), pltpu.VMEM((D, tn), BF16)],
        compiler_params=_cparams("arbitrary", "arbitrary", "arbitrary"),
    )(h2, h2, h2, w_up, w_up, conv_w, conv_b, keep_prev, keep_next).reshape(nb * tb, D_FF)


def _ffn_down_kernel(u_ref, w_ref, xs_ref, mod_ref, g_ref, b_ref, o_ref, acc_ref, *, sel, groups, nk):
    k = pl.program_id(2)

    @pl.when(k == 0)
    def _():
        acc_ref[...] = jnp.zeros(acc_ref.shape, F32)

    acc_ref[...] += jnp.dot(u_ref[...], w_ref[...], preferred_element_type=F32)

    @pl.when(k == nk - 1)
    def _():
        g, b = g_ref[...], b_ref[...]
        for q in range(groups):
            mod = mod_ref[sel(pl.program_id(0), pl.program_id(1) * groups + q)]
            sl = slice(q * ROW_GROUP, (q + 1) * ROW_GROUP)
            o_ref[sl, :] = _ln(DEEPNORM_ALPHA * xs_ref[sl, :] + mod[5:6, :] * acc_ref[sl, :]) * g + b


def _ffn_down(u, w_down, xs1, mod, ln_g, ln_b, sel, nb, tb):
    D = D_MODEL
    tm = _pick(tb, (512, 256))
    tk = D_FF // 2
    nk = D_FF // tk
    u, xs1 = _by_batch(u, nb), _by_batch(xs1, nb)
    row = lambda b, i, k: (b, i, 0)
    full2 = lambda b, i, k: (0, 0)
    return pl.pallas_call(
        functools.partial(_ffn_down_kernel, sel=sel, groups=tm // ROW_GROUP, nk=nk),
        out_shape=jax.ShapeDtypeStruct((nb, tb, D), F32),
        grid=(nb, tb // tm, nk),
        in_specs=[pl.BlockSpec((None, tm, tk), lambda b, i, k: (b, i, k)),
                  pl.BlockSpec((tk, D), lambda b, i, k: (k, 0)),
                  pl.BlockSpec((None, tm, D), row),
                  pl.BlockSpec(mod.shape, lambda b, i, k: (0, 0, 0)),
                  pl.BlockSpec((1, D), full2), pl.BlockSpec((1, D), full2)],
        out_specs=pl.BlockSpec((None, tm, D), row),
        scratch_shapes=[pltpu.VMEM((tm, D), F32)],
        compiler_params=_cparams("arbitrary", "arbitrary", "arbitrary"),
    )(u, w_down, xs1, mod, ln_g, ln_b).reshape(nb * tb, D)


def _rope_tables(B, n, nc):
    half = DA_DQK // 2
    inv = ROPE_BASE ** (-jnp.arange(0, half, 2, dtype=F32) / half)
    t = jnp.arange(n, dtype=jnp.int32)
    ang_r = (t // GRID_W).astype(F32)[:, None] * inv[None, :]
    ang_c = (t % GRID_W).astype(F32)[:, None] * inv[None, :]
    ang = jnp.concatenate([ang_r, ang_r, ang_c, ang_c], axis=-1)
    cos, sin = jnp.cos(ang), jnp.sin(ang)
    upper = (jnp.arange(DA_DQK) % half) >= half // 2
    sa = jnp.where(upper, sin, 0.0)
    sb = jnp.where(upper, 0.0, -sin)

    def full(tab, fill):
        tab = jnp.concatenate([tab, jnp.full((nc, DA_DQK), fill, F32)], axis=0)
        tab = jnp.tile(tab, (B, LANE // DA_DQK))
        return tab

    return full(cos, 1.0), full(sa, 0.0), full(sb, 0.0)


def _conv_masks(B, n, nc):
    T = n + nc
    t = jnp.arange(T)
    keep_prev = ((t != 0) & (t != n)).astype(F32)
    keep_next = ((t != n - 1) & (t != T - 1)).astype(F32)
    widen = lambda m: jnp.tile(m[:, None], (B, LANE))
    return widen(keep_prev), widen(keep_next)


W_IN_WIDTHS = (DA_W, DA_W, DA_W, MLA_Q_RANK, MLA_KV_RANK, MLA_ROPE, 3 * HY_CH, N_BRANCH * D_MODEL)
W_IN_OFFS = tuple(sum(W_IN_WIDTHS[:j]) for j in range(len(W_IN_WIDTHS)))
W_MLA_COLS = MLA_Q_RANK + MLA_KV_RANK + LANE


W_HALF = LANE // 2


def _prep_w_in_kernel(a_ref, b_ref, o_ref, *, zero_b_at):
    b = b_ref[...]
    if zero_b_at is not None:
        b = jnp.where(pl.program_id(0) == zero_b_at, 0.0, b)
    o_ref[...] = jnp.transpose(jnp.concatenate([a_ref[...], b], axis=0)).astype(BF16)


def _prep_group(w_t, layer, n_blocks, src, zero_b_at=None):
    D = w_t.shape[2]
    return pl.pallas_call(
        functools.partial(_prep_w_in_kernel, zero_b_at=zero_b_at),
        out_shape=jax.ShapeDtypeStruct((D, n_blocks * LANE), BF16),
        grid=(n_blocks,),
        in_specs=[pl.BlockSpec((None, W_HALF, D), lambda c: (layer, src(c)[0], 0)),
                  pl.BlockSpec((None, W_HALF, D), lambda c: (layer, src(c)[1], 0))],
        out_specs=pl.BlockSpec((D, LANE), lambda c: (0, c)),
        compiler_params=_cparams("arbitrary"),
    )(w_t, w_t)


def _prep_w_in(w_in, layer):
    assert w_in.shape[2] == sum(W_IN_WIDTHS) and all(o % W_HALF == 0 for o in W_IN_OFFS)
    w_t = jnp.swapaxes(w_in, 1, 2)
    o_q, o_k, o_v, o_cq, _, o_kr, o_hy, o_g = (o // W_HALF for o in W_IN_OFFS)
    H = DA_HEADS

    def src_qkv(c):
        part, h = c // H, c % H
        a = jnp.where(part < 2, part * 2 * H + h, o_v + 2 * h)
        return a, jnp.where(part < 2, a + H, a + 1)

    pairs = lambda first: (lambda c: (first + 2 * c, first + 2 * c + 1))
    n_mla = W_MLA_COLS // LANE
    src_mla = lambda c: (o_cq + 2 * c, jnp.minimum(o_cq + 2 * c + 1, o_kr))
    return (_prep_group(w_t, layer, 3 * H, src_qkv),
            _prep_group(w_t, layer, n_mla, src_mla, zero_b_at=n_mla - 1),
            _prep_group(w_t, layer, 3 * HY_CH // LANE, pairs(o_hy)),
            _prep_group(w_t, layer, N_BRANCH * D_MODEL // LANE, pairs(o_g)))


def _pad_w_uq(w):
    w = w.reshape(MLA_Q_RANK, MLA_HEADS, MLA_NOPE + MLA_ROPE)
    w = jnp.pad(w, ((0, 0), (0, 0), (0, MLA_QK_PAD - MLA_NOPE - MLA_ROPE)))
    return w.reshape(MLA_Q_RANK, MLA_HEADS * MLA_QK_PAD).astype(BF16)


def _split_w_ukv(w):
    w = w.reshape(MLA_KV_RANK, MLA_HEADS, 2, MLA_NOPE).transpose(0, 2, 1, 3)
    return w.reshape(MLA_KV_RANK, 2 * MLA_W).astype(BF16)


def _hyena(u3, layer, p, n, nc, with_ctx, tables):
    B = u3.shape[0]
    C = HY_CH
    P = B // 2
    g1d, g1f, g2, g2i, g3, gk, gd, gi = tables
    S = FFT_S
    N1 = 2 * n // S
    mlp = (p["hy_ffn_w1p"][layer], p["hy_ffn_b1"][layer][None], p["hy_ffn_w2"][layer], p["hy_ffn_b2"][layer][None],
           p["hy_ffn_w3"][layer], p["hy_ffn_b3"][layer][None], p["hy_ffn_w4"][layer], p["hy_freq"][layer][None])
    skip = p["hy_skip"][layer]
    cw, cb = p["hy_conv_w"][layer], p["hy_conv_b"][layer][None]

    kc, sums = _hy_filter(n, *mlp)
    kf = _fft_s1(g1f, kc.reshape(HY_ORDER, N1, S, C))
    kf = _fft_s2_filt(kf.reshape(HY_ORDER, 2, N1, S, C), g2, 1.0 / sums[:, 0:1, :])
    dw = [a.reshape(P, N1, S, C) for a in _hy_dwconv(u3, cw, cb, 0, n)]
    z = dw[0]
    for o in range(HY_ORDER):
        a = _fft_s1(g1d, z).reshape(P, 2, N1, S, C)
        b = _fft_s2(a, g2, g2i, kf, o).reshape(P, 2 * N1, S, C)
        z = _fft_s3(g3, b, z, dw[1 + o], skip[o][None], F32 if o + 1 < HY_ORDER else BF16)
    oc_lat = z.reshape(B, n, C)

    if with_ctx:
        kcc, sumc = _hy_filter(nc, *mlp)
        kfc = _dense_spec(kcc, gk, 1.0 / sumc[:, 0:1, :])
        dwc = [a.reshape(P, 2 * nc, C) for a in _hy_dwconv(u3, cw, cb, n // nc, nc)]
        zc = dwc[0]
        for o in range(HY_ORDER):
            zc = _dense_conv(zc, gd, gi, kfc, o, dwc[1 + o], skip[o][None], F32 if o + 1 < HY_ORDER else BF16)
        return jnp.concatenate([oc_lat, zc.reshape(B, nc, C)], axis=1)
    return oc_lat


def kernel(x, c, ctx, c_ctx, ada_w, ada_b, w_in, da_lambda, da_subln_g, mla_q_g, mla_w_uq, mla_kv_g, mla_w_ukv, hy_conv_w, hy_conv_b, hy_ffn_w1, hy_ffn_b1, hy_ffn_w2, hy_ffn_b2, hy_ffn_w3, hy_ffn_b3, hy_ffn_w4, hy_freq, hy_skip, w_branch_a, w_branch_b, w_branch_c, w_out, ln1_g, ln1_b, ffn_w_up, ffn_conv_w, ffn_conv_b, ffn_w_down, ln2_g, ln2_b):
    B, n, D = x.shape
    nc = ctx.shape[1]
    T = n + nc
    rows = B * T
    assert D == D_MODEL and B % 2 == 0 and B < 8
    assert n % ROW_GROUP == 0 and nc % ROW_GROUP == 0 and n % nc == 0 and n % GRID_W == 0
    assert (2 * n) % (8 * FFT_S) == 0 and T % ATTN_TK == 0 and nc % ATTN_TK == 0
    geom = (T // ROW_GROUP, n // ROW_GROUP, B)

    hy = dict(hy_ffn_w1p=jnp.pad(hy_ffn_w1, ((0, 0), (0, HY_FFN - HY_EMB), (0, 0))), hy_ffn_b1=hy_ffn_b1,
              hy_ffn_w2=hy_ffn_w2, hy_ffn_b2=hy_ffn_b2, hy_ffn_w3=hy_ffn_w3, hy_ffn_b3=hy_ffn_b3,
              hy_ffn_w4=hy_ffn_w4, hy_freq=hy_freq, hy_skip=hy_skip, hy_conv_w=hy_conv_w, hy_conv_b=hy_conv_b)
    tables = _dft_tables(n) + _dense_tables(nc)
    rope = _rope_tables(B, n, nc)
    keep_prev, keep_next = _conv_masks(B, n, nc)

    cc = jnp.concatenate([c, c_ctx[None], jnp.zeros((8 - B - 1, D), F32)], axis=0)
    mods = _ada(cc, ada_w, ada_b[:, None, :]).reshape(DEPTH, 8, 6, D)

    xs = jnp.concatenate([x, ctx], axis=1).reshape(rows, D)
    for i in range(DEPTH):
        last = i == DEPTH - 1
        lam_init = 0.8 - 0.6 * math.exp(-0.3 * i)
        lq1, lk1, lq2, lk2 = da_lambda[i].astype(F32)
        lam = (jnp.exp(jnp.sum(lq1 * lk1)) - jnp.exp(jnp.sum(lq2 * lk2)) + lam_init).reshape(1)
        mod = mods[i]
        w_qkv, w_mla, w_hy, w_g = _prep_w_in(w_in, i)

        if last:
            nb, tb = B, n
            sel = lambda b, g: b
        else:
            nb, tb = 1, rows
            sel = lambda b, g: _mod_row(g, *geom)

        h = _lnmod(xs, mod, geom)
        qkv = _qkv_proj(h, w_qkv, rope)
        p_mla = _matmul(h, w_mla, F32, w_mla.shape[1])
        u_hy = _matmul(h, w_hy, F32, 768, nb=nb, tb=tb)
        gates = _matmul(h, w_g, BF16, 1024, act="sigmoid", nb=nb, tb=tb)

        q_m, k_m, v_m = _mla_prep(p_mla, mla_q_g[i][None], mla_kv_g[i][None], _pad_w_uq(mla_w_uq[i]),
                                  _split_w_ukv(mla_w_ukv[i]), rope)
        oa = _diff_attn(qkv.reshape(B, T, 3 * DA_W), lam, da_subln_g[i][None], n, nc, not last, lam_init)
        ob = _mla_attn(q_m.reshape(B, T, -1), k_m.reshape(B, T, -1), v_m.reshape(B, T, -1), n, nc, not last)
        oc = _hyena(u_hy.reshape(B, -1, 3 * HY_CH), i, hy, n, nc, not last, tables)

        merged = _merge(oa.reshape(-1, DA_W), ob.reshape(-1, MLA_W), oc.reshape(-1, HY_CH),
                        w_branch_a[i].astype(BF16), w_branch_b[i].astype(BF16), w_branch_c[i].astype(BF16), gates,
                        nb, tb)
        xs1, h2 = _wo_ln(merged, w_out[i].astype(BF16), xs, mod, ln1_g[i][None], ln1_b[i][None], sel, nb, tb)
        u = _ffn_up(h2, ffn_w_up, i, ffn_conv_w[i], ffn_conv_b[i][None], keep_prev, keep_next, nb, tb)
        xs = _ffn_down(u, ffn_w_down[i].astype(BF16), xs1, mod, ln2_g[i][None], ln2_b[i][None], sel, nb, tb)
    return xs.reshape(B, n, D)
```

```python
import functools
import math

import numpy as np
import jax
import jax.numpy as jnp
from jax import lax
from jax.experimental import pallas as pl
from jax.experimental.pallas import tpu as pltpu

F32 = jnp.float32
BF16 = jnp.bfloat16

D_MODEL = 2048
DEPTH = 2
GRID_W = 64
ROPE_BASE = 10000.0
NORM_EPS = 1e-6
DA_HEADS = 6
DA_DQK = 64
DA_DV = 128
DA_W = DA_HEADS * DA_DV
MLA_HEADS = 6
MLA_Q_RANK = 512
MLA_KV_RANK = 256
MLA_NOPE = 128
MLA_ROPE = 64
MLA_DV = 128
MLA_W = MLA_HEADS * MLA_DV
MLA_QK_PAD = 256
HY_CH = 512
HY_ORDER = 2
HY_EMB = 33
HY_BANDS = (HY_EMB - 1) // 2
HY_FFN = 64
HY_MIN_DECAY = math.log(1e-2) / 1.5
HY_MAX_DECAY = math.log(1e-2) / 0.3
D_FF = 5632
N_BRANCH = 3
DEEPNORM_ALPHA = (2 * DEPTH) ** 0.25
LOG2E = 1.4426950408889634

ROW_GROUP = 256
LANE = 128
FFT_S = 128
VMEM_LIMIT = 52 * 1024 * 1024


def _cparams(*sem):
    return pltpu.CompilerParams(dimension_semantics=sem, vmem_limit_bytes=VMEM_LIMIT)


def _pick(total, prefs):
    for p in prefs:
        if total % p == 0:
            return p
    raise ValueError(f"no tile for {total} in {prefs}")


def _ln(x):
    mu = jnp.mean(x, axis=-1, keepdims=True)
    xc = x - mu
    var = jnp.mean(xc * xc, axis=-1, keepdims=True)
    return xc * lax.rsqrt(var + NORM_EPS)


def _rms(x):
    return x * lax.rsqrt(jnp.mean(x * x, axis=-1, keepdims=True) + NORM_EPS)


def _rope128(u, cos, sa, sb):
    return u * cos + pltpu.roll(u, 16, 1) * sa + pltpu.roll(u, LANE - 16, 1) * sb


def _ada_kernel(c_ref, w_ref, b_ref, o_ref):
    a = c_ref[...]
    a = a * jax.nn.sigmoid(a)
    o_ref[...] = jnp.dot(a.astype(BF16), w_ref[...].astype(BF16), preferred_element_type=F32) + b_ref[...]


def _ada(cc, ada_w, ada_b):
    L, D, N = ada_w.shape
    tn = 1024
    return pl.pallas_call(
        _ada_kernel,
        out_shape=jax.ShapeDtypeStruct((L, 8, N), F32),
        grid=(L, N // tn),
        in_specs=[pl.BlockSpec((8, D), lambda l, j: (0, 0)),
                  pl.BlockSpec((None, D, tn), lambda l, j: (l, 0, j)),
                  pl.BlockSpec((None, 1, tn), lambda l, j: (l, 0, j))],
        out_specs=pl.BlockSpec((None, 8, tn), lambda l, j: (l, 0, j)),
        compiler_params=_cparams("arbitrary", "arbitrary"),
    )(cc, ada_w, ada_b)


def _mod_row(g, gpb, lat_groups, n_batch):
    return jnp.where(g % gpb < lat_groups, g // gpb, n_batch)


def _mm_kernel(a_ref, w_ref, o_ref, *, act):
    acc = jnp.dot(a_ref[...], w_ref[...], preferred_element_type=F32)
    if act == "sigmoid":
        acc = jax.nn.sigmoid(acc)
    o_ref[...] = acc.astype(o_ref.dtype)


def _by_batch(a, nb):
    return a.reshape(nb, a.shape[0] // nb, a.shape[1])


def _matmul(a, w, out_dtype, tn, act=None, nb=1, tb=None):
    a = _by_batch(a, nb)
    K = a.shape[2]
    tb = tb or a.shape[1]
    N = w.shape[1]
    tm = _pick(tb, (1024, 512, 256))
    return pl.pallas_call(
        functools.partial(_mm_kernel, act=act),
        out_shape=jax.ShapeDtypeStruct((nb, tb, N), out_dtype),
        grid=(nb, tb // tm, N // tn),
        in_specs=[pl.BlockSpec((None, tm, K), lambda b, i, j: (b, i, 0)),
                  pl.BlockSpec((K, tn), lambda b, i, j: (0, j))],
        out_specs=pl.BlockSpec((None, tm, tn), lambda b, i, j: (b, i, j)),
        compiler_params=_cparams("arbitrary", "arbitrary", "arbitrary"),
    )(a, w).reshape(nb * tb, N)


def _qkv_kernel(x_ref, mod_ref, w_ref, cos_ref, sa_ref, sb_ref, o_ref, h_ref, *, qscale, geom, groups):
    j = pl.program_id(1)

    @pl.when(j == 0)
    def _():
        for q in range(groups):
            mod = mod_ref[_mod_row(pl.program_id(0) * groups + q, *geom)]
            sl = slice(q * ROW_GROUP, (q + 1) * ROW_GROUP)
            h_ref[sl, :] = (_ln(x_ref[sl, :]) * (1.0 + mod[1:2, :]) + mod[0:1, :]).astype(BF16)

    acc = jnp.dot(h_ref[...], w_ref[...], preferred_element_type=F32)

    @pl.when(j < 2)
    def _():
        cos, sa, sb = cos_ref[...], sa_ref[...], sb_ref[...]
        scale = jnp.where(j == 0, qscale, 1.0).astype(F32)
        for c in range(DA_HEADS):
            u = acc[:, c * LANE:(c + 1) * LANE]
            o_ref[:, c * LANE:(c + 1) * LANE] = (_rope128(u, cos, sa, sb) * scale).astype(BF16)

    @pl.when(j == 2)
    def _():
        o_ref[...] = acc.astype(BF16)


def _qkv_proj(xs, mod, w_qkv, tabs, geom):
    M, K = xs.shape
    tm = _pick(M, (1024, 512, 256))
    tn = DA_W
    row = lambda i, j: (i, 0)
    tab_spec = pl.BlockSpec((tm, LANE), row)
    return pl.pallas_call(
        functools.partial(_qkv_kernel, qscale=DA_DQK ** -0.5 * LOG2E, geom=geom, groups=tm // ROW_GROUP),
        out_shape=(jax.ShapeDtypeStruct((M, 3 * DA_W), BF16), jax.ShapeDtypeStruct((M, K), BF16)),
        grid=(M // tm, 3),
        in_specs=[pl.BlockSpec((tm, K), row),
                  pl.BlockSpec(mod.shape, lambda i, j: (0, 0, 0)),
                  pl.BlockSpec((K, tn), lambda i, j: (0, j)),
                  tab_spec, tab_spec, tab_spec],
        out_specs=(pl.BlockSpec((tm, tn), lambda i, j: (i, j)), pl.BlockSpec((tm, K), row)),
        compiler_params=_cparams("arbitrary", "arbitrary"),
    )(xs, mod, w_qkv, *tabs)


def _mla_prep_kernel(h_ref, wm_ref, qg_ref, kvg_ref, wuq_ref, wukv_ref, cos_ref, sa_ref, sb_ref,
                     q_ref, k_ref, v_ref, *, qscale):
    p = jnp.dot(h_ref[...], wm_ref[...], preferred_element_type=F32)
    cos, sa, sb = cos_ref[...], sa_ref[...], sb_ref[...]
    cq = p[:, :MLA_Q_RANK]
    ckv = p[:, MLA_Q_RANK:MLA_Q_RANK + MLA_KV_RANK]
    kr = p[:, MLA_Q_RANK + MLA_KV_RANK:]
    qn = (_rms(cq) * qg_ref[...]).astype(BF16)
    q = jnp.dot(qn, wuq_ref[...], preferred_element_type=F32)
    kvn = (_rms(ckv) * kvg_ref[...]).astype(BF16)
    kv = jnp.dot(kvn, wukv_ref[...], preferred_element_type=F32)
    krr = _rope128(kr, cos, sa, sb).astype(BF16)
    for h in range(MLA_HEADS):
        o = h * MLA_QK_PAD
        q_ref[:, o:o + LANE] = (q[:, o:o + LANE] * qscale).astype(BF16)
        q_ref[:, o + LANE:o + 2 * LANE] = (_rope128(q[:, o + LANE:o + 2 * LANE], cos, sa, sb) * qscale).astype(BF16)
        k_ref[:, o:o + LANE] = kv[:, h * LANE:(h + 1) * LANE].astype(BF16)
        k_ref[:, o + LANE:o + 2 * LANE] = krr
    v_ref[...] = kv[:, MLA_W:].astype(BF16)


def _mla_prep(h, w_mla, q_g, kv_g, w_uq, w_ukv, tabs):
    M, D = h.shape
    tm = _pick(M, (512, 256))
    row = lambda i: (i, 0)
    full = lambda i: (0, 0)
    qk_w = MLA_HEADS * MLA_QK_PAD
    return pl.pallas_call(
        functools.partial(_mla_prep_kernel, qscale=(MLA_NOPE + MLA_ROPE) ** -0.5 * LOG2E),
        out_shape=(jax.ShapeDtypeStruct((M, qk_w), BF16),
                   jax.ShapeDtypeStruct((M, qk_w), BF16),
                   jax.ShapeDtypeStruct((M, MLA_W), BF16)),
        grid=(M // tm,),
        in_specs=[pl.BlockSpec((tm, D), row),
                  pl.BlockSpec(w_mla.shape, full),
                  pl.BlockSpec((1, MLA_Q_RANK), full),
                  pl.BlockSpec((1, MLA_KV_RANK), full),
                  pl.BlockSpec(w_uq.shape, full),
                  pl.BlockSpec(w_ukv.shape, full),
                  pl.BlockSpec((tm, LANE), row), pl.BlockSpec((tm, LANE), row), pl.BlockSpec((tm, LANE), row)],
        out_specs=(pl.BlockSpec((tm, qk_w), row), pl.BlockSpec((tm, qk_w), row), pl.BlockSpec((tm, MLA_W), row)),
        compiler_params=_cparams("arbitrary"),
    )(h, w_mla, q_g, kv_g, w_uq, w_ukv, *tabs)


_NT = (((1,), (1,)), ((), ()))
ATTN_TQ = 256
ATTN_CHAINS = 8
ATTN_TK = 256


def _skewed_pipeline(n_chains, n_chunks, stages):
    for t in range(n_chains + len(stages) - 1):
        active = [(s, t - s) for s in range(len(stages)) if 0 <= t - s < n_chains]
        for c in range(n_chunks):
            for s, chain in active:
                stages[s][0](chain, c)
        for s, chain in active:
            stages[s][1](chain)


def _acc(old, new, op):
    return new if old is None else op(old, new)


def _lane_halves(x, op):
    return op(x[:, :LANE], x[:, LANE:])


def _attn_steps(run, q_rows, n_keys, lat_steps, n_lat, ctx_rows):
    if ctx_rows == 0:
        run(q_rows, 0, n_keys)
        return
    step = pl.program_id(2)

    @pl.when(step < lat_steps)
    def _():
        run(q_rows, 0, n_keys)

    @pl.when(step == lat_steps)
    def _():
        run(ctx_rows, n_lat, n_keys - n_lat)


def _diff_attn_kernel(lam_ref, q_ref, k_ref, v_ref, g_ref, o_ref, *, out_scale, lat_steps, n_lat, ctx_rows):
    run = functools.partial(_diff_attn_run, lam_ref, q_ref, k_ref, v_ref, g_ref, o_ref, out_scale)
    _attn_steps(run, q_ref.shape[0], k_ref.shape[0], lat_steps, n_lat, ctx_rows)


def _diff_attn_run(lam_ref, q_ref, k_ref, v_ref, g_ref, o_ref, out_scale, q_rows, key0, n_keys):
    lam, g = lam_ref[0], g_ref[...]
    tq = min(ATTN_TQ, q_rows)
    n_chains = q_rows // tq
    n_chunks = n_keys // ATTN_TK
    lane = lax.broadcasted_iota(jnp.int32, (1, LANE), 1)
    lo = (lane < DA_DQK).astype(F32)
    st = [dict(s=[], e=[], mx=[None, None], l=[None, None], o=None) for _ in range(n_chains)]
    rows = lambda i: slice(i * tq, (i + 1) * tq)
    keys = lambda c: slice(key0 + c * ATTN_TK, key0 + (c + 1) * ATTN_TK)

    def qk_chunk(i, c):
        d = st[i]
        if c == 0:
            qf = q_ref[rows(i), :].astype(F32)
            d["q"] = jnp.concatenate([(qf * lo).astype(BF16), (qf * (1.0 - lo)).astype(BF16)], axis=0)
        both = lax.dot_general(d["q"], k_ref[keys(c), :], _NT, preferred_element_type=F32)
        pair = []
        for m in range(2):
            s = both[m * tq:(m + 1) * tq]
            d["mx"][m] = _acc(d["mx"][m], _lane_halves(s, jnp.maximum), jnp.maximum)
            pair.append(s)
        d["s"].append(pair)

    def qk_done(i):
        st[i]["m"] = [jnp.max(mx, axis=-1, keepdims=True) for mx in st[i]["mx"]]

    def exp_chunk(i, c):
        d = st[i]
        pair = []
        for m in range(2):
            e = jnp.exp2(d["s"][c][m] - d["m"][m])
            d["l"][m] = _acc(d["l"][m], _lane_halves(e, jnp.add), jnp.add)
            pair.append(e)
        d["s"][c] = None
        d["e"].append(pair)

    def exp_done(i):
        d = st[i]
        l1, l2 = [jnp.sum(l, axis=-1, keepdims=True) for l in d["l"]]
        d["r1"] = 1.0 / l1
        d["cf"] = lam * l1 / l2

    def pv_chunk(i, c):
        d = st[i]
        w = (d["e"][c][0] - d["cf"] * d["e"][c][1]).astype(BF16)
        d["e"][c] = None
        d["o"] = _acc(d["o"], jnp.dot(w, v_ref[keys(c), :], preferred_element_type=F32), jnp.add)

    def pv_done(i):
        o = st[i]["o"] * st[i]["r1"]
        o_ref[rows(i), :] = (_rms(o) * g * out_scale).astype(BF16)

    _skewed_pipeline(n_chains, n_chunks, [(qk_chunk, qk_done), (exp_chunk, exp_done), (pv_chunk, pv_done)])


def _mla_attn_kernel(q_ref, k_ref, v_ref, o_ref, *, lat_steps, n_lat, ctx_rows):
    run = functools.partial(_mla_attn_run, q_ref, k_ref, v_ref, o_ref)
    _attn_steps(run, q_ref.shape[0], k_ref.shape[0], lat_steps, n_lat, ctx_rows)


def _mla_attn_run(q_ref, k_ref, v_ref, o_ref, q_rows, key0, n_keys):
    tq = min(ATTN_TQ, q_rows)
    n_chains = q_rows // tq
    n_chunks = n_keys // ATTN_TK
    st = [dict(s=[], mx=None, l=None, o=None) for _ in range(n_chains)]
    rows = lambda i: slice(i * tq, (i + 1) * tq)
    keys = lambda c: slice(key0 + c * ATTN_TK, key0 + (c + 1) * ATTN_TK)

    def qk_chunk(i, c):
        d = st[i]
        s = lax.dot_general(q_ref[rows(i), :], k_ref[keys(c), :], _NT, preferred_element_type=F32)
        d["mx"] = _acc(d["mx"], _lane_halves(s, jnp.maximum), jnp.maximum)
        d["s"].append(s)

    def qk_done(i):
        st[i]["m"] = jnp.max(st[i]["mx"], axis=-1, keepdims=True)

    def pv_chunk(i, c):
        d = st[i]
        e = jnp.exp2(d["s"][c] - d["m"])
        d["s"][c] = None
        d["l"] = _acc(d["l"], _lane_halves(e, jnp.add), jnp.add)
        d["o"] = _acc(d["o"], jnp.dot(e.astype(BF16), v_ref[keys(c), :], preferred_element_type=F32), jnp.add)

    def pv_done(i):
        d = st[i]
        o_ref[rows(i), :] = (d["o"] * (1.0 / jnp.sum(d["l"], axis=-1, keepdims=True))).astype(BF16)

    _skewed_pipeline(n_chains, n_chunks, [(qk_chunk, qk_done), (pv_chunk, pv_done)])


def _diff_attn(qkv, lam, subln_g, n, nc, ctx_queries, lam_init):
    B, T, _ = qkv.shape
    H = DA_HEADS
    tq = _pick(n, (ATTN_TQ * ATTN_CHAINS, ATTN_TQ))
    lat_steps = n // tq
    kern = functools.partial(_diff_attn_kernel, out_scale=1.0 - lam_init, lat_steps=lat_steps, n_lat=n,
                             ctx_rows=nc if ctx_queries else 0)
    return pl.pallas_call(
        kern,
        out_shape=jax.ShapeDtypeStruct((B, T if ctx_queries else n, DA_W), BF16),
        grid=(B, H, lat_steps + int(ctx_queries)),
        in_specs=[pl.BlockSpec(memory_space=pltpu.SMEM),
                  pl.BlockSpec((None, tq, LANE), lambda b, h, i: (b, i, h)),
                  pl.BlockSpec((None, T, LANE), lambda b, h, i: (b, 0, H + h)),
                  pl.BlockSpec((None, T, LANE), lambda b, h, i: (b, 0, 2 * H + h)),
                  pl.BlockSpec((1, DA_DV), lambda b, h, i: (0, 0))],
        out_specs=pl.BlockSpec((None, tq, LANE), lambda b, h, i: (b, i, h)),
        compiler_params=_cparams("arbitrary", "arbitrary", "arbitrary"),
    )(lam, qkv, qkv, qkv, subln_g)


def _mla_attn(q, k, v, n, nc, ctx_queries):
    B, T, _ = q.shape
    H = MLA_HEADS
    tq = _pick(n, (ATTN_TQ * ATTN_CHAINS, ATTN_TQ))
    lat_steps = n // tq
    kern = functools.partial(_mla_attn_kernel, lat_steps=lat_steps, n_lat=n, ctx_rows=nc if ctx_queries else 0)
    return pl.pallas_call(
        kern,
        out_shape=jax.ShapeDtypeStruct((B, T if ctx_queries else n, MLA_W), BF16),
        grid=(B, H, lat_steps + int(ctx_queries)),
        in_specs=[pl.BlockSpec((None, tq, MLA_QK_PAD), lambda b, h, i: (b, i, h)),
                  pl.BlockSpec((None, T, MLA_QK_PAD), lambda b, h, i: (b, 0, h)),
                  pl.BlockSpec((None, T, MLA_DV), lambda b, h, i: (b, 0, h))],
        out_specs=pl.BlockSpec((None, tq, MLA_DV), lambda b, h, i: (b, i, h)),
        compiler_params=_cparams("arbitrary", "arbitrary", "arbitrary"),
    )(q, k, v)


def _hy_filter_kernel(feat_ref, dec_ref, w1_ref, b1_ref, w2_ref, b2_ref, w3_ref, b3_ref, w4_ref, fr_ref,
                      k_ref, s_ref):
    hp = lax.Precision.HIGHEST
    d, r = pl.program_id(0), pl.program_id(1)
    fr = fr_ref[...]
    h = jnp.sin(fr * (jnp.dot(feat_ref[...], w1_ref[...], precision=hp, preferred_element_type=F32) + b1_ref[...]))
    h = jnp.sin(fr * (jnp.dot(h, w2_ref[...], precision=hp, preferred_element_type=F32) + b2_ref[...]))
    h = jnp.sin(fr * (jnp.dot(h, w3_ref[...], precision=hp, preferred_element_type=F32) + b3_ref[...]))
    h = jnp.dot(h, w4_ref[...], precision=hp, preferred_element_type=F32)
    row = lax.broadcasted_iota(jnp.int32, (h.shape[0], 1), 0)
    first_bwd = jnp.where((d == 1) & (r == 0), 1.0, 0.0)
    scale = dec_ref[...] * (1.0 - jnp.where(row == 0, 1.0, 0.0) * first_bwd)

    @pl.when((d == 0) & (r == 0))
    def _():
        s_ref[...] = jnp.zeros(s_ref.shape, F32)

    for o in range(HY_ORDER):
        ko = h[:, o * HY_CH:(o + 1) * HY_CH] * scale
        k_ref[o] = ko
        s_ref[o] += jnp.broadcast_to(jnp.sum(jnp.abs(ko), axis=0, keepdims=True), (8, HY_CH))


def _hy_filter(n, w1, b1, w2, b2, w3, b3, w4, fr):
    C = HY_CH
    t = jnp.linspace(0.0, 1.0, n, dtype=F32)
    pos = jnp.arange(n, dtype=F32)
    t2 = jnp.concatenate([t, t[::-1]])[:, None]
    pos2 = jnp.concatenate([pos, pos[::-1]])[:, None]
    phase = (2.0 * math.pi / n) * pos2 * jnp.linspace(1e-4, HY_BANDS - 1, HY_BANDS, dtype=F32)[None, :]
    feat = jnp.concatenate([t2, jnp.cos(phase), -jnp.sin(phase)], axis=-1)
    feat = jnp.pad(feat, ((0, 0), (0, HY_FFN - HY_EMB)))
    dec = jnp.exp(-t2 * jnp.abs(jnp.linspace(HY_MIN_DECAY, HY_MAX_DECAY, C, dtype=F32)))
    w4d = w4.reshape(HY_FFN, HY_ORDER, 2, C).transpose(2, 0, 1, 3).reshape(2, HY_FFN, HY_ORDER * C)
    rb = min(512, n)
    nb = n // rb
    full = lambda d, r: (0, 0)
    return pl.pallas_call(
        _hy_filter_kernel,
        out_shape=(jax.ShapeDtypeStruct((HY_ORDER, 2 * n, C), F32),
                   jax.ShapeDtypeStruct((HY_ORDER, 8, C), F32)),
        grid=(2, nb),
        in_specs=[pl.BlockSpec((rb, HY_FFN), lambda d, r: (d * nb + r, 0)),
                  pl.BlockSpec((rb, C), lambda d, r: (d * nb + r, 0)),
                  pl.BlockSpec(w1.shape, full), pl.BlockSpec(b1.shape, full),
                  pl.BlockSpec(w2.shape, full), pl.BlockSpec(b2.shape, full),
                  pl.BlockSpec(w3.shape, full), pl.BlockSpec(b3.shape, full),
                  pl.BlockSpec((None, HY_FFN, HY_ORDER * C), lambda d, r: (d, 0, 0)),
                  pl.BlockSpec(fr.shape, full)],
        out_specs=(pl.BlockSpec((HY_ORDER, rb, C), lambda d, r: (0, d * nb + r, 0)),
                   pl.BlockSpec((HY_ORDER, 8, C), lambda d, r: (0, 0, 0))),
        compiler_params=_cparams("arbitrary", "arbitrary"),
    )(feat, dec, w1, b1, w2, b2, w3, b3, w4d, fr)


def _hy_dwconv_kernel(*refs):
    for k in range(3):
        u_ref, w_ref, b_ref, o_ref = refs[3 * k], refs[3 * k + 1], refs[3 * k + 2], refs[9 + k]
        u = u_ref[...]
        L = u.shape[0]
        row = lax.broadcasted_iota(jnp.int32, (L, 1), 0)
        up = jnp.where(row == 0, 0.0, pltpu.roll(u, 1, 0))
        dn = jnp.where(row == L - 1, 0.0, pltpu.roll(u, L - 1, 0))
        o_ref[...] = up * w_ref[0:1, :] + u * w_ref[1:2, :] + dn * w_ref[2:3, :] + b_ref[...]


def _hy_dwconv(u, w, b, row_block, length):
    B = u.shape[0]
    cw = LANE
    per = HY_CH // cw
    in_specs, args = [], []
    for k in range(3):
        in_specs += [pl.BlockSpec((None, length, cw), lambda bb, j, k=k: (bb, row_block, k * per + j)),
                     pl.BlockSpec((3, cw), lambda bb, j, k=k: (0, k * per + j)),
                     pl.BlockSpec((1, cw), lambda bb, j, k=k: (0, k * per + j))]
        args += [u, w, b]
    ospec = pl.BlockSpec((None, length, cw), lambda bb, j: (bb, 0, j))
    return pl.pallas_call(
        _hy_dwconv_kernel,
        out_shape=(jax.ShapeDtypeStruct((B, length, HY_CH), F32),) * 3,
        grid=(B, per),
        in_specs=in_specs,
        out_specs=(ospec,) * 3,
        compiler_params=_cparams("arbitrary", "arbitrary"),
    )(*args)


def _dft_tables(n):
    S = FFT_S
    M = 2 * n
    N1 = M // S
    H = N1 // 2
    s2 = np.arange(S)[:, None, None]
    k1 = np.arange(N1)[None, :, None]
    s1 = np.arange(N1)[None, None, :]
    ang = -2.0 * np.pi * ((k1 * (S * s1 + s2)) % M) / M
    fr, fi = np.cos(ang), np.sin(ang)
    g1f = np.concatenate([fr, fi], axis=1)
    frh, fih = fr[:, :, :H], fi[:, :, :H]
    g1d = np.concatenate([np.concatenate([frh, -fih], axis=2),
                          np.concatenate([fih, frh], axis=2)], axis=1)
    er = np.transpose(frh, (0, 2, 1)) / M
    ei = -np.transpose(fih, (0, 2, 1)) / M
    g3 = np.concatenate([np.concatenate([er, -ei], axis=2),
                         np.concatenate([ei, er], axis=2)], axis=1)
    a2 = -2.0 * np.pi * ((np.arange(S)[:, None] * np.arange(S)[None, :]) % S) / S
    f2r, f2i = np.cos(a2), np.sin(a2)
    g2 = np.block([[f2r, -f2i], [f2i, f2r]])
    g2i = np.block([[f2r, f2i], [-f2i, f2r]])
    cast = lambda a: jnp.asarray(a, dtype=F32).astype(BF16)
    return cast(g1d), cast(g1f), cast(g2), cast(g2i), cast(g3)


FFT_NS = 16


def _fft_s1_kernel(g_ref, x_ref, o_ref):
    xt = jnp.swapaxes(x_ref[...], 0, 1)
    y = jnp.stack([jnp.dot(g_ref[j], xt[j].astype(BF16), preferred_element_type=F32)
                   for j in range(xt.shape[0])], axis=0)
    o_ref[...] = jnp.swapaxes(y, 0, 1).astype(o_ref.dtype)


def _fft_s1(g, x):
    P, Ri, S, C = x.shape
    Ro = g.shape[1]
    ns = FFT_NS
    return pl.pallas_call(
        _fft_s1_kernel,
        out_shape=jax.ShapeDtypeStruct((P, Ro, S, C), BF16),
        grid=(S // ns, P),
        in_specs=[pl.BlockSpec((ns, Ro, Ri), lambda j, p: (j, 0, 0)),
                  pl.BlockSpec((None, Ri, ns, C), lambda j, p: (p, 0, j, 0))],
        out_specs=pl.BlockSpec((None, Ro, ns, C), lambda j, p: (p, 0, j, 0)),
        compiler_params=_cparams("arbitrary", "arbitrary"),
    )(g, x)


def _fft_s2_filt_kernel(a_ref, g_ref, rn_ref, o_ref, *, nk):
    S = FFT_S
    rn = rn_ref[...]
    for t in range(nk):
        d = jnp.concatenate([a_ref[0, t], a_ref[1, t]], axis=0)
        y = jnp.dot(g_ref[...], d, preferred_element_type=F32)
        o_ref[0, t] = y[:S] * rn
        o_ref[1, t] = y[S:] * rn


def _fft_s2_filt(a, g2, rnorm):
    O, _, N1, S, C = a.shape
    nk = 4
    blk = (None, 2, nk, S, C)
    return pl.pallas_call(
        functools.partial(_fft_s2_filt_kernel, nk=nk),
        out_shape=jax.ShapeDtypeStruct(a.shape, F32),
        grid=(O, N1 // nk),
        in_specs=[pl.BlockSpec(blk, lambda o, j: (o, 0, j, 0, 0)),
                  pl.BlockSpec(g2.shape, lambda o, j: (0, 0)),
                  pl.BlockSpec((None, 1, C), lambda o, j: (o, 0, 0))],
        out_specs=pl.BlockSpec(blk, lambda o, j: (o, 0, j, 0, 0)),
        compiler_params=_cparams("arbitrary", "arbitrary"),
    )(a, g2, rnorm)


def _fft_s2_kernel(a_ref, g_ref, gi_ref, kf_ref, o_ref, *, nk):
    S = FFT_S
    for t in range(nk):
        d = jnp.concatenate([a_ref[0, t], a_ref[1, t]], axis=0)
        y = jnp.dot(g_ref[...], d, preferred_element_type=F32)
        yr, yi = y[:S], y[S:]
        kr, ki = kf_ref[0, t], kf_ref[1, t]
        p = jnp.concatenate([yr * kr - yi * ki, yr * ki + yi * kr], axis=0).astype(BF16)
        b = jnp.dot(gi_ref[...], p, preferred_element_type=F32)
        o_ref[0, t] = b[:S].astype(BF16)
        o_ref[1, t] = b[S:].astype(BF16)


def _fft_s2(a, g2, g2i, kf, order):
    P, _, N1, S, C = a.shape
    nk = 4
    blk = (None, 2, nk, S, C)
    return pl.pallas_call(
        functools.partial(_fft_s2_kernel, nk=nk),
        out_shape=jax.ShapeDtypeStruct(a.shape, BF16),
        grid=(N1 // nk, P),
        in_specs=[pl.BlockSpec(blk, lambda j, p: (p, 0, j, 0, 0)),
                  pl.BlockSpec(g2.shape, lambda j, p: (0, 0)),
                  pl.BlockSpec(g2i.shape, lambda j, p: (0, 0)),
                  pl.BlockSpec(blk, lambda j, p: (order, 0, j, 0, 0))],
        out_specs=pl.BlockSpec(blk, lambda j, p: (p, 0, j, 0, 0)),
        compiler_params=_cparams("arbitrary", "arbitrary"),
    )(a, g2, g2i, kf)


def _fft_s3_kernel(g_ref, b_ref, z_ref, gate_ref, skip_ref, o_ref):
    bt = jnp.swapaxes(b_ref[...].astype(F32), 0, 1).astype(BF16)
    y = jnp.stack([jnp.dot(g_ref[j], bt[j], preferred_element_type=F32) for j in range(bt.shape[0])], axis=0)
    y = jnp.swapaxes(y, 0, 1)
    o_ref[...] = (gate_ref[...] * (y + skip_ref[...] * z_ref[...])).astype(o_ref.dtype)


def _fft_s3(g3, b, z, gate, skip, out_dtype):
    P, Ri, S, C = b.shape
    Ro = g3.shape[1]
    ns = FFT_NS
    dspec = pl.BlockSpec((None, Ro, ns, C), lambda j, p: (p, 0, j, 0))
    return pl.pallas_call(
        _fft_s3_kernel,
        out_shape=jax.ShapeDtypeStruct((P, Ro, S, C), out_dtype),
        grid=(S // ns, P),
        in_specs=[pl.BlockSpec((ns, Ro, Ri), lambda j, p: (j, 0, 0)),
                  pl.BlockSpec((None, Ri, ns, C), lambda j, p: (p, 0, j, 0)),
                  dspec, dspec,
                  pl.BlockSpec((1, C), lambda j, p: (0, 0))],
        out_specs=dspec,
        compiler_params=_cparams("arbitrary", "arbitrary"),
    )(g3, b, z, gate, skip)


def _dense_tables(m):
    M = 2 * m
    ang = -2.0 * np.pi * ((np.arange(M)[:, None] * np.arange(M)[None, :]) % M) / M
    fr, fi = np.cos(ang), np.sin(ang)
    gk = np.concatenate([fr, fi], axis=0)
    gd = np.block([[fr[:, :m], -fi[:, :m]], [fi[:, :m], fr[:, :m]]])
    er, ei = fr[:m, :] / M, -fi[:m, :] / M
    gi = np.block([[er, -ei], [ei, er]])
    cast = lambda a: jnp.asarray(a, dtype=F32).astype(BF16)
    return cast(gk), cast(gd), cast(gi)


def _dense_spec_kernel(k_ref, g_ref, rn_ref, o_ref):
    o_ref[...] = jnp.dot(g_ref[...], k_ref[...].astype(BF16), preferred_element_type=F32) * rn_ref[...]


def _dense_spec(kc, gk, rnorm):
    O, M, C = kc.shape
    return pl.pallas_call(
        _dense_spec_kernel,
        out_shape=jax.ShapeDtypeStruct((O, 2 * M, C), F32),
        grid=(O,),
        in_specs=[pl.BlockSpec((None, M, C), lambda o: (o, 0, 0)),
                  pl.BlockSpec(gk.shape, lambda o: (0, 0)),
                  pl.BlockSpec((None, 1, C), lambda o: (o, 0, 0))],
        out_specs=pl.BlockSpec((None, 2 * M, C), lambda o: (o, 0, 0)),
        compiler_params=_cparams("arbitrary"),
    )(kc, gk, rnorm)


def _dense_conv_kernel(x_ref, gd_ref, gi_ref, kf_ref, gate_ref, skip_ref, o_ref):
    x = x_ref[...]
    y = jnp.dot(gd_ref[...], x.astype(BF16), preferred_element_type=F32)
    M = y.shape[0] // 2
    yr, yi = y[:M], y[M:]
    kr, ki = kf_ref[:M], kf_ref[M:]
    p = jnp.concatenate([yr * kr - yi * ki, yr * ki + yi * kr], axis=0).astype(BF16)
    conv = jnp.dot(gi_ref[...], p, preferred_element_type=F32)
    o_ref[...] = (gate_ref[...] * (conv + skip_ref[...] * x)).astype(o_ref.dtype)


def _dense_conv(x, gd, gi, kf, order, gate, skip, out_dtype):
    P, R, C = x.shape
    dspec = pl.BlockSpec((None, R, C), lambda p: (p, 0, 0))
    return pl.pallas_call(
        _dense_conv_kernel,
        out_shape=jax.ShapeDtypeStruct((P, R, C), out_dtype),
        grid=(P,),
        in_specs=[dspec,
                  pl.BlockSpec(gd.shape, lambda p: (0, 0)),
                  pl.BlockSpec(gi.shape, lambda p: (0, 0)),
                  pl.BlockSpec((None,) + kf.shape[1:], lambda p: (order, 0, 0)),
                  dspec,
                  pl.BlockSpec((1, C), lambda p: (0, 0))],
        out_specs=dspec,
        compiler_params=_cparams("arbitrary"),
    )(x, gd, gi, kf, gate, skip)


def _merge_kernel(oa_ref, ob_ref, oc_ref, wa_ref, wb_ref, wc_ref, g0_ref, g1_ref, g2_ref, o_ref):
    ya = jnp.dot(oa_ref[...], wa_ref[...], preferred_element_type=F32)
    yb = jnp.dot(ob_ref[...], wb_ref[...], preferred_element_type=F32)
    yc = jnp.dot(oc_ref[...], wc_ref[...], preferred_element_type=F32)
    m = g0_ref[...].astype(F32) * ya + g1_ref[...].astype(F32) * yb + g2_ref[...].astype(F32) * yc
    o_ref[...] = m.astype(BF16)


def _merge(oa, ob, oc, w_ba, w_bb, w_bc, gates, nb, tb):
    D = D_MODEL
    tm = _pick(tb, (1024, 512, 256))
    tn = 512
    nt = D // tn
    oa, ob, oc, gates = (_by_batch(a, nb) for a in (oa, ob, oc, gates))
    row = lambda b, i, j: (b, i, 0)
    col = lambda b, i, j: (0, j)
    return pl.pallas_call(
        _merge_kernel,
        out_shape=jax.ShapeDtypeStruct((nb, tb, D), BF16),
        grid=(nb, tb // tm, nt),
        in_specs=[pl.BlockSpec((None, tm, DA_W), row), pl.BlockSpec((None, tm, MLA_W), row),
                  pl.BlockSpec((None, tm, HY_CH), row),
                  pl.BlockSpec((DA_W, tn), col), pl.BlockSpec((MLA_W, tn), col), pl.BlockSpec((HY_CH, tn), col),
                  pl.BlockSpec((None, tm, tn), lambda b, i, j: (b, i, j)),
                  pl.BlockSpec((None, tm, tn), lambda b, i, j: (b, i, nt + j)),
                  pl.BlockSpec((None, tm, tn), lambda b, i, j: (b, i, 2 * nt + j))],
        out_specs=pl.BlockSpec((None, tm, tn), lambda b, i, j: (b, i, j)),
        compiler_params=_cparams("arbitrary", "arbitrary", "arbitrary"),
    )(oa, ob, oc, w_ba, w_bb, w_bc, gates, gates, gates).reshape(nb * tb, D)


def _wo_ln_kernel(m_ref, w_ref, xs_ref, mod_ref, g_ref, b_ref, xs1_ref, h2_ref, *, sel, groups):
    y = jnp.dot(m_ref[...], w_ref[...], preferred_element_type=F32)
    g, b = g_ref[...], b_ref[...]
    for q in range(groups):
        mod = mod_ref[sel(pl.program_id(0), pl.program_id(1) * groups + q)]
        sl = slice(q * ROW_GROUP, (q + 1) * ROW_GROUP)
        x1 = _ln(DEEPNORM_ALPHA * xs_ref[sl, :] + mod[2:3, :] * y[sl, :]) * g + b
        xs1_ref[sl, :] = x1
        h2_ref[sl, :] = (_ln(x1) * (1.0 + mod[4:5, :]) + mod[3:4, :]).astype(BF16)


def _wo_ln(merged, w_o, xs, mod, ln_g, ln_b, sel, nb, tb):
    D = D_MODEL
    tm = _pick(tb, (512, 256))
    merged, xs = _by_batch(merged, nb), _by_batch(xs, nb)
    row = lambda b, i: (b, i, 0)
    full2 = lambda b, i: (0, 0)
    xs1, h2 = pl.pallas_call(
        functools.partial(_wo_ln_kernel, sel=sel, groups=tm // ROW_GROUP),
        out_shape=(jax.ShapeDtypeStruct((nb, tb, D), F32), jax.ShapeDtypeStruct((nb, tb, D), BF16)),
        grid=(nb, tb // tm),
        in_specs=[pl.BlockSpec((None, tm, D), row),
                  pl.BlockSpec((D, D), full2),
                  pl.BlockSpec((None, tm, D), row),
                  pl.BlockSpec(mod.shape, lambda b, i: (0, 0, 0)),
                  pl.BlockSpec((1, D), full2), pl.BlockSpec((1, D), full2)],
        out_specs=(pl.BlockSpec((None, tm, D), row), pl.BlockSpec((None, tm, D), row)),
        compiler_params=_cparams("arbitrary", "arbitrary"),
    )(merged, w_o, xs, mod, ln_g, ln_b)
    return xs1.reshape(nb * tb, D), h2.reshape(nb * tb, D)


FFN_HALO = 16


def _ffn_up_kernel(hp_ref, h_ref, hn_ref, wa_ref, wv_ref, cw_ref, cb_ref, kp_ref, kn_ref, o_ref, wa_s, wv_s):
    tm, tn = o_ref.shape

    @pl.when((pl.program_id(1) == 0) & (pl.program_id(2) == 0))
    def _():
        wa_s[...] = wa_ref[...].astype(BF16)
        wv_s[...] = wv_ref[...].astype(BF16)

    hm = h_ref[...]
    hext = jnp.concatenate([hp_ref[...], hm, hn_ref[...]], axis=0)
    a = jnp.dot(hext, wa_s[...], preferred_element_type=F32)
    v = jnp.dot(hm, wv_s[...], preferred_element_type=F32)
    ext = tm + 2 * FFN_HALO
    rep = tn // LANE
    keep_prev = jnp.tile(kp_ref[...], (1, rep))
    keep_next = jnp.tile(kn_ref[...], (1, rep))
    a_prev = pltpu.roll(a, 1, 0)[FFN_HALO:FFN_HALO + tm] * keep_prev
    a_next = pltpu.roll(a, ext - 1, 0)[FFN_HALO:FFN_HALO + tm] * keep_next
    cv = a_prev * cw_ref[0:1, :] + a[FFN_HALO:FFN_HALO + tm] * cw_ref[1:2, :] + a_next * cw_ref[2:3, :] + cb_ref[...]
    o_ref[...] = (cv * jax.nn.sigmoid(cv) * v).astype(BF16)


def _ffn_up(h2, w_up, layer, conv_w, conv_b, keep_prev, keep_next, nb, tb):
    D = D_MODEL
    tm = _pick(tb, (1024, 512, 256))
    tn = 512
    nt = D_FF // tn
    hb = tm // FFN_HALO
    last = tb // FFN_HALO - 1
    h2, keep_prev, keep_next = (_by_batch(a, nb) for a in (h2, keep_prev, keep_next))
    mask_spec = pl.BlockSpec((None, tm, LANE), lambda j, b, i: (b, i, 0))
    return pl.pallas_call(
        _ffn_up_kernel,
        out_shape=jax.ShapeDtypeStruct((nb, tb, D_FF), BF16),
        grid=(nt, nb, tb // tm),
        in_specs=[pl.BlockSpec((None, FFN_HALO, D), lambda j, b, i: (b, jnp.maximum(i * hb - 1, 0), 0)),
                  pl.BlockSpec((None, tm, D), lambda j, b, i: (b, i, 0)),
                  pl.BlockSpec((None, FFN_HALO, D), lambda j, b, i: (b, jnp.minimum((i + 1) * hb, last), 0)),
                  pl.BlockSpec((None, D, tn), lambda j, b, i: (layer, 0, j)),
                  pl.BlockSpec((None, D, tn), lambda j, b, i: (layer, 0, nt + j)),
                  pl.BlockSpec((3, tn), lambda j, b, i: (0, j)),
                  pl.BlockSpec((1, tn), lambda j, b, i: (0, j)),
                  mask_spec, mask_spec],
        out_specs=pl.BlockSpec((None, tm, tn), lambda j, b, i: (b, i, j)),
        scratch_shapes=[pltpu.VMEM((D, tn), BF16), pltpu.VMEM((D, tn), BF16)],
        compiler_params=_cparams("arbitrary", "arbitrary", "arbitrary"),
    )(h2, h2, h2, w_up, w_up, conv_w, conv_b, keep_prev, keep_next).reshape(nb * tb, D_FF)


def _ffn_down_kernel(u_ref, w_ref, xs_ref, mod_ref, g_ref, b_ref, o_ref, acc_ref, *, sel, groups, nk):
    k = pl.program_id(2)

    @pl.when(k == 0)
    def _():
        acc_ref[...] = jnp.zeros(acc_ref.shape, F32)

    acc_ref[...] += jnp.dot(u_ref[...], w_ref[...], preferred_element_type=F32)

    @pl.when(k == nk - 1)
    def _():
        g, b = g_ref[...], b_ref[...]
        for q in range(groups):
            mod = mod_ref[sel(pl.program_id(0), pl.program_id(1) * groups + q)]
            sl = slice(q * ROW_GROUP, (q + 1) * ROW_GROUP)
            o_ref[sl, :] = _ln(DEEPNORM_ALPHA * xs_ref[sl, :] + mod[5:6, :] * acc_ref[sl, :]) * g + b


def _ffn_down(u, w_down, xs1, mod, ln_g, ln_b, sel, nb, tb):
    D = D_MODEL
    tm = _pick(tb, (512, 256))
    tk = D_FF // 2
    nk = D_FF // tk
    u, xs1 = _by_batch(u, nb), _by_batch(xs1, nb)
    row = lambda b, i, k: (b, i, 0)
    full2 = lambda b, i, k: (0, 0)
    return pl.pallas_call(
        functools.partial(_ffn_down_kernel, sel=sel, groups=tm // ROW_GROUP, nk=nk),
        out_shape=jax.ShapeDtypeStruct((nb, tb, D), F32),
        grid=(nb, tb // tm, nk),
        in_specs=[pl.BlockSpec((None, tm, tk), lambda b, i, k: (b, i, k)),
                  pl.BlockSpec((tk, D), lambda b, i, k: (k, 0)),
                  pl.BlockSpec((None, tm, D), row),
                  pl.BlockSpec(mod.shape, lambda b, i, k: (0, 0, 0)),
                  pl.BlockSpec((1, D), full2), pl.BlockSpec((1, D), full2)],
        out_specs=pl.BlockSpec((None, tm, D), row),
        scratch_shapes=[pltpu.VMEM((tm, D), F32)],
        compiler_params=_cparams("arbitrary", "arbitrary", "arbitrary"),
    )(u, w_down, xs1, mod, ln_g, ln_b).reshape(nb * tb, D)


def _rope_tables(B, n, nc):
    half = DA_DQK // 2
    inv = ROPE_BASE ** (-jnp.arange(0, half, 2, dtype=F32) / half)
    t = jnp.arange(n, dtype=jnp.int32)
    ang_r = (t // GRID_W).astype(F32)[:, None] * inv[None, :]
    ang_c = (t % GRID_W).astype(F32)[:, None] * inv[None, :]
    ang = jnp.concatenate([ang_r, ang_r, ang_c, ang_c], axis=-1)
    cos, sin = jnp.cos(ang), jnp.sin(ang)
    upper = (jnp.arange(DA_DQK) % half) >= half // 2
    sa = jnp.where(upper, sin, 0.0)
    sb = jnp.where(upper, 0.0, -sin)

    def full(tab, fill):
        tab = jnp.concatenate([tab, jnp.full((nc, DA_DQK), fill, F32)], axis=0)
        tab = jnp.tile(tab, (B, LANE // DA_DQK))
        return tab

    return full(cos, 1.0), full(sa, 0.0), full(sb, 0.0)


def _conv_masks(B, n, nc):
    T = n + nc
    t = jnp.arange(T)
    keep_prev = ((t != 0) & (t != n)).astype(F32)
    keep_next = ((t != n - 1) & (t != T - 1)).astype(F32)
    widen = lambda m: jnp.tile(m[:, None], (B, LANE))
    return widen(keep_prev), widen(keep_next)


W_IN_WIDTHS = (DA_W, DA_W, DA_W, MLA_Q_RANK, MLA_KV_RANK, MLA_ROPE, 3 * HY_CH, N_BRANCH * D_MODEL)
W_IN_OFFS = tuple(sum(W_IN_WIDTHS[:j]) for j in range(len(W_IN_WIDTHS)))
W_MLA_COLS = MLA_Q_RANK + MLA_KV_RANK + LANE


W_HALF = LANE // 2


def _prep_w_in_kernel(a_ref, b_ref, o_ref, *, zero_b_at):
    b = b_ref[...]
    if zero_b_at is not None:
        b = jnp.where(pl.program_id(0) == zero_b_at, 0.0, b)
    o_ref[...] = jnp.transpose(jnp.concatenate([a_ref[...], b], axis=0)).astype(BF16)


def _prep_group(w_t, layer, n_blocks, src, zero_b_at=None):
    D = w_t.shape[2]
    return pl.pallas_call(
        functools.partial(_prep_w_in_kernel, zero_b_at=zero_b_at),
        out_shape=jax.ShapeDtypeStruct((D, n_blocks * LANE), BF16),
        grid=(n_blocks,),
        in_specs=[pl.BlockSpec((None, W_HALF, D), lambda c: (layer, src(c)[0], 0)),
                  pl.BlockSpec((None, W_HALF, D), lambda c: (layer, src(c)[1], 0))],
        out_specs=pl.BlockSpec((D, LANE), lambda c: (0, c)),
        compiler_params=_cparams("arbitrary"),
    )(w_t, w_t)


def _prep_w_in(w_in, layer):
    assert w_in.shape[2] == sum(W_IN_WIDTHS) and all(o % W_HALF == 0 for o in W_IN_OFFS)
    w_t = jnp.swapaxes(w_in, 1, 2)
    o_q, o_k, o_v, o_cq, _, o_kr, o_hy, o_g = (o // W_HALF for o in W_IN_OFFS)
    H = DA_HEADS

    def src_qkv(c):
        part, h = c // H, c % H
        a = jnp.where(part < 2, part * 2 * H + h, o_v + 2 * h)
        return a, jnp.where(part < 2, a + H, a + 1)

    pairs = lambda first: (lambda c: (first + 2 * c, first + 2 * c + 1))
    n_mla = W_MLA_COLS // LANE
    src_mla = lambda c: (o_cq + 2 * c, jnp.minimum(o_cq + 2 * c + 1, o_kr))
    return (_prep_group(w_t, layer, 3 * H, src_qkv),
            _prep_group(w_t, layer, n_mla, src_mla, zero_b_at=n_mla - 1),
            _prep_group(w_t, layer, 3 * HY_CH // LANE, pairs(o_hy)),
            _prep_group(w_t, layer, N_BRANCH * D_MODEL // LANE, pairs(o_g)))


def _pad_w_uq(w):
    w = w.reshape(MLA_Q_RANK, MLA_HEADS, MLA_NOPE + MLA_ROPE)
    w = jnp.pad(w, ((0, 0), (0, 0), (0, MLA_QK_PAD - MLA_NOPE - MLA_ROPE)))
    return w.reshape(MLA_Q_RANK, MLA_HEADS * MLA_QK_PAD).astype(BF16)


def _split_w_ukv(w):
    w = w.reshape(MLA_KV_RANK, MLA_HEADS, 2, MLA_NOPE).transpose(0, 2, 1, 3)
    return w.reshape(MLA_KV_RANK, 2 * MLA_W).astype(BF16)


def _hyena(u3, layer, p, n, nc, with_ctx, tables):
    B = u3.shape[0]
    C = HY_CH
    P = B // 2
    g1d, g1f, g2, g2i, g3, gk, gd, gi = tables
    S = FFT_S
    N1 = 2 * n // S
    mlp = (p["hy_ffn_w1p"][layer], p["hy_ffn_b1"][layer][None], p["hy_ffn_w2"][layer], p["hy_ffn_b2"][layer][None],
           p["hy_ffn_w3"][layer], p["hy_ffn_b3"][layer][None], p["hy_ffn_w4"][layer], p["hy_freq"][layer][None])
    skip = p["hy_skip"][layer]
    cw, cb = p["hy_conv_w"][layer], p["hy_conv_b"][layer][None]

    kc, sums = _hy_filter(n, *mlp)
    kf = _fft_s1(g1f, kc.reshape(HY_ORDER, N1, S, C))
    kf = _fft_s2_filt(kf.reshape(HY_ORDER, 2, N1, S, C), g2, 1.0 / sums[:, 0:1, :])
    dw = [a.reshape(P, N1, S, C) for a in _hy_dwconv(u3, cw, cb, 0, n)]
    z = dw[0]
    for o in range(HY_ORDER):
        a = _fft_s1(g1d, z).reshape(P, 2, N1, S, C)
        b = _fft_s2(a, g2, g2i, kf, o).reshape(P, 2 * N1, S, C)
        z = _fft_s3(g3, b, z, dw[1 + o], skip[o][None], F32 if o + 1 < HY_ORDER else BF16)
    oc_lat = z.reshape(B, n, C)

    if with_ctx:
        kcc, sumc = _hy_filter(nc, *mlp)
        kfc = _dense_spec(kcc, gk, 1.0 / sumc[:, 0:1, :])
        dwc = [a.reshape(P, 2 * nc, C) for a in _hy_dwconv(u3, cw, cb, n // nc, nc)]
        zc = dwc[0]
        for o in range(HY_ORDER):
            zc = _dense_conv(zc, gd, gi, kfc, o, dwc[1 + o], skip[o][None], F32 if o + 1 < HY_ORDER else BF16)
        return jnp.concatenate([oc_lat, zc.reshape(B, nc, C)], axis=1)
    return oc_lat


def kernel(x, c, ctx, c_ctx, ada_w, ada_b, w_in, da_lambda, da_subln_g, mla_q_g, mla_w_uq, mla_kv_g, mla_w_ukv, hy_conv_w, hy_conv_b, hy_ffn_w1, hy_ffn_b1, hy_ffn_w2, hy_ffn_b2, hy_ffn_w3, hy_ffn_b3, hy_ffn_w4, hy_freq, hy_skip, w_branch_a, w_branch_b, w_branch_c, w_out, ln1_g, ln1_b, ffn_w_up, ffn_conv_w, ffn_conv_b, ffn_w_down, ln2_g, ln2_b):
    B, n, D = x.shape
    nc = ctx.shape[1]
    T = n + nc
    rows = B * T
    assert D == D_MODEL and B % 2 == 0 and B < 8
    assert n % ROW_GROUP == 0 and nc % ROW_GROUP == 0 and n % nc == 0 and n % GRID_W == 0
    assert (2 * n) % (8 * FFT_S) == 0 and T % ATTN_TK == 0 and nc % ATTN_TK == 0
    geom = (T // ROW_GROUP, n // ROW_GROUP, B)

    hy = dict(hy_ffn_w1p=jnp.pad(hy_ffn_w1, ((0, 0), (0, HY_FFN - HY_EMB), (0, 0))), hy_ffn_b1=hy_ffn_b1,
              hy_ffn_w2=hy_ffn_w2, hy_ffn_b2=hy_ffn_b2, hy_ffn_w3=hy_ffn_w3, hy_ffn_b3=hy_ffn_b3,
              hy_ffn_w4=hy_ffn_w4, hy_freq=hy_freq, hy_skip=hy_skip, hy_conv_w=hy_conv_w, hy_conv_b=hy_conv_b)
    tables = _dft_tables(n) + _dense_tables(nc)
    rope = _rope_tables(B, n, nc)
    keep_prev, keep_next = _conv_masks(B, n, nc)

    cc = jnp.concatenate([c, c_ctx[None], jnp.zeros((8 - B - 1, D), F32)], axis=0)
    mods = _ada(cc, ada_w, ada_b[:, None, :]).reshape(DEPTH, 8, 6, D)

    xs = jnp.concatenate([x, ctx], axis=1).reshape(rows, D)
    for i in range(DEPTH):
        last = i == DEPTH - 1
        lam_init = 0.8 - 0.6 * math.exp(-0.3 * i)
        lq1, lk1, lq2, lk2 = da_lambda[i].astype(F32)
        lam = (jnp.exp(jnp.sum(lq1 * lk1)) - jnp.exp(jnp.sum(lq2 * lk2)) + lam_init).reshape(1)
        mod = mods[i]
        w_qkv, w_mla, w_hy, w_g = _prep_w_in(w_in, i)

        if last:
            nb, tb = B, n
            sel = lambda b, g: b
        else:
            nb, tb = 1, rows
            sel = lambda b, g: _mod_row(g, *geom)

        qkv, h = _qkv_proj(xs, mod, w_qkv, rope, geom)
        u_hy = _matmul(h, w_hy, F32, 768, nb=nb, tb=tb)
        gates = _matmul(h, w_g, BF16, 1024, act="sigmoid", nb=nb, tb=tb)

        q_m, k_m, v_m = _mla_prep(h, w_mla, mla_q_g[i][None], mla_kv_g[i][None], _pad_w_uq(mla_w_uq[i]),
                                  _split_w_ukv(mla_w_ukv[i]), rope)
        oa = _diff_attn(qkv.reshape(B, T, 3 * DA_W), lam, da_subln_g[i][None], n, nc, not last, lam_init)
        ob = _mla_attn(q_m.reshape(B, T, -1), k_m.reshape(B, T, -1), v_m.reshape(B, T, -1), n, nc, not last)
        oc = _hyena(u_hy.reshape(B, -1, 3 * HY_CH), i, hy, n, nc, not last, tables)

        merged = _merge(oa.reshape(-1, DA_W), ob.reshape(-1, MLA_W), oc.reshape(-1, HY_CH),
                        w_branch_a[i].astype(BF16), w_branch_b[i].astype(BF16), w_branch_c[i].astype(BF16), gates,
                        nb, tb)
        xs1, h2 = _wo_ln(merged, w_out[i].astype(BF16), xs, mod, ln1_g[i][None], ln1_b[i][None], sel, nb, tb)
        u = _ffn_up(h2, ffn_w_up, i, ffn_conv_w[i], ffn_conv_b[i][None], keep_prev, keep_next, nb, tb)
        xs = _ffn_down(u, ffn_w_down[i].astype(BF16), xs1, mod, ln2_g[i][None], ln2_b[i][None], sel, nb, tb)
    return xs.reshape(B, n, D)
```

```python
import functools
import math

import numpy as np
import jax
import jax.numpy as jnp
from jax import lax
from jax.experimental import pallas as pl
from jax.experimental.pallas import tpu as pltpu

F32 = jnp.float32
BF16 = jnp.bfloat16

D_MODEL = 2048
DEPTH = 2
GRID_W = 64
ROPE_BASE = 10000.0
NORM_EPS = 1e-6
DA_HEADS = 6
DA_DQK = 64
DA_DV = 128
DA_W = DA_HEADS * DA_DV
MLA_HEADS = 6
MLA_Q_RANK = 512
MLA_KV_RANK = 256
MLA_NOPE = 128
MLA_ROPE = 64
MLA_DV = 128
MLA_W = MLA_HEADS * MLA_DV
MLA_QK_PAD = 256
HY_CH = 512
HY_ORDER = 2
HY_EMB = 33
HY_BANDS = (HY_EMB - 1) // 2
HY_FFN = 64
HY_MIN_DECAY = math.log(1e-2) / 1.5
HY_MAX_DECAY = math.log(1e-2) / 0.3
D_FF = 5632
N_BRANCH = 3
DEEPNORM_ALPHA = (2 * DEPTH) ** 0.25
LOG2E = 1.4426950408889634

ROW_GROUP = 256
LANE = 128
FFT_S = 128
VMEM_LIMIT = 52 * 1024 * 1024


def _cparams(*sem):
    return pltpu.CompilerParams(dimension_semantics=sem, vmem_limit_bytes=VMEM_LIMIT)


def _pick(total, prefs):
    for p in prefs:
        if total % p == 0:
            return p
    raise ValueError(f"no tile for {total} in {prefs}")


def _ln(x):
    mu = jnp.mean(x, axis=-1, keepdims=True)
    xc = x - mu
    var = jnp.mean(xc * xc, axis=-1, keepdims=True)
    return xc * lax.rsqrt(var + NORM_EPS)


def _rms(x):
    return x * lax.rsqrt(jnp.mean(x * x, axis=-1, keepdims=True) + NORM_EPS)


def _rope128(u, cos, sa, sb):
    return u * cos + pltpu.roll(u, 16, 1) * sa + pltpu.roll(u, LANE - 16, 1) * sb


def _ada_kernel(c_ref, w_ref, b_ref, o_ref):
    a = c_ref[...]
    a = a * jax.nn.sigmoid(a)
    o_ref[...] = jnp.dot(a.astype(BF16), w_ref[...].astype(BF16), preferred_element_type=F32) + b_ref[...]


def _ada(cc, ada_w, ada_b):
    L, D, N = ada_w.shape
    tn = 1024
    return pl.pallas_call(
        _ada_kernel,
        out_shape=jax.ShapeDtypeStruct((L, 8, N), F32),
        grid=(L, N // tn),
        in_specs=[pl.BlockSpec((8, D), lambda l, j: (0, 0)),
                  pl.BlockSpec((None, D, tn), lambda l, j: (l, 0, j)),
                  pl.BlockSpec((None, 1, tn), lambda l, j: (l, 0, j))],
        out_specs=pl.BlockSpec((None, 8, tn), lambda l, j: (l, 0, j)),
        compiler_params=_cparams("arbitrary", "arbitrary"),
    )(cc, ada_w, ada_b)


def _mod_row(g, gpb, lat_groups, n_batch):
    return jnp.where(g % gpb < lat_groups, g // gpb, n_batch)


def _mm_kernel(a_ref, w_ref, o_ref, *, act):
    acc = jnp.dot(a_ref[...], w_ref[...], preferred_element_type=F32)
    if act == "sigmoid":
        acc = jax.nn.sigmoid(acc)
    o_ref[...] = acc.astype(o_ref.dtype)


def _by_batch(a, nb):
    return a.reshape(nb, a.shape[0] // nb, a.shape[1])


def _matmul(a, w, out_dtype, tn, act=None, nb=1, tb=None):
    a = _by_batch(a, nb)
    K = a.shape[2]
    tb = tb or a.shape[1]
    N = w.shape[1]
    tm = _pick(tb, (1024, 512, 256))
    return pl.pallas_call(
        functools.partial(_mm_kernel, act=act),
        out_shape=jax.ShapeDtypeStruct((nb, tb, N), out_dtype),
        grid=(nb, tb // tm, N // tn),
        in_specs=[pl.BlockSpec((None, tm, K), lambda b, i, j: (b, i, 0)),
                  pl.BlockSpec((K, tn), lambda b, i, j: (0, j))],
        out_specs=pl.BlockSpec((None, tm, tn), lambda b, i, j: (b, i, j)),
        compiler_params=_cparams("arbitrary", "arbitrary", "arbitrary"),
    )(a, w).reshape(nb * tb, N)


def _qkv_kernel(x_ref, mod_ref, w_ref, cos_ref, sa_ref, sb_ref, o_ref, h_ref, *, qscale, geom, groups):
    j = pl.program_id(1)

    @pl.when(j == 0)
    def _():
        for q in range(groups):
            mod = mod_ref[_mod_row(pl.program_id(0) * groups + q, *geom)]
            sl = slice(q * ROW_GROUP, (q + 1) * ROW_GROUP)
            h_ref[sl, :] = (_ln(x_ref[sl, :]) * (1.0 + mod[1:2, :]) + mod[0:1, :]).astype(BF16)

    acc = jnp.dot(h_ref[...], w_ref[...], preferred_element_type=F32)

    @pl.when(j < 2)
    def _():
        cos, sa, sb = cos_ref[...], sa_ref[...], sb_ref[...]
        scale = jnp.where(j == 0, qscale, 1.0).astype(F32)
        for c in range(DA_HEADS):
            u = acc[:, c * LANE:(c + 1) * LANE]
            o_ref[:, c * LANE:(c + 1) * LANE] = (_rope128(u, cos, sa, sb) * scale).astype(BF16)

    @pl.when(j == 2)
    def _():
        o_ref[...] = acc.astype(BF16)


def _qkv_proj(xs, mod, w_qkv, tabs, geom):
    M, K = xs.shape
    tm = _pick(M, (1024, 512, 256))
    tn = DA_W
    row = lambda i, j: (i, 0)
    tab_spec = pl.BlockSpec((tm, LANE), row)
    return pl.pallas_call(
        functools.partial(_qkv_kernel, qscale=DA_DQK ** -0.5 * LOG2E, geom=geom, groups=tm // ROW_GROUP),
        out_shape=(jax.ShapeDtypeStruct((M, 3 * DA_W), BF16), jax.ShapeDtypeStruct((M, K), BF16)),
        grid=(M // tm, 3),
        in_specs=[pl.BlockSpec((tm, K), row),
                  pl.BlockSpec(mod.shape, lambda i, j: (0, 0, 0)),
                  pl.BlockSpec((K, tn), lambda i, j: (0, j)),
                  tab_spec, tab_spec, tab_spec],
        out_specs=(pl.BlockSpec((tm, tn), lambda i, j: (i, j)), pl.BlockSpec((tm, K), row)),
        compiler_params=_cparams("arbitrary", "arbitrary"),
    )(xs, mod, w_qkv, *tabs)


def _mla_prep_kernel(h_ref, wm_ref, qg_ref, kvg_ref, wuq_ref, wukv_ref, cos_ref, sa_ref, sb_ref,
                     q_ref, k_ref, v_ref, *, qscale):
    p = jnp.dot(h_ref[...], wm_ref[...], preferred_element_type=F32)
    cos, sa, sb = cos_ref[...], sa_ref[...], sb_ref[...]
    cq = p[:, :MLA_Q_RANK]
    ckv = p[:, MLA_Q_RANK:MLA_Q_RANK + MLA_KV_RANK]
    kr = p[:, MLA_Q_RANK + MLA_KV_RANK:]
    qn = (_rms(cq) * qg_ref[...]).astype(BF16)
    q = jnp.dot(qn, wuq_ref[...], preferred_element_type=F32)
    kvn = (_rms(ckv) * kvg_ref[...]).astype(BF16)
    kv = jnp.dot(kvn, wukv_ref[...], preferred_element_type=F32)
    krr = _rope128(kr, cos, sa, sb).astype(BF16)
    for h in range(MLA_HEADS):
        o = h * MLA_QK_PAD
        q_ref[:, o:o + LANE] = (q[:, o:o + LANE] * qscale).astype(BF16)
        q_ref[:, o + LANE:o + 2 * LANE] = (_rope128(q[:, o + LANE:o + 2 * LANE], cos, sa, sb) * qscale).astype(BF16)
        k_ref[:, o:o + LANE] = kv[:, h * LANE:(h + 1) * LANE].astype(BF16)
        k_ref[:, o + LANE:o + 2 * LANE] = krr
    v_ref[...] = kv[:, MLA_W:].astype(BF16)


def _mla_prep(h, w_mla, q_g, kv_g, w_uq, w_ukv, tabs):
    M, D = h.shape
    tm = _pick(M, (512, 256))
    row = lambda i: (i, 0)
    full = lambda i: (0, 0)
    qk_w = MLA_HEADS * MLA_QK_PAD
    return pl.pallas_call(
        functools.partial(_mla_prep_kernel, qscale=(MLA_NOPE + MLA_ROPE) ** -0.5 * LOG2E),
        out_shape=(jax.ShapeDtypeStruct((M, qk_w), BF16),
                   jax.ShapeDtypeStruct((M, qk_w), BF16),
                   jax.ShapeDtypeStruct((M, MLA_W), BF16)),
        grid=(M // tm,),
        in_specs=[pl.BlockSpec((tm, D), row),
                  pl.BlockSpec(w_mla.shape, full),
                  pl.BlockSpec((1, MLA_Q_RANK), full),
                  pl.BlockSpec((1, MLA_KV_RANK), full),
                  pl.BlockSpec(w_uq.shape, full),
                  pl.BlockSpec(w_ukv.shape, full),
                  pl.BlockSpec((tm, LANE), row), pl.BlockSpec((tm, LANE), row), pl.BlockSpec((tm, LANE), row)],
        out_specs=(pl.BlockSpec((tm, qk_w), row), pl.BlockSpec((tm, qk_w), row), pl.BlockSpec((tm, MLA_W), row)),
        compiler_params=_cparams("arbitrary"),
    )(h, w_mla, q_g, kv_g, w_uq, w_ukv, *tabs)


_NT = (((1,), (1,)), ((), ()))
ATTN_TQ = 256
ATTN_CHAINS = 8
ATTN_TK = 256


def _skewed_pipeline(n_chains, n_chunks, stages):
    for t in range(n_chains + len(stages) - 1):
        active = [(s, t - s) for s in range(len(stages)) if 0 <= t - s < n_chains]
        for c in range(n_chunks):
            for s, chain in active:
                stages[s][0](chain, c)
        for s, chain in active:
            stages[s][1](chain)


def _acc(old, new, op):
    return new if old is None else op(old, new)


def _lane_halves(x, op):
    return op(x[:, :LANE], x[:, LANE:])


def _attn_steps(run, q_rows, n_keys, lat_steps, n_lat, ctx_rows):
    if ctx_rows == 0:
        run(q_rows, 0, n_keys)
        return
    step = pl.program_id(2)

    @pl.when(step < lat_steps)
    def _():
        run(q_rows, 0, n_keys)

    @pl.when(step == lat_steps)
    def _():
        run(ctx_rows, n_lat, n_keys - n_lat)


def _diff_attn_kernel(lam_ref, q_ref, k_ref, v_ref, g_ref, o_ref, *, out_scale, lat_steps, n_lat, ctx_rows):
    run = functools.partial(_diff_attn_run, lam_ref, q_ref, k_ref, v_ref, g_ref, o_ref, out_scale)
    _attn_steps(run, q_ref.shape[0], k_ref.shape[0], lat_steps, n_lat, ctx_rows)


def _diff_attn_run(lam_ref, q_ref, k_ref, v_ref, g_ref, o_ref, out_scale, q_rows, key0, n_keys):
    lam, g = lam_ref[0], g_ref[...]
    tq = min(ATTN_TQ, q_rows)
    n_chains = q_rows // tq
    n_chunks = n_keys // ATTN_TK
    lane = lax.broadcasted_iota(jnp.int32, (1, LANE), 1)
    lo = (lane < DA_DQK).astype(F32)
    st = [dict(s=[], e=[], mx=[None, None], l=[None, None], o=None) for _ in range(n_chains)]
    rows = lambda i: slice(i * tq, (i + 1) * tq)
    keys = lambda c: slice(key0 + c * ATTN_TK, key0 + (c + 1) * ATTN_TK)

    def qk_chunk(i, c):
        d = st[i]
        if c == 0:
            qf = q_ref[rows(i), :].astype(F32)
            d["q"] = jnp.concatenate([(qf * lo).astype(BF16), (qf * (1.0 - lo)).astype(BF16)], axis=0)
        both = lax.dot_general(d["q"], k_ref[keys(c), :], _NT, preferred_element_type=F32)
        pair = []
        for m in range(2):
            s = both[m * tq:(m + 1) * tq]
            d["mx"][m] = _acc(d["mx"][m], _lane_halves(s, jnp.maximum), jnp.maximum)
            pair.append(s)
        d["s"].append(pair)

    def qk_done(i):
        st[i]["m"] = [jnp.max(mx, axis=-1, keepdims=True) for mx in st[i]["mx"]]

    def exp_chunk(i, c):
        d = st[i]
        pair = []
        for m in range(2):
            e = jnp.exp2(d["s"][c][m] - d["m"][m])
            d["l"][m] = _acc(d["l"][m], _lane_halves(e, jnp.add), jnp.add)
            pair.append(e)
        d["s"][c] = None
        d["e"].append(pair)

    def exp_done(i):
        d = st[i]
        l1, l2 = [jnp.sum(l, axis=-1, keepdims=True) for l in d["l"]]
        d["r1"] = 1.0 / l1
        d["cf"] = lam * l1 / l2

    def pv_chunk(i, c):
        d = st[i]
        w = (d["e"][c][0] - d["cf"] * d["e"][c][1]).astype(BF16)
        d["e"][c] = None
        d["o"] = _acc(d["o"], jnp.dot(w, v_ref[keys(c), :], preferred_element_type=F32), jnp.add)

    def pv_done(i):
        o = st[i]["o"] * st[i]["r1"]
        o_ref[rows(i), :] = (_rms(o) * g * out_scale).astype(BF16)

    _skewed_pipeline(n_chains, n_chunks, [(qk_chunk, qk_done), (exp_chunk, exp_done), (pv_chunk, pv_done)])


def _mla_attn_kernel(q_ref, k_ref, v_ref, o_ref, *, lat_steps, n_lat, ctx_rows):
    run = functools.partial(_mla_attn_run, q_ref, k_ref, v_ref, o_ref)
    _attn_steps(run, q_ref.shape[0], k_ref.shape[0], lat_steps, n_lat, ctx_rows)


def _mla_attn_run(q_ref, k_ref, v_ref, o_ref, q_rows, key0, n_keys):
    tq = min(ATTN_TQ, q_rows)
    n_chains = q_rows // tq
    n_chunks = n_keys // ATTN_TK
    st = [dict(s=[], mx=None, l=None, o=None) for _ in range(n_chains)]
    rows = lambda i: slice(i * tq, (i + 1) * tq)
    keys = lambda c: slice(key0 + c * ATTN_TK, key0 + (c + 1) * ATTN_TK)

    def qk_chunk(i, c):
        d = st[i]
        s = lax.dot_general(q_ref[rows(i), :], k_ref[keys(c), :], _NT, preferred_element_type=F32)
        d["mx"] = _acc(d["mx"], _lane_halves(s, jnp.maximum), jnp.maximum)
        d["s"].append(s)

    def qk_done(i):
        st[i]["m"] = jnp.max(st[i]["mx"], axis=-1, keepdims=True)

    def pv_chunk(i, c):
        d = st[i]
        e = jnp.exp2(d["s"][c] - d["m"])
        d["s"][c] = None
        d["l"] = _acc(d["l"], _lane_halves(e, jnp.add), jnp.add)
        d["o"] = _acc(d["o"], jnp.dot(e.astype(BF16), v_ref[keys(c), :], preferred_element_type=F32), jnp.add)

    def pv_done(i):
        d = st[i]
        o_ref[rows(i), :] = (d["o"] * (1.0 / jnp.sum(d["l"], axis=-1, keepdims=True))).astype(BF16)

    _skewed_pipeline(n_chains, n_chunks, [(qk_chunk, qk_done), (pv_chunk, pv_done)])


def _diff_attn(qkv, lam, subln_g, n, nc, ctx_queries, lam_init):
    B, T, _ = qkv.shape
    H = DA_HEADS
    tq = _pick(n, (ATTN_TQ * ATTN_CHAINS, ATTN_TQ))
    lat_steps = n // tq
    kern = functools.partial(_diff_attn_kernel, out_scale=1.0 - lam_init, lat_steps=lat_steps, n_lat=n,
                             ctx_rows=nc if ctx_queries else 0)
    return pl.pallas_call(
        kern,
        out_shape=jax.ShapeDtypeStruct((B, T if ctx_queries else n, DA_W), BF16),
        grid=(B, H, lat_steps + int(ctx_queries)),
        in_specs=[pl.BlockSpec(memory_space=pltpu.SMEM),
                  pl.BlockSpec((None, tq, LANE), lambda b, h, i: (b, i, h)),
                  pl.BlockSpec((None, T, LANE), lambda b, h, i: (b, 0, H + h)),
                  pl.BlockSpec((None, T, LANE), lambda b, h, i: (b, 0, 2 * H + h)),
                  pl.BlockSpec((1, DA_DV), lambda b, h, i: (0, 0))],
        out_specs=pl.BlockSpec((None, tq, LANE), lambda b, h, i: (b, i, h)),
        compiler_params=_cparams("arbitrary", "arbitrary", "arbitrary"),
    )(lam, qkv, qkv, qkv, subln_g)


def _mla_attn(q, k, v, n, nc, ctx_queries):
    B, T, _ = q.shape
    H = MLA_HEADS
    tq = _pick(n, (ATTN_TQ * ATTN_CHAINS, ATTN_TQ))
    lat_steps = n // tq
    kern = functools.partial(_mla_attn_kernel, lat_steps=lat_steps, n_lat=n, ctx_rows=nc if ctx_queries else 0)
    return pl.pallas_call(
        kern,
        out_shape=jax.ShapeDtypeStruct((B, T if ctx_queries else n, MLA_W), BF16),
        grid=(B, H, lat_steps + int(ctx_queries)),
        in_specs=[pl.BlockSpec((None, tq, MLA_QK_PAD), lambda b, h, i: (b, i, h)),
                  pl.BlockSpec((None, T, MLA_QK_PAD), lambda b, h, i: (b, 0, h)),
                  pl.BlockSpec((None, T, MLA_DV), lambda b, h, i: (b, 0, h))],
        out_specs=pl.BlockSpec((None, tq, MLA_DV), lambda b, h, i: (b, i, h)),
        compiler_params=_cparams("arbitrary", "arbitrary", "arbitrary"),
    )(q, k, v)


def _hy_filter_kernel(feat_ref, dec_ref, w1_ref, b1_ref, w2_ref, b2_ref, w3_ref, b3_ref, w4_ref, fr_ref,
                      k_ref, s_ref):
    hp = lax.Precision.HIGHEST
    d, r = pl.program_id(0), pl.program_id(1)
    fr = fr_ref[...]
    h = jnp.sin(fr * (jnp.dot(feat_ref[...], w1_ref[...], precision=hp, preferred_element_type=F32) + b1_ref[...]))
    h = jnp.sin(fr * (jnp.dot(h, w2_ref[...], precision=hp, preferred_element_type=F32) + b2_ref[...]))
    h = jnp.sin(fr * (jnp.dot(h, w3_ref[...], precision=hp, preferred_element_type=F32) + b3_ref[...]))
    h = jnp.dot(h, w4_ref[...], precision=hp, preferred_element_type=F32)
    row = lax.broadcasted_iota(jnp.int32, (h.shape[0], 1), 0)
    first_bwd = jnp.where((d == 1) & (r == 0), 1.0, 0.0)
    scale = dec_ref[...] * (1.0 - jnp.where(row == 0, 1.0, 0.0) * first_bwd)

    @pl.when((d == 0) & (r == 0))
    def _():
        s_ref[...] = jnp.zeros(s_ref.shape, F32)

    for o in range(HY_ORDER):
        ko = h[:, o * HY_CH:(o + 1) * HY_CH] * scale
        k_ref[o] = ko
        s_ref[o] += jnp.broadcast_to(jnp.sum(jnp.abs(ko), axis=0, keepdims=True), (8, HY_CH))


def _hy_filter(n, w1, b1, w2, b2, w3, b3, w4, fr):
    C = HY_CH
    t = jnp.linspace(0.0, 1.0, n, dtype=F32)
    pos = jnp.arange(n, dtype=F32)
    t2 = jnp.concatenate([t, t[::-1]])[:, None]
    pos2 = jnp.concatenate([pos, pos[::-1]])[:, None]
    phase = (2.0 * math.pi / n) * pos2 * jnp.linspace(1e-4, HY_BANDS - 1, HY_BANDS, dtype=F32)[None, :]
    feat = jnp.concatenate([t2, jnp.cos(phase), -jnp.sin(phase)], axis=-1)
    feat = jnp.pad(feat, ((0, 0), (0, HY_FFN - HY_EMB)))
    dec = jnp.exp(-t2 * jnp.abs(jnp.linspace(HY_MIN_DECAY, HY_MAX_DECAY, C, dtype=F32)))
    w4d = w4.reshape(HY_FFN, HY_ORDER, 2, C).transpose(2, 0, 1, 3).reshape(2, HY_FFN, HY_ORDER * C)
    rb = min(512, n)
    nb = n // rb
    full = lambda d, r: (0, 0)
    return pl.pallas_call(
        _hy_filter_kernel,
        out_shape=(jax.ShapeDtypeStruct((HY_ORDER, 2 * n, C), F32),
                   jax.ShapeDtypeStruct((HY_ORDER, 8, C), F32)),
        grid=(2, nb),
        in_specs=[pl.BlockSpec((rb, HY_FFN), lambda d, r: (d * nb + r, 0)),
                  pl.BlockSpec((rb, C), lambda d, r: (d * nb + r, 0)),
                  pl.BlockSpec(w1.shape, full), pl.BlockSpec(b1.shape, full),
                  pl.BlockSpec(w2.shape, full), pl.BlockSpec(b2.shape, full),
                  pl.BlockSpec(w3.shape, full), pl.BlockSpec(b3.shape, full),
                  pl.BlockSpec((None, HY_FFN, HY_ORDER * C), lambda d, r: (d, 0, 0)),
                  pl.BlockSpec(fr.shape, full)],
        out_specs=(pl.BlockSpec((HY_ORDER, rb, C), lambda d, r: (0, d * nb + r, 0)),
                   pl.BlockSpec((HY_ORDER, 8, C), lambda d, r: (0, 0, 0))),
        compiler_params=_cparams("arbitrary", "arbitrary"),
    )(feat, dec, w1, b1, w2, b2, w3, b3, w4d, fr)


def _hy_dwconv_kernel(*refs):
    for k in range(3):
        u_ref, w_ref, b_ref, o_ref = refs[3 * k], refs[3 * k + 1], refs[3 * k + 2], refs[9 + k]
        u = u_ref[...]
        L = u.shape[0]
        row = lax.broadcasted_iota(jnp.int32, (L, 1), 0)
        up = jnp.where(row == 0, 0.0, pltpu.roll(u, 1, 0))
        dn = jnp.where(row == L - 1, 0.0, pltpu.roll(u, L - 1, 0))
        o_ref[...] = up * w_ref[0:1, :] + u * w_ref[1:2, :] + dn * w_ref[2:3, :] + b_ref[...]


def _hy_dwconv(u, w, b, row_block, length):
    B = u.shape[0]
    cw = LANE
    per = HY_CH // cw
    in_specs, args = [], []
    for k in range(3):
        in_specs += [pl.BlockSpec((None, length, cw), lambda bb, j, k=k: (bb, row_block, k * per + j)),
                     pl.BlockSpec((3, cw), lambda bb, j, k=k: (0, k * per + j)),
                     pl.BlockSpec((1, cw), lambda bb, j, k=k: (0, k * per + j))]
        args += [u, w, b]
    ospec = pl.BlockSpec((None, length, cw), lambda bb, j: (bb, 0, j))
    return pl.pallas_call(
        _hy_dwconv_kernel,
        out_shape=(jax.ShapeDtypeStruct((B, length, HY_CH), F32),) * 3,
        grid=(B, per),
        in_specs=in_specs,
        out_specs=(ospec,) * 3,
        compiler_params=_cparams("arbitrary", "arbitrary"),
    )(*args)


def _dft_tables(n):
    S = FFT_S
    M = 2 * n
    N1 = M // S
    H = N1 // 2
    s2 = np.arange(S)[:, None, None]
    k1 = np.arange(N1)[None, :, None]
    s1 = np.arange(N1)[None, None, :]
    ang = -2.0 * np.pi * ((k1 * (S * s1 + s2)) % M) / M
    fr, fi = np.cos(ang), np.sin(ang)
    g1f = np.concatenate([fr, fi], axis=1)
    frh, fih = fr[:, :, :H], fi[:, :, :H]
    g1d = np.concatenate([np.concatenate([frh, -fih], axis=2),
                          np.concatenate([fih, frh], axis=2)], axis=1)
    er = np.transpose(frh, (0, 2, 1)) / M
    ei = -np.transpose(fih, (0, 2, 1)) / M
    g3 = np.concatenate([np.concatenate([er, -ei], axis=2),
                         np.concatenate([ei, er], axis=2)], axis=1)
    a2 = -2.0 * np.pi * ((np.arange(S)[:, None] * np.arange(S)[None, :]) % S) / S
    f2r, f2i = np.cos(a2), np.sin(a2)
    g2 = np.block([[f2r, -f2i], [f2i, f2r]])
    g2i = np.block([[f2r, f2i], [-f2i, f2r]])
    cast = lambda a: jnp.asarray(a, dtype=F32).astype(BF16)
    return cast(g1d), cast(g1f), cast(g2), cast(g2i), cast(g3)


FFT_NS = 32
FFT_NK = 8


def _fft_s1_kernel(g_ref, x_ref, o_ref):
    xt = jnp.swapaxes(x_ref[...], 0, 1)
    y = jnp.stack([jnp.dot(g_ref[j], xt[j].astype(BF16), preferred_element_type=F32)
                   for j in range(xt.shape[0])], axis=0)
    o_ref[...] = jnp.swapaxes(y, 0, 1).astype(o_ref.dtype)


def _fft_s1(g, x):
    P, Ri, S, C = x.shape
    Ro = g.shape[1]
    ns = FFT_NS
    return pl.pallas_call(
        _fft_s1_kernel,
        out_shape=jax.ShapeDtypeStruct((P, Ro, S, C), BF16),
        grid=(S // ns, P),
        in_specs=[pl.BlockSpec((ns, Ro, Ri), lambda j, p: (j, 0, 0)),
                  pl.BlockSpec((None, Ri, ns, C), lambda j, p: (p, 0, j, 0))],
        out_specs=pl.BlockSpec((None, Ro, ns, C), lambda j, p: (p, 0, j, 0)),
        compiler_params=_cparams("arbitrary", "arbitrary"),
    )(g, x)


def _fft_s2_filt_kernel(a_ref, g_ref, rn_ref, o_ref, *, nk):
    S = FFT_S
    rn = rn_ref[...]
    for t in range(nk):
        d = jnp.concatenate([a_ref[0, t], a_ref[1, t]], axis=0)
        y = jnp.dot(g_ref[...], d, preferred_element_type=F32)
        o_ref[0, t] = y[:S] * rn
        o_ref[1, t] = y[S:] * rn


def _fft_s2_filt(a, g2, rnorm):
    O, _, N1, S, C = a.shape
    nk = FFT_NK
    blk = (None, 2, nk, S, C)
    return pl.pallas_call(
        functools.partial(_fft_s2_filt_kernel, nk=nk),
        out_shape=jax.ShapeDtypeStruct(a.shape, F32),
        grid=(O, N1 // nk),
        in_specs=[pl.BlockSpec(blk, lambda o, j: (o, 0, j, 0, 0)),
                  pl.BlockSpec(g2.shape, lambda o, j: (0, 0)),
                  pl.BlockSpec((None, 1, C), lambda o, j: (o, 0, 0))],
        out_specs=pl.BlockSpec(blk, lambda o, j: (o, 0, j, 0, 0)),
        compiler_params=_cparams("arbitrary", "arbitrary"),
    )(a, g2, rnorm)


def _fft_s2_kernel(a_ref, g_ref, gi_ref, kf_ref, o_ref, *, nk):
    S = FFT_S
    for t in range(nk):
        d = jnp.concatenate([a_ref[0, t], a_ref[1, t]], axis=0)
        y = jnp.dot(g_ref[...], d, preferred_element_type=F32)
        yr, yi = y[:S], y[S:]
        kr, ki = kf_ref[0, t], kf_ref[1, t]
        p = jnp.concatenate([yr * kr - yi * ki, yr * ki + yi * kr], axis=0).astype(BF16)
        b = jnp.dot(gi_ref[...], p, preferred_element_type=F32)
        o_ref[0, t] = b[:S].astype(BF16)
        o_ref[1, t] = b[S:].astype(BF16)


def _fft_s2(a, g2, g2i, kf, order):
    P, _, N1, S, C = a.shape
    nk = FFT_NK
    blk = (None, 2, nk, S, C)
    return pl.pallas_call(
        functools.partial(_fft_s2_kernel, nk=nk),
        out_shape=jax.ShapeDtypeStruct(a.shape, BF16),
        grid=(N1 // nk, P),
        in_specs=[pl.BlockSpec(blk, lambda j, p: (p, 0, j, 0, 0)),
                  pl.BlockSpec(g2.shape, lambda j, p: (0, 0)),
                  pl.BlockSpec(g2i.shape, lambda j, p: (0, 0)),
                  pl.BlockSpec(blk, lambda j, p: (order, 0, j, 0, 0))],
        out_specs=pl.BlockSpec(blk, lambda j, p: (p, 0, j, 0, 0)),
        compiler_params=_cparams("arbitrary", "arbitrary"),
    )(a, g2, g2i, kf)


def _fft_s3_kernel(g_ref, b_ref, z_ref, gate_ref, skip_ref, o_ref):
    bt = jnp.swapaxes(b_ref[...].astype(F32), 0, 1).astype(BF16)
    y = jnp.stack([jnp.dot(g_ref[j], bt[j], preferred_element_type=F32) for j in range(bt.shape[0])], axis=0)
    y = jnp.swapaxes(y, 0, 1)
    o_ref[...] = (gate_ref[...] * (y + skip_ref[...] * z_ref[...])).astype(o_ref.dtype)


def _fft_s3(g3, b, z, gate, skip, out_dtype):
    P, Ri, S, C = b.shape
    Ro = g3.shape[1]
    ns = FFT_NS
    dspec = pl.BlockSpec((None, Ro, ns, C), lambda j, p: (p, 0, j, 0))
    return pl.pallas_call(
        _fft_s3_kernel,
        out_shape=jax.ShapeDtypeStruct((P, Ro, S, C), out_dtype),
        grid=(S // ns, P),
        in_specs=[pl.BlockSpec((ns, Ro, Ri), lambda j, p: (j, 0, 0)),
                  pl.BlockSpec((None, Ri, ns, C), lambda j, p: (p, 0, j, 0)),
                  dspec, dspec,
                  pl.BlockSpec((1, C), lambda j, p: (0, 0))],
        out_specs=dspec,
        compiler_params=_cparams("arbitrary", "arbitrary"),
    )(g3, b, z, gate, skip)


def _dense_tables(m):
    M = 2 * m
    ang = -2.0 * np.pi * ((np.arange(M)[:, None] * np.arange(M)[None, :]) % M) / M
    fr, fi = np.cos(ang), np.sin(ang)
    gk = np.concatenate([fr, fi], axis=0)
    gd = np.block([[fr[:, :m], -fi[:, :m]], [fi[:, :m], fr[:, :m]]])
    er, ei = fr[:m, :] / M, -fi[:m, :] / M
    gi = np.block([[er, -ei], [ei, er]])
    cast = lambda a: jnp.asarray(a, dtype=F32).astype(BF16)
    return cast(gk), cast(gd), cast(gi)


def _dense_spec_kernel(k_ref, g_ref, rn_ref, o_ref):
    o_ref[...] = jnp.dot(g_ref[...], k_ref[...].astype(BF16), preferred_element_type=F32) * rn_ref[...]


def _dense_spec(kc, gk, rnorm):
    O, M, C = kc.shape
    return pl.pallas_call(
        _dense_spec_kernel,
        out_shape=jax.ShapeDtypeStruct((O, 2 * M, C), F32),
        grid=(O,),
        in_specs=[pl.BlockSpec((None, M, C), lambda o: (o, 0, 0)),
                  pl.BlockSpec(gk.shape, lambda o: (0, 0)),
                  pl.BlockSpec((None, 1, C), lambda o: (o, 0, 0))],
        out_specs=pl.BlockSpec((None, 2 * M, C), lambda o: (o, 0, 0)),
        compiler_params=_cparams("arbitrary"),
    )(kc, gk, rnorm)


def _dense_conv_kernel(x_ref, gd_ref, gi_ref, kf_ref, gate_ref, skip_ref, o_ref):
    x = x_ref[...]
    y = jnp.dot(gd_ref[...], x.astype(BF16), preferred_element_type=F32)
    M = y.shape[0] // 2
    yr, yi = y[:M], y[M:]
    kr, ki = kf_ref[:M], kf_ref[M:]
    p = jnp.concatenate([yr * kr - yi * ki, yr * ki + yi * kr], axis=0).astype(BF16)
    conv = jnp.dot(gi_ref[...], p, preferred_element_type=F32)
    o_ref[...] = (gate_ref[...] * (conv + skip_ref[...] * x)).astype(o_ref.dtype)


def _dense_conv(x, gd, gi, kf, order, gate, skip, out_dtype):
    P, R, C = x.shape
    dspec = pl.BlockSpec((None, R, C), lambda p: (p, 0, 0))
    return pl.pallas_call(
        _dense_conv_kernel,
        out_shape=jax.ShapeDtypeStruct((P, R, C), out_dtype),
        grid=(P,),
        in_specs=[dspec,
                  pl.BlockSpec(gd.shape, lambda p: (0, 0)),
                  pl.BlockSpec(gi.shape, lambda p: (0, 0)),
                  pl.BlockSpec((None,) + kf.shape[1:], lambda p: (order, 0, 0)),
                  dspec,
                  pl.BlockSpec((1, C), lambda p: (0, 0))],
        out_specs=dspec,
        compiler_params=_cparams("arbitrary"),
    )(x, gd, gi, kf, gate, skip)


def _merge_kernel(oa_ref, ob_ref, oc_ref, wa_ref, wb_ref, wc_ref, g0_ref, g1_ref, g2_ref, o_ref):
    ya = jnp.dot(oa_ref[...], wa_ref[...], preferred_element_type=F32)
    yb = jnp.dot(ob_ref[...], wb_ref[...], preferred_element_type=F32)
    yc = jnp.dot(oc_ref[...], wc_ref[...], preferred_element_type=F32)
    m = g0_ref[...].astype(F32) * ya + g1_ref[...].astype(F32) * yb + g2_ref[...].astype(F32) * yc
    o_ref[...] = m.astype(BF16)


def _merge(oa, ob, oc, w_ba, w_bb, w_bc, gates, nb, tb):
    D = D_MODEL
    tm = _pick(tb, (1024, 512, 256))
    tn = 512
    nt = D // tn
    oa, ob, oc, gates = (_by_batch(a, nb) for a in (oa, ob, oc, gates))
    row = lambda b, i, j: (b, i, 0)
    col = lambda b, i, j: (0, j)
    return pl.pallas_call(
        _merge_kernel,
        out_shape=jax.ShapeDtypeStruct((nb, tb, D), BF16),
        grid=(nb, tb // tm, nt),
        in_specs=[pl.BlockSpec((None, tm, DA_W), row), pl.BlockSpec((None, tm, MLA_W), row),
                  pl.BlockSpec((None, tm, HY_CH), row),
                  pl.BlockSpec((DA_W, tn), col), pl.BlockSpec((MLA_W, tn), col), pl.BlockSpec((HY_CH, tn), col),
                  pl.BlockSpec((None, tm, tn), lambda b, i, j: (b, i, j)),
                  pl.BlockSpec((None, tm, tn), lambda b, i, j: (b, i, nt + j)),
                  pl.BlockSpec((None, tm, tn), lambda b, i, j: (b, i, 2 * nt + j))],
        out_specs=pl.BlockSpec((None, tm, tn), lambda b, i, j: (b, i, j)),
        compiler_params=_cparams("arbitrary", "arbitrary", "arbitrary"),
    )(oa, ob, oc, w_ba, w_bb, w_bc, gates, gates, gates).reshape(nb * tb, D)


def _wo_ln_kernel(m_ref, w_ref, xs_ref, mod_ref, g_ref, b_ref, xs1_ref, h2_ref, *, sel, groups):
    y = jnp.dot(m_ref[...], w_ref[...], preferred_element_type=F32)
    g, b = g_ref[...], b_ref[...]
    for q in range(groups):
        mod = mod_ref[sel(pl.program_id(0), pl.program_id(1) * groups + q)]
        sl = slice(q * ROW_GROUP, (q + 1) * ROW_GROUP)
        x1 = _ln(DEEPNORM_ALPHA * xs_ref[sl, :] + mod[2:3, :] * y[sl, :]) * g + b
        xs1_ref[sl, :] = x1
        h2_ref[sl, :] = (_ln(x1) * (1.0 + mod[4:5, :]) + mod[3:4, :]).astype(BF16)


def _wo_ln(merged, w_o, xs, mod, ln_g, ln_b, sel, nb, tb):
    D = D_MODEL
    tm = _pick(tb, (512, 256))
    merged, xs = _by_batch(merged, nb), _by_batch(xs, nb)
    row = lambda b, i: (b, i, 0)
    full2 = lambda b, i: (0, 0)
    xs1, h2 = pl.pallas_call(
        functools.partial(_wo_ln_kernel, sel=sel, groups=tm // ROW_GROUP),
        out_shape=(jax.ShapeDtypeStruct((nb, tb, D), F32), jax.ShapeDtypeStruct((nb, tb, D), BF16)),
        grid=(nb, tb // tm),
        in_specs=[pl.BlockSpec((None, tm, D), row),
                  pl.BlockSpec((D, D), full2),
                  pl.BlockSpec((None, tm, D), row),
                  pl.BlockSpec(mod.shape, lambda b, i: (0, 0, 0)),
                  pl.BlockSpec((1, D), full2), pl.BlockSpec((1, D), full2)],
        out_specs=(pl.BlockSpec((None, tm, D), row), pl.BlockSpec((None, tm, D), row)),
        compiler_params=_cparams("arbitrary", "arbitrary"),
    )(merged, w_o, xs, mod, ln_g, ln_b)
    return xs1.reshape(nb * tb, D), h2.reshape(nb * tb, D)


FFN_HALO = 16


def _ffn_up_kernel(hp_ref, h_ref, hn_ref, wa_ref, wv_ref, cw_ref, cb_ref, kp_ref, kn_ref, o_ref, wa_s, wv_s):
    tm, tn = o_ref.shape

    @pl.when((pl.program_id(1) == 0) & (pl.program_id(2) == 0))
    def _():
        wa_s[...] = wa_ref[...].astype(BF16)
        wv_s[...] = wv_ref[...].astype(BF16)

    hm = h_ref[...]
    hext = jnp.concatenate([hp_ref[...], hm, hn_ref[...]], axis=0)
    a = jnp.dot(hext, wa_s[...], preferred_element_type=F32)
    v = jnp.dot(hm, wv_s[...], preferred_element_type=F32)
    ext = tm + 2 * FFN_HALO
    rep = tn // LANE
    keep_prev = jnp.tile(kp_ref[...], (1, rep))
    keep_next = jnp.tile(kn_ref[...], (1, rep))
    a_prev = pltpu.roll(a, 1, 0)[FFN_HALO:FFN_HALO + tm] * keep_prev
    a_next = pltpu.roll(a, ext - 1, 0)[FFN_HALO:FFN_HALO + tm] * keep_next
    cv = a_prev * cw_ref[0:1, :] + a[FFN_HALO:FFN_HALO + tm] * cw_ref[1:2, :] + a_next * cw_ref[2:3, :] + cb_ref[...]
    o_ref[...] = (cv * jax.nn.sigmoid(cv) * v).astype(BF16)


def _ffn_up(h2, w_up, layer, conv_w, conv_b, keep_prev, keep_next, nb, tb):
    D = D_MODEL
    tm = _pick(tb, (1024, 512, 256))
    tn = 512
    nt = D_FF // tn
    hb = tm // FFN_HALO
    last = tb // FFN_HALO - 1
    h2, keep_prev, keep_next = (_by_batch(a, nb) for a in (h2, keep_prev, keep_next))
    mask_spec = pl.BlockSpec((None, tm, LANE), lambda j, b, i: (b, i, 0))
    return pl.pallas_call(
        _ffn_up_kernel,
        out_shape=jax.ShapeDtypeStruct((nb, tb, D_FF), BF16),
        grid=(nt, nb, tb // tm),
        in_specs=[pl.BlockSpec((None, FFN_HALO, D), lambda j, b, i: (b, jnp.maximum(i * hb - 1, 0), 0)),
                  pl.BlockSpec((None, tm, D), lambda j, b, i: (b, i, 0)),
                  pl.BlockSpec((None, FFN_HALO, D), lambda j, b, i: (b, jnp.minimum((i + 1) * hb, last), 0)),
                  pl.BlockSpec((None, D, tn), lambda j, b, i: (layer, 0, j)),
                  pl.BlockSpec((None, D, tn), lambda j, b, i: (layer, 0, nt + j)),
                  pl.BlockSpec((3, tn), lambda j, b, i: (0, j)),
                  pl.BlockSpec((1, tn), lambda j, b, i: (0, j)),
                  mask_spec, mask_spec],
        out_specs=pl.BlockSpec((None, tm, tn), lambda j, b, i: (b, i, j)),
        scratch_shapes=[pltpu.VMEM((D, tn), BF16), pltpu.VMEM((D, tn), BF16)],
        compiler_params=_cparams("arbitrary", "arbitrary", "arbitrary"),
    )(h2, h2, h2, w_up, w_up, conv_w, conv_b, keep_prev, keep_next).reshape(nb * tb, D_FF)


def _ffn_down_kernel(u_ref, w_ref, xs_ref, mod_ref, g_ref, b_ref, o_ref, acc_ref, *, sel, groups, nk):
    k = pl.program_id(2)

    @pl.when(k == 0)
    def _():
        acc_ref[...] = jnp.zeros(acc_ref.shape, F32)

    acc_ref[...] += jnp.dot(u_ref[...], w_ref[...], preferred_element_type=F32)

    @pl.when(k == nk - 1)
    def _():
        g, b = g_ref[...], b_ref[...]
        for q in range(groups):
            mod = mod_ref[sel(pl.program_id(0), pl.program_id(1) * groups + q)]
            sl = slice(q * ROW_GROUP, (q + 1) * ROW_GROUP)
            o_ref[sl, :] = _ln(DEEPNORM_ALPHA * xs_ref[sl, :] + mod[5:6, :] * acc_ref[sl, :]) * g + b


def _ffn_down(u, w_down, xs1, mod, ln_g, ln_b, sel, nb, tb):
    D = D_MODEL
    tm = _pick(tb, (512, 256))
    tk = D_FF // 2
    nk = D_FF // tk
    u, xs1 = _by_batch(u, nb), _by_batch(xs1, nb)
    row = lambda b, i, k: (b, i, 0)
    full2 = lambda b, i, k: (0, 0)
    return pl.pallas_call(
        functools.partial(_ffn_down_kernel, sel=sel, groups=tm // ROW_GROUP, nk=nk),
        out_shape=jax.ShapeDtypeStruct((nb, tb, D), F32),
        grid=(nb, tb // tm, nk),
        in_specs=[pl.BlockSpec((None, tm, tk), lambda b, i, k: (b, i, k)),
                  pl.BlockSpec((tk, D), lambda b, i, k: (k, 0)),
                  pl.BlockSpec((None, tm, D), row),
                  pl.BlockSpec(mod.shape, lambda b, i, k: (0, 0, 0)),
                  pl.BlockSpec((1, D), full2), pl.BlockSpec((1, D), full2)],
        out_specs=pl.BlockSpec((None, tm, D), row),
        scratch_shapes=[pltpu.VMEM((tm, D), F32)],
        compiler_params=_cparams("arbitrary", "arbitrary", "arbitrary"),
    )(u, w_down, xs1, mod, ln_g, ln_b).reshape(nb * tb, D)


def _rope_tables(B, n, nc):
    half = DA_DQK // 2
    inv = ROPE_BASE ** (-jnp.arange(0, half, 2, dtype=F32) / half)
    t = jnp.arange(n, dtype=jnp.int32)
    ang_r = (t // GRID_W).astype(F32)[:, None] * inv[None, :]
    ang_c = (t % GRID_W).astype(F32)[:, None] * inv[None, :]
    ang = jnp.concatenate([ang_r, ang_r, ang_c, ang_c], axis=-1)
    cos, sin = jnp.cos(ang), jnp.sin(ang)
    upper = (jnp.arange(DA_DQK) % half) >= half // 2
    sa = jnp.where(upper, sin, 0.0)
    sb = jnp.where(upper, 0.0, -sin)

    def full(tab, fill):
        tab = jnp.concatenate([tab, jnp.full((nc, DA_DQK), fill, F32)], axis=0)
        tab = jnp.tile(tab, (B, LANE // DA_DQK))
        return tab

    return full(cos, 1.0), full(sa, 0.0), full(sb, 0.0)


def _conv_masks(B, n, nc):
    T = n + nc
    t = jnp.arange(T)
    keep_prev = ((t != 0) & (t != n)).astype(F32)
    keep_next = ((t != n - 1) & (t != T - 1)).astype(F32)
    widen = lambda m: jnp.tile(m[:, None], (B, LANE))
    return widen(keep_prev), widen(keep_next)


W_IN_WIDTHS = (DA_W, DA_W, DA_W, MLA_Q_RANK, MLA_KV_RANK, MLA_ROPE, 3 * HY_CH, N_BRANCH * D_MODEL)
W_IN_OFFS = tuple(sum(W_IN_WIDTHS[:j]) for j in range(len(W_IN_WIDTHS)))
W_MLA_COLS = MLA_Q_RANK + MLA_KV_RANK + LANE


W_HALF = LANE // 2


def _prep_w_in_kernel(a_ref, b_ref, o_ref, *, zero_b_at):
    b = b_ref[...]
    if zero_b_at is not None:
        b = jnp.where(pl.program_id(0) == zero_b_at, 0.0, b)
    o_ref[...] = jnp.transpose(jnp.concatenate([a_ref[...], b], axis=0)).astype(BF16)


def _prep_group(w_t, layer, n_blocks, src, zero_b_at=None):
    D = w_t.shape[2]
    return pl.pallas_call(
        functools.partial(_prep_w_in_kernel, zero_b_at=zero_b_at),
        out_shape=jax.ShapeDtypeStruct((D, n_blocks * LANE), BF16),
        grid=(n_blocks,),
        in_specs=[pl.BlockSpec((None, W_HALF, D), lambda c: (layer, src(c)[0], 0)),
                  pl.BlockSpec((None, W_HALF, D), lambda c: (layer, src(c)[1], 0))],
        out_specs=pl.BlockSpec((D, LANE), lambda c: (0, c)),
        compiler_params=_cparams("arbitrary"),
    )(w_t, w_t)


def _prep_w_in(w_in, layer):
    assert w_in.shape[2] == sum(W_IN_WIDTHS) and all(o % W_HALF == 0 for o in W_IN_OFFS)
    w_t = jnp.swapaxes(w_in, 1, 2)
    o_q, o_k, o_v, o_cq, _, o_kr, o_hy, o_g = (o // W_HALF for o in W_IN_OFFS)
    H = DA_HEADS

    def src_qkv(c):
        part, h = c // H, c % H
        a = jnp.where(part < 2, part * 2 * H + h, o_v + 2 * h)
        return a, jnp.where(part < 2, a + H, a + 1)

    pairs = lambda first: (lambda c: (first + 2 * c, first + 2 * c + 1))
    n_mla = W_MLA_COLS // LANE
    src_mla = lambda c: (o_cq + 2 * c, jnp.minimum(o_cq + 2 * c + 1, o_kr))
    return (_prep_group(w_t, layer, 3 * H, src_qkv),
            _prep_group(w_t, layer, n_mla, src_mla, zero_b_at=n_mla - 1),
            _prep_group(w_t, layer, 3 * HY_CH // LANE, pairs(o_hy)),
            _prep_group(w_t, layer, N_BRANCH * D_MODEL // LANE, pairs(o_g)))


def _pad_w_uq(w):
    w = w.reshape(MLA_Q_RANK, MLA_HEADS, MLA_NOPE + MLA_ROPE)
    w = jnp.pad(w, ((0, 0), (0, 0), (0, MLA_QK_PAD - MLA_NOPE - MLA_ROPE)))
    return w.reshape(MLA_Q_RANK, MLA_HEADS * MLA_QK_PAD).astype(BF16)


def _split_w_ukv(w):
    w = w.reshape(MLA_KV_RANK, MLA_HEADS, 2, MLA_NOPE).transpose(0, 2, 1, 3)
    return w.reshape(MLA_KV_RANK, 2 * MLA_W).astype(BF16)


def _hyena(u3, layer, p, n, nc, with_ctx, tables):
    B = u3.shape[0]
    C = HY_CH
    P = B // 2
    g1d, g1f, g2, g2i, g3, gk, gd, gi = tables
    S = FFT_S
    N1 = 2 * n // S
    mlp = (p["hy_ffn_w1p"][layer], p["hy_ffn_b1"][layer][None], p["hy_ffn_w2"][layer], p["hy_ffn_b2"][layer][None],
           p["hy_ffn_w3"][layer], p["hy_ffn_b3"][layer][None], p["hy_ffn_w4"][layer], p["hy_freq"][layer][None])
    skip = p["hy_skip"][layer]
    cw, cb = p["hy_conv_w"][layer], p["hy_conv_b"][layer][None]

    kc, sums = _hy_filter(n, *mlp)
    kf = _fft_s1(g1f, kc.reshape(HY_ORDER, N1, S, C))
    kf = _fft_s2_filt(kf.reshape(HY_ORDER, 2, N1, S, C), g2, 1.0 / sums[:, 0:1, :])
    dw = [a.reshape(P, N1, S, C) for a in _hy_dwconv(u3, cw, cb, 0, n)]
    z = dw[0]
    for o in range(HY_ORDER):
        a = _fft_s1(g1d, z).reshape(P, 2, N1, S, C)
        b = _fft_s2(a, g2, g2i, kf, o).reshape(P, 2 * N1, S, C)
        z = _fft_s3(g3, b, z, dw[1 + o], skip[o][None], F32 if o + 1 < HY_ORDER else BF16)
    oc_lat = z.reshape(B, n, C)

    if with_ctx:
        kcc, sumc = _hy_filter(nc, *mlp)
        kfc = _dense_spec(kcc, gk, 1.0 / sumc[:, 0:1, :])
        dwc = [a.reshape(P, 2 * nc, C) for a in _hy_dwconv(u3, cw, cb, n // nc, nc)]
        zc = dwc[0]
        for o in range(HY_ORDER):
            zc = _dense_conv(zc, gd, gi, kfc, o, dwc[1 + o], skip[o][None], F32 if o + 1 < HY_ORDER else BF16)
        return jnp.concatenate([oc_lat, zc.reshape(B, nc, C)], axis=1)
    return oc_lat


def kernel(x, c, ctx, c_ctx, ada_w, ada_b, w_in, da_lambda, da_subln_g, mla_q_g, mla_w_uq, mla_kv_g, mla_w_ukv, hy_conv_w, hy_conv_b, hy_ffn_w1, hy_ffn_b1, hy_ffn_w2, hy_ffn_b2, hy_ffn_w3, hy_ffn_b3, hy_ffn_w4, hy_freq, hy_skip, w_branch_a, w_branch_b, w_branch_c, w_out, ln1_g, ln1_b, ffn_w_up, ffn_conv_w, ffn_conv_b, ffn_w_down, ln2_g, ln2_b):
    B, n, D = x.shape
    nc = ctx.shape[1]
    T = n + nc
    rows = B * T
    assert D == D_MODEL and B % 2 == 0 and B < 8
    assert n % ROW_GROUP == 0 and nc % ROW_GROUP == 0 and n % nc == 0 and n % GRID_W == 0
    assert (2 * n) % (8 * FFT_S) == 0 and T % ATTN_TK == 0 and nc % ATTN_TK == 0
    geom = (T // ROW_GROUP, n // ROW_GROUP, B)

    hy = dict(hy_ffn_w1p=jnp.pad(hy_ffn_w1, ((0, 0), (0, HY_FFN - HY_EMB), (0, 0))), hy_ffn_b1=hy_ffn_b1,
              hy_ffn_w2=hy_ffn_w2, hy_ffn_b2=hy_ffn_b2, hy_ffn_w3=hy_ffn_w3, hy_ffn_b3=hy_ffn_b3,
              hy_ffn_w4=hy_ffn_w4, hy_freq=hy_freq, hy_skip=hy_skip, hy_conv_w=hy_conv_w, hy_conv_b=hy_conv_b)
    tables = _dft_tables(n) + _dense_tables(nc)
    rope = _rope_tables(B, n, nc)
    keep_prev, keep_next = _conv_masks(B, n, nc)

    cc = jnp.concatenate([c, c_ctx[None], jnp.zeros((8 - B - 1, D), F32)], axis=0)
    mods = _ada(cc, ada_w, ada_b[:, None, :]).reshape(DEPTH, 8, 6, D)

    xs = jnp.concatenate([x, ctx], axis=1).reshape(rows, D)
    for i in range(DEPTH):
        last = i == DEPTH - 1
        lam_init = 0.8 - 0.6 * math.exp(-0.3 * i)
        lq1, lk1, lq2, lk2 = da_lambda[i].astype(F32)
        lam = (jnp.exp(jnp.sum(lq1 * lk1)) - jnp.exp(jnp.sum(lq2 * lk2)) + lam_init).reshape(1)
        mod = mods[i]
        w_qkv, w_mla, w_hy, w_g = _prep_w_in(w_in, i)

        if last:
            nb, tb = B, n
            sel = lambda b, g: b
        else:
            nb, tb = 1, rows
            sel = lambda b, g: _mod_row(g, *geom)

        qkv, h = _qkv_proj(xs, mod, w_qkv, rope, geom)
        u_hy = _matmul(h, w_hy, F32, 768, nb=nb, tb=tb)
        gates = _matmul(h, w_g, BF16, 1024, act="sigmoid", nb=nb, tb=tb)

        q_m, k_m, v_m = _mla_prep(h, w_mla, mla_q_g[i][None], mla_kv_g[i][None], _pad_w_uq(mla_w_uq[i]),
                                  _split_w_ukv(mla_w_ukv[i]), rope)
        oa = _diff_attn(qkv.reshape(B, T, 3 * DA_W), lam, da_subln_g[i][None], n, nc, not last, lam_init)
        ob = _mla_attn(q_m.reshape(B, T, -1), k_m.reshape(B, T, -1), v_m.reshape(B, T, -1), n, nc, not last)
        oc = _hyena(u_hy.reshape(B, -1, 3 * HY_CH), i, hy, n, nc, not last, tables)

        merged = _merge(oa.reshape(-1, DA_W), ob.reshape(-1, MLA_W), oc.reshape(-1, HY_CH),
                        w_branch_a[i].astype(BF16), w_branch_b[i].astype(BF16), w_branch_c[i].astype(BF16), gates,
                        nb, tb)
        xs1, h2 = _wo_ln(merged, w_out[i].astype(BF16), xs, mod, ln1_g[i][None], ln1_b[i][None], sel, nb, tb)
        u = _ffn_up(h2, ffn_w_up, i, ffn_conv_w[i], ffn_conv_b[i][None], keep_prev, keep_next, nb, tb)
        xs = _ffn_down(u, ffn_w_down[i].astype(BF16), xs1, mod, ln2_g[i][None], ln2_b[i][None], sel, nb, tb)
    return xs.reshape(B, n, D)
```

```python
import functools
import math

import numpy as np
import jax
import jax.numpy as jnp
from jax import lax
from jax.experimental import pallas as pl
from jax.experimental.pallas import tpu as pltpu

F32 = jnp.float32
BF16 = jnp.bfloat16

D_MODEL = 2048
DEPTH = 2
GRID_W = 64
ROPE_BASE = 10000.0
NORM_EPS = 1e-6
DA_HEADS = 6
DA_DQK = 64
DA_DV = 128
DA_W = DA_HEADS * DA_DV
MLA_HEADS = 6
MLA_Q_RANK = 512
MLA_KV_RANK = 256
MLA_NOPE = 128
MLA_ROPE = 64
MLA_DV = 128
MLA_W = MLA_HEADS * MLA_DV
MLA_QK_PAD = 256
HY_CH = 512
HY_ORDER = 2
HY_EMB = 33
HY_BANDS = (HY_EMB - 1) // 2
HY_FFN = 64
HY_MIN_DECAY = math.log(1e-2) / 1.5
HY_MAX_DECAY = math.log(1e-2) / 0.3
D_FF = 5632
N_BRANCH = 3
DEEPNORM_ALPHA = (2 * DEPTH) ** 0.25
LOG2E = 1.4426950408889634

ROW_GROUP = 256
LANE = 128
FFT_S = 128
VMEM_LIMIT = 52 * 1024 * 1024


def _cparams(*sem):
    return pltpu.CompilerParams(dimension_semantics=sem, vmem_limit_bytes=VMEM_LIMIT)


def _pick(total, prefs):
    for p in prefs:
        if total % p == 0:
            return p
    raise ValueError(f"no tile for {total} in {prefs}")


def _ln(x):
    mu = jnp.mean(x, axis=-1, keepdims=True)
    xc = x - mu
    var = jnp.mean(xc * xc, axis=-1, keepdims=True)
    return xc * lax.rsqrt(var + NORM_EPS)


def _rms(x):
    return x * lax.rsqrt(jnp.mean(x * x, axis=-1, keepdims=True) + NORM_EPS)


def _rope128(u, cos, sa, sb):
    return u * cos + pltpu.roll(u, 16, 1) * sa + pltpu.roll(u, LANE - 16, 1) * sb


def _ada_kernel(c_ref, w_ref, b_ref, o_ref):
    a = c_ref[...]
    a = a * jax.nn.sigmoid(a)
    o_ref[...] = jnp.dot(a.astype(BF16), w_ref[...].astype(BF16), preferred_element_type=F32) + b_ref[...]


def _ada(cc, ada_w, ada_b):
    L, D, N = ada_w.shape
    tn = 1024
    return pl.pallas_call(
        _ada_kernel,
        out_shape=jax.ShapeDtypeStruct((L, 8, N), F32),
        grid=(L, N // tn),
        in_specs=[pl.BlockSpec((8, D), lambda l, j: (0, 0)),
                  pl.BlockSpec((None, D, tn), lambda l, j: (l, 0, j)),
                  pl.BlockSpec((None, 1, tn), lambda l, j: (l, 0, j))],
        out_specs=pl.BlockSpec((None, 8, tn), lambda l, j: (l, 0, j)),
        compiler_params=_cparams("arbitrary", "arbitrary"),
    )(cc, ada_w, ada_b)


def _mod_row(g, gpb, lat_groups, n_batch):
    return jnp.where(g % gpb < lat_groups, g // gpb, n_batch)


def _mm_kernel(a_ref, w_ref, o_ref, *, act):
    acc = jnp.dot(a_ref[...], w_ref[...], preferred_element_type=F32)
    if act == "sigmoid":
        acc = jax.nn.sigmoid(acc)
    o_ref[...] = acc.astype(o_ref.dtype)


def _by_batch(a, nb):
    return a.reshape(nb, a.shape[0] // nb, a.shape[1])


def _matmul(a, w, out_dtype, tn, act=None, nb=1, tb=None):
    a = _by_batch(a, nb)
    K = a.shape[2]
    tb = tb or a.shape[1]
    N = w.shape[1]
    tm = _pick(tb, (1024, 512, 256))
    return pl.pallas_call(
        functools.partial(_mm_kernel, act=act),
        out_shape=jax.ShapeDtypeStruct((nb, tb, N), out_dtype),
        grid=(nb, tb // tm, N // tn),
        in_specs=[pl.BlockSpec((None, tm, K), lambda b, i, j: (b, i, 0)),
                  pl.BlockSpec((K, tn), lambda b, i, j: (0, j))],
        out_specs=pl.BlockSpec((None, tm, tn), lambda b, i, j: (b, i, j)),
        compiler_params=_cparams("arbitrary", "arbitrary", "arbitrary"),
    )(a, w).reshape(nb * tb, N)


def _qkv_kernel(x_ref, mod_ref, w_ref, cos_ref, sa_ref, sb_ref, o_ref, h_ref, *, qscale, geom, groups):
    j = pl.program_id(1)

    @pl.when(j == 0)
    def _():
        for q in range(groups):
            mod = mod_ref[_mod_row(pl.program_id(0) * groups + q, *geom)]
            sl = slice(q * ROW_GROUP, (q + 1) * ROW_GROUP)
            h_ref[sl, :] = (_ln(x_ref[sl, :]) * (1.0 + mod[1:2, :]) + mod[0:1, :]).astype(BF16)

    acc = jnp.dot(h_ref[...], w_ref[...], preferred_element_type=F32)

    @pl.when(j < 2)
    def _():
        cos, sa, sb = cos_ref[...], sa_ref[...], sb_ref[...]
        scale = jnp.where(j == 0, qscale, 1.0).astype(F32)
        for c in range(DA_HEADS):
            u = acc[:, c * LANE:(c + 1) * LANE]
            o_ref[:, c * LANE:(c + 1) * LANE] = (_rope128(u, cos, sa, sb) * scale).astype(BF16)

    @pl.when(j == 2)
    def _():
        o_ref[...] = acc.astype(BF16)


def _qkv_proj(xs, mod, w_qkv, tabs, geom):
    M, K = xs.shape
    tm = _pick(M, (1024, 512, 256))
    tn = DA_W
    row = lambda i, j: (i, 0)
    tab_spec = pl.BlockSpec((tm, LANE), row)
    return pl.pallas_call(
        functools.partial(_qkv_kernel, qscale=DA_DQK ** -0.5 * LOG2E, geom=geom, groups=tm // ROW_GROUP),
        out_shape=(jax.ShapeDtypeStruct((M, 3 * DA_W), BF16), jax.ShapeDtypeStruct((M, K), BF16)),
        grid=(M // tm, 3),
        in_specs=[pl.BlockSpec((tm, K), row),
                  pl.BlockSpec(mod.shape, lambda i, j: (0, 0, 0)),
                  pl.BlockSpec((K, tn), lambda i, j: (0, j)),
                  tab_spec, tab_spec, tab_spec],
        out_specs=(pl.BlockSpec((tm, tn), lambda i, j: (i, j)), pl.BlockSpec((tm, K), row)),
        compiler_params=_cparams("arbitrary", "arbitrary"),
    )(xs, mod, w_qkv, *tabs)


def _mla_prep_kernel(h_ref, wm_ref, qg_ref, kvg_ref, wuq_ref, wukv_ref, cos_ref, sa_ref, sb_ref,
                     q_ref, k_ref, v_ref, *, qscale):
    p = jnp.dot(h_ref[...], wm_ref[...], preferred_element_type=F32)
    cos, sa, sb = cos_ref[...], sa_ref[...], sb_ref[...]
    cq = p[:, :MLA_Q_RANK]
    ckv = p[:, MLA_Q_RANK:MLA_Q_RANK + MLA_KV_RANK]
    kr = p[:, MLA_Q_RANK + MLA_KV_RANK:]
    qn = (_rms(cq) * qg_ref[...]).astype(BF16)
    q = jnp.dot(qn, wuq_ref[...], preferred_element_type=F32)
    kvn = (_rms(ckv) * kvg_ref[...]).astype(BF16)
    kv = jnp.dot(kvn, wukv_ref[...], preferred_element_type=F32)
    krr = _rope128(kr, cos, sa, sb).astype(BF16)
    for h in range(MLA_HEADS):
        o = h * MLA_QK_PAD
        q_ref[:, o:o + LANE] = (q[:, o:o + LANE] * qscale).astype(BF16)
        q_ref[:, o + LANE:o + 2 * LANE] = (_rope128(q[:, o + LANE:o + 2 * LANE], cos, sa, sb) * qscale).astype(BF16)
        k_ref[:, o:o + LANE] = kv[:, h * LANE:(h + 1) * LANE].astype(BF16)
        k_ref[:, o + LANE:o + 2 * LANE] = krr
    v_ref[...] = kv[:, MLA_W:].astype(BF16)


def _mla_prep(h, w_mla, q_g, kv_g, w_uq, w_ukv, tabs):
    M, D = h.shape
    tm = _pick(M, (512, 256))
    row = lambda i: (i, 0)
    full = lambda i: (0, 0)
    qk_w = MLA_HEADS * MLA_QK_PAD
    return pl.pallas_call(
        functools.partial(_mla_prep_kernel, qscale=(MLA_NOPE + MLA_ROPE) ** -0.5 * LOG2E),
        out_shape=(jax.ShapeDtypeStruct((M, qk_w), BF16),
                   jax.ShapeDtypeStruct((M, qk_w), BF16),
                   jax.ShapeDtypeStruct((M, MLA_W), BF16)),
        grid=(M // tm,),
        in_specs=[pl.BlockSpec((tm, D), row),
                  pl.BlockSpec(w_mla.shape, full),
                  pl.BlockSpec((1, MLA_Q_RANK), full),
                  pl.BlockSpec((1, MLA_KV_RANK), full),
                  pl.BlockSpec(w_uq.shape, full),
                  pl.BlockSpec(w_ukv.shape, full),
                  pl.BlockSpec((tm, LANE), row), pl.BlockSpec((tm, LANE), row), pl.BlockSpec((tm, LANE), row)],
        out_specs=(pl.BlockSpec((tm, qk_w), row), pl.BlockSpec((tm, qk_w), row), pl.BlockSpec((tm, MLA_W), row)),
        compiler_params=_cparams("arbitrary"),
    )(h, w_mla, q_g, kv_g, w_uq, w_ukv, *tabs)


_NT = (((1,), (1,)), ((), ()))
ATTN_TQ = 256
ATTN_CHAINS = 8
ATTN_TK = 256


def _skewed_pipeline(n_chains, n_chunks, stages):
    for t in range(n_chains + len(stages) - 1):
        active = [(s, t - s) for s in range(len(stages)) if 0 <= t - s < n_chains]
        for c in range(n_chunks):
            for s, chain in active:
                stages[s][0](chain, c)
        for s, chain in active:
            stages[s][1](chain)


def _acc(old, new, op):
    return new if old is None else op(old, new)


def _lane_halves(x, op):
    return op(x[:, :LANE], x[:, LANE:])


def _attn_steps(run, q_rows, n_keys, lat_steps, n_lat, ctx_rows):
    if ctx_rows == 0:
        run(q_rows, 0, n_keys)
        return
    step = pl.program_id(2)

    @pl.when(step < lat_steps)
    def _():
        run(q_rows, 0, n_keys)

    @pl.when(step == lat_steps)
    def _():
        run(ctx_rows, n_lat, n_keys - n_lat)


def _diff_attn_kernel(lam_ref, q_ref, k_ref, v_ref, g_ref, o_ref, *, out_scale, lat_steps, n_lat, ctx_rows):
    run = functools.partial(_diff_attn_run, lam_ref, q_ref, k_ref, v_ref, g_ref, o_ref, out_scale)
    _attn_steps(run, q_ref.shape[0], k_ref.shape[0], lat_steps, n_lat, ctx_rows)


def _diff_attn_run(lam_ref, q_ref, k_ref, v_ref, g_ref, o_ref, out_scale, q_rows, key0, n_keys):
    lam, g = lam_ref[0], g_ref[...]
    tq = min(ATTN_TQ, q_rows)
    n_chains = q_rows // tq
    n_chunks = n_keys // ATTN_TK
    lane = lax.broadcasted_iota(jnp.int32, (1, LANE), 1)
    lo = (lane < DA_DQK).astype(F32)
    st = [dict(s=[], e=[], mx=[None, None], l=[None, None], o=None) for _ in range(n_chains)]
    rows = lambda i: slice(i * tq, (i + 1) * tq)
    keys = lambda c: slice(key0 + c * ATTN_TK, key0 + (c + 1) * ATTN_TK)

    def qk_chunk(i, c):
        d = st[i]
        if c == 0:
            qf = q_ref[rows(i), :].astype(F32)
            d["q"] = jnp.concatenate([(qf * lo).astype(BF16), (qf * (1.0 - lo)).astype(BF16)], axis=0)
        both = lax.dot_general(d["q"], k_ref[keys(c), :], _NT, preferred_element_type=F32)
        pair = []
        for m in range(2):
            s = both[m * tq:(m + 1) * tq]
            d["mx"][m] = _acc(d["mx"][m], _lane_halves(s, jnp.maximum), jnp.maximum)
            pair.append(s)
        d["s"].append(pair)

    def qk_done(i):
        st[i]["m"] = [jnp.max(mx, axis=-1, keepdims=True) for mx in st[i]["mx"]]

    def exp_chunk(i, c):
        d = st[i]
        pair = []
        for m in range(2):
            e = jnp.exp2(d["s"][c][m] - d["m"][m])
            d["l"][m] = _acc(d["l"][m], _lane_halves(e, jnp.add), jnp.add)
            pair.append(e)
        d["s"][c] = None
        d["e"].append(pair)

    def exp_done(i):
        d = st[i]
        l1, l2 = [jnp.sum(l, axis=-1, keepdims=True) for l in d["l"]]
        d["r1"] = 1.0 / l1
        d["cf"] = lam * l1 / l2

    def pv_chunk(i, c):
        d = st[i]
        w = (d["e"][c][0] - d["cf"] * d["e"][c][1]).astype(BF16)
        d["e"][c] = None
        d["o"] = _acc(d["o"], jnp.dot(w, v_ref[keys(c), :], preferred_element_type=F32), jnp.add)

    def pv_done(i):
        o = st[i]["o"] * st[i]["r1"]
        o_ref[rows(i), :] = (_rms(o) * g * out_scale).astype(BF16)

    _skewed_pipeline(n_chains, n_chunks, [(qk_chunk, qk_done), (exp_chunk, exp_done), (pv_chunk, pv_done)])


def _mla_attn_kernel(q_ref, k_ref, v_ref, o_ref, *, lat_steps, n_lat, ctx_rows):
    run = functools.partial(_mla_attn_run, q_ref, k_ref, v_ref, o_ref)
    _attn_steps(run, q_ref.shape[0], k_ref.shape[0], lat_steps, n_lat, ctx_rows)


def _mla_attn_run(q_ref, k_ref, v_ref, o_ref, q_rows, key0, n_keys):
    tq = min(ATTN_TQ, q_rows)
    n_chains = q_rows // tq
    n_chunks = n_keys // ATTN_TK
    st = [dict(s=[], mx=None, l=None, o=None) for _ in range(n_chains)]
    rows = lambda i: slice(i * tq, (i + 1) * tq)
    keys = lambda c: slice(key0 + c * ATTN_TK, key0 + (c + 1) * ATTN_TK)

    def qk_chunk(i, c):
        d = st[i]
        s = lax.dot_general(q_ref[rows(i), :], k_ref[keys(c), :], _NT, preferred_element_type=F32)
        d["mx"] = _acc(d["mx"], _lane_halves(s, jnp.maximum), jnp.maximum)
        d["s"].append(s)

    def qk_done(i):
        st[i]["m"] = jnp.max(st[i]["mx"], axis=-1, keepdims=True)

    def pv_chunk(i, c):
        d = st[i]
        e = jnp.exp2(d["s"][c] - d["m"])
        d["s"][c] = None
        d["l"] = _acc(d["l"], _lane_halves(e, jnp.add), jnp.add)
        d["o"] = _acc(d["o"], jnp.dot(e.astype(BF16), v_ref[keys(c), :], preferred_element_type=F32), jnp.add)

    def pv_done(i):
        d = st[i]
        o_ref[rows(i), :] = (d["o"] * (1.0 / jnp.sum(d["l"], axis=-1, keepdims=True))).astype(BF16)

    _skewed_pipeline(n_chains, n_chunks, [(qk_chunk, qk_done), (pv_chunk, pv_done)])


def _diff_attn(qkv, lam, subln_g, n, nc, ctx_queries, lam_init):
    B, T, _ = qkv.shape
    H = DA_HEADS
    tq = _pick(n, (ATTN_TQ * ATTN_CHAINS, ATTN_TQ))
    lat_steps = n // tq
    kern = functools.partial(_diff_attn_kernel, out_scale=1.0 - lam_init, lat_steps=lat_steps, n_lat=n,
                             ctx_rows=nc if ctx_queries else 0)
    return pl.pallas_call(
        kern,
        out_shape=jax.ShapeDtypeStruct((B, T if ctx_queries else n, DA_W), BF16),
        grid=(B, H, lat_steps + int(ctx_queries)),
        in_specs=[pl.BlockSpec(memory_space=pltpu.SMEM),
                  pl.BlockSpec((None, tq, LANE), lambda b, h, i: (b, i, h)),
                  pl.BlockSpec((None, T, LANE), lambda b, h, i: (b, 0, H + h)),
                  pl.BlockSpec((None, T, LANE), lambda b, h, i: (b, 0, 2 * H + h)),
                  pl.BlockSpec((1, DA_DV), lambda b, h, i: (0, 0))],
        out_specs=pl.BlockSpec((None, tq, LANE), lambda b, h, i: (b, i, h)),
        compiler_params=_cparams("arbitrary", "arbitrary", "arbitrary"),
    )(lam, qkv, qkv, qkv, subln_g)


def _mla_attn(q, k, v, n, nc, ctx_queries):
    B, T, _ = q.shape
    H = MLA_HEADS
    tq = _pick(n, (ATTN_TQ * ATTN_CHAINS, ATTN_TQ))
    lat_steps = n // tq
    kern = functools.partial(_mla_attn_kernel, lat_steps=lat_steps, n_lat=n, ctx_rows=nc if ctx_queries else 0)
    return pl.pallas_call(
        kern,
        out_shape=jax.ShapeDtypeStruct((B, T if ctx_queries else n, MLA_W), BF16),
        grid=(B, H, lat_steps + int(ctx_queries)),
        in_specs=[pl.BlockSpec((None, tq, MLA_QK_PAD), lambda b, h, i: (b, i, h)),
                  pl.BlockSpec((None, T, MLA_QK_PAD), lambda b, h, i: (b, 0, h)),
                  pl.BlockSpec((None, T, MLA_DV), lambda b, h, i: (b, 0, h))],
        out_specs=pl.BlockSpec((None, tq, MLA_DV), lambda b, h, i: (b, i, h)),
        compiler_params=_cparams("arbitrary", "arbitrary", "arbitrary"),
    )(q, k, v)


def _hy_filter_kernel(feat_ref, dec_ref, w1_ref, b1_ref, w2_ref, b2_ref, w3_ref, b3_ref, w4_ref, fr_ref,
                      k_ref, s_ref):
    hp = lax.Precision.HIGHEST
    d, r = pl.program_id(0), pl.program_id(1)
    fr = fr_ref[...]
    h = jnp.sin(fr * (jnp.dot(feat_ref[...], w1_ref[...], precision=hp, preferred_element_type=F32) + b1_ref[...]))
    h = jnp.sin(fr * (jnp.dot(h, w2_ref[...], precision=hp, preferred_element_type=F32) + b2_ref[...]))
    h = jnp.sin(fr * (jnp.dot(h, w3_ref[...], precision=hp, preferred_element_type=F32) + b3_ref[...]))
    h = jnp.dot(h, w4_ref[...], precision=hp, preferred_element_type=F32)
    row = lax.broadcasted_iota(jnp.int32, (h.shape[0], 1), 0)
    first_bwd = jnp.where((d == 1) & (r == 0), 1.0, 0.0)
    scale = dec_ref[...] * (1.0 - jnp.where(row == 0, 1.0, 0.0) * first_bwd)

    @pl.when((d == 0) & (r == 0))
    def _():
        s_ref[...] = jnp.zeros(s_ref.shape, F32)

    for o in range(HY_ORDER):
        ko = h[:, o * HY_CH:(o + 1) * HY_CH] * scale
        k_ref[o] = ko
        s_ref[o] += jnp.broadcast_to(jnp.sum(jnp.abs(ko), axis=0, keepdims=True), (8, HY_CH))


def _hy_filter(n, w1, b1, w2, b2, w3, b3, w4, fr):
    C = HY_CH
    t = jnp.linspace(0.0, 1.0, n, dtype=F32)
    pos = jnp.arange(n, dtype=F32)
    t2 = jnp.concatenate([t, t[::-1]])[:, None]
    pos2 = jnp.concatenate([pos, pos[::-1]])[:, None]
    phase = (2.0 * math.pi / n) * pos2 * jnp.linspace(1e-4, HY_BANDS - 1, HY_BANDS, dtype=F32)[None, :]
    feat = jnp.concatenate([t2, jnp.cos(phase), -jnp.sin(phase)], axis=-1)
    feat = jnp.pad(feat, ((0, 0), (0, HY_FFN - HY_EMB)))
    dec = jnp.exp(-t2 * jnp.abs(jnp.linspace(HY_MIN_DECAY, HY_MAX_DECAY, C, dtype=F32)))
    w4d = w4.reshape(HY_FFN, HY_ORDER, 2, C).transpose(2, 0, 1, 3).reshape(2, HY_FFN, HY_ORDER * C)
    rb = min(512, n)
    nb = n // rb
    full = lambda d, r: (0, 0)
    return pl.pallas_call(
        _hy_filter_kernel,
        out_shape=(jax.ShapeDtypeStruct((HY_ORDER, 2 * n, C), F32),
                   jax.ShapeDtypeStruct((HY_ORDER, 8, C), F32)),
        grid=(2, nb),
        in_specs=[pl.BlockSpec((rb, HY_FFN), lambda d, r: (d * nb + r, 0)),
                  pl.BlockSpec((rb, C), lambda d, r: (d * nb + r, 0)),
                  pl.BlockSpec(w1.shape, full), pl.BlockSpec(b1.shape, full),
                  pl.BlockSpec(w2.shape, full), pl.BlockSpec(b2.shape, full),
                  pl.BlockSpec(w3.shape, full), pl.BlockSpec(b3.shape, full),
                  pl.BlockSpec((None, HY_FFN, HY_ORDER * C), lambda d, r: (d, 0, 0)),
                  pl.BlockSpec(fr.shape, full)],
        out_specs=(pl.BlockSpec((HY_ORDER, rb, C), lambda d, r: (0, d * nb + r, 0)),
                   pl.BlockSpec((HY_ORDER, 8, C), lambda d, r: (0, 0, 0))),
        compiler_params=_cparams("arbitrary", "arbitrary"),
    )(feat, dec, w1, b1, w2, b2, w3, b3, w4d, fr)


def _hy_dwconv_kernel(*refs):
    for k in range(3):
        u_ref, w_ref, b_ref, o_ref = refs[3 * k], refs[3 * k + 1], refs[3 * k + 2], refs[9 + k]
        u = u_ref[...]
        L = u.shape[0]
        row = lax.broadcasted_iota(jnp.int32, (L, 1), 0)
        up = jnp.where(row == 0, 0.0, pltpu.roll(u, 1, 0))
        dn = jnp.where(row == L - 1, 0.0, pltpu.roll(u, L - 1, 0))
        o_ref[...] = up * w_ref[0:1, :] + u * w_ref[1:2, :] + dn * w_ref[2:3, :] + b_ref[...]


def _hy_dwconv(u, w, b, row_block, length):
    B = u.shape[0]
    cw = LANE
    per = HY_CH // cw
    in_specs, args = [], []
    for k in range(3):
        in_specs += [pl.BlockSpec((None, length, cw), lambda bb, j, k=k: (bb, row_block, k * per + j)),
                     pl.BlockSpec((3, cw), lambda bb, j, k=k: (0, k * per + j)),
                     pl.BlockSpec((1, cw), lambda bb, j, k=k: (0, k * per + j))]
        args += [u, w, b]
    ospec = pl.BlockSpec((None, length, cw), lambda bb, j: (bb, 0, j))
    return pl.pallas_call(
        _hy_dwconv_kernel,
        out_shape=(jax.ShapeDtypeStruct((B, length, HY_CH), F32),) * 3,
        grid=(B, per),
        in_specs=in_specs,
        out_specs=(ospec,) * 3,
        compiler_params=_cparams("arbitrary", "arbitrary"),
    )(*args)


def _dft_tables(n):
    S = FFT_S
    M = 2 * n
    N1 = M // S
    H = N1 // 2
    s2 = np.arange(S)[:, None, None]
    k1 = np.arange(N1)[None, :, None]
    s1 = np.arange(N1)[None, None, :]
    ang = -2.0 * np.pi * ((k1 * (S * s1 + s2)) % M) / M
    fr, fi = np.cos(ang), np.sin(ang)
    g1f = np.concatenate([fr, fi], axis=1)
    frh, fih = fr[:, :, :H], fi[:, :, :H]
    g1d = np.concatenate([np.concatenate([frh, -fih], axis=2),
                          np.concatenate([fih, frh], axis=2)], axis=1)
    er = np.transpose(frh, (0, 2, 1)) / M
    ei = -np.transpose(fih, (0, 2, 1)) / M
    g3 = np.concatenate([np.concatenate([er, -ei], axis=2),
                         np.concatenate([ei, er], axis=2)], axis=1)
    a2 = -2.0 * np.pi * ((np.arange(S)[:, None] * np.arange(S)[None, :]) % S) / S
    f2r, f2i = np.cos(a2), np.sin(a2)
    g2 = np.block([[f2r, -f2i], [f2i, f2r]])
    g2i = np.block([[f2r, f2i], [-f2i, f2r]])
    cast = lambda a: jnp.asarray(a, dtype=F32).astype(BF16)
    return cast(g1d), cast(g1f), cast(g2), cast(g2i), cast(g3)


FFT_NS = 32
FFT_NK = 8


def _fft_s1_kernel(g_ref, x_ref, o_ref):
    xt = jnp.swapaxes(x_ref[...], 0, 1)
    y = jnp.stack([jnp.dot(g_ref[j], xt[j].astype(BF16), preferred_element_type=F32)
                   for j in range(xt.shape[0])], axis=0)
    o_ref[...] = jnp.swapaxes(y, 0, 1).astype(o_ref.dtype)


def _fft_s1(g, x):
    P, Ri, S, C = x.shape
    Ro = g.shape[1]
    ns = FFT_NS
    return pl.pallas_call(
        _fft_s1_kernel,
        out_shape=jax.ShapeDtypeStruct((P, Ro, S, C), BF16),
        grid=(S // ns, P),
        in_specs=[pl.BlockSpec((ns, Ro, Ri), lambda j, p: (j, 0, 0)),
                  pl.BlockSpec((None, Ri, ns, C), lambda j, p: (p, 0, j, 0))],
        out_specs=pl.BlockSpec((None, Ro, ns, C), lambda j, p: (p, 0, j, 0)),
        compiler_params=_cparams("arbitrary", "arbitrary"),
    )(g, x)


def _fft_s2_filt_kernel(a_ref, g_ref, rn_ref, o_ref, *, nk):
    S = FFT_S
    rn = rn_ref[...]
    for t in range(nk):
        d = jnp.concatenate([a_ref[0, t], a_ref[1, t]], axis=0)
        y = jnp.dot(g_ref[...], d, preferred_element_type=F32)
        o_ref[0, t] = y[:S] * rn
        o_ref[1, t] = y[S:] * rn


def _fft_s2_filt(a, g2, rnorm):
    O, _, N1, S, C = a.shape
    nk = FFT_NK
    blk = (None, 2, nk, S, C)
    return pl.pallas_call(
        functools.partial(_fft_s2_filt_kernel, nk=nk),
        out_shape=jax.ShapeDtypeStruct(a.shape, F32),
        grid=(O, N1 // nk),
        in_specs=[pl.BlockSpec(blk, lambda o, j: (o, 0, j, 0, 0)),
                  pl.BlockSpec(g2.shape, lambda o, j: (0, 0)),
                  pl.BlockSpec((None, 1, C), lambda o, j: (o, 0, 0))],
        out_specs=pl.BlockSpec(blk, lambda o, j: (o, 0, j, 0, 0)),
        compiler_params=_cparams("arbitrary", "arbitrary"),
    )(a, g2, rnorm)


def _fft_s2_kernel(a_ref, g_ref, gi_ref, kf_ref, o_ref, *, nk):
    S = FFT_S
    for t in range(nk):
        d = jnp.concatenate([a_ref[0, t], a_ref[1, t]], axis=0)
        y = jnp.dot(g_ref[...], d, preferred_element_type=F32)
        yr, yi = y[:S], y[S:]
        kr, ki = kf_ref[0, t], kf_ref[1, t]
        p = jnp.concatenate([yr * kr - yi * ki, yr * ki + yi * kr], axis=0).astype(BF16)
        b = jnp.dot(gi_ref[...], p, preferred_element_type=F32)
        o_ref[0, t] = b[:S].astype(BF16)
        o_ref[1, t] = b[S:].astype(BF16)


def _fft_s2(a, g2, g2i, kf, order):
    P, _, N1, S, C = a.shape
    nk = FFT_NK
    blk = (None, 2, nk, S, C)
    return pl.pallas_call(
        functools.partial(_fft_s2_kernel, nk=nk),
        out_shape=jax.ShapeDtypeStruct(a.shape, BF16),
        grid=(N1 // nk, P),
        in_specs=[pl.BlockSpec(blk, lambda j, p: (p, 0, j, 0, 0)),
                  pl.BlockSpec(g2.shape, lambda j, p: (0, 0)),
                  pl.BlockSpec(g2i.shape, lambda j, p: (0, 0)),
                  pl.BlockSpec(blk, lambda j, p: (order, 0, j, 0, 0))],
        out_specs=pl.BlockSpec(blk, lambda j, p: (p, 0, j, 0, 0)),
        compiler_params=_cparams("arbitrary", "arbitrary"),
    )(a, g2, g2i, kf)


def _fft_s3_kernel(g_ref, b_ref, z_ref, gate_ref, skip_ref, o_ref):
    bt = jnp.swapaxes(b_ref[...].astype(F32), 0, 1).astype(BF16)
    y = jnp.stack([jnp.dot(g_ref[j], bt[j], preferred_element_type=F32) for j in range(bt.shape[0])], axis=0)
    y = jnp.swapaxes(y, 0, 1)
    o_ref[...] = (gate_ref[...] * (y + skip_ref[...] * z_ref[...])).astype(o_ref.dtype)


def _fft_s3(g3, b, z, gate, skip, out_dtype):
    P, Ri, S, C = b.shape
    Ro = g3.shape[1]
    ns = FFT_NS
    dspec = pl.BlockSpec((None, Ro, ns, C), lambda j, p: (p, 0, j, 0))
    return pl.pallas_call(
        _fft_s3_kernel,
        out_shape=jax.ShapeDtypeStruct((P, Ro, S, C), out_dtype),
        grid=(S // ns, P),
        in_specs=[pl.BlockSpec((ns, Ro, Ri), lambda j, p: (j, 0, 0)),
                  pl.BlockSpec((None, Ri, ns, C), lambda j, p: (p, 0, j, 0)),
                  dspec, dspec,
                  pl.BlockSpec((1, C), lambda j, p: (0, 0))],
        out_specs=dspec,
        compiler_params=_cparams("arbitrary", "arbitrary"),
    )(g3, b, z, gate, skip)


def _dense_tables(m):
    M = 2 * m
    ang = -2.0 * np.pi * ((np.arange(M)[:, None] * np.arange(M)[None, :]) % M) / M
    fr, fi = np.cos(ang), np.sin(ang)
    gk = np.concatenate([fr, fi], axis=0)
    gd = np.block([[fr[:, :m], -fi[:, :m]], [fi[:, :m], fr[:, :m]]])
    er, ei = fr[:m, :] / M, -fi[:m, :] / M
    gi = np.block([[er, -ei], [ei, er]])
    cast = lambda a: jnp.asarray(a, dtype=F32).astype(BF16)
    return cast(gk), cast(gd), cast(gi)


def _dense_spec_kernel(k_ref, g_ref, rn_ref, o_ref):
    o_ref[...] = jnp.dot(g_ref[...], k_ref[...].astype(BF16), preferred_element_type=F32) * rn_ref[...]


def _dense_spec(kc, gk, rnorm):
    O, M, C = kc.shape
    return pl.pallas_call(
        _dense_spec_kernel,
        out_shape=jax.ShapeDtypeStruct((O, 2 * M, C), F32),
        grid=(O,),
        in_specs=[pl.BlockSpec((None, M, C), lambda o: (o, 0, 0)),
                  pl.BlockSpec(gk.shape, lambda o: (0, 0)),
                  pl.BlockSpec((None, 1, C), lambda o: (o, 0, 0))],
        out_specs=pl.BlockSpec((None, 2 * M, C), lambda o: (o, 0, 0)),
        compiler_params=_cparams("arbitrary"),
    )(kc, gk, rnorm)


def _dense_conv_kernel(x_ref, gd_ref, gi_ref, kf_ref, gate_ref, skip_ref, o_ref):
    x = x_ref[...]
    y = jnp.dot(gd_ref[...], x.astype(BF16), preferred_element_type=F32)
    M = y.shape[0] // 2
    yr, yi = y[:M], y[M:]
    kr, ki = kf_ref[:M], kf_ref[M:]
    p = jnp.concatenate([yr * kr - yi * ki, yr * ki + yi * kr], axis=0).astype(BF16)
    conv = jnp.dot(gi_ref[...], p, preferred_element_type=F32)
    o_ref[...] = (gate_ref[...] * (conv + skip_ref[...] * x)).astype(o_ref.dtype)


def _dense_conv(x, gd, gi, kf, order, gate, skip, out_dtype):
    P, R, C = x.shape
    dspec = pl.BlockSpec((None, R, C), lambda p: (p, 0, 0))
    return pl.pallas_call(
        _dense_conv_kernel,
        out_shape=jax.ShapeDtypeStruct((P, R, C), out_dtype),
        grid=(P,),
        in_specs=[dspec,
                  pl.BlockSpec(gd.shape, lambda p: (0, 0)),
                  pl.BlockSpec(gi.shape, lambda p: (0, 0)),
                  pl.BlockSpec((None,) + kf.shape[1:], lambda p: (order, 0, 0)),
                  dspec,
                  pl.BlockSpec((1, C), lambda p: (0, 0))],
        out_specs=dspec,
        compiler_params=_cparams("arbitrary"),
    )(x, gd, gi, kf, gate, skip)


def _merge_kernel(oa_ref, ob_ref, oc_ref, wa_ref, wb_ref, wc_ref, g0_ref, g1_ref, g2_ref, o_ref):
    ya = jnp.dot(oa_ref[...], wa_ref[...], preferred_element_type=F32)
    yb = jnp.dot(ob_ref[...], wb_ref[...], preferred_element_type=F32)
    yc = jnp.dot(oc_ref[...], wc_ref[...], preferred_element_type=F32)
    m = g0_ref[...].astype(F32) * ya + g1_ref[...].astype(F32) * yb + g2_ref[...].astype(F32) * yc
    o_ref[...] = m.astype(BF16)


def _merge(oa, ob, oc, w_ba, w_bb, w_bc, gates, nb, tb):
    D = D_MODEL
    tm = _pick(tb, (1024, 512, 256))
    tn = 1024
    nt = D // tn
    oa, ob, oc, gates = (_by_batch(a, nb) for a in (oa, ob, oc, gates))
    row = lambda b, i, j: (b, i, 0)
    col = lambda b, i, j: (0, j)
    return pl.pallas_call(
        _merge_kernel,
        out_shape=jax.ShapeDtypeStruct((nb, tb, D), BF16),
        grid=(nb, tb // tm, nt),
        in_specs=[pl.BlockSpec((None, tm, DA_W), row), pl.BlockSpec((None, tm, MLA_W), row),
                  pl.BlockSpec((None, tm, HY_CH), row),
                  pl.BlockSpec((DA_W, tn), col), pl.BlockSpec((MLA_W, tn), col), pl.BlockSpec((HY_CH, tn), col),
                  pl.BlockSpec((None, tm, tn), lambda b, i, j: (b, i, j)),
                  pl.BlockSpec((None, tm, tn), lambda b, i, j: (b, i, nt + j)),
                  pl.BlockSpec((None, tm, tn), lambda b, i, j: (b, i, 2 * nt + j))],
        out_specs=pl.BlockSpec((None, tm, tn), lambda b, i, j: (b, i, j)),
        compiler_params=_cparams("arbitrary", "arbitrary", "arbitrary"),
    )(oa, ob, oc, w_ba, w_bb, w_bc, gates, gates, gates).reshape(nb * tb, D)


def _wo_ln_kernel(m_ref, w_ref, xs_ref, mod_ref, g_ref, b_ref, xs1_ref, h2_ref, *, sel, groups):
    y = jnp.dot(m_ref[...], w_ref[...], preferred_element_type=F32)
    g, b = g_ref[...], b_ref[...]
    for q in range(groups):
        mod = mod_ref[sel(pl.program_id(0), pl.program_id(1) * groups + q)]
        sl = slice(q * ROW_GROUP, (q + 1) * ROW_GROUP)
        x1 = _ln(DEEPNORM_ALPHA * xs_ref[sl, :] + mod[2:3, :] * y[sl, :]) * g + b
        xs1_ref[sl, :] = x1
        h2_ref[sl, :] = (_ln(x1) * (1.0 + mod[4:5, :]) + mod[3:4, :]).astype(BF16)


def _wo_ln(merged, w_o, xs, mod, ln_g, ln_b, sel, nb, tb):
    D = D_MODEL
    tm = _pick(tb, (512, 256))
    merged, xs = _by_batch(merged, nb), _by_batch(xs, nb)
    row = lambda b, i: (b, i, 0)
    full2 = lambda b, i: (0, 0)
    xs1, h2 = pl.pallas_call(
        functools.partial(_wo_ln_kernel, sel=sel, groups=tm // ROW_GROUP),
        out_shape=(jax.ShapeDtypeStruct((nb, tb, D), F32), jax.ShapeDtypeStruct((nb, tb, D), BF16)),
        grid=(nb, tb // tm),
        in_specs=[pl.BlockSpec((None, tm, D), row),
                  pl.BlockSpec((D, D), full2),
                  pl.BlockSpec((None, tm, D), row),
                  pl.BlockSpec(mod.shape, lambda b, i: (0, 0, 0)),
                  pl.BlockSpec((1, D), full2), pl.BlockSpec((1, D), full2)],
        out_specs=(pl.BlockSpec((None, tm, D), row), pl.BlockSpec((None, tm, D), row)),
        compiler_params=_cparams("arbitrary", "arbitrary"),
    )(merged, w_o, xs, mod, ln_g, ln_b)
    return xs1.reshape(nb * tb, D), h2.reshape(nb * tb, D)


FFN_HALO = 16


def _ffn_up_kernel(hp_ref, h_ref, hn_ref, wa_ref, wv_ref, cw_ref, cb_ref, kp_ref, kn_ref, o_ref, wa_s, wv_s):
    tm, tn = o_ref.shape

    @pl.when((pl.program_id(1) == 0) & (pl.program_id(2) == 0))
    def _():
        wa_s[...] = wa_ref[...].astype(BF16)
        wv_s[...] = wv_ref[...].astype(BF16)

    hm = h_ref[...]
    hext = jnp.concatenate([hp_ref[...], hm, hn_ref[...]], axis=0)
    a = jnp.dot(hext, wa_s[...], preferred_element_type=F32)
    v = jnp.dot(hm, wv_s[...], preferred_element_type=F32)
    ext = tm + 2 * FFN_HALO
    rep = tn // LANE
    keep_prev = jnp.tile(kp_ref[...], (1, rep))
    keep_next = jnp.tile(kn_ref[...], (1, rep))
    a_prev = pltpu.roll(a, 1, 0)[FFN_HALO:FFN_HALO + tm] * keep_prev
    a_next = pltpu.roll(a, ext - 1, 0)[FFN_HALO:FFN_HALO + tm] * keep_next
    cv = a_prev * cw_ref[0:1, :] + a[FFN_HALO:FFN_HALO + tm] * cw_ref[1:2, :] + a_next * cw_ref[2:3, :] + cb_ref[...]
    o_ref[...] = (cv * jax.nn.sigmoid(cv) * v).astype(BF16)


def _ffn_up(h2, w_up, layer, conv_w, conv_b, keep_prev, keep_next, nb, tb):
    D = D_MODEL
    tm = _pick(tb, (1024, 512, 256))
    tn = 512
    nt = D_FF // tn
    hb = tm // FFN_HALO
    last = tb // FFN_HALO - 1
    h2, keep_prev, keep_next = (_by_batch(a, nb) for a in (h2, keep_prev, keep_next))
    mask_spec = pl.BlockSpec((None, tm, LANE), lambda j, b, i: (b, i, 0))
    return pl.pallas_call(
        _ffn_up_kernel,
        out_shape=jax.ShapeDtypeStruct((nb, tb, D_FF), BF16),
        grid=(nt, nb, tb // tm),
        in_specs=[pl.BlockSpec((None, FFN_HALO, D), lambda j, b, i: (b, jnp.maximum(i * hb - 1, 0), 0)),
                  pl.BlockSpec((None, tm, D), lambda j, b, i: (b, i, 0)),
                  pl.BlockSpec((None, FFN_HALO, D), lambda j, b, i: (b, jnp.minimum((i + 1) * hb, last), 0)),
                  pl.BlockSpec((None, D, tn), lambda j, b, i: (layer, 0, j)),
                  pl.BlockSpec((None, D, tn), lambda j, b, i: (layer, 0, nt + j)),
                  pl.BlockSpec((3, tn), lambda j, b, i: (0, j)),
                  pl.BlockSpec((1, tn), lambda j, b, i: (0, j)),
                  mask_spec, mask_spec],
        out_specs=pl.BlockSpec((None, tm, tn), lambda j, b, i: (b, i, j)),
        scratch_shapes=[pltpu.VMEM((D, tn), BF16), pltpu.VMEM((D, tn), BF16)],
        compiler_params=_cparams("arbitrary", "arbitrary", "arbitrary"),
    )(h2, h2, h2, w_up, w_up, conv_w, conv_b, keep_prev, keep_next).reshape(nb * tb, D_FF)


def _ffn_down_kernel(u_ref, w_ref, xs_ref, mod_ref, g_ref, b_ref, o_ref, acc_ref, *, sel, groups, nk):
    k = pl.program_id(2)

    @pl.when(k == 0)
    def _():
        acc_ref[...] = jnp.zeros(acc_ref.shape, F32)

    acc_ref[...] += jnp.dot(u_ref[...], w_ref[...], preferred_element_type=F32)

    @pl.when(k == nk - 1)
    def _():
        g, b = g_ref[...], b_ref[...]
        for q in range(groups):
            mod = mod_ref[sel(pl.program_id(0), pl.program_id(1) * groups + q)]
            sl = slice(q * ROW_GROUP, (q + 1) * ROW_GROUP)
            o_ref[sl, :] = _ln(DEEPNORM_ALPHA * xs_ref[sl, :] + mod[5:6, :] * acc_ref[sl, :]) * g + b


def _ffn_down(u, w_down, xs1, mod, ln_g, ln_b, sel, nb, tb):
    D = D_MODEL
    tm = _pick(tb, (512, 256))
    tk = D_FF // 2
    nk = D_FF // tk
    u, xs1 = _by_batch(u, nb), _by_batch(xs1, nb)
    row = lambda b, i, k: (b, i, 0)
    full2 = lambda b, i, k: (0, 0)
    return pl.pallas_call(
        functools.partial(_ffn_down_kernel, sel=sel, groups=tm // ROW_GROUP, nk=nk),
        out_shape=jax.ShapeDtypeStruct((nb, tb, D), F32),
        grid=(nb, tb // tm, nk),
        in_specs=[pl.BlockSpec((None, tm, tk), lambda b, i, k: (b, i, k)),
                  pl.BlockSpec((tk, D), lambda b, i, k: (k, 0)),
                  pl.BlockSpec((None, tm, D), row),
                  pl.BlockSpec(mod.shape, lambda b, i, k: (0, 0, 0)),
                  pl.BlockSpec((1, D), full2), pl.BlockSpec((1, D), full2)],
        out_specs=pl.BlockSpec((None, tm, D), row),
        scratch_shapes=[pltpu.VMEM((tm, D), F32)],
        compiler_params=_cparams("arbitrary", "arbitrary", "arbitrary"),
    )(u, w_down, xs1, mod, ln_g, ln_b).reshape(nb * tb, D)


def _rope_tables(B, n, nc):
    half = DA_DQK // 2
    inv = ROPE_BASE ** (-jnp.arange(0, half, 2, dtype=F32) / half)
    t = jnp.arange(n, dtype=jnp.int32)
    ang_r = (t // GRID_W).astype(F32)[:, None] * inv[None, :]
    ang_c = (t % GRID_W).astype(F32)[:, None] * inv[None, :]
    ang = jnp.concatenate([ang_r, ang_r, ang_c, ang_c], axis=-1)
    cos, sin = jnp.cos(ang), jnp.sin(ang)
    upper = (jnp.arange(DA_DQK) % half) >= half // 2
    sa = jnp.where(upper, sin, 0.0)
    sb = jnp.where(upper, 0.0, -sin)

    def full(tab, fill):
        tab = jnp.concatenate([tab, jnp.full((nc, DA_DQK), fill, F32)], axis=0)
        tab = jnp.tile(tab, (B, LANE // DA_DQK))
        return tab

    return full(cos, 1.0), full(sa, 0.0), full(sb, 0.0)


def _conv_masks(B, n, nc):
    T = n + nc
    t = jnp.arange(T)
    keep_prev = ((t != 0) & (t != n)).astype(F32)
    keep_next = ((t != n - 1) & (t != T - 1)).astype(F32)
    widen = lambda m: jnp.tile(m[:, None], (B, LANE))
    return widen(keep_prev), widen(keep_next)


W_IN_WIDTHS = (DA_W, DA_W, DA_W, MLA_Q_RANK, MLA_KV_RANK, MLA_ROPE, 3 * HY_CH, N_BRANCH * D_MODEL)
W_IN_OFFS = tuple(sum(W_IN_WIDTHS[:j]) for j in range(len(W_IN_WIDTHS)))
W_MLA_COLS = MLA_Q_RANK + MLA_KV_RANK + LANE


W_HALF = LANE // 2


def _prep_w_in_kernel(a_ref, b_ref, o_ref, *, zero_b_at):
    b = b_ref[...]
    if zero_b_at is not None:
        b = jnp.where(pl.program_id(0) == zero_b_at, 0.0, b)
    o_ref[...] = jnp.transpose(jnp.concatenate([a_ref[...], b], axis=0)).astype(BF16)


def _prep_group(w_t, layer, n_blocks, src, zero_b_at=None):
    D = w_t.shape[2]
    return pl.pallas_call(
        functools.partial(_prep_w_in_kernel, zero_b_at=zero_b_at),
        out_shape=jax.ShapeDtypeStruct((D, n_blocks * LANE), BF16),
        grid=(n_blocks,),
        in_specs=[pl.BlockSpec((None, W_HALF, D), lambda c: (layer, src(c)[0], 0)),
                  pl.BlockSpec((None, W_HALF, D), lambda c: (layer, src(c)[1], 0))],
        out_specs=pl.BlockSpec((D, LANE), lambda c: (0, c)),
        compiler_params=_cparams("arbitrary"),
    )(w_t, w_t)


def _prep_w_in(w_in, layer):
    assert w_in.shape[2] == sum(W_IN_WIDTHS) and all(o % W_HALF == 0 for o in W_IN_OFFS)
    w_t = jnp.swapaxes(w_in, 1, 2)
    o_q, o_k, o_v, o_cq, _, o_kr, o_hy, o_g = (o // W_HALF for o in W_IN_OFFS)
    H = DA_HEADS

    def src_qkv(c):
        part, h = c // H, c % H
        a = jnp.where(part < 2, part * 2 * H + h, o_v + 2 * h)
        return a, jnp.where(part < 2, a + H, a + 1)

    pairs = lambda first: (lambda c: (first + 2 * c, first + 2 * c + 1))
    n_mla = W_MLA_COLS // LANE
    src_mla = lambda c: (o_cq + 2 * c, jnp.minimum(o_cq + 2 * c + 1, o_kr))
    return (_prep_group(w_t, layer, 3 * H, src_qkv),
            _prep_group(w_t, layer, n_mla, src_mla, zero_b_at=n_mla - 1),
            _prep_group(w_t, layer, 3 * HY_CH // LANE, pairs(o_hy)),
            _prep_group(w_t, layer, N_BRANCH * D_MODEL // LANE, pairs(o_g)))


def _pad_w_uq(w):
    w = w.reshape(MLA_Q_RANK, MLA_HEADS, MLA_NOPE + MLA_ROPE)
    w = jnp.pad(w, ((0, 0), (0, 0), (0, MLA_QK_PAD - MLA_NOPE - MLA_ROPE)))
    return w.reshape(MLA_Q_RANK, MLA_HEADS * MLA_QK_PAD).astype(BF16)


def _split_w_ukv(w):
    w = w.reshape(MLA_KV_RANK, MLA_HEADS, 2, MLA_NOPE).transpose(0, 2, 1, 3)
    return w.reshape(MLA_KV_RANK, 2 * MLA_W).astype(BF16)


def _hyena(u3, layer, p, n, nc, with_ctx, tables):
    B = u3.shape[0]
    C = HY_CH
    P = B // 2
    g1d, g1f, g2, g2i, g3, gk, gd, gi = tables
    S = FFT_S
    N1 = 2 * n // S
    mlp = (p["hy_ffn_w1p"][layer], p["hy_ffn_b1"][layer][None], p["hy_ffn_w2"][layer], p["hy_ffn_b2"][layer][None],
           p["hy_ffn_w3"][layer], p["hy_ffn_b3"][layer][None], p["hy_ffn_w4"][layer], p["hy_freq"][layer][None])
    skip = p["hy_skip"][layer]
    cw, cb = p["hy_conv_w"][layer], p["hy_conv_b"][layer][None]

    kc, sums = _hy_filter(n, *mlp)
    kf = _fft_s1(g1f, kc.reshape(HY_ORDER, N1, S, C))
    kf = _fft_s2_filt(kf.reshape(HY_ORDER, 2, N1, S, C), g2, 1.0 / sums[:, 0:1, :])
    dw = [a.reshape(P, N1, S, C) for a in _hy_dwconv(u3, cw, cb, 0, n)]
    z = dw[0]
    for o in range(HY_ORDER):
        a = _fft_s1(g1d, z).reshape(P, 2, N1, S, C)
        b = _fft_s2(a, g2, g2i, kf, o).reshape(P, 2 * N1, S, C)
        z = _fft_s3(g3, b, z, dw[1 + o], skip[o][None], F32 if o + 1 < HY_ORDER else BF16)
    oc_lat = z.reshape(B, n, C)

    if with_ctx:
        kcc, sumc = _hy_filter(nc, *mlp)
        kfc = _dense_spec(kcc, gk, 1.0 / sumc[:, 0:1, :])
        dwc = [a.reshape(P, 2 * nc, C) for a in _hy_dwconv(u3, cw, cb, n // nc, nc)]
        zc = dwc[0]
        for o in range(HY_ORDER):
            zc = _dense_conv(zc, gd, gi, kfc, o, dwc[1 + o], skip[o][None], F32 if o + 1 < HY_ORDER else BF16)
        return jnp.concatenate([oc_lat, zc.reshape(B, nc, C)], axis=1)
    return oc_lat


def kernel(x, c, ctx, c_ctx, ada_w, ada_b, w_in, da_lambda, da_subln_g, mla_q_g, mla_w_uq, mla_kv_g, mla_w_ukv, hy_conv_w, hy_conv_b, hy_ffn_w1, hy_ffn_b1, hy_ffn_w2, hy_ffn_b2, hy_ffn_w3, hy_ffn_b3, hy_ffn_w4, hy_freq, hy_skip, w_branch_a, w_branch_b, w_branch_c, w_out, ln1_g, ln1_b, ffn_w_up, ffn_conv_w, ffn_conv_b, ffn_w_down, ln2_g, ln2_b):
    B, n, D = x.shape
    nc = ctx.shape[1]
    T = n + nc
    rows = B * T
    assert D == D_MODEL and B % 2 == 0 and B < 8
    assert n % ROW_GROUP == 0 and nc % ROW_GROUP == 0 and n % nc == 0 and n % GRID_W == 0
    assert (2 * n) % (8 * FFT_S) == 0 and T % ATTN_TK == 0 and nc % ATTN_TK == 0
    geom = (T // ROW_GROUP, n // ROW_GROUP, B)

    hy = dict(hy_ffn_w1p=jnp.pad(hy_ffn_w1, ((0, 0), (0, HY_FFN - HY_EMB), (0, 0))), hy_ffn_b1=hy_ffn_b1,
              hy_ffn_w2=hy_ffn_w2, hy_ffn_b2=hy_ffn_b2, hy_ffn_w3=hy_ffn_w3, hy_ffn_b3=hy_ffn_b3,
              hy_ffn_w4=hy_ffn_w4, hy_freq=hy_freq, hy_skip=hy_skip, hy_conv_w=hy_conv_w, hy_conv_b=hy_conv_b)
    tables = _dft_tables(n) + _dense_tables(nc)
    rope = _rope_tables(B, n, nc)
    keep_prev, keep_next = _conv_masks(B, n, nc)

    cc = jnp.concatenate([c, c_ctx[None], jnp.zeros((8 - B - 1, D), F32)], axis=0)
    mods = _ada(cc, ada_w, ada_b[:, None, :]).reshape(DEPTH, 8, 6, D)

    xs = jnp.concatenate([x, ctx], axis=1).reshape(rows, D)
    for i in range(DEPTH):
        last = i == DEPTH - 1
        lam_init = 0.8 - 0.6 * math.exp(-0.3 * i)
        lq1, lk1, lq2, lk2 = da_lambda[i].astype(F32)
        lam = (jnp.exp(jnp.sum(lq1 * lk1)) - jnp.exp(jnp.sum(lq2 * lk2)) + lam_init).reshape(1)
        mod = mods[i]
        w_qkv, w_mla, w_hy, w_g = _prep_w_in(w_in, i)

        if last:
            nb, tb = B, n
            sel = lambda b, g: b
        else:
            nb, tb = 1, rows
            sel = lambda b, g: _mod_row(g, *geom)

        qkv, h = _qkv_proj(xs, mod, w_qkv, rope, geom)
        u_hy = _matmul(h, w_hy, F32, 3 * HY_CH, nb=nb, tb=tb)
        gates = _matmul(h, w_g, BF16, D_MODEL, act="sigmoid", nb=nb, tb=tb)

        q_m, k_m, v_m = _mla_prep(h, w_mla, mla_q_g[i][None], mla_kv_g[i][None], _pad_w_uq(mla_w_uq[i]),
                                  _split_w_ukv(mla_w_ukv[i]), rope)
        oa = _diff_attn(qkv.reshape(B, T, 3 * DA_W), lam, da_subln_g[i][None], n, nc, not last, lam_init)
        ob = _mla_attn(q_m.reshape(B, T, -1), k_m.reshape(B, T, -1), v_m.reshape(B, T, -1), n, nc, not last)
        oc = _hyena(u_hy.reshape(B, -1, 3 * HY_CH), i, hy, n, nc, not last, tables)

        merged = _merge(oa.reshape(-1, DA_W), ob.reshape(-1, MLA_W), oc.reshape(-1, HY_CH),
                        w_branch_a[i].astype(BF16), w_branch_b[i].astype(BF16), w_branch_c[i].astype(BF16), gates,
                        nb, tb)
        xs1, h2 = _wo_ln(merged, w_out[i].astype(BF16), xs, mod, ln1_g[i][None], ln1_b[i][None], sel, nb, tb)
        u = _ffn_up(h2, ffn_w_up, i, ffn_conv_w[i], ffn_conv_b[i][None], keep_prev, keep_next, nb, tb)
        xs = _ffn_down(u, ffn_w_down[i].astype(BF16), xs1, mod, ln2_g[i][None], ln2_b[i][None], sel, nb, tb)
    return xs.reshape(B, n, D)
```

```python
import functools
import math

import numpy as np
import jax
import jax.numpy as jnp
from jax import lax
from jax.experimental import pallas as pl
from jax.experimental.pallas import tpu as pltpu

F32 = jnp.float32
BF16 = jnp.bfloat16

D_MODEL = 2048
DEPTH = 2
GRID_W = 64
ROPE_BASE = 10000.0
NORM_EPS = 1e-6
DA_HEADS = 6
DA_DQK = 64
DA_DV = 128
DA_W = DA_HEADS * DA_DV
MLA_HEADS = 6
MLA_Q_RANK = 512
MLA_KV_RANK = 256
MLA_NOPE = 128
MLA_ROPE = 64
MLA_DV = 128
MLA_W = MLA_HEADS * MLA_DV
MLA_QK_PAD = 256
HY_CH = 512
HY_ORDER = 2
HY_EMB = 33
HY_BANDS = (HY_EMB - 1) // 2
HY_FFN = 64
HY_MIN_DECAY = math.log(1e-2) / 1.5
HY_MAX_DECAY = math.log(1e-2) / 0.3
D_FF = 5632
N_BRANCH = 3
DEEPNORM_ALPHA = (2 * DEPTH) ** 0.25
LOG2E = 1.4426950408889634

ROW_GROUP = 256
LANE = 128
FFT_S = 128
VMEM_LIMIT = 52 * 1024 * 1024


def _cparams(*sem):
    return pltpu.CompilerParams(dimension_semantics=sem, vmem_limit_bytes=VMEM_LIMIT)


def _pick(total, prefs):
    for p in prefs:
        if total % p == 0:
            return p
    raise ValueError(f"no tile for {total} in {prefs}")


def _ln(x):
    mu = jnp.mean(x, axis=-1, keepdims=True)
    xc = x - mu
    var = jnp.mean(xc * xc, axis=-1, keepdims=True)
    return xc * lax.rsqrt(var + NORM_EPS)


def _rms(x):
    return x * lax.rsqrt(jnp.mean(x * x, axis=-1, keepdims=True) + NORM_EPS)


def _rope128(u, cos, sa, sb):
    return u * cos + pltpu.roll(u, 16, 1) * sa + pltpu.roll(u, LANE - 16, 1) * sb


def _ada_kernel(c_ref, w_ref, b_ref, o_ref):
    a = c_ref[...]
    a = a * jax.nn.sigmoid(a)
    o_ref[...] = jnp.dot(a.astype(BF16), w_ref[...].astype(BF16), preferred_element_type=F32) + b_ref[...]


def _ada(cc, ada_w, ada_b):
    L, D, N = ada_w.shape
    tn = 1024
    return pl.pallas_call(
        _ada_kernel,
        out_shape=jax.ShapeDtypeStruct((L, 8, N), F32),
        grid=(L, N // tn),
        in_specs=[pl.BlockSpec((8, D), lambda l, j: (0, 0)),
                  pl.BlockSpec((None, D, tn), lambda l, j: (l, 0, j)),
                  pl.BlockSpec((None, 1, tn), lambda l, j: (l, 0, j))],
        out_specs=pl.BlockSpec((None, 8, tn), lambda l, j: (l, 0, j)),
        compiler_params=_cparams("arbitrary", "arbitrary"),
    )(cc, ada_w, ada_b)


def _mod_row(g, gpb, lat_groups, n_batch):
    return jnp.where(g % gpb < lat_groups, g // gpb, n_batch)


def _mm_kernel(a_ref, w_ref, o_ref, *, act):
    acc = jnp.dot(a_ref[...], w_ref[...], preferred_element_type=F32)
    if act == "sigmoid":
        acc = jax.nn.sigmoid(acc)
    o_ref[...] = acc.astype(o_ref.dtype)


def _by_batch(a, nb):
    return a.reshape(nb, a.shape[0] // nb, a.shape[1])


def _matmul(a, w, out_dtype, tn, act=None, nb=1, tb=None):
    a = _by_batch(a, nb)
    K = a.shape[2]
    tb = tb or a.shape[1]
    N = w.shape[1]
    tm = _pick(tb, (1024, 512, 256))
    return pl.pallas_call(
        functools.partial(_mm_kernel, act=act),
        out_shape=jax.ShapeDtypeStruct((nb, tb, N), out_dtype),
        grid=(nb, tb // tm, N // tn),
        in_specs=[pl.BlockSpec((None, tm, K), lambda b, i, j: (b, i, 0)),
                  pl.BlockSpec((K, tn), lambda b, i, j: (0, j))],
        out_specs=pl.BlockSpec((None, tm, tn), lambda b, i, j: (b, i, j)),
        compiler_params=_cparams("arbitrary", "arbitrary", "arbitrary"),
    )(a, w).reshape(nb * tb, N)


def _qkv_kernel(x_ref, mod_ref, w_ref, cos_ref, sa_ref, sb_ref, o_ref, h_ref, *, qscale, geom, groups):
    j = pl.program_id(1)

    @pl.when(j == 0)
    def _():
        for q in range(groups):
            mod = mod_ref[_mod_row(pl.program_id(0) * groups + q, *geom)]
            sl = slice(q * ROW_GROUP, (q + 1) * ROW_GROUP)
            h_ref[sl, :] = (_ln(x_ref[sl, :]) * (1.0 + mod[1:2, :]) + mod[0:1, :]).astype(BF16)

    acc = jnp.dot(h_ref[...], w_ref[...], preferred_element_type=F32)

    @pl.when(j < 2)
    def _():
        cos, sa, sb = cos_ref[...], sa_ref[...], sb_ref[...]
        scale = jnp.where(j == 0, qscale, 1.0).astype(F32)
        for c in range(DA_HEADS):
            u = acc[:, c * LANE:(c + 1) * LANE]
            o_ref[:, c * LANE:(c + 1) * LANE] = (_rope128(u, cos, sa, sb) * scale).astype(BF16)

    @pl.when(j == 2)
    def _():
        o_ref[...] = acc.astype(BF16)


def _qkv_proj(xs, mod, w_qkv, tabs, geom):
    M, K = xs.shape
    tm = _pick(M, (1024, 512, 256))
    tn = DA_W
    row = lambda i, j: (i, 0)
    tab_spec = pl.BlockSpec((tm, LANE), row)
    return pl.pallas_call(
        functools.partial(_qkv_kernel, qscale=DA_DQK ** -0.5 * LOG2E, geom=geom, groups=tm // ROW_GROUP),
        out_shape=(jax.ShapeDtypeStruct((M, 3 * DA_W), BF16), jax.ShapeDtypeStruct((M, K), BF16)),
        grid=(M // tm, 3),
        in_specs=[pl.BlockSpec((tm, K), row),
                  pl.BlockSpec(mod.shape, lambda i, j: (0, 0, 0)),
                  pl.BlockSpec((K, tn), lambda i, j: (0, j)),
                  tab_spec, tab_spec, tab_spec],
        out_specs=(pl.BlockSpec((tm, tn), lambda i, j: (i, j)), pl.BlockSpec((tm, K), row)),
        compiler_params=_cparams("arbitrary", "arbitrary"),
    )(xs, mod, w_qkv, *tabs)


def _mla_prep_kernel(h_ref, wm_ref, qg_ref, kvg_ref, wuq_ref, wukv_ref, cos_ref, sa_ref, sb_ref,
                     q_ref, k_ref, v_ref, *, qscale):
    p = jnp.dot(h_ref[...], wm_ref[...], preferred_element_type=F32)
    cos, sa, sb = cos_ref[...], sa_ref[...], sb_ref[...]
    cq = p[:, :MLA_Q_RANK]
    ckv = p[:, MLA_Q_RANK:MLA_Q_RANK + MLA_KV_RANK]
    kr = p[:, MLA_Q_RANK + MLA_KV_RANK:]
    qn = (_rms(cq) * qg_ref[...]).astype(BF16)
    q = jnp.dot(qn, wuq_ref[...], preferred_element_type=F32)
    kvn = (_rms(ckv) * kvg_ref[...]).astype(BF16)
    kv = jnp.dot(kvn, wukv_ref[...], preferred_element_type=F32)
    krr = _rope128(kr, cos, sa, sb).astype(BF16)
    for h in range(MLA_HEADS):
        o = h * MLA_QK_PAD
        q_ref[:, o:o + LANE] = (q[:, o:o + LANE] * qscale).astype(BF16)
        q_ref[:, o + LANE:o + 2 * LANE] = (_rope128(q[:, o + LANE:o + 2 * LANE], cos, sa, sb) * qscale).astype(BF16)
        k_ref[:, o:o + LANE] = kv[:, h * LANE:(h + 1) * LANE].astype(BF16)
        k_ref[:, o + LANE:o + 2 * LANE] = krr
    v_ref[...] = kv[:, MLA_W:].astype(BF16)


def _mla_prep(h, w_mla, q_g, kv_g, w_uq, w_ukv, tabs):
    M, D = h.shape
    tm = _pick(M, (512, 256))
    row = lambda i: (i, 0)
    full = lambda i: (0, 0)
    qk_w = MLA_HEADS * MLA_QK_PAD
    return pl.pallas_call(
        functools.partial(_mla_prep_kernel, qscale=(MLA_NOPE + MLA_ROPE) ** -0.5 * LOG2E),
        out_shape=(jax.ShapeDtypeStruct((M, qk_w), BF16),
                   jax.ShapeDtypeStruct((M, qk_w), BF16),
                   jax.ShapeDtypeStruct((M, MLA_W), BF16)),
        grid=(M // tm,),
        in_specs=[pl.BlockSpec((tm, D), row),
                  pl.BlockSpec(w_mla.shape, full),
                  pl.BlockSpec((1, MLA_Q_RANK), full),
                  pl.BlockSpec((1, MLA_KV_RANK), full),
                  pl.BlockSpec(w_uq.shape, full),
                  pl.BlockSpec(w_ukv.shape, full),
                  pl.BlockSpec((tm, LANE), row), pl.BlockSpec((tm, LANE), row), pl.BlockSpec((tm, LANE), row)],
        out_specs=(pl.BlockSpec((tm, qk_w), row), pl.BlockSpec((tm, qk_w), row), pl.BlockSpec((tm, MLA_W), row)),
        compiler_params=_cparams("arbitrary"),
    )(h, w_mla, q_g, kv_g, w_uq, w_ukv, *tabs)


_NT = (((1,), (1,)), ((), ()))
ATTN_TQ = 256
ATTN_CHAINS = 8
ATTN_TK = 256


def _skewed_pipeline(n_chains, n_chunks, stages):
    for t in range(n_chains + len(stages) - 1):
        active = [(s, t - s) for s in range(len(stages)) if 0 <= t - s < n_chains]
        for c in range(n_chunks):
            for s, chain in active:
                stages[s][0](chain, c)
        for s, chain in active:
            stages[s][1](chain)


def _acc(old, new, op):
    return new if old is None else op(old, new)


def _lane_halves(x, op):
    return op(x[:, :LANE], x[:, LANE:])


def _attn_steps(run, q_rows, n_keys, lat_steps, n_lat, ctx_rows):
    if ctx_rows == 0:
        run(q_rows, 0, n_keys)
        return
    step = pl.program_id(2)

    @pl.when(step < lat_steps)
    def _():
        run(q_rows, 0, n_keys)

    @pl.when(step == lat_steps)
    def _():
        run(ctx_rows, n_lat, n_keys - n_lat)


def _diff_attn_kernel(lam_ref, q_ref, k_ref, v_ref, g_ref, o_ref, *, out_scale, lat_steps, n_lat, ctx_rows):
    run = functools.partial(_diff_attn_run, lam_ref, q_ref, k_ref, v_ref, g_ref, o_ref, out_scale)
    _attn_steps(run, q_ref.shape[0], k_ref.shape[0], lat_steps, n_lat, ctx_rows)


def _diff_attn_run(lam_ref, q_ref, k_ref, v_ref, g_ref, o_ref, out_scale, q_rows, key0, n_keys):
    lam, g = lam_ref[0], g_ref[...]
    tq = min(ATTN_TQ, q_rows)
    n_chains = q_rows // tq
    n_chunks = n_keys // ATTN_TK
    lane = lax.broadcasted_iota(jnp.int32, (1, LANE), 1)
    lo = (lane < DA_DQK).astype(F32)
    st = [dict(s=[], e=[], mx=[None, None], l=[None, None], o=None) for _ in range(n_chains)]
    rows = lambda i: slice(i * tq, (i + 1) * tq)
    keys = lambda c: slice(key0 + c * ATTN_TK, key0 + (c + 1) * ATTN_TK)

    def qk_chunk(i, c):
        d = st[i]
        if c == 0:
            qf = q_ref[rows(i), :].astype(F32)
            d["q"] = jnp.concatenate([(qf * lo).astype(BF16), (qf * (1.0 - lo)).astype(BF16)], axis=0)
        both = lax.dot_general(d["q"], k_ref[keys(c), :], _NT, preferred_element_type=F32)
        pair = []
        for m in range(2):
            s = both[m * tq:(m + 1) * tq]
            d["mx"][m] = _acc(d["mx"][m], _lane_halves(s, jnp.maximum), jnp.maximum)
            pair.append(s)
        d["s"].append(pair)

    def qk_done(i):
        st[i]["m"] = [jnp.max(mx, axis=-1, keepdims=True) for mx in st[i]["mx"]]

    def exp_chunk(i, c):
        d = st[i]
        pair = []
        for m in range(2):
            e = jnp.exp2(d["s"][c][m] - d["m"][m])
            d["l"][m] = _acc(d["l"][m], _lane_halves(e, jnp.add), jnp.add)
            pair.append(e)
        d["s"][c] = None
        d["e"].append(pair)

    def exp_done(i):
        d = st[i]
        l1, l2 = [jnp.sum(l, axis=-1, keepdims=True) for l in d["l"]]
        d["r1"] = 1.0 / l1
        d["cf"] = lam * l1 / l2

    def pv_chunk(i, c):
        d = st[i]
        w = (d["e"][c][0] - d["cf"] * d["e"][c][1]).astype(BF16)
        d["e"][c] = None
        d["o"] = _acc(d["o"], jnp.dot(w, v_ref[keys(c), :], preferred_element_type=F32), jnp.add)

    def pv_done(i):
        o = st[i]["o"] * st[i]["r1"]
        o_ref[rows(i), :] = (_rms(o) * g * out_scale).astype(BF16)

    _skewed_pipeline(n_chains, n_chunks, [(qk_chunk, qk_done), (exp_chunk, exp_done), (pv_chunk, pv_done)])


def _mla_attn_kernel(q_ref, k_ref, v_ref, o_ref, *, lat_steps, n_lat, ctx_rows):
    run = functools.partial(_mla_attn_run, q_ref, k_ref, v_ref, o_ref)
    _attn_steps(run, q_ref.shape[0], k_ref.shape[0], lat_steps, n_lat, ctx_rows)


def _mla_attn_run(q_ref, k_ref, v_ref, o_ref, q_rows, key0, n_keys):
    tq = min(ATTN_TQ, q_rows)
    n_chains = q_rows // tq
    n_chunks = n_keys // ATTN_TK
    st = [dict(s=[], mx=None, l=None, o=None) for _ in range(n_chains)]
    rows = lambda i: slice(i * tq, (i + 1) * tq)
    keys = lambda c: slice(key0 + c * ATTN_TK, key0 + (c + 1) * ATTN_TK)

    def qk_chunk(i, c):
        d = st[i]
        s = lax.dot_general(q_ref[rows(i), :], k_ref[keys(c), :], _NT, preferred_element_type=F32)
        d["mx"] = _acc(d["mx"], _lane_halves(s, jnp.maximum), jnp.maximum)
        d["s"].append(s)

    def qk_done(i):
        st[i]["m"] = jnp.max(st[i]["mx"], axis=-1, keepdims=True)

    def pv_chunk(i, c):
        d = st[i]
        e = jnp.exp2(d["s"][c] - d["m"])
        d["s"][c] = None
        d["l"] = _acc(d["l"], _lane_halves(e, jnp.add), jnp.add)
        d["o"] = _acc(d["o"], jnp.dot(e.astype(BF16), v_ref[keys(c), :], preferred_element_type=F32), jnp.add)

    def pv_done(i):
        d = st[i]
        o_ref[rows(i), :] = (d["o"] * (1.0 / jnp.sum(d["l"], axis=-1, keepdims=True))).astype(BF16)

    _skewed_pipeline(n_chains, n_chunks, [(qk_chunk, qk_done), (pv_chunk, pv_done)])


def _diff_attn(qkv, lam, subln_g, n, nc, ctx_queries, lam_init):
    B, T, _ = qkv.shape
    H = DA_HEADS
    tq = _pick(n, (ATTN_TQ * ATTN_CHAINS, ATTN_TQ))
    lat_steps = n // tq
    kern = functools.partial(_diff_attn_kernel, out_scale=1.0 - lam_init, lat_steps=lat_steps, n_lat=n,
                             ctx_rows=nc if ctx_queries else 0)
    return pl.pallas_call(
        kern,
        out_shape=jax.ShapeDtypeStruct((B, T if ctx_queries else n, DA_W), BF16),
        grid=(B, H, lat_steps + int(ctx_queries)),
        in_specs=[pl.BlockSpec(memory_space=pltpu.SMEM),
                  pl.BlockSpec((None, tq, LANE), lambda b, h, i: (b, i, h)),
                  pl.BlockSpec((None, T, LANE), lambda b, h, i: (b, 0, H + h)),
                  pl.BlockSpec((None, T, LANE), lambda b, h, i: (b, 0, 2 * H + h)),
                  pl.BlockSpec((1, DA_DV), lambda b, h, i: (0, 0))],
        out_specs=pl.BlockSpec((None, tq, LANE), lambda b, h, i: (b, i, h)),
        compiler_params=_cparams("arbitrary", "arbitrary", "arbitrary"),
    )(lam, qkv, qkv, qkv, subln_g)


def _mla_attn(q, k, v, n, nc, ctx_queries):
    B, T, _ = q.shape
    H = MLA_HEADS
    tq = _pick(n, (ATTN_TQ * ATTN_CHAINS, ATTN_TQ))
    lat_steps = n // tq
    kern = functools.partial(_mla_attn_kernel, lat_steps=lat_steps, n_lat=n, ctx_rows=nc if ctx_queries else 0)
    return pl.pallas_call(
        kern,
        out_shape=jax.ShapeDtypeStruct((B, T if ctx_queries else n, MLA_W), BF16),
        grid=(B, H, lat_steps + int(ctx_queries)),
        in_specs=[pl.BlockSpec((None, tq, MLA_QK_PAD), lambda b, h, i: (b, i, h)),
                  pl.BlockSpec((None, T, MLA_QK_PAD), lambda b, h, i: (b, 0, h)),
                  pl.BlockSpec((None, T, MLA_DV), lambda b, h, i: (b, 0, h))],
        out_specs=pl.BlockSpec((None, tq, MLA_DV), lambda b, h, i: (b, i, h)),
        compiler_params=_cparams("arbitrary", "arbitrary", "arbitrary"),
    )(q, k, v)


def _hy_filter_kernel(feat_ref, dec_ref, w1_ref, b1_ref, w2_ref, b2_ref, w3_ref, b3_ref, w4_ref, fr_ref,
                      k_ref, s_ref):
    hp = lax.Precision.HIGHEST
    d, r = pl.program_id(0), pl.program_id(1)
    fr = fr_ref[...]
    h = jnp.sin(fr * (jnp.dot(feat_ref[...], w1_ref[...], precision=hp, preferred_element_type=F32) + b1_ref[...]))
    h = jnp.sin(fr * (jnp.dot(h, w2_ref[...], precision=hp, preferred_element_type=F32) + b2_ref[...]))
    h = jnp.sin(fr * (jnp.dot(h, w3_ref[...], precision=hp, preferred_element_type=F32) + b3_ref[...]))
    h = jnp.dot(h, w4_ref[...], precision=hp, preferred_element_type=F32)
    row = lax.broadcasted_iota(jnp.int32, (h.shape[0], 1), 0)
    first_bwd = jnp.where((d == 1) & (r == 0), 1.0, 0.0)
    scale = dec_ref[...] * (1.0 - jnp.where(row == 0, 1.0, 0.0) * first_bwd)

    @pl.when((d == 0) & (r == 0))
    def _():
        s_ref[...] = jnp.zeros(s_ref.shape, F32)

    for o in range(HY_ORDER):
        ko = h[:, o * HY_CH:(o + 1) * HY_CH] * scale
        k_ref[o] = ko
        s_ref[o] += jnp.broadcast_to(jnp.sum(jnp.abs(ko), axis=0, keepdims=True), (8, HY_CH))


def _hy_filter(n, w1, b1, w2, b2, w3, b3, w4, fr):
    C = HY_CH
    t = jnp.linspace(0.0, 1.0, n, dtype=F32)
    pos = jnp.arange(n, dtype=F32)
    t2 = jnp.concatenate([t, t[::-1]])[:, None]
    pos2 = jnp.concatenate([pos, pos[::-1]])[:, None]
    phase = (2.0 * math.pi / n) * pos2 * jnp.linspace(1e-4, HY_BANDS - 1, HY_BANDS, dtype=F32)[None, :]
    feat = jnp.concatenate([t2, jnp.cos(phase), -jnp.sin(phase)], axis=-1)
    feat = jnp.pad(feat, ((0, 0), (0, HY_FFN - HY_EMB)))
    dec = jnp.exp(-t2 * jnp.abs(jnp.linspace(HY_MIN_DECAY, HY_MAX_DECAY, C, dtype=F32)))
    w4d = w4.reshape(HY_FFN, HY_ORDER, 2, C).transpose(2, 0, 1, 3).reshape(2, HY_FFN, HY_ORDER * C)
    rb = min(1024, n)
    nb = n // rb
    full = lambda d, r: (0, 0)
    return pl.pallas_call(
        _hy_filter_kernel,
        out_shape=(jax.ShapeDtypeStruct((HY_ORDER, 2 * n, C), F32),
                   jax.ShapeDtypeStruct((HY_ORDER, 8, C), F32)),
        grid=(2, nb),
        in_specs=[pl.BlockSpec((rb, HY_FFN), lambda d, r: (d * nb + r, 0)),
                  pl.BlockSpec((rb, C), lambda d, r: (d * nb + r, 0)),
                  pl.BlockSpec(w1.shape, full), pl.BlockSpec(b1.shape, full),
                  pl.BlockSpec(w2.shape, full), pl.BlockSpec(b2.shape, full),
                  pl.BlockSpec(w3.shape, full), pl.BlockSpec(b3.shape, full),
                  pl.BlockSpec((None, HY_FFN, HY_ORDER * C), lambda d, r: (d, 0, 0)),
                  pl.BlockSpec(fr.shape, full)],
        out_specs=(pl.BlockSpec((HY_ORDER, rb, C), lambda d, r: (0, d * nb + r, 0)),
                   pl.BlockSpec((HY_ORDER, 8, C), lambda d, r: (0, 0, 0))),
        compiler_params=_cparams("arbitrary", "arbitrary"),
    )(feat, dec, w1, b1, w2, b2, w3, b3, w4d, fr)


def _hy_dwconv_kernel(*refs):
    for k in range(3):
        u_ref, w_ref, b_ref, o_ref = refs[3 * k], refs[3 * k + 1], refs[3 * k + 2], refs[9 + k]
        u = u_ref[...]
        L = u.shape[0]
        row = lax.broadcasted_iota(jnp.int32, (L, 1), 0)
        up = jnp.where(row == 0, 0.0, pltpu.roll(u, 1, 0))
        dn = jnp.where(row == L - 1, 0.0, pltpu.roll(u, L - 1, 0))
        o_ref[...] = up * w_ref[0:1, :] + u * w_ref[1:2, :] + dn * w_ref[2:3, :] + b_ref[...]


def _hy_dwconv(u, w, b, row_block, length):
    B = u.shape[0]
    cw = LANE
    per = HY_CH // cw
    in_specs, args = [], []
    for k in range(3):
        in_specs += [pl.BlockSpec((None, length, cw), lambda bb, j, k=k: (bb, row_block, k * per + j)),
                     pl.BlockSpec((3, cw), lambda bb, j, k=k: (0, k * per + j)),
                     pl.BlockSpec((1, cw), lambda bb, j, k=k: (0, k * per + j))]
        args += [u, w, b]
    ospec = pl.BlockSpec((None, length, cw), lambda bb, j: (bb, 0, j))
    return pl.pallas_call(
        _hy_dwconv_kernel,
        out_shape=(jax.ShapeDtypeStruct((B, length, HY_CH), F32),) * 3,
        grid=(B, per),
        in_specs=in_specs,
        out_specs=(ospec,) * 3,
        compiler_params=_cparams("arbitrary", "arbitrary"),
    )(*args)


def _dft_tables(n):
    S = FFT_S
    M = 2 * n
    N1 = M // S
    H = N1 // 2
    s2 = np.arange(S)[:, None, None]
    k1 = np.arange(N1)[None, :, None]
    s1 = np.arange(N1)[None, None, :]
    ang = -2.0 * np.pi * ((k1 * (S * s1 + s2)) % M) / M
    fr, fi = np.cos(ang), np.sin(ang)
    g1f = np.concatenate([fr, fi], axis=1)
    frh, fih = fr[:, :, :H], fi[:, :, :H]
    g1d = np.concatenate([np.concatenate([frh, -fih], axis=2),
                          np.concatenate([fih, frh], axis=2)], axis=1)
    er = np.transpose(frh, (0, 2, 1)) / M
    ei = -np.transpose(fih, (0, 2, 1)) / M
    g3 = np.concatenate([np.concatenate([er, -ei], axis=2),
                         np.concatenate([ei, er], axis=2)], axis=1)
    a2 = -2.0 * np.pi * ((np.arange(S)[:, None] * np.arange(S)[None, :]) % S) / S
    f2r, f2i = np.cos(a2), np.sin(a2)
    g2 = np.block([[f2r, -f2i], [f2i, f2r]])
    g2i = np.block([[f2r, f2i], [-f2i, f2r]])
    cast = lambda a: jnp.asarray(a, dtype=F32).astype(BF16)
    return cast(g1d), cast(g1f), cast(g2), cast(g2i), cast(g3)


FFT_NS = 32
FFT_NK = 16


def _fft_s1_kernel(g_ref, x_ref, o_ref):
    xt = jnp.swapaxes(x_ref[...], 0, 1)
    y = jnp.stack([jnp.dot(g_ref[j], xt[j].astype(BF16), preferred_element_type=F32)
                   for j in range(xt.shape[0])], axis=0)
    o_ref[...] = jnp.swapaxes(y, 0, 1).astype(o_ref.dtype)


def _fft_s1(g, x):
    P, Ri, S, C = x.shape
    Ro = g.shape[1]
    ns = FFT_NS
    return pl.pallas_call(
        _fft_s1_kernel,
        out_shape=jax.ShapeDtypeStruct((P, Ro, S, C), BF16),
        grid=(S // ns, P),
        in_specs=[pl.BlockSpec((ns, Ro, Ri), lambda j, p: (j, 0, 0)),
                  pl.BlockSpec((None, Ri, ns, C), lambda j, p: (p, 0, j, 0))],
        out_specs=pl.BlockSpec((None, Ro, ns, C), lambda j, p: (p, 0, j, 0)),
        compiler_params=_cparams("arbitrary", "arbitrary"),
    )(g, x)


def _fft_s2_filt_kernel(a_ref, g_ref, rn_ref, o_ref, *, nk):
    S = FFT_S
    rn = rn_ref[...]
    for t in range(nk):
        d = jnp.concatenate([a_ref[0, t], a_ref[1, t]], axis=0)
        y = jnp.dot(g_ref[...], d, preferred_element_type=F32)
        o_ref[0, t] = y[:S] * rn
        o_ref[1, t] = y[S:] * rn


def _fft_s2_filt(a, g2, rnorm):
    O, _, N1, S, C = a.shape
    nk = min(FFT_NK, N1)
    blk = (None, 2, nk, S, C)
    return pl.pallas_call(
        functools.partial(_fft_s2_filt_kernel, nk=nk),
        out_shape=jax.ShapeDtypeStruct(a.shape, F32),
        grid=(O, N1 // nk),
        in_specs=[pl.BlockSpec(blk, lambda o, j: (o, 0, j, 0, 0)),
                  pl.BlockSpec(g2.shape, lambda o, j: (0, 0)),
                  pl.BlockSpec((None, 1, C), lambda o, j: (o, 0, 0))],
        out_specs=pl.BlockSpec(blk, lambda o, j: (o, 0, j, 0, 0)),
        compiler_params=_cparams("arbitrary", "arbitrary"),
    )(a, g2, rnorm)


def _fft_s2_kernel(a_ref, g_ref, gi_ref, kf_ref, o_ref, *, nk):
    S = FFT_S
    for t in range(nk):
        d = jnp.concatenate([a_ref[0, t], a_ref[1, t]], axis=0)
        y = jnp.dot(g_ref[...], d, preferred_element_type=F32)
        yr, yi = y[:S], y[S:]
        kr, ki = kf_ref[0, t], kf_ref[1, t]
        p = jnp.concatenate([yr * kr - yi * ki, yr * ki + yi * kr], axis=0).astype(BF16)
        b = jnp.dot(gi_ref[...], p, preferred_element_type=F32)
        o_ref[0, t] = b[:S].astype(BF16)
        o_ref[1, t] = b[S:].astype(BF16)


def _fft_s2(a, g2, g2i, kf, order):
    P, _, N1, S, C = a.shape
    nk = min(FFT_NK, N1)
    blk = (None, 2, nk, S, C)
    return pl.pallas_call(
        functools.partial(_fft_s2_kernel, nk=nk),
        out_shape=jax.ShapeDtypeStruct(a.shape, BF16),
        grid=(N1 // nk, P),
        in_specs=[pl.BlockSpec(blk, lambda j, p: (p, 0, j, 0, 0)),
                  pl.BlockSpec(g2.shape, lambda j, p: (0, 0)),
                  pl.BlockSpec(g2i.shape, lambda j, p: (0, 0)),
                  pl.BlockSpec(blk, lambda j, p: (order, 0, j, 0, 0))],
        out_specs=pl.BlockSpec(blk, lambda j, p: (p, 0, j, 0, 0)),
        compiler_params=_cparams("arbitrary", "arbitrary"),
    )(a, g2, g2i, kf)


def _fft_s3_kernel(g_ref, b_ref, z_ref, gate_ref, skip_ref, o_ref):
    bt = jnp.swapaxes(b_ref[...].astype(F32), 0, 1).astype(BF16)
    y = jnp.stack([jnp.dot(g_ref[j], bt[j], preferred_element_type=F32) for j in range(bt.shape[0])], axis=0)
    y = jnp.swapaxes(y, 0, 1)
    o_ref[...] = (gate_ref[...] * (y + skip_ref[...] * z_ref[...])).astype(o_ref.dtype)


def _fft_s3(g3, b, z, gate, skip, out_dtype):
    P, Ri, S, C = b.shape
    Ro = g3.shape[1]
    ns = FFT_NS
    dspec = pl.BlockSpec((None, Ro, ns, C), lambda j, p: (p, 0, j, 0))
    return pl.pallas_call(
        _fft_s3_kernel,
        out_shape=jax.ShapeDtypeStruct((P, Ro, S, C), out_dtype),
        grid=(S // ns, P),
        in_specs=[pl.BlockSpec((ns, Ro, Ri), lambda j, p: (j, 0, 0)),
                  pl.BlockSpec((None, Ri, ns, C), lambda j, p: (p, 0, j, 0)),
                  dspec, dspec,
                  pl.BlockSpec((1, C), lambda j, p: (0, 0))],
        out_specs=dspec,
        compiler_params=_cparams("arbitrary", "arbitrary"),
    )(g3, b, z, gate, skip)


def _dense_tables(m):
    M = 2 * m
    ang = -2.0 * np.pi * ((np.arange(M)[:, None] * np.arange(M)[None, :]) % M) / M
    fr, fi = np.cos(ang), np.sin(ang)
    gk = np.concatenate([fr, fi], axis=0)
    gd = np.block([[fr[:, :m], -fi[:, :m]], [fi[:, :m], fr[:, :m]]])
    er, ei = fr[:m, :] / M, -fi[:m, :] / M
    gi = np.block([[er, -ei], [ei, er]])
    cast = lambda a: jnp.asarray(a, dtype=F32).astype(BF16)
    return cast(gk), cast(gd), cast(gi)


def _dense_spec_kernel(k_ref, g_ref, rn_ref, o_ref):
    o_ref[...] = jnp.dot(g_ref[...], k_ref[...].astype(BF16), preferred_element_type=F32) * rn_ref[...]


def _dense_spec(kc, gk, rnorm):
    O, M, C = kc.shape
    return pl.pallas_call(
        _dense_spec_kernel,
        out_shape=jax.ShapeDtypeStruct((O, 2 * M, C), F32),
        grid=(O,),
        in_specs=[pl.BlockSpec((None, M, C), lambda o: (o, 0, 0)),
                  pl.BlockSpec(gk.shape, lambda o: (0, 0)),
                  pl.BlockSpec((None, 1, C), lambda o: (o, 0, 0))],
        out_specs=pl.BlockSpec((None, 2 * M, C), lambda o: (o, 0, 0)),
        compiler_params=_cparams("arbitrary"),
    )(kc, gk, rnorm)


def _dense_conv_kernel(x_ref, gd_ref, gi_ref, kf_ref, gate_ref, skip_ref, o_ref):
    x = x_ref[...]
    y = jnp.dot(gd_ref[...], x.astype(BF16), preferred_element_type=F32)
    M = y.shape[0] // 2
    yr, yi = y[:M], y[M:]
    kr, ki = kf_ref[:M], kf_ref[M:]
    p = jnp.concatenate([yr * kr - yi * ki, yr * ki + yi * kr], axis=0).astype(BF16)
    conv = jnp.dot(gi_ref[...], p, preferred_element_type=F32)
    o_ref[...] = (gate_ref[...] * (conv + skip_ref[...] * x)).astype(o_ref.dtype)


def _dense_conv(x, gd, gi, kf, order, gate, skip, out_dtype):
    P, R, C = x.shape
    dspec = pl.BlockSpec((None, R, C), lambda p: (p, 0, 0))
    return pl.pallas_call(
        _dense_conv_kernel,
        out_shape=jax.ShapeDtypeStruct((P, R, C), out_dtype),
        grid=(P,),
        in_specs=[dspec,
                  pl.BlockSpec(gd.shape, lambda p: (0, 0)),
                  pl.BlockSpec(gi.shape, lambda p: (0, 0)),
                  pl.BlockSpec((None,) + kf.shape[1:], lambda p: (order, 0, 0)),
                  dspec,
                  pl.BlockSpec((1, C), lambda p: (0, 0))],
        out_specs=dspec,
        compiler_params=_cparams("arbitrary"),
    )(x, gd, gi, kf, gate, skip)


def _merge_kernel(oa_ref, ob_ref, oc_ref, wa_ref, wb_ref, wc_ref, g0_ref, g1_ref, g2_ref, o_ref):
    ya = jnp.dot(oa_ref[...], wa_ref[...], preferred_element_type=F32)
    yb = jnp.dot(ob_ref[...], wb_ref[...], preferred_element_type=F32)
    yc = jnp.dot(oc_ref[...], wc_ref[...], preferred_element_type=F32)
    m = g0_ref[...].astype(F32) * ya + g1_ref[...].astype(F32) * yb + g2_ref[...].astype(F32) * yc
    o_ref[...] = m.astype(BF16)


def _merge(oa, ob, oc, w_ba, w_bb, w_bc, gates, nb, tb):
    D = D_MODEL
    tm = _pick(tb, (1024, 512, 256))
    tn = 1024
    nt = D // tn
    oa, ob, oc, gates = (_by_batch(a, nb) for a in (oa, ob, oc, gates))
    row = lambda b, i, j: (b, i, 0)
    col = lambda b, i, j: (0, j)
    return pl.pallas_call(
        _merge_kernel,
        out_shape=jax.ShapeDtypeStruct((nb, tb, D), BF16),
        grid=(nb, tb // tm, nt),
        in_specs=[pl.BlockSpec((None, tm, DA_W), row), pl.BlockSpec((None, tm, MLA_W), row),
                  pl.BlockSpec((None, tm, HY_CH), row),
                  pl.BlockSpec((DA_W, tn), col), pl.BlockSpec((MLA_W, tn), col), pl.BlockSpec((HY_CH, tn), col),
                  pl.BlockSpec((None, tm, tn), lambda b, i, j: (b, i, j)),
                  pl.BlockSpec((None, tm, tn), lambda b, i, j: (b, i, nt + j)),
                  pl.BlockSpec((None, tm, tn), lambda b, i, j: (b, i, 2 * nt + j))],
        out_specs=pl.BlockSpec((None, tm, tn), lambda b, i, j: (b, i, j)),
        compiler_params=_cparams("arbitrary", "arbitrary", "arbitrary"),
    )(oa, ob, oc, w_ba, w_bb, w_bc, gates, gates, gates).reshape(nb * tb, D)


def _wo_ln_kernel(m_ref, w_ref, xs_ref, mod_ref, g_ref, b_ref, xs1_ref, h2_ref, *, sel, groups):
    y = jnp.dot(m_ref[...], w_ref[...], preferred_element_type=F32)
    g, b = g_ref[...], b_ref[...]
    for q in range(groups):
        mod = mod_ref[sel(pl.program_id(0), pl.program_id(1) * groups + q)]
        sl = slice(q * ROW_GROUP, (q + 1) * ROW_GROUP)
        x1 = _ln(DEEPNORM_ALPHA * xs_ref[sl, :] + mod[2:3, :] * y[sl, :]) * g + b
        xs1_ref[sl, :] = x1
        h2_ref[sl, :] = (_ln(x1) * (1.0 + mod[4:5, :]) + mod[3:4, :]).astype(BF16)


def _wo_ln(merged, w_o, xs, mod, ln_g, ln_b, sel, nb, tb):
    D = D_MODEL
    tm = _pick(tb, (512, 256))
    merged, xs = _by_batch(merged, nb), _by_batch(xs, nb)
    row = lambda b, i: (b, i, 0)
    full2 = lambda b, i: (0, 0)
    xs1, h2 = pl.pallas_call(
        functools.partial(_wo_ln_kernel, sel=sel, groups=tm // ROW_GROUP),
        out_shape=(jax.ShapeDtypeStruct((nb, tb, D), F32), jax.ShapeDtypeStruct((nb, tb, D), BF16)),
        grid=(nb, tb // tm),
        in_specs=[pl.BlockSpec((None, tm, D), row),
                  pl.BlockSpec((D, D), full2),
                  pl.BlockSpec((None, tm, D), row),
                  pl.BlockSpec(mod.shape, lambda b, i: (0, 0, 0)),
                  pl.BlockSpec((1, D), full2), pl.BlockSpec((1, D), full2)],
        out_specs=(pl.BlockSpec((None, tm, D), row), pl.BlockSpec((None, tm, D), row)),
        compiler_params=_cparams("arbitrary", "arbitrary"),
    )(merged, w_o, xs, mod, ln_g, ln_b)
    return xs1.reshape(nb * tb, D), h2.reshape(nb * tb, D)


FFN_HALO = 16


def _ffn_up_kernel(hp_ref, h_ref, hn_ref, wa_ref, wv_ref, cw_ref, cb_ref, kp_ref, kn_ref, o_ref, wa_s, wv_s):
    tm, tn = o_ref.shape

    @pl.when((pl.program_id(1) == 0) & (pl.program_id(2) == 0))
    def _():
        wa_s[...] = wa_ref[...].astype(BF16)
        wv_s[...] = wv_ref[...].astype(BF16)

    hm = h_ref[...]
    hext = jnp.concatenate([hp_ref[...], hm, hn_ref[...]], axis=0)
    a = jnp.dot(hext, wa_s[...], preferred_element_type=F32)
    v = jnp.dot(hm, wv_s[...], preferred_element_type=F32)
    ext = tm + 2 * FFN_HALO
    rep = tn // LANE
    keep_prev = jnp.tile(kp_ref[...], (1, rep))
    keep_next = jnp.tile(kn_ref[...], (1, rep))
    a_prev = pltpu.roll(a, 1, 0)[FFN_HALO:FFN_HALO + tm] * keep_prev
    a_next = pltpu.roll(a, ext - 1, 0)[FFN_HALO:FFN_HALO + tm] * keep_next
    cv = a_prev * cw_ref[0:1, :] + a[FFN_HALO:FFN_HALO + tm] * cw_ref[1:2, :] + a_next * cw_ref[2:3, :] + cb_ref[...]
    o_ref[...] = (cv * jax.nn.sigmoid(cv) * v).astype(BF16)


def _ffn_up(h2, w_up, layer, conv_w, conv_b, keep_prev, keep_next, nb, tb):
    D = D_MODEL
    tm = _pick(tb, (1024, 512, 256))
    tn = 512
    nt = D_FF // tn
    hb = tm // FFN_HALO
    last = tb // FFN_HALO - 1
    h2, keep_prev, keep_next = (_by_batch(a, nb) for a in (h2, keep_prev, keep_next))
    mask_spec = pl.BlockSpec((None, tm, LANE), lambda j, b, i: (b, i, 0))
    return pl.pallas_call(
        _ffn_up_kernel,
        out_shape=jax.ShapeDtypeStruct((nb, tb, D_FF), BF16),
        grid=(nt, nb, tb // tm),
        in_specs=[pl.BlockSpec((None, FFN_HALO, D), lambda j, b, i: (b, jnp.maximum(i * hb - 1, 0), 0)),
                  pl.BlockSpec((None, tm, D), lambda j, b, i: (b, i, 0)),
                  pl.BlockSpec((None, FFN_HALO, D), lambda j, b, i: (b, jnp.minimum((i + 1) * hb, last), 0)),
                  pl.BlockSpec((None, D, tn), lambda j, b, i: (layer, 0, j)),
                  pl.BlockSpec((None, D, tn), lambda j, b, i: (layer, 0, nt + j)),
                  pl.BlockSpec((3, tn), lambda j, b, i: (0, j)),
                  pl.BlockSpec((1, tn), lambda j, b, i: (0, j)),
                  mask_spec, mask_spec],
        out_specs=pl.BlockSpec((None, tm, tn), lambda j, b, i: (b, i, j)),
        scratch_shapes=[pltpu.VMEM((D, tn), BF16), pltpu.VMEM((D, tn), BF16)],
        compiler_params=_cparams("arbitrary", "arbitrary", "arbitrary"),
    )(h2, h2, h2, w_up, w_up, conv_w, conv_b, keep_prev, keep_next).reshape(nb * tb, D_FF)


def _ffn_down_kernel(u_ref, w_ref, xs_ref, mod_ref, g_ref, b_ref, o_ref, acc_ref, *, sel, groups, nk):
    k = pl.program_id(2)

    @pl.when(k == 0)
    def _():
        acc_ref[...] = jnp.zeros(acc_ref.shape, F32)

    acc_ref[...] += jnp.dot(u_ref[...], w_ref[...], preferred_element_type=F32)

    @pl.when(k == nk - 1)
    def _():
        g, b = g_ref[...], b_ref[...]
        for q in range(groups):
            mod = mod_ref[sel(pl.program_id(0), pl.program_id(1) * groups + q)]
            sl = slice(q * ROW_GROUP, (q + 1) * ROW_GROUP)
            o_ref[sl, :] = _ln(DEEPNORM_ALPHA * xs_ref[sl, :] + mod[5:6, :] * acc_ref[sl, :]) * g + b


def _ffn_down(u, w_down, xs1, mod, ln_g, ln_b, sel, nb, tb):
    D = D_MODEL
    tm = _pick(tb, (512, 256))
    tk = D_FF // 2
    nk = D_FF // tk
    u, xs1 = _by_batch(u, nb), _by_batch(xs1, nb)
    row = lambda b, i, k: (b, i, 0)
    full2 = lambda b, i, k: (0, 0)
    return pl.pallas_call(
        functools.partial(_ffn_down_kernel, sel=sel, groups=tm // ROW_GROUP, nk=nk),
        out_shape=jax.ShapeDtypeStruct((nb, tb, D), F32),
        grid=(nb, tb // tm, nk),
        in_specs=[pl.BlockSpec((None, tm, tk), lambda b, i, k: (b, i, k)),
                  pl.BlockSpec((tk, D), lambda b, i, k: (k, 0)),
                  pl.BlockSpec((None, tm, D), row),
                  pl.BlockSpec(mod.shape, lambda b, i, k: (0, 0, 0)),
                  pl.BlockSpec((1, D), full2), pl.BlockSpec((1, D), full2)],
        out_specs=pl.BlockSpec((None, tm, D), row),
        scratch_shapes=[pltpu.VMEM((tm, D), F32)],
        compiler_params=_cparams("arbitrary", "arbitrary", "arbitrary"),
    )(u, w_down, xs1, mod, ln_g, ln_b).reshape(nb * tb, D)


def _rope_tables(B, n, nc):
    half = DA_DQK // 2
    inv = ROPE_BASE ** (-jnp.arange(0, half, 2, dtype=F32) / half)
    t = jnp.arange(n, dtype=jnp.int32)
    ang_r = (t // GRID_W).astype(F32)[:, None] * inv[None, :]
    ang_c = (t % GRID_W).astype(F32)[:, None] * inv[None, :]
    ang = jnp.concatenate([ang_r, ang_r, ang_c, ang_c], axis=-1)
    cos, sin = jnp.cos(ang), jnp.sin(ang)
    upper = (jnp.arange(DA_DQK) % half) >= half // 2
    sa = jnp.where(upper, sin, 0.0)
    sb = jnp.where(upper, 0.0, -sin)

    def full(tab, fill):
        tab = jnp.concatenate([tab, jnp.full((nc, DA_DQK), fill, F32)], axis=0)
        tab = jnp.tile(tab, (B, LANE // DA_DQK))
        return tab

    return full(cos, 1.0), full(sa, 0.0), full(sb, 0.0)


def _conv_masks(B, n, nc):
    T = n + nc
    t = jnp.arange(T)
    keep_prev = ((t != 0) & (t != n)).astype(F32)
    keep_next = ((t != n - 1) & (t != T - 1)).astype(F32)
    widen = lambda m: jnp.tile(m[:, None], (B, LANE))
    return widen(keep_prev), widen(keep_next)


W_IN_WIDTHS = (DA_W, DA_W, DA_W, MLA_Q_RANK, MLA_KV_RANK, MLA_ROPE, 3 * HY_CH, N_BRANCH * D_MODEL)
W_IN_OFFS = tuple(sum(W_IN_WIDTHS[:j]) for j in range(len(W_IN_WIDTHS)))
W_MLA_COLS = MLA_Q_RANK + MLA_KV_RANK + LANE


W_HALF = LANE // 2


def _prep_w_in_kernel(a_ref, b_ref, o_ref, *, zero_b_at):
    b = b_ref[...]
    if zero_b_at is not None:
        b = jnp.where(pl.program_id(0) == zero_b_at, 0.0, b)
    o_ref[...] = jnp.transpose(jnp.concatenate([a_ref[...], b], axis=0)).astype(BF16)


def _prep_group(w_t, layer, n_blocks, src, zero_b_at=None):
    D = w_t.shape[2]
    return pl.pallas_call(
        functools.partial(_prep_w_in_kernel, zero_b_at=zero_b_at),
        out_shape=jax.ShapeDtypeStruct((D, n_blocks * LANE), BF16),
        grid=(n_blocks,),
        in_specs=[pl.BlockSpec((None, W_HALF, D), lambda c: (layer, src(c)[0], 0)),
                  pl.BlockSpec((None, W_HALF, D), lambda c: (layer, src(c)[1], 0))],
        out_specs=pl.BlockSpec((D, LANE), lambda c: (0, c)),
        compiler_params=_cparams("arbitrary"),
    )(w_t, w_t)


def _prep_w_in(w_in, layer):
    assert w_in.shape[2] == sum(W_IN_WIDTHS) and all(o % W_HALF == 0 for o in W_IN_OFFS)
    w_t = jnp.swapaxes(w_in, 1, 2)
    o_q, o_k, o_v, o_cq, _, o_kr, o_hy, o_g = (o // W_HALF for o in W_IN_OFFS)
    H = DA_HEADS

    def src_qkv(c):
        part, h = c // H, c % H
        a = jnp.where(part < 2, part * 2 * H + h, o_v + 2 * h)
        return a, jnp.where(part < 2, a + H, a + 1)

    pairs = lambda first: (lambda c: (first + 2 * c, first + 2 * c + 1))
    n_mla = W_MLA_COLS // LANE
    src_mla = lambda c: (o_cq + 2 * c, jnp.minimum(o_cq + 2 * c + 1, o_kr))
    return (_prep_group(w_t, layer, 3 * H, src_qkv),
            _prep_group(w_t, layer, n_mla, src_mla, zero_b_at=n_mla - 1),
            _prep_group(w_t, layer, 3 * HY_CH // LANE, pairs(o_hy)),
            _prep_group(w_t, layer, N_BRANCH * D_MODEL // LANE, pairs(o_g)))


def _pad_w_uq(w):
    w = w.reshape(MLA_Q_RANK, MLA_HEADS, MLA_NOPE + MLA_ROPE)
    w = jnp.pad(w, ((0, 0), (0, 0), (0, MLA_QK_PAD - MLA_NOPE - MLA_ROPE)))
    return w.reshape(MLA_Q_RANK, MLA_HEADS * MLA_QK_PAD).astype(BF16)


def _split_w_ukv(w):
    w = w.reshape(MLA_KV_RANK, MLA_HEADS, 2, MLA_NOPE).transpose(0, 2, 1, 3)
    return w.reshape(MLA_KV_RANK, 2 * MLA_W).astype(BF16)


def _hyena(u3, layer, p, n, nc, with_ctx, tables):
    B = u3.shape[0]
    C = HY_CH
    P = B // 2
    g1d, g1f, g2, g2i, g3, gk, gd, gi = tables
    S = FFT_S
    N1 = 2 * n // S
    mlp = (p["hy_ffn_w1p"][layer], p["hy_ffn_b1"][layer][None], p["hy_ffn_w2"][layer], p["hy_ffn_b2"][layer][None],
           p["hy_ffn_w3"][layer], p["hy_ffn_b3"][layer][None], p["hy_ffn_w4"][layer], p["hy_freq"][layer][None])
    skip = p["hy_skip"][layer]
    cw, cb = p["hy_conv_w"][layer], p["hy_conv_b"][layer][None]

    kc, sums = _hy_filter(n, *mlp)
    kf = _fft_s1(g1f, kc.reshape(HY_ORDER, N1, S, C))
    kf = _fft_s2_filt(kf.reshape(HY_ORDER, 2, N1, S, C), g2, 1.0 / sums[:, 0:1, :])
    dw = [a.reshape(P, N1, S, C) for a in _hy_dwconv(u3, cw, cb, 0, n)]
    z = dw[0]
    for o in range(HY_ORDER):
        a = _fft_s1(g1d, z).reshape(P, 2, N1, S, C)
        b = _fft_s2(a, g2, g2i, kf, o).reshape(P, 2 * N1, S, C)
        z = _fft_s3(g3, b, z, dw[1 + o], skip[o][None], F32 if o + 1 < HY_ORDER else BF16)
    oc_lat = z.reshape(B, n, C)

    if with_ctx:
        kcc, sumc = _hy_filter(nc, *mlp)
        kfc = _dense_spec(kcc, gk, 1.0 / sumc[:, 0:1, :])
        dwc = [a.reshape(P, 2 * nc, C) for a in _hy_dwconv(u3, cw, cb, n // nc, nc)]
        zc = dwc[0]
        for o in range(HY_ORDER):
            zc = _dense_conv(zc, gd, gi, kfc, o, dwc[1 + o], skip[o][None], F32 if o + 1 < HY_ORDER else BF16)
        return jnp.concatenate([oc_lat, zc.reshape(B, nc, C)], axis=1)
    return oc_lat


def kernel(x, c, ctx, c_ctx, ada_w, ada_b, w_in, da_lambda, da_subln_g, mla_q_g, mla_w_uq, mla_kv_g, mla_w_ukv, hy_conv_w, hy_conv_b, hy_ffn_w1, hy_ffn_b1, hy_ffn_w2, hy_ffn_b2, hy_ffn_w3, hy_ffn_b3, hy_ffn_w4, hy_freq, hy_skip, w_branch_a, w_branch_b, w_branch_c, w_out, ln1_g, ln1_b, ffn_w_up, ffn_conv_w, ffn_conv_b, ffn_w_down, ln2_g, ln2_b):
    B, n, D = x.shape
    nc = ctx.shape[1]
    T = n + nc
    rows = B * T
    assert D == D_MODEL and B % 2 == 0 and B < 8
    assert n % ROW_GROUP == 0 and nc % ROW_GROUP == 0 and n % nc == 0 and n % GRID_W == 0
    assert (2 * n) % (8 * FFT_S) == 0 and T % ATTN_TK == 0 and nc % ATTN_TK == 0
    geom = (T // ROW_GROUP, n // ROW_GROUP, B)

    hy = dict(hy_ffn_w1p=jnp.pad(hy_ffn_w1, ((0, 0), (0, HY_FFN - HY_EMB), (0, 0))), hy_ffn_b1=hy_ffn_b1,
              hy_ffn_w2=hy_ffn_w2, hy_ffn_b2=hy_ffn_b2, hy_ffn_w3=hy_ffn_w3, hy_ffn_b3=hy_ffn_b3,
              hy_ffn_w4=hy_ffn_w4, hy_freq=hy_freq, hy_skip=hy_skip, hy_conv_w=hy_conv_w, hy_conv_b=hy_conv_b)
    tables = _dft_tables(n) + _dense_tables(nc)
    rope = _rope_tables(B, n, nc)
    keep_prev, keep_next = _conv_masks(B, n, nc)

    cc = jnp.concatenate([c, c_ctx[None], jnp.zeros((8 - B - 1, D), F32)], axis=0)
    mods = _ada(cc, ada_w, ada_b[:, None, :]).reshape(DEPTH, 8, 6, D)

    xs = jnp.concatenate([x, ctx], axis=1).reshape(rows, D)
    for i in range(DEPTH):
        last = i == DEPTH - 1
        lam_init = 0.8 - 0.6 * math.exp(-0.3 * i)
        lq1, lk1, lq2, lk2 = da_lambda[i].astype(F32)
        lam = (jnp.exp(jnp.sum(lq1 * lk1)) - jnp.exp(jnp.sum(lq2 * lk2)) + lam_init).reshape(1)
        mod = mods[i]
        w_qkv, w_mla, w_hy, w_g = _prep_w_in(w_in, i)

        if last:
            nb, tb = B, n
            sel = lambda b, g: b
        else:
            nb, tb = 1, rows
            sel = lambda b, g: _mod_row(g, *geom)

        qkv, h = _qkv_proj(xs, mod, w_qkv, rope, geom)
        u_hy = _matmul(h, w_hy, F32, 3 * HY_CH, nb=nb, tb=tb)
        gates = _matmul(h, w_g, BF16, D_MODEL, act="sigmoid", nb=nb, tb=tb)

        q_m, k_m, v_m = _mla_prep(h, w_mla, mla_q_g[i][None], mla_kv_g[i][None], _pad_w_uq(mla_w_uq[i]),
                                  _split_w_ukv(mla_w_ukv[i]), rope)
        oa = _diff_attn(qkv.reshape(B, T, 3 * DA_W), lam, da_subln_g[i][None], n, nc, not last, lam_init)
        ob = _mla_attn(q_m.reshape(B, T, -1), k_m.reshape(B, T, -1), v_m.reshape(B, T, -1), n, nc, not last)
        oc = _hyena(u_hy.reshape(B, -1, 3 * HY_CH), i, hy, n, nc, not last, tables)

        merged = _merge(oa.reshape(-1, DA_W), ob.reshape(-1, MLA_W), oc.reshape(-1, HY_CH),
                        w_branch_a[i].astype(BF16), w_branch_b[i].astype(BF16), w_branch_c[i].astype(BF16), gates,
                        nb, tb)
        xs1, h2 = _wo_ln(merged, w_out[i].astype(BF16), xs, mod, ln1_g[i][None], ln1_b[i][None], sel, nb, tb)
        u = _ffn_up(h2, ffn_w_up, i, ffn_conv_w[i], ffn_conv_b[i][None], keep_prev, keep_next, nb, tb)
        xs = _ffn_down(u, ffn_w_down[i].astype(BF16), xs1, mod, ln2_g[i][None], ln2_b[i][None], sel, nb, tb)
    return xs.reshape(B, n, D)
```

```python
import functools
import math

import numpy as np
import jax
import jax.numpy as jnp
from jax import lax
from jax.experimental import pallas as pl
from jax.experimental.pallas import tpu as pltpu

F32 = jnp.float32
BF16 = jnp.bfloat16

D_MODEL = 2048
DEPTH = 2
GRID_W = 64
ROPE_BASE = 10000.0
NORM_EPS = 1e-6
DA_HEADS = 6
DA_DQK = 64
DA_DV = 128
DA_W = DA_HEADS * DA_DV
MLA_HEADS = 6
MLA_Q_RANK = 512
MLA_KV_RANK = 256
MLA_NOPE = 128
MLA_ROPE = 64
MLA_DV = 128
MLA_W = MLA_HEADS * MLA_DV
MLA_QK_PAD = 256
HY_CH = 512
HY_ORDER = 2
HY_EMB = 33
HY_BANDS = (HY_EMB - 1) // 2
HY_FFN = 64
HY_MIN_DECAY = math.log(1e-2) / 1.5
HY_MAX_DECAY = math.log(1e-2) / 0.3
D_FF = 5632
N_BRANCH = 3
DEEPNORM_ALPHA = (2 * DEPTH) ** 0.25
LOG2E = 1.4426950408889634

ROW_GROUP = 256
LANE = 128
FFT_S = 128
VMEM_LIMIT = 52 * 1024 * 1024


def _cparams(*sem):
    return pltpu.CompilerParams(dimension_semantics=sem, vmem_limit_bytes=VMEM_LIMIT)


def _pick(total, prefs):
    for p in prefs:
        if total % p == 0:
            return p
    raise ValueError(f"no tile for {total} in {prefs}")


def _ln(x):
    mu = jnp.mean(x, axis=-1, keepdims=True)
    xc = x - mu
    var = jnp.mean(xc * xc, axis=-1, keepdims=True)
    return xc * lax.rsqrt(var + NORM_EPS)


def _rms(x):
    return x * lax.rsqrt(jnp.mean(x * x, axis=-1, keepdims=True) + NORM_EPS)


def _rope128(u, cos, sa, sb):
    return u * cos + pltpu.roll(u, 16, 1) * sa + pltpu.roll(u, LANE - 16, 1) * sb


def _ada_kernel(c_ref, w_ref, b_ref, o_ref):
    a = c_ref[...]
    a = a * jax.nn.sigmoid(a)
    o_ref[...] = jnp.dot(a.astype(BF16), w_ref[...].astype(BF16), preferred_element_type=F32) + b_ref[...]


def _ada(cc, ada_w, ada_b):
    L, D, N = ada_w.shape
    tn = 1024
    return pl.pallas_call(
        _ada_kernel,
        out_shape=jax.ShapeDtypeStruct((L, 8, N), F32),
        grid=(L, N // tn),
        in_specs=[pl.BlockSpec((8, D), lambda l, j: (0, 0)),
                  pl.BlockSpec((None, D, tn), lambda l, j: (l, 0, j)),
                  pl.BlockSpec((None, 1, tn), lambda l, j: (l, 0, j))],
        out_specs=pl.BlockSpec((None, 8, tn), lambda l, j: (l, 0, j)),
        compiler_params=_cparams("arbitrary", "arbitrary"),
    )(cc, ada_w, ada_b)


def _mod_row(g, gpb, lat_groups, n_batch):
    return jnp.where(g % gpb < lat_groups, g // gpb, n_batch)


def _mm_kernel(a_ref, w_ref, o_ref, *, act):
    acc = jnp.dot(a_ref[...], w_ref[...], preferred_element_type=F32)
    if act == "sigmoid":
        acc = jax.nn.sigmoid(acc)
    o_ref[...] = acc.astype(o_ref.dtype)


def _by_batch(a, nb):
    return a.reshape(nb, a.shape[0] // nb, a.shape[1])


def _matmul(a, w, out_dtype, tn, act=None, nb=1, tb=None):
    a = _by_batch(a, nb)
    K = a.shape[2]
    tb = tb or a.shape[1]
    N = w.shape[1]
    tm = _pick(tb, (1024, 512, 256))
    return pl.pallas_call(
        functools.partial(_mm_kernel, act=act),
        out_shape=jax.ShapeDtypeStruct((nb, tb, N), out_dtype),
        grid=(nb, tb // tm, N // tn),
        in_specs=[pl.BlockSpec((None, tm, K), lambda b, i, j: (b, i, 0)),
                  pl.BlockSpec((K, tn), lambda b, i, j: (0, j))],
        out_specs=pl.BlockSpec((None, tm, tn), lambda b, i, j: (b, i, j)),
        compiler_params=_cparams("arbitrary", "arbitrary", "arbitrary"),
    )(a, w).reshape(nb * tb, N)


def _qkv_kernel(x_ref, mod_ref, w_ref, cos_ref, sa_ref, sb_ref, o_ref, h_ref, *, qscale, geom, groups):
    j = pl.program_id(1)

    @pl.when(j == 0)
    def _():
        for q in range(groups):
            mod = mod_ref[_mod_row(pl.program_id(0) * groups + q, *geom)]
            sl = slice(q * ROW_GROUP, (q + 1) * ROW_GROUP)
            h_ref[sl, :] = (_ln(x_ref[sl, :]) * (1.0 + mod[1:2, :]) + mod[0:1, :]).astype(BF16)

    acc = jnp.dot(h_ref[...], w_ref[...], preferred_element_type=F32)

    @pl.when(j < 2)
    def _():
        cos, sa, sb = cos_ref[...], sa_ref[...], sb_ref[...]
        scale = jnp.where(j == 0, qscale, 1.0).astype(F32)
        for c in range(DA_HEADS):
            u = acc[:, c * LANE:(c + 1) * LANE]
            o_ref[:, c * LANE:(c + 1) * LANE] = (_rope128(u, cos, sa, sb) * scale).astype(BF16)

    @pl.when(j == 2)
    def _():
        o_ref[...] = acc.astype(BF16)


def _qkv_proj(xs, mod, w_qkv, tabs, geom):
    M, K = xs.shape
    tm = _pick(M, (1024, 512, 256))
    tn = DA_W
    row = lambda i, j: (i, 0)
    tab_spec = pl.BlockSpec((tm, LANE), row)
    return pl.pallas_call(
        functools.partial(_qkv_kernel, qscale=DA_DQK ** -0.5 * LOG2E, geom=geom, groups=tm // ROW_GROUP),
        out_shape=(jax.ShapeDtypeStruct((M, 3 * DA_W), BF16), jax.ShapeDtypeStruct((M, K), BF16)),
        grid=(M // tm, 3),
        in_specs=[pl.BlockSpec((tm, K), row),
                  pl.BlockSpec(mod.shape, lambda i, j: (0, 0, 0)),
                  pl.BlockSpec((K, tn), lambda i, j: (0, j)),
                  tab_spec, tab_spec, tab_spec],
        out_specs=(pl.BlockSpec((tm, tn), lambda i, j: (i, j)), pl.BlockSpec((tm, K), row)),
        compiler_params=_cparams("arbitrary", "arbitrary"),
    )(xs, mod, w_qkv, *tabs)


def _mla_prep_kernel(h_ref, wm_ref, qg_ref, kvg_ref, wuq_ref, wukv_ref, cos_ref, sa_ref, sb_ref,
                     q_ref, k_ref, v_ref, *, qscale):
    p = jnp.dot(h_ref[...], wm_ref[...], preferred_element_type=F32)
    cos, sa, sb = cos_ref[...], sa_ref[...], sb_ref[...]
    cq = p[:, :MLA_Q_RANK]
    ckv = p[:, MLA_Q_RANK:MLA_Q_RANK + MLA_KV_RANK]
    kr = p[:, MLA_Q_RANK + MLA_KV_RANK:]
    qn = (_rms(cq) * qg_ref[...]).astype(BF16)
    q = jnp.dot(qn, wuq_ref[...], preferred_element_type=F32)
    kvn = (_rms(ckv) * kvg_ref[...]).astype(BF16)
    kv = jnp.dot(kvn, wukv_ref[...], preferred_element_type=F32)
    krr = _rope128(kr, cos, sa, sb).astype(BF16)
    for h in range(MLA_HEADS):
        o = h * MLA_QK_PAD
        q_ref[:, o:o + LANE] = (q[:, o:o + LANE] * qscale).astype(BF16)
        q_ref[:, o + LANE:o + 2 * LANE] = (_rope128(q[:, o + LANE:o + 2 * LANE], cos, sa, sb) * qscale).astype(BF16)
        k_ref[:, o:o + LANE] = kv[:, h * LANE:(h + 1) * LANE].astype(BF16)
        k_ref[:, o + LANE:o + 2 * LANE] = krr
    v_ref[...] = kv[:, MLA_W:].astype(BF16)


def _mla_prep(h, w_mla, q_g, kv_g, w_uq, w_ukv, tabs):
    M, D = h.shape
    tm = _pick(M, (512, 256))
    row = lambda i: (i, 0)
    full = lambda i: (0, 0)
    qk_w = MLA_HEADS * MLA_QK_PAD
    return pl.pallas_call(
        functools.partial(_mla_prep_kernel, qscale=(MLA_NOPE + MLA_ROPE) ** -0.5 * LOG2E),
        out_shape=(jax.ShapeDtypeStruct((M, qk_w), BF16),
                   jax.ShapeDtypeStruct((M, qk_w), BF16),
                   jax.ShapeDtypeStruct((M, MLA_W), BF16)),
        grid=(M // tm,),
        in_specs=[pl.BlockSpec((tm, D), row),
                  pl.BlockSpec(w_mla.shape, full),
                  pl.BlockSpec((1, MLA_Q_RANK), full),
                  pl.BlockSpec((1, MLA_KV_RANK), full),
                  pl.BlockSpec(w_uq.shape, full),
                  pl.BlockSpec(w_ukv.shape, full),
                  pl.BlockSpec((tm, LANE), row), pl.BlockSpec((tm, LANE), row), pl.BlockSpec((tm, LANE), row)],
        out_specs=(pl.BlockSpec((tm, qk_w), row), pl.BlockSpec((tm, qk_w), row), pl.BlockSpec((tm, MLA_W), row)),
        compiler_params=_cparams("arbitrary"),
    )(h, w_mla, q_g, kv_g, w_uq, w_ukv, *tabs)


_NT = (((1,), (1,)), ((), ()))
ATTN_TQ = 256
ATTN_CHAINS = 8
ATTN_TK = 256


def _skewed_pipeline(n_chains, n_chunks, stages):
    for t in range(n_chains + len(stages) - 1):
        active = [(s, t - s) for s in range(len(stages)) if 0 <= t - s < n_chains]
        for c in range(n_chunks):
            for s, chain in active:
                stages[s][0](chain, c)
        for s, chain in active:
            stages[s][1](chain)


def _acc(old, new, op):
    return new if old is None else op(old, new)


def _lane_halves(x, op):
    return op(x[:, :LANE], x[:, LANE:])


def _attn_steps(run, q_rows, n_keys, lat_steps, n_lat, ctx_rows):
    if ctx_rows == 0:
        run(q_rows, 0, n_keys)
        return
    step = pl.program_id(2)

    @pl.when(step < lat_steps)
    def _():
        run(q_rows, 0, n_keys)

    @pl.when(step == lat_steps)
    def _():
        run(ctx_rows, n_lat, n_keys - n_lat)


def _diff_attn_kernel(lam_ref, q_ref, k_ref, v_ref, g_ref, o_ref, *, out_scale, lat_steps, n_lat, ctx_rows):
    run = functools.partial(_diff_attn_run, lam_ref, q_ref, k_ref, v_ref, g_ref, o_ref, out_scale)
    _attn_steps(run, q_ref.shape[0], k_ref.shape[0], lat_steps, n_lat, ctx_rows)


def _diff_attn_run(lam_ref, q_ref, k_ref, v_ref, g_ref, o_ref, out_scale, q_rows, key0, n_keys):
    lam, g = lam_ref[0], g_ref[...]
    tq = min(ATTN_TQ, q_rows)
    n_chains = q_rows // tq
    n_chunks = n_keys // ATTN_TK
    lane = lax.broadcasted_iota(jnp.int32, (1, LANE), 1)
    lo = (lane < DA_DQK).astype(F32)
    st = [dict(s=[], e=[], mx=[None, None], l=[None, None], o=None) for _ in range(n_chains)]
    rows = lambda i: slice(i * tq, (i + 1) * tq)
    keys = lambda c: slice(key0 + c * ATTN_TK, key0 + (c + 1) * ATTN_TK)

    def qk_chunk(i, c):
        d = st[i]
        if c == 0:
            qf = q_ref[rows(i), :].astype(F32)
            d["q"] = jnp.concatenate([(qf * lo).astype(BF16), (qf * (1.0 - lo)).astype(BF16)], axis=0)
        both = lax.dot_general(d["q"], k_ref[keys(c), :], _NT, preferred_element_type=F32)
        pair = []
        for m in range(2):
            s = both[m * tq:(m + 1) * tq]
            d["mx"][m] = _acc(d["mx"][m], _lane_halves(s, jnp.maximum), jnp.maximum)
            pair.append(s)
        d["s"].append(pair)

    def qk_done(i):
        st[i]["m"] = [jnp.max(mx, axis=-1, keepdims=True) for mx in st[i]["mx"]]

    def exp_chunk(i, c):
        d = st[i]
        pair = []
        for m in range(2):
            e = jnp.exp2(d["s"][c][m] - d["m"][m])
            d["l"][m] = _acc(d["l"][m], _lane_halves(e, jnp.add), jnp.add)

    def exp_done(i):
        d = st[i]
        l1, l2 = [jnp.sum(l, axis=-1, keepdims=True) for l in d["l"]]
        d["r1"] = 1.0 / l1
        d["cf"] = lam * l1 / l2

    def pv_chunk(i, c):
        d = st[i]
        e1, e2 = [jnp.exp2(d["s"][c][m] - d["m"][m]) for m in range(2)]
        w = (e1 - d["cf"] * e2).astype(BF16)
        d["s"][c] = None
        d["o"] = _acc(d["o"], jnp.dot(w, v_ref[keys(c), :], preferred_element_type=F32), jnp.add)

    def pv_done(i):
        o = st[i]["o"] * st[i]["r1"]
        o_ref[rows(i), :] = (_rms(o) * g * out_scale).astype(BF16)

    _skewed_pipeline(n_chains, n_chunks, [(qk_chunk, qk_done), (exp_chunk, exp_done), (pv_chunk, pv_done)])


def _mla_attn_kernel(q_ref, k_ref, v_ref, o_ref, *, lat_steps, n_lat, ctx_rows):
    run = functools.partial(_mla_attn_run, q_ref, k_ref, v_ref, o_ref)
    _attn_steps(run, q_ref.shape[0], k_ref.shape[0], lat_steps, n_lat, ctx_rows)


def _mla_attn_run(q_ref, k_ref, v_ref, o_ref, q_rows, key0, n_keys):
    tq = min(ATTN_TQ, q_rows)
    n_chains = q_rows // tq
    n_chunks = n_keys // ATTN_TK
    st = [dict(s=[], mx=None, l=None, o=None) for _ in range(n_chains)]
    rows = lambda i: slice(i * tq, (i + 1) * tq)
    keys = lambda c: slice(key0 + c * ATTN_TK, key0 + (c + 1) * ATTN_TK)

    def qk_chunk(i, c):
        d = st[i]
        s = lax.dot_general(q_ref[rows(i), :], k_ref[keys(c), :], _NT, preferred_element_type=F32)
        d["mx"] = _acc(d["mx"], _lane_halves(s, jnp.maximum), jnp.maximum)
        d["s"].append(s)

    def qk_done(i):
        st[i]["m"] = jnp.max(st[i]["mx"], axis=-1, keepdims=True)

    def pv_chunk(i, c):
        d = st[i]
        e = jnp.exp2(d["s"][c] - d["m"])
        d["s"][c] = None
        d["l"] = _acc(d["l"], _lane_halves(e, jnp.add), jnp.add)
        d["o"] = _acc(d["o"], jnp.dot(e.astype(BF16), v_ref[keys(c), :], preferred_element_type=F32), jnp.add)

    def pv_done(i):
        d = st[i]
        o_ref[rows(i), :] = (d["o"] * (1.0 / jnp.sum(d["l"], axis=-1, keepdims=True))).astype(BF16)

    _skewed_pipeline(n_chains, n_chunks, [(qk_chunk, qk_done), (pv_chunk, pv_done)])


def _diff_attn(qkv, lam, subln_g, n, nc, ctx_queries, lam_init):
    B, T, _ = qkv.shape
    H = DA_HEADS
    tq = _pick(n, (ATTN_TQ * ATTN_CHAINS, ATTN_TQ))
    lat_steps = n // tq
    kern = functools.partial(_diff_attn_kernel, out_scale=1.0 - lam_init, lat_steps=lat_steps, n_lat=n,
                             ctx_rows=nc if ctx_queries else 0)
    return pl.pallas_call(
        kern,
        out_shape=jax.ShapeDtypeStruct((B, T if ctx_queries else n, DA_W), BF16),
        grid=(B, H, lat_steps + int(ctx_queries)),
        in_specs=[pl.BlockSpec(memory_space=pltpu.SMEM),
                  pl.BlockSpec((None, tq, LANE), lambda b, h, i: (b, i, h)),
                  pl.BlockSpec((None, T, LANE), lambda b, h, i: (b, 0, H + h)),
                  pl.BlockSpec((None, T, LANE), lambda b, h, i: (b, 0, 2 * H + h)),
                  pl.BlockSpec((1, DA_DV), lambda b, h, i: (0, 0))],
        out_specs=pl.BlockSpec((None, tq, LANE), lambda b, h, i: (b, i, h)),
        compiler_params=_cparams("arbitrary", "arbitrary", "arbitrary"),
    )(lam, qkv, qkv, qkv, subln_g)


def _mla_attn(q, k, v, n, nc, ctx_queries):
    B, T, _ = q.shape
    H = MLA_HEADS
    tq = _pick(n, (ATTN_TQ * ATTN_CHAINS, ATTN_TQ))
    lat_steps = n // tq
    kern = functools.partial(_mla_attn_kernel, lat_steps=lat_steps, n_lat=n, ctx_rows=nc if ctx_queries else 0)
    return pl.pallas_call(
        kern,
        out_shape=jax.ShapeDtypeStruct((B, T if ctx_queries else n, MLA_W), BF16),
        grid=(B, H, lat_steps + int(ctx_queries)),
        in_specs=[pl.BlockSpec((None, tq, MLA_QK_PAD), lambda b, h, i: (b, i, h)),
                  pl.BlockSpec((None, T, MLA_QK_PAD), lambda b, h, i: (b, 0, h)),
                  pl.BlockSpec((None, T, MLA_DV), lambda b, h, i: (b, 0, h))],
        out_specs=pl.BlockSpec((None, tq, MLA_DV), lambda b, h, i: (b, i, h)),
        compiler_params=_cparams("arbitrary", "arbitrary", "arbitrary"),
    )(q, k, v)


def _hy_filter_kernel(feat_ref, dec_ref, w1_ref, b1_ref, w2_ref, b2_ref, w3_ref, b3_ref, w4_ref, fr_ref,
                      k_ref, s_ref):
    hp = lax.Precision.HIGHEST
    d, r = pl.program_id(0), pl.program_id(1)
    fr = fr_ref[...]
    h = jnp.sin(fr * (jnp.dot(feat_ref[...], w1_ref[...], precision=hp, preferred_element_type=F32) + b1_ref[...]))
    h = jnp.sin(fr * (jnp.dot(h, w2_ref[...], precision=hp, preferred_element_type=F32) + b2_ref[...]))
    h = jnp.sin(fr * (jnp.dot(h, w3_ref[...], precision=hp, preferred_element_type=F32) + b3_ref[...]))
    h = jnp.dot(h, w4_ref[...], precision=hp, preferred_element_type=F32)
    row = lax.broadcasted_iota(jnp.int32, (h.shape[0], 1), 0)
    first_bwd = jnp.where((d == 1) & (r == 0), 1.0, 0.0)
    scale = dec_ref[...] * (1.0 - jnp.where(row == 0, 1.0, 0.0) * first_bwd)

    @pl.when((d == 0) & (r == 0))
    def _():
        s_ref[...] = jnp.zeros(s_ref.shape, F32)

    for o in range(HY_ORDER):
        ko = h[:, o * HY_CH:(o + 1) * HY_CH] * scale
        k_ref[o] = ko
        s_ref[o] += jnp.broadcast_to(jnp.sum(jnp.abs(ko), axis=0, keepdims=True), (8, HY_CH))


def _hy_filter(n, w1, b1, w2, b2, w3, b3, w4, fr):
    C = HY_CH
    t = jnp.linspace(0.0, 1.0, n, dtype=F32)
    pos = jnp.arange(n, dtype=F32)
    t2 = jnp.concatenate([t, t[::-1]])[:, None]
    pos2 = jnp.concatenate([pos, pos[::-1]])[:, None]
    phase = (2.0 * math.pi / n) * pos2 * jnp.linspace(1e-4, HY_BANDS - 1, HY_BANDS, dtype=F32)[None, :]
    feat = jnp.concatenate([t2, jnp.cos(phase), -jnp.sin(phase)], axis=-1)
    feat = jnp.pad(feat, ((0, 0), (0, HY_FFN - HY_EMB)))
    dec = jnp.exp(-t2 * jnp.abs(jnp.linspace(HY_MIN_DECAY, HY_MAX_DECAY, C, dtype=F32)))
    w4d = w4.reshape(HY_FFN, HY_ORDER, 2, C).transpose(2, 0, 1, 3).reshape(2, HY_FFN, HY_ORDER * C)
    rb = min(1024, n)
    nb = n // rb
    full = lambda d, r: (0, 0)
    return pl.pallas_call(
        _hy_filter_kernel,
        out_shape=(jax.ShapeDtypeStruct((HY_ORDER, 2 * n, C), F32),
                   jax.ShapeDtypeStruct((HY_ORDER, 8, C), F32)),
        grid=(2, nb),
        in_specs=[pl.BlockSpec((rb, HY_FFN), lambda d, r: (d * nb + r, 0)),
                  pl.BlockSpec((rb, C), lambda d, r: (d * nb + r, 0)),
                  pl.BlockSpec(w1.shape, full), pl.BlockSpec(b1.shape, full),
                  pl.BlockSpec(w2.shape, full), pl.BlockSpec(b2.shape, full),
                  pl.BlockSpec(w3.shape, full), pl.BlockSpec(b3.shape, full),
                  pl.BlockSpec((None, HY_FFN, HY_ORDER * C), lambda d, r: (d, 0, 0)),
                  pl.BlockSpec(fr.shape, full)],
        out_specs=(pl.BlockSpec((HY_ORDER, rb, C), lambda d, r: (0, d * nb + r, 0)),
                   pl.BlockSpec((HY_ORDER, 8, C), lambda d, r: (0, 0, 0))),
        compiler_params=_cparams("arbitrary", "arbitrary"),
    )(feat, dec, w1, b1, w2, b2, w3, b3, w4d, fr)


def _hy_dwconv_kernel(*refs):
    for k in range(3):
        u_ref, w_ref, b_ref, o_ref = refs[3 * k], refs[3 * k + 1], refs[3 * k + 2], refs[9 + k]
        u = u_ref[...]
        L = u.shape[0]
        row = lax.broadcasted_iota(jnp.int32, (L, 1), 0)
        up = jnp.where(row == 0, 0.0, pltpu.roll(u, 1, 0))
        dn = jnp.where(row == L - 1, 0.0, pltpu.roll(u, L - 1, 0))
        o_ref[...] = up * w_ref[0:1, :] + u * w_ref[1:2, :] + dn * w_ref[2:3, :] + b_ref[...]


def _hy_dwconv(u, w, b, row_block, length):
    B = u.shape[0]
    cw = LANE
    per = HY_CH // cw
    in_specs, args = [], []
    for k in range(3):
        in_specs += [pl.BlockSpec((None, length, cw), lambda bb, j, k=k: (bb, row_block, k * per + j)),
                     pl.BlockSpec((3, cw), lambda bb, j, k=k: (0, k * per + j)),
                     pl.BlockSpec((1, cw), lambda bb, j, k=k: (0, k * per + j))]
        args += [u, w, b]
    ospec = pl.BlockSpec((None, length, cw), lambda bb, j: (bb, 0, j))
    return pl.pallas_call(
        _hy_dwconv_kernel,
        out_shape=(jax.ShapeDtypeStruct((B, length, HY_CH), F32),) * 3,
        grid=(B, per),
        in_specs=in_specs,
        out_specs=(ospec,) * 3,
        compiler_params=_cparams("arbitrary", "arbitrary"),
    )(*args)


def _dft_tables(n):
    S = FFT_S
    M = 2 * n
    N1 = M // S
    H = N1 // 2
    s2 = np.arange(S)[:, None, None]
    k1 = np.arange(N1)[None, :, None]
    s1 = np.arange(N1)[None, None, :]
    ang = -2.0 * np.pi * ((k1 * (S * s1 + s2)) % M) / M
    fr, fi = np.cos(ang), np.sin(ang)
    g1f = np.concatenate([fr, fi], axis=1)
    frh, fih = fr[:, :, :H], fi[:, :, :H]
    g1d = np.concatenate([np.concatenate([frh, -fih], axis=2),
                          np.concatenate([fih, frh], axis=2)], axis=1)
    er = np.transpose(frh, (0, 2, 1)) / M
    ei = -np.transpose(fih, (0, 2, 1)) / M
    g3 = np.concatenate([np.concatenate([er, -ei], axis=2),
                         np.concatenate([ei, er], axis=2)], axis=1)
    a2 = -2.0 * np.pi * ((np.arange(S)[:, None] * np.arange(S)[None, :]) % S) / S
    f2r, f2i = np.cos(a2), np.sin(a2)
    g2 = np.block([[f2r, -f2i], [f2i, f2r]])
    g2i = np.block([[f2r, f2i], [-f2i, f2r]])
    cast = lambda a: jnp.asarray(a, dtype=F32).astype(BF16)
    return cast(g1d), cast(g1f), cast(g2), cast(g2i), cast(g3)


FFT_NS = 32
FFT_NK = 16


def _fft_s1_kernel(g_ref, x_ref, o_ref):
    xt = jnp.swapaxes(x_ref[...], 0, 1)
    y = jnp.stack([jnp.dot(g_ref[j], xt[j].astype(BF16), preferred_element_type=F32)
                   for j in range(xt.shape[0])], axis=0)
    o_ref[...] = jnp.swapaxes(y, 0, 1).astype(o_ref.dtype)


def _fft_s1(g, x):
    P, Ri, S, C = x.shape
    Ro = g.shape[1]
    ns = FFT_NS
    return pl.pallas_call(
        _fft_s1_kernel,
        out_shape=jax.ShapeDtypeStruct((P, Ro, S, C), BF16),
        grid=(S // ns, P),
        in_specs=[pl.BlockSpec((ns, Ro, Ri), lambda j, p: (j, 0, 0)),
                  pl.BlockSpec((None, Ri, ns, C), lambda j, p: (p, 0, j, 0))],
        out_specs=pl.BlockSpec((None, Ro, ns, C), lambda j, p: (p, 0, j, 0)),
        compiler_params=_cparams("arbitrary", "arbitrary"),
    )(g, x)


def _fft_s2_filt_kernel(a_ref, g_ref, rn_ref, o_ref, *, nk):
    S = FFT_S
    rn = rn_ref[...]
    for t in range(nk):
        d = jnp.concatenate([a_ref[0, t], a_ref[1, t]], axis=0)
        y = jnp.dot(g_ref[...], d, preferred_element_type=F32)
        o_ref[0, t] = y[:S] * rn
        o_ref[1, t] = y[S:] * rn


def _fft_s2_filt(a, g2, rnorm):
    O, _, N1, S, C = a.shape
    nk = min(FFT_NK, N1)
    blk = (None, 2, nk, S, C)
    return pl.pallas_call(
        functools.partial(_fft_s2_filt_kernel, nk=nk),
        out_shape=jax.ShapeDtypeStruct(a.shape, F32),
        grid=(O, N1 // nk),
        in_specs=[pl.BlockSpec(blk, lambda o, j: (o, 0, j, 0, 0)),
                  pl.BlockSpec(g2.shape, lambda o, j: (0, 0)),
                  pl.BlockSpec((None, 1, C), lambda o, j: (o, 0, 0))],
        out_specs=pl.BlockSpec(blk, lambda o, j: (o, 0, j, 0, 0)),
        compiler_params=_cparams("arbitrary", "arbitrary"),
    )(a, g2, rnorm)


def _fft_s2_kernel(a_ref, g_ref, gi_ref, kf_ref, o_ref, *, nk):
    S = FFT_S
    for t in range(nk):
        d = jnp.concatenate([a_ref[0, t], a_ref[1, t]], axis=0)
        y = jnp.dot(g_ref[...], d, preferred_element_type=F32)
        yr, yi = y[:S], y[S:]
        kr, ki = kf_ref[0, t], kf_ref[1, t]
        p = jnp.concatenate([yr * kr - yi * ki, yr * ki + yi * kr], axis=0).astype(BF16)
        b = jnp.dot(gi_ref[...], p, preferred_element_type=F32)
        o_ref[0, t] = b[:S].astype(BF16)
        o_ref[1, t] = b[S:].astype(BF16)


def _fft_s2(a, g2, g2i, kf, order):
    P, _, N1, S, C = a.shape
    nk = min(FFT_NK, N1)
    blk = (None, 2, nk, S, C)
    return pl.pallas_call(
        functools.partial(_fft_s2_kernel, nk=nk),
        out_shape=jax.ShapeDtypeStruct(a.shape, BF16),
        grid=(N1 // nk, P),
        in_specs=[pl.BlockSpec(blk, lambda j, p: (p, 0, j, 0, 0)),
                  pl.BlockSpec(g2.shape, lambda j, p: (0, 0)),
                  pl.BlockSpec(g2i.shape, lambda j, p: (0, 0)),
                  pl.BlockSpec(blk, lambda j, p: (order, 0, j, 0, 0))],
        out_specs=pl.BlockSpec(blk, lambda j, p: (p, 0, j, 0, 0)),
        compiler_params=_cparams("arbitrary", "arbitrary"),
    )(a, g2, g2i, kf)


def _fft_s3_kernel(g_ref, b_ref, z_ref, gate_ref, skip_ref, o_ref):
    bt = jnp.swapaxes(b_ref[...].astype(F32), 0, 1).astype(BF16)
    y = jnp.stack([jnp.dot(g_ref[j], bt[j], preferred_element_type=F32) for j in range(bt.shape[0])], axis=0)
    y = jnp.swapaxes(y, 0, 1)
    o_ref[...] = (gate_ref[...] * (y + skip_ref[...] * z_ref[...])).astype(o_ref.dtype)


def _fft_s3(g3, b, z, gate, skip, out_dtype):
    P, Ri, S, C = b.shape
    Ro = g3.shape[1]
    ns = FFT_NS
    dspec = pl.BlockSpec((None, Ro, ns, C), lambda j, p: (p, 0, j, 0))
    return pl.pallas_call(
        _fft_s3_kernel,
        out_shape=jax.ShapeDtypeStruct((P, Ro, S, C), out_dtype),
        grid=(S // ns, P),
        in_specs=[pl.BlockSpec((ns, Ro, Ri), lambda j, p: (j, 0, 0)),
                  pl.BlockSpec((None, Ri, ns, C), lambda j, p: (p, 0, j, 0)),
                  dspec, dspec,
                  pl.BlockSpec((1, C), lambda j, p: (0, 0))],
        out_specs=dspec,
        compiler_params=_cparams("arbitrary", "arbitrary"),
    )(g3, b, z, gate, skip)


def _dense_tables(m):
    M = 2 * m
    ang = -2.0 * np.pi * ((np.arange(M)[:, None] * np.arange(M)[None, :]) % M) / M
    fr, fi = np.cos(ang), np.sin(ang)
    gk = np.concatenate([fr, fi], axis=0)
    gd = np.block([[fr[:, :m], -fi[:, :m]], [fi[:, :m], fr[:, :m]]])
    er, ei = fr[:m, :] / M, -fi[:m, :] / M
    gi = np.block([[er, -ei], [ei, er]])
    cast = lambda a: jnp.asarray(a, dtype=F32).astype(BF16)
    return cast(gk), cast(gd), cast(gi)


def _dense_spec_kernel(k_ref, g_ref, rn_ref, o_ref):
    o_ref[...] = jnp.dot(g_ref[...], k_ref[...].astype(BF16), preferred_element_type=F32) * rn_ref[...]


def _dense_spec(kc, gk, rnorm):
    O, M, C = kc.shape
    return pl.pallas_call(
        _dense_spec_kernel,
        out_shape=jax.ShapeDtypeStruct((O, 2 * M, C), F32),
        grid=(O,),
        in_specs=[pl.BlockSpec((None, M, C), lambda o: (o, 0, 0)),
                  pl.BlockSpec(gk.shape, lambda o: (0, 0)),
                  pl.BlockSpec((None, 1, C), lambda o: (o, 0, 0))],
        out_specs=pl.BlockSpec((None, 2 * M, C), lambda o: (o, 0, 0)),
        compiler_params=_cparams("arbitrary"),
    )(kc, gk, rnorm)


def _dense_conv_kernel(x_ref, gd_ref, gi_ref, kf_ref, gate_ref, skip_ref, o_ref):
    x = x_ref[...]
    y = jnp.dot(gd_ref[...], x.astype(BF16), preferred_element_type=F32)
    M = y.shape[0] // 2
    yr, yi = y[:M], y[M:]
    kr, ki = kf_ref[:M], kf_ref[M:]
    p = jnp.concatenate([yr * kr - yi * ki, yr * ki + yi * kr], axis=0).astype(BF16)
    conv = jnp.dot(gi_ref[...], p, preferred_element_type=F32)
    o_ref[...] = (gate_ref[...] * (conv + skip_ref[...] * x)).astype(o_ref.dtype)


def _dense_conv(x, gd, gi, kf, order, gate, skip, out_dtype):
    P, R, C = x.shape
    dspec = pl.BlockSpec((None, R, C), lambda p: (p, 0, 0))
    return pl.pallas_call(
        _dense_conv_kernel,
        out_shape=jax.ShapeDtypeStruct((P, R, C), out_dtype),
        grid=(P,),
        in_specs=[dspec,
                  pl.BlockSpec(gd.shape, lambda p: (0, 0)),
                  pl.BlockSpec(gi.shape, lambda p: (0, 0)),
                  pl.BlockSpec((None,) + kf.shape[1:], lambda p: (order, 0, 0)),
                  dspec,
                  pl.BlockSpec((1, C), lambda p: (0, 0))],
        out_specs=dspec,
        compiler_params=_cparams("arbitrary"),
    )(x, gd, gi, kf, gate, skip)


def _merge_kernel(oa_ref, ob_ref, oc_ref, wa_ref, wb_ref, wc_ref, g0_ref, g1_ref, g2_ref, o_ref):
    ya = jnp.dot(oa_ref[...], wa_ref[...], preferred_element_type=F32)
    yb = jnp.dot(ob_ref[...], wb_ref[...], preferred_element_type=F32)
    yc = jnp.dot(oc_ref[...], wc_ref[...], preferred_element_type=F32)
    m = g0_ref[...].astype(F32) * ya + g1_ref[...].astype(F32) * yb + g2_ref[...].astype(F32) * yc
    o_ref[...] = m.astype(BF16)


def _merge(oa, ob, oc, w_ba, w_bb, w_bc, gates, nb, tb):
    D = D_MODEL
    tm = _pick(tb, (1024, 512, 256))
    tn = 1024
    nt = D // tn
    oa, ob, oc, gates = (_by_batch(a, nb) for a in (oa, ob, oc, gates))
    row = lambda b, i, j: (b, i, 0)
    col = lambda b, i, j: (0, j)
    return pl.pallas_call(
        _merge_kernel,
        out_shape=jax.ShapeDtypeStruct((nb, tb, D), BF16),
        grid=(nb, tb // tm, nt),
        in_specs=[pl.BlockSpec((None, tm, DA_W), row), pl.BlockSpec((None, tm, MLA_W), row),
                  pl.BlockSpec((None, tm, HY_CH), row),
                  pl.BlockSpec((DA_W, tn), col), pl.BlockSpec((MLA_W, tn), col), pl.BlockSpec((HY_CH, tn), col),
                  pl.BlockSpec((None, tm, tn), lambda b, i, j: (b, i, j)),
                  pl.BlockSpec((None, tm, tn), lambda b, i, j: (b, i, nt + j)),
                  pl.BlockSpec((None, tm, tn), lambda b, i, j: (b, i, 2 * nt + j))],
        out_specs=pl.BlockSpec((None, tm, tn), lambda b, i, j: (b, i, j)),
        compiler_params=_cparams("arbitrary", "arbitrary", "arbitrary"),
    )(oa, ob, oc, w_ba, w_bb, w_bc, gates, gates, gates).reshape(nb * tb, D)


def _wo_ln_kernel(m_ref, w_ref, xs_ref, mod_ref, g_ref, b_ref, xs1_ref, h2_ref, *, sel, groups):
    y = jnp.dot(m_ref[...], w_ref[...], preferred_element_type=F32)
    g, b = g_ref[...], b_ref[...]
    for q in range(groups):
        mod = mod_ref[sel(pl.program_id(0), pl.program_id(1) * groups + q)]
        sl = slice(q * ROW_GROUP, (q + 1) * ROW_GROUP)
        x1 = _ln(DEEPNORM_ALPHA * xs_ref[sl, :] + mod[2:3, :] * y[sl, :]) * g + b
        xs1_ref[sl, :] = x1
        h2_ref[sl, :] = (_ln(x1) * (1.0 + mod[4:5, :]) + mod[3:4, :]).astype(BF16)


def _wo_ln(merged, w_o, xs, mod, ln_g, ln_b, sel, nb, tb):
    D = D_MODEL
    tm = _pick(tb, (512, 256))
    merged, xs = _by_batch(merged, nb), _by_batch(xs, nb)
    row = lambda b, i: (b, i, 0)
    full2 = lambda b, i: (0, 0)
    xs1, h2 = pl.pallas_call(
        functools.partial(_wo_ln_kernel, sel=sel, groups=tm // ROW_GROUP),
        out_shape=(jax.ShapeDtypeStruct((nb, tb, D), F32), jax.ShapeDtypeStruct((nb, tb, D), BF16)),
        grid=(nb, tb // tm),
        in_specs=[pl.BlockSpec((None, tm, D), row),
                  pl.BlockSpec((D, D), full2),
                  pl.BlockSpec((None, tm, D), row),
                  pl.BlockSpec(mod.shape, lambda b, i: (0, 0, 0)),
                  pl.BlockSpec((1, D), full2), pl.BlockSpec((1, D), full2)],
        out_specs=(pl.BlockSpec((None, tm, D), row), pl.BlockSpec((None, tm, D), row)),
        compiler_params=_cparams("arbitrary", "arbitrary"),
    )(merged, w_o, xs, mod, ln_g, ln_b)
    return xs1.reshape(nb * tb, D), h2.reshape(nb * tb, D)


FFN_HALO = 16


def _ffn_up_kernel(hp_ref, h_ref, hn_ref, wa_ref, wv_ref, cw_ref, cb_ref, kp_ref, kn_ref, o_ref, wa_s, wv_s):
    tm, tn = o_ref.shape

    @pl.when((pl.program_id(1) == 0) & (pl.program_id(2) == 0))
    def _():
        wa_s[...] = wa_ref[...].astype(BF16)
        wv_s[...] = wv_ref[...].astype(BF16)

    hm = h_ref[...]
    hext = jnp.concatenate([hp_ref[...], hm, hn_ref[...]], axis=0)
    a = jnp.dot(hext, wa_s[...], preferred_element_type=F32)
    v = jnp.dot(hm, wv_s[...], preferred_element_type=F32)
    ext = tm + 2 * FFN_HALO
    rep = tn // LANE
    keep_prev = jnp.tile(kp_ref[...], (1, rep))
    keep_next = jnp.tile(kn_ref[...], (1, rep))
    a_prev = pltpu.roll(a, 1, 0)[FFN_HALO:FFN_HALO + tm] * keep_prev
    a_next = pltpu.roll(a, ext - 1, 0)[FFN_HALO:FFN_HALO + tm] * keep_next
    cv = a_prev * cw_ref[0:1, :] + a[FFN_HALO:FFN_HALO + tm] * cw_ref[1:2, :] + a_next * cw_ref[2:3, :] + cb_ref[...]
    o_ref[...] = (cv * jax.nn.sigmoid(cv) * v).astype(BF16)


def _ffn_up(h2, w_up, layer, conv_w, conv_b, keep_prev, keep_next, nb, tb):
    D = D_MODEL
    tm = _pick(tb, (1024, 512, 256))
    tn = 512
    nt = D_FF // tn
    hb = tm // FFN_HALO
    last = tb // FFN_HALO - 1
    h2, keep_prev, keep_next = (_by_batch(a, nb) for a in (h2, keep_prev, keep_next))
    mask_spec = pl.BlockSpec((None, tm, LANE), lambda j, b, i: (b, i, 0))
    return pl.pallas_call(
        _ffn_up_kernel,
        out_shape=jax.ShapeDtypeStruct((nb, tb, D_FF), BF16),
        grid=(nt, nb, tb // tm),
        in_specs=[pl.BlockSpec((None, FFN_HALO, D), lambda j, b, i: (b, jnp.maximum(i * hb - 1, 0), 0)),
                  pl.BlockSpec((None, tm, D), lambda j, b, i: (b, i, 0)),
                  pl.BlockSpec((None, FFN_HALO, D), lambda j, b, i: (b, jnp.minimum((i + 1) * hb, last), 0)),
                  pl.BlockSpec((None, D, tn), lambda j, b, i: (layer, 0, j)),
                  pl.BlockSpec((None, D, tn), lambda j, b, i: (layer, 0, nt + j)),
                  pl.BlockSpec((3, tn), lambda j, b, i: (0, j)),
                  pl.BlockSpec((1, tn), lambda j, b, i: (0, j)),
                  mask_spec, mask_spec],
        out_specs=pl.BlockSpec((None, tm, tn), lambda j, b, i: (b, i, j)),
        scratch_shapes=[pltpu.VMEM((D, tn), BF16), pltpu.VMEM((D, tn), BF16)],
        compiler_params=_cparams("arbitrary", "arbitrary", "arbitrary"),
    )(h2, h2, h2, w_up, w_up, conv_w, conv_b, keep_prev, keep_next).reshape(nb * tb, D_FF)


def _ffn_down_kernel(u_ref, w_ref, xs_ref, mod_ref, g_ref, b_ref, o_ref, acc_ref, *, sel, groups, nk):
    k = pl.program_id(2)

    @pl.when(k == 0)
    def _():
        acc_ref[...] = jnp.zeros(acc_ref.shape, F32)

    acc_ref[...] += jnp.dot(u_ref[...], w_ref[...], preferred_element_type=F32)

    @pl.when(k == nk - 1)
    def _():
        g, b = g_ref[...], b_ref[...]
        for q in range(groups):
            mod = mod_ref[sel(pl.program_id(0), pl.program_id(1) * groups + q)]
            sl = slice(q * ROW_GROUP, (q + 1) * ROW_GROUP)
            o_ref[sl, :] = _ln(DEEPNORM_ALPHA * xs_ref[sl, :] + mod[5:6, :] * acc_ref[sl, :]) * g + b


def _ffn_down(u, w_down, xs1, mod, ln_g, ln_b, sel, nb, tb):
    D = D_MODEL
    tm = _pick(tb, (512, 256))
    tk = D_FF // 2
    nk = D_FF // tk
    u, xs1 = _by_batch(u, nb), _by_batch(xs1, nb)
    row = lambda b, i, k: (b, i, 0)
    full2 = lambda b, i, k: (0, 0)
    return pl.pallas_call(
        functools.partial(_ffn_down_kernel, sel=sel, groups=tm // ROW_GROUP, nk=nk),
        out_shape=jax.ShapeDtypeStruct((nb, tb, D), F32),
        grid=(nb, tb // tm, nk),
        in_specs=[pl.BlockSpec((None, tm, tk), lambda b, i, k: (b, i, k)),
                  pl.BlockSpec((tk, D), lambda b, i, k: (k, 0)),
                  pl.BlockSpec((None, tm, D), row),
                  pl.BlockSpec(mod.shape, lambda b, i, k: (0, 0, 0)),
                  pl.BlockSpec((1, D), full2), pl.BlockSpec((1, D), full2)],
        out_specs=pl.BlockSpec((None, tm, D), row),
        scratch_shapes=[pltpu.VMEM((tm, D), F32)],
        compiler_params=_cparams("arbitrary", "arbitrary", "arbitrary"),
    )(u, w_down, xs1, mod, ln_g, ln_b).reshape(nb * tb, D)


def _rope_tables(B, n, nc):
    half = DA_DQK // 2
    inv = ROPE_BASE ** (-jnp.arange(0, half, 2, dtype=F32) / half)
    t = jnp.arange(n, dtype=jnp.int32)
    ang_r = (t // GRID_W).astype(F32)[:, None] * inv[None, :]
    ang_c = (t % GRID_W).astype(F32)[:, None] * inv[None, :]
    ang = jnp.concatenate([ang_r, ang_r, ang_c, ang_c], axis=-1)
    cos, sin = jnp.cos(ang), jnp.sin(ang)
    upper = (jnp.arange(DA_DQK) % half) >= half // 2
    sa = jnp.where(upper, sin, 0.0)
    sb = jnp.where(upper, 0.0, -sin)

    def full(tab, fill):
        tab = jnp.concatenate([tab, jnp.full((nc, DA_DQK), fill, F32)], axis=0)
        tab = jnp.tile(tab, (B, LANE // DA_DQK))
        return tab

    return full(cos, 1.0), full(sa, 0.0), full(sb, 0.0)


def _conv_masks(B, n, nc):
    T = n + nc
    t = jnp.arange(T)
    keep_prev = ((t != 0) & (t != n)).astype(F32)
    keep_next = ((t != n - 1) & (t != T - 1)).astype(F32)
    widen = lambda m: jnp.tile(m[:, None], (B, LANE))
    return widen(keep_prev), widen(keep_next)


W_IN_WIDTHS = (DA_W, DA_W, DA_W, MLA_Q_RANK, MLA_KV_RANK, MLA_ROPE, 3 * HY_CH, N_BRANCH * D_MODEL)
W_IN_OFFS = tuple(sum(W_IN_WIDTHS[:j]) for j in range(len(W_IN_WIDTHS)))
W_MLA_COLS = MLA_Q_RANK + MLA_KV_RANK + LANE


W_HALF = LANE // 2


def _prep_w_in_kernel(a_ref, b_ref, o_ref, *, zero_b_at):
    b = b_ref[...]
    if zero_b_at is not None:
        b = jnp.where(pl.program_id(0) == zero_b_at, 0.0, b)
    o_ref[...] = jnp.transpose(jnp.concatenate([a_ref[...], b], axis=0)).astype(BF16)


def _prep_group(w_t, layer, n_blocks, src, zero_b_at=None):
    D = w_t.shape[2]
    return pl.pallas_call(
        functools.partial(_prep_w_in_kernel, zero_b_at=zero_b_at),
        out_shape=jax.ShapeDtypeStruct((D, n_blocks * LANE), BF16),
        grid=(n_blocks,),
        in_specs=[pl.BlockSpec((None, W_HALF, D), lambda c: (layer, src(c)[0], 0)),
                  pl.BlockSpec((None, W_HALF, D), lambda c: (layer, src(c)[1], 0))],
        out_specs=pl.BlockSpec((D, LANE), lambda c: (0, c)),
        compiler_params=_cparams("arbitrary"),
    )(w_t, w_t)


def _prep_w_in(w_in, layer):
    assert w_in.shape[2] == sum(W_IN_WIDTHS) and all(o % W_HALF == 0 for o in W_IN_OFFS)
    w_t = jnp.swapaxes(w_in, 1, 2)
    o_q, o_k, o_v, o_cq, _, o_kr, o_hy, o_g = (o // W_HALF for o in W_IN_OFFS)
    H = DA_HEADS

    def src_qkv(c):
        part, h = c // H, c % H
        a = jnp.where(part < 2, part * 2 * H + h, o_v + 2 * h)
        return a, jnp.where(part < 2, a + H, a + 1)

    pairs = lambda first: (lambda c: (first + 2 * c, first + 2 * c + 1))
    n_mla = W_MLA_COLS // LANE
    src_mla = lambda c: (o_cq + 2 * c, jnp.minimum(o_cq + 2 * c + 1, o_kr))
    return (_prep_group(w_t, layer, 3 * H, src_qkv),
            _prep_group(w_t, layer, n_mla, src_mla, zero_b_at=n_mla - 1),
            _prep_group(w_t, layer, 3 * HY_CH // LANE, pairs(o_hy)),
            _prep_group(w_t, layer, N_BRANCH * D_MODEL // LANE, pairs(o_g)))


def _pad_w_uq(w):
    w = w.reshape(MLA_Q_RANK, MLA_HEADS, MLA_NOPE + MLA_ROPE)
    w = jnp.pad(w, ((0, 0), (0, 0), (0, MLA_QK_PAD - MLA_NOPE - MLA_ROPE)))
    return w.reshape(MLA_Q_RANK, MLA_HEADS * MLA_QK_PAD).astype(BF16)


def _split_w_ukv(w):
    w = w.reshape(MLA_KV_RANK, MLA_HEADS, 2, MLA_NOPE).transpose(0, 2, 1, 3)
    return w.reshape(MLA_KV_RANK, 2 * MLA_W).astype(BF16)


def _hyena(u3, layer, p, n, nc, with_ctx, tables):
    B = u3.shape[0]
    C = HY_CH
    P = B // 2
    g1d, g1f, g2, g2i, g3, gk, gd, gi = tables
    S = FFT_S
    N1 = 2 * n // S
    mlp = (p["hy_ffn_w1p"][layer], p["hy_ffn_b1"][layer][None], p["hy_ffn_w2"][layer], p["hy_ffn_b2"][layer][None],
           p["hy_ffn_w3"][layer], p["hy_ffn_b3"][layer][None], p["hy_ffn_w4"][layer], p["hy_freq"][layer][None])
    skip = p["hy_skip"][layer]
    cw, cb = p["hy_conv_w"][layer], p["hy_conv_b"][layer][None]

    kc, sums = _hy_filter(n, *mlp)
    kf = _fft_s1(g1f, kc.reshape(HY_ORDER, N1, S, C))
    kf = _fft_s2_filt(kf.reshape(HY_ORDER, 2, N1, S, C), g2, 1.0 / sums[:, 0:1, :])
    dw = [a.reshape(P, N1, S, C) for a in _hy_dwconv(u3, cw, cb, 0, n)]
    z = dw[0]
    for o in range(HY_ORDER):
        a = _fft_s1(g1d, z).reshape(P, 2, N1, S, C)
        b = _fft_s2(a, g2, g2i, kf, o).reshape(P, 2 * N1, S, C)
        z = _fft_s3(g3, b, z, dw[1 + o], skip[o][None], F32 if o + 1 < HY_ORDER else BF16)
    oc_lat = z.reshape(B, n, C)

    if with_ctx:
        kcc, sumc = _hy_filter(nc, *mlp)
        kfc = _dense_spec(kcc, gk, 1.0 / sumc[:, 0:1, :])
        dwc = [a.reshape(P, 2 * nc, C) for a in _hy_dwconv(u3, cw, cb, n // nc, nc)]
        zc = dwc[0]
        for o in range(HY_ORDER):
            zc = _dense_conv(zc, gd, gi, kfc, o, dwc[1 + o], skip[o][None], F32 if o + 1 < HY_ORDER else BF16)
        return jnp.concatenate([oc_lat, zc.reshape(B, nc, C)], axis=1)
    return oc_lat


def kernel(x, c, ctx, c_ctx, ada_w, ada_b, w_in, da_lambda, da_subln_g, mla_q_g, mla_w_uq, mla_kv_g, mla_w_ukv, hy_conv_w, hy_conv_b, hy_ffn_w1, hy_ffn_b1, hy_ffn_w2, hy_ffn_b2, hy_ffn_w3, hy_ffn_b3, hy_ffn_w4, hy_freq, hy_skip, w_branch_a, w_branch_b, w_branch_c, w_out, ln1_g, ln1_b, ffn_w_up, ffn_conv_w, ffn_conv_b, ffn_w_down, ln2_g, ln2_b):
    B, n, D = x.shape
    nc = ctx.shape[1]
    T = n + nc
    rows = B * T
    assert D == D_MODEL and B % 2 == 0 and B < 8
    assert n % ROW_GROUP == 0 and nc % ROW_GROUP == 0 and n % nc == 0 and n % GRID_W == 0
    assert (2 * n) % (8 * FFT_S) == 0 and T % ATTN_TK == 0 and nc % ATTN_TK == 0
    geom = (T // ROW_GROUP, n // ROW_GROUP, B)

    hy = dict(hy_ffn_w1p=jnp.pad(hy_ffn_w1, ((0, 0), (0, HY_FFN - HY_EMB), (0, 0))), hy_ffn_b1=hy_ffn_b1,
              hy_ffn_w2=hy_ffn_w2, hy_ffn_b2=hy_ffn_b2, hy_ffn_w3=hy_ffn_w3, hy_ffn_b3=hy_ffn_b3,
              hy_ffn_w4=hy_ffn_w4, hy_freq=hy_freq, hy_skip=hy_skip, hy_conv_w=hy_conv_w, hy_conv_b=hy_conv_b)
    tables = _dft_tables(n) + _dense_tables(nc)
    rope = _rope_tables(B, n, nc)
    keep_prev, keep_next = _conv_masks(B, n, nc)

    cc = jnp.concatenate([c, c_ctx[None], jnp.zeros((8 - B - 1, D), F32)], axis=0)
    mods = _ada(cc, ada_w, ada_b[:, None, :]).reshape(DEPTH, 8, 6, D)

    xs = jnp.concatenate([x, ctx], axis=1).reshape(rows, D)
    for i in range(DEPTH):
        last = i == DEPTH - 1
        lam_init = 0.8 - 0.6 * math.exp(-0.3 * i)
        lq1, lk1, lq2, lk2 = da_lambda[i].astype(F32)
        lam = (jnp.exp(jnp.sum(lq1 * lk1)) - jnp.exp(jnp.sum(lq2 * lk2)) + lam_init).reshape(1)
        mod = mods[i]
        w_qkv, w_mla, w_hy, w_g = _prep_w_in(w_in, i)

        if last:
            nb, tb = B, n
            sel = lambda b, g: b
        else:
            nb, tb = 1, rows
            sel = lambda b, g: _mod_row(g, *geom)

        qkv, h = _qkv_proj(xs, mod, w_qkv, rope, geom)
        u_hy = _matmul(h, w_hy, F32, 3 * HY_CH, nb=nb, tb=tb)
        gates = _matmul(h, w_g, BF16, D_MODEL, act="sigmoid", nb=nb, tb=tb)

        q_m, k_m, v_m = _mla_prep(h, w_mla, mla_q_g[i][None], mla_kv_g[i][None], _pad_w_uq(mla_w_uq[i]),
                                  _split_w_ukv(mla_w_ukv[i]), rope)
        oa = _diff_attn(qkv.reshape(B, T, 3 * DA_W), lam, da_subln_g[i][None], n, nc, not last, lam_init)
        ob = _mla_attn(q_m.reshape(B, T, -1), k_m.reshape(B, T, -1), v_m.reshape(B, T, -1), n, nc, not last)
        oc = _hyena(u_hy.reshape(B, -1, 3 * HY_CH), i, hy, n, nc, not last, tables)

        merged = _merge(oa.reshape(-1, DA_W), ob.reshape(-1, MLA_W), oc.reshape(-1, HY_CH),
                        w_branch_a[i].astype(BF16), w_branch_b[i].astype(BF16), w_branch_c[i].astype(BF16), gates,
                        nb, tb)
        xs1, h2 = _wo_ln(merged, w_out[i].astype(BF16), xs, mod, ln1_g[i][None], ln1_b[i][None], sel, nb, tb)
        u = _ffn_up(h2, ffn_w_up, i, ffn_conv_w[i], ffn_conv_b[i][None], keep_prev, keep_next, nb, tb)
        xs = _ffn_down(u, ffn_w_down[i].astype(BF16), xs1, mod, ln2_g[i][None], ln2_b[i][None], sel, nb, tb)
    return xs.reshape(B, n, D)
```
